```python
import math
import jax, jax.numpy as jnp
from jax import lax
import numpy as np

D_MODEL = 2048
BATCH = 32
SEQ = 256
DEPTH = 2
DEC_BATCH = 8
DEC_SEQ = 1024
PAST_LEN = 512

GRID_W = 64
HEAD_DIM = 128
QBLK = 128
ROPE_BASE = 10000.0
EPS = 1e-6
NEG_INF = -1e30

MLA_HEADS = 4
MLA_Q_RANK = 384
MLA_KV_RANK = 128
MLA_NOPE = 128
MLA_ROPE = 64
MLA_V = 128
SWA_HEADS = 4
SWA_KV_HEADS = 2
SWA_WINDOW = 128
SWA_BLK = 128
NA_HEADS = 4
NA_ROWS = 8
NA_COLS = 16
DIFF_HEADS = 4
DIFF_QK = 64
DIFF_V = 2 * DIFF_QK

N_BRANCH = 4
BRANCH_W = 512
FFN_HIDDEN = ((8 * D_MODEL + 3 * 256 - 1) // (3 * 256)) * 256

PROJ_SIZES = (MLA_Q_RANK, MLA_KV_RANK, MLA_ROPE,
              SWA_HEADS * HEAD_DIM, SWA_KV_HEADS * HEAD_DIM, SWA_KV_HEADS * HEAD_DIM,
              NA_HEADS * HEAD_DIM, NA_HEADS * HEAD_DIM, NA_HEADS * HEAD_DIM,
              DIFF_HEADS * 2 * DIFF_QK, DIFF_HEADS * 2 * DIFF_QK, DIFF_HEADS * DIFF_V)
IN_WIDTH = (MLA_Q_RANK + MLA_KV_RANK + MLA_ROPE + (SWA_HEADS + 2 * SWA_KV_HEADS) * HEAD_DIM
            + 3 * NA_HEADS * HEAD_DIM + DIFF_HEADS * (4 * DIFF_QK + DIFF_V))

kernel_name = 'hybrid_diffusion_prefix_trunk_step'


def split_points():
    pts, acc = [], 0
    for s in PROJ_SIZES[:-1]:
        acc += s
        pts.append(acc)
    return pts


def lambda_init(l):
    return 0.8 - 0.6 * math.exp(-0.3 * l)


def rmsnorm(x, g):
    xf = x.astype(jnp.float32)
    y = xf * lax.rsqrt(jnp.mean(xf * xf, axis=-1, keepdims=True) + EPS)
    return (y * g.astype(jnp.float32)).astype(x.dtype)


def axial_rope(x):
    n, r = x.shape[1], x.shape[-1]
    t = jnp.arange(n)
    row = (t // GRID_W).astype(jnp.float32)
    col = (t % GRID_W).astype(jnp.float32)
    quarter = r // 4
    inv = 1.0 / (ROPE_BASE ** (jnp.arange(quarter, dtype=jnp.float32) / quarter))
    ang = jnp.concatenate([row[:, None] * inv, col[:, None] * inv], axis=-1)
    cos = jnp.cos(ang)[None, :, None, :]
    sin = jnp.sin(ang)[None, :, None, :]
    xf = x.astype(jnp.float32)
    x1, x2 = xf[..., :r // 2], xf[..., r // 2:]
    return jnp.concatenate([x1 * cos - x2 * sin, x1 * sin + x2 * cos], axis=-1).astype(x.dtype)


def attend_dense(q, k, v, scale, sink=None):
    b, nq, h, dq = q.shape
    hk = k.shape[2]
    g = h // hk
    dv = v.shape[-1]
    qb = q.reshape(b, nq // QBLK, QBLK, hk, g, dq).swapaxes(0, 1)

    def block(qi):
        s = jnp.einsum('bqkgd,bskd->bkgqs', qi, k).astype(jnp.float32) * scale
        if sink is None:
            p = jax.nn.softmax(s, axis=-1)
        else:
            sk = jnp.broadcast_to(sink.astype(jnp.float32).reshape(hk, g)[None, :, :, None, None],
                                  s.shape[:-1] + (1,))
            p = jax.nn.softmax(jnp.concatenate([sk, s], axis=-1), axis=-1)[..., 1:]
        return jnp.einsum('bkgqs,bskd->bqkgd', p.astype(v.dtype), v)

    o = lax.map(block, qb)
    return o.swapaxes(0, 1).reshape(b, nq, h, dv)


def mla_attend(q_nope, q_rope, ckv, krope, w_ukv):
    b, nk = ckv.shape[0], ckv.shape[1]
    kv = (ckv @ w_ukv).reshape(b, nk, MLA_HEADS, MLA_NOPE + MLA_V)
    k = jnp.concatenate([kv[..., :MLA_NOPE],
                         jnp.broadcast_to(krope, (b, nk, MLA_HEADS, MLA_ROPE))], axis=-1)
    q = jnp.concatenate([q_nope, q_rope], axis=-1)
    return attend_dense(q, k, kv[..., MLA_NOPE:], (MLA_NOPE + MLA_ROPE) ** -0.5)


def swa_latent(q, k, v, k_ctx, v_ctx, sink):
    b, n, h, d = q.shape
    hk = k.shape[2]
    g = h // hk
    nb = n // SWA_BLK
    nctx = k_ctx.shape[1]
    pad = ((0, 0), (SWA_BLK, SWA_BLK), (0, 0), (0, 0))
    kp = jnp.pad(k, pad)
    vp = jnp.pad(v, pad)
    idx = jnp.arange(nb)[:, None] * SWA_BLK + jnp.arange(3 * SWA_BLK)[None, :]
    kb = kp[:, idx]
    vb = vp[:, idx]
    kpos = idx - SWA_BLK
    qpos = jnp.arange(nb)[:, None] * SWA_BLK + jnp.arange(SWA_BLK)[None, :]
    valid = ((jnp.abs(qpos[:, :, None] - kpos[:, None, :]) <= SWA_WINDOW)
             & (kpos[:, None, :] >= 0) & (kpos[:, None, :] < n))
    qg = q.reshape(b, nb, SWA_BLK, hk, g, d)
    scale = d ** -0.5
    s_loc = jnp.einsum('bnqkgd,bnskd->bnkgqs', qg, kb).astype(jnp.float32) * scale
    s_loc = jnp.where(valid[None, :, None, None, :, :], s_loc, NEG_INF)
    s_ctx = jnp.einsum('bnqkgd,bskd->bnkgqs', qg, k_ctx).astype(jnp.float32) * scale
    sk = jnp.broadcast_to(sink.astype(jnp.float32).reshape(hk, g)[None, None, :, :, None, None],
                          s_ctx.shape[:-1] + (1,))
    p = jax.nn.softmax(jnp.concatenate([sk, s_ctx, s_loc], axis=-1), axis=-1).astype(v.dtype)
    o = (jnp.einsum('bnkgqs,bskd->bnqkgd', p[..., 1:1 + nctx], v_ctx)
         + jnp.einsum('bnkgqs,bnskd->bnqkgd', p[..., 1 + nctx:], vb))
    return o.reshape(b, n, h, d)


def na_latent(q, k, v, k_ctx, v_ctx, rpb):
    b, n, h, d = q.shape
    rows = n // GRID_W
    kh = min(NA_ROWS, rows)
    nctx = k_ctx.shape[1]
    qg = q.reshape(b, rows, GRID_W, h, d)
    kg = k.reshape(b, rows, GRID_W, h, d)
    vg = v.reshape(b, rows, GRID_W, h, d)
    r = jnp.arange(rows)
    rs = jnp.clip(r - kh // 2, 0, rows - kh)
    row_idx = rs[:, None] + jnp.arange(kh)[None, :]
    k_rows = kg[:, row_idx]
    v_rows = vg[:, row_idx]
    cols = jnp.arange(GRID_W)
    cs = jnp.clip(cols - NA_COLS // 2, 0, GRID_W - NA_COLS)
    col_valid = (cols[None, :] >= cs[:, None]) & (cols[None, :] < cs[:, None] + NA_COLS)
    dr = row_idx - r[:, None] + (NA_ROWS - 1)
    dc = jnp.clip(cols[None, :] - cols[:, None] + NA_COLS - 1, 0, 2 * NA_COLS - 2)
    bias = rpb[:, dr[:, None, :, None], dc[None, :, None, :]]
    bias = bias.astype(jnp.float32).transpose(1, 0, 2, 3, 4)[None]
    scale = d ** -0.5
    s_loc = jnp.einsum('brchd,brkwhd->brhckw', qg, k_rows).astype(jnp.float32) * scale + bias
    s_loc = jnp.where(col_valid[:, None, :], s_loc, NEG_INF).reshape(b, rows, h, GRID_W, kh * GRID_W)
    s_ctx = jnp.einsum('brchd,bshd->brhcs', qg, k_ctx).astype(jnp.float32) * scale
    p = jax.nn.softmax(jnp.concatenate([s_ctx, s_loc], axis=-1), axis=-1).astype(v.dtype)
    p_loc = p[..., nctx:].reshape(b, rows, h, GRID_W, kh, GRID_W)
    o = (jnp.einsum('brhcs,bshd->brchd', p[..., :nctx], v_ctx)
         + jnp.einsum('brhckw,brkwhd->brchd', p_loc, v_rows))
    return o.reshape(b, n, h, d)


def diff_attend(q, k, v, P, lam_init):
    b, nq, h, _ = q.shape
    f32 = jnp.float32
    lam = (jnp.exp(jnp.sum(P['diff_lq1'].astype(f32) * P['diff_lk1'].astype(f32)))
           - jnp.exp(jnp.sum(P['diff_lq2'].astype(f32) * P['diff_lk2'].astype(f32))) + lam_init)
    k1, k2 = k[..., :DIFF_QK], k[..., DIFF_QK:]
    scale = DIFF_QK ** -0.5
    qb = q.reshape(b, nq // QBLK, QBLK, h, 2 * DIFF_QK).swapaxes(0, 1)

    def block(qi):
        s1 = jnp.einsum('bqhd,bshd->bhqs', qi[..., :DIFF_QK], k1).astype(f32) * scale
        s2 = jnp.einsum('bqhd,bshd->bhqs', qi[..., DIFF_QK:], k2).astype(f32) * scale
        p = jax.nn.softmax(s1, axis=-1) - lam * jax.nn.softmax(s2, axis=-1)
        return jnp.einsum('bhqs,bshd->bqhd', p.astype(v.dtype), v)

    o = lax.map(block, qb).swapaxes(0, 1).reshape(b, nq, h, DIFF_V)
    return rmsnorm(o, P['diff_subln_g']) * (1.0 - lam_init)


def modulation(cvec, w_ada, b_ada):
    m = (jax.nn.silu(cvec) @ w_ada + b_ada)[:, None, :]
    return jnp.split(m, 6, axis=-1)


def project_mixers(h, P, latent):
    b, n = h.shape[0], h.shape[1]
    proj = h @ P['w_in']
    (q_down, ckv_raw, krope, qb, kb, vb, qc, kc, vc, qd, kd, vd) = jnp.split(proj, split_points(), axis=-1)
    cq = rmsnorm(q_down, P['mla_q_norm_g'])
    qa = (cq @ P['w_mla_uq']).reshape(b, n, MLA_HEADS, MLA_NOPE + MLA_ROPE)
    qa_nope, qa_rope = qa[..., :MLA_NOPE], qa[..., MLA_NOPE:]
    ckv = rmsnorm(ckv_raw, P['mla_kv_norm_g'])
    krope = krope.reshape(b, n, 1, MLA_ROPE)
    qb = qb.reshape(b, n, SWA_HEADS, HEAD_DIM)
    kb = kb.reshape(b, n, SWA_KV_HEADS, HEAD_DIM)
    vb = vb.reshape(b, n, SWA_KV_HEADS, HEAD_DIM)
    qc = qc.reshape(b, n, NA_HEADS, HEAD_DIM)
    kc = kc.reshape(b, n, NA_HEADS, HEAD_DIM)
    vc = vc.reshape(b, n, NA_HEADS, HEAD_DIM)
    qd = qd.reshape(b, n, DIFF_HEADS, 2 * DIFF_QK)
    kd = kd.reshape(b, n, DIFF_HEADS, 2 * DIFF_QK)
    vd = vd.reshape(b, n, DIFF_HEADS, DIFF_V)
    if latent:
        qa_rope = axial_rope(qa_rope)
        krope = axial_rope(krope)
        qb = axial_rope(qb)
        kb = axial_rope(kb)
        qd = axial_rope(qd.reshape(b, n, 2 * DIFF_HEADS, DIFF_QK)).reshape(b, n, DIFF_HEADS, 2 * DIFF_QK)
        kd = axial_rope(kd.reshape(b, n, 2 * DIFF_HEADS, DIFF_QK)).reshape(b, n, DIFF_HEADS, 2 * DIFF_QK)
    return (qa_nope, qa_rope, ckv, krope, qb, kb, vb, qc, kc, vc, qd, kd, vd)


def merge_branches(h, outs, P):
    b, n = h.shape[0], h.shape[1]
    mixed = None
    for i, o in enumerate(outs):
        gate = jax.nn.sigmoid(h @ P['w_mix_gate'][i])
        term = gate * (o.reshape(b, n, BRANCH_W) @ P['w_branch'][i])
        mixed = term if mixed is None else mixed + term
    return mixed @ P['w_out']


def trunk_layer(x, cvec, P, lam_init, ctx_cache):
    sh1, sc1, g1, sh2, sc2, g2 = modulation(cvec, P['w_ada'], P['b_ada'])
    h = rmsnorm(x, P['mix_pre_g']) * (1.0 + sc1) + sh1
    latent = ctx_cache is not None
    (qa_nope, qa_rope, ckv, krope, qb, kb, vb, qc, kc, vc, qd, kd, vd) = project_mixers(h, P, latent)
    own = None
    if latent:
        ckv_c, krope_c, kb_c, vb_c, kc_c, vc_c, kd_c, vd_c = ctx_cache
        o_a = mla_attend(qa_nope, qa_rope, jnp.concatenate([ckv_c, ckv], axis=1),
                         jnp.concatenate([krope_c[:, :, None, :], krope], axis=1), P['w_mla_ukv'])
        o_b = swa_latent(qb, kb, vb, kb_c, vb_c, P['swa_sink'])
        o_c = na_latent(qc, kc, vc, kc_c, vc_c, P['na_rpb'])
        o_d = diff_attend(qd, jnp.concatenate([kd_c, kd], axis=1),
                          jnp.concatenate([vd_c, vd], axis=1), P, lam_init)
    else:
        o_a = mla_attend(qa_nope, qa_rope, ckv, krope, P['w_mla_ukv'])
        o_b = attend_dense(qb, kb, vb, HEAD_DIM ** -0.5, sink=P['swa_sink'])
        o_c = attend_dense(qc, kc, vc, HEAD_DIM ** -0.5)
        o_d = diff_attend(qd, kd, vd, P, lam_init)
        own = (ckv, krope[:, :, 0, :], kb, vb, kc, vc, kd, vd)
    mixed = merge_branches(h, (o_a, o_b, o_c, o_d), P)
    x = x + g1 * rmsnorm(mixed, P['mix_post_g'])
    h2 = rmsnorm(x, P['ffn_pre_g']) * (1.0 + sc2) + sh2
    f = (jax.nn.silu(h2 @ P['w_ffn_gate']) * (h2 @ P['w_ffn_up'])) @ P['w_ffn_down']
    x = x + g2 * rmsnorm(f, P['ffn_post_g'])
    return x, own


def setup_inputs(seed: int = 0) -> dict:
    key = jax.random.key(seed)
    ks = jax.random.split(key, 40)
    f32 = jnp.float32

    def nrm(i, shape, scale=1.0):
        return jax.random.normal(ks[i], shape, f32) * scale

    def gain(i, shape):
        return 1.0 + 0.05 * jax.random.normal(ks[i], shape, f32)

    D = D_MODEL
    return {
        'x_prompt': nrm(0, (BATCH, SEQ, D)),
        'x_sample': nrm(1, (DEC_BATCH, DEC_SEQ, D)),
        'cache_mla_ckv': nrm(2, (DEC_BATCH, DEPTH, PAST_LEN, MLA_KV_RANK)),
        'cache_mla_krope': nrm(3, (DEC_BATCH, DEPTH, PAST_LEN, MLA_ROPE)),
        'cache_swa_k': nrm(4, (DEC_BATCH, DEPTH, PAST_LEN, SWA_KV_HEADS, HEAD_DIM)),
        'cache_swa_v': nrm(5, (DEC_BATCH, DEPTH, PAST_LEN, SWA_KV_HEADS, HEAD_DIM)),
        'cache_na_k': nrm(6, (DEC_BATCH, DEPTH, PAST_LEN, NA_HEADS, HEAD_DIM)),
        'cache_na_v': nrm(7, (DEC_BATCH, DEPTH, PAST_LEN, NA_HEADS, HEAD_DIM)),
        'cache_diff_k': nrm(8, (DEC_BATCH, DEPTH, PAST_LEN, DIFF_HEADS, 2 * DIFF_QK)),
        'cache_diff_v': nrm(9, (DEC_BATCH, DEPTH, PAST_LEN, DIFF_HEADS, DIFF_V)),
        'c': nrm(10, (DEC_BATCH, D)),
        'c_ctx': nrm(11, (D,)),
        'w_ada': nrm(12, (DEPTH, D, 6 * D), 0.5 * D ** -0.5),
        'b_ada': nrm(13, (DEPTH, 6 * D), 0.01),
        'mix_pre_g': gain(14, (DEPTH, D)),
        'mix_post_g': gain(15, (DEPTH, D)),
        'ffn_pre_g': gain(16, (DEPTH, D)),
        'ffn_post_g': gain(17, (DEPTH, D)),
        'w_in': nrm(18, (DEPTH, D, IN_WIDTH), D ** -0.5),
        'mla_q_norm_g': gain(19, (DEPTH, MLA_Q_RANK)),
        'mla_kv_norm_g': gain(20, (DEPTH, MLA_KV_RANK)),
        'w_mla_uq': nrm(21, (DEPTH, MLA_Q_RANK, MLA_HEADS * (MLA_NOPE + MLA_ROPE)), MLA_Q_RANK ** -0.5),
        'w_mla_ukv': nrm(22, (DEPTH, MLA_KV_RANK, MLA_HEADS * (MLA_NOPE + MLA_V)), MLA_KV_RANK ** -0.5),
        'swa_sink': nrm(23, (DEPTH, SWA_HEADS)),
        'na_rpb': nrm(24, (DEPTH, NA_HEADS, 2 * NA_ROWS - 1, 2 * NA_COLS - 1), 0.5),
        'diff_lq1': nrm(25, (DEPTH, DIFF_QK), 0.1),
        'diff_lk1': nrm(26, (DEPTH, DIFF_QK), 0.1),
        'diff_lq2': nrm(27, (DEPTH, DIFF_QK), 0.1),
        'diff_lk2': nrm(28, (DEPTH, DIFF_QK), 0.1),
        'diff_subln_g': gain(29, (DEPTH, DIFF_V)),
        'w_mix_gate': nrm(30, (DEPTH, N_BRANCH, D, D), D ** -0.5),
        'w_branch': nrm(31, (DEPTH, N_BRANCH, BRANCH_W, D), BRANCH_W ** -0.5),
        'w_out': nrm(32, (DEPTH, D, D), D ** -0.5),
        'w_ffn_gate': nrm(33, (DEPTH, D, FFN_HIDDEN), D ** -0.5),
        'w_ffn_up': nrm(34, (DEPTH, D, FFN_HIDDEN), D ** -0.5),
        'w_ffn_down': nrm(35, (DEPTH, FFN_HIDDEN, D), FFN_HIDDEN ** -0.5),
    }


def reference(x_prompt, x_sample, cache_mla_ckv, cache_mla_krope, cache_swa_k, cache_swa_v,
              cache_na_k, cache_na_v, cache_diff_k, cache_diff_v, c, c_ctx,
              w_ada, b_ada, mix_pre_g, mix_post_g, ffn_pre_g, ffn_post_g, w_in,
              mla_q_norm_g, mla_kv_norm_g, w_mla_uq, w_mla_ukv, swa_sink, na_rpb,
              diff_lq1, diff_lk1, diff_lq2, diff_lk2, diff_subln_g,
              w_mix_gate, w_branch, w_out, w_ffn_gate, w_ffn_up, w_ffn_down):
    def layer_params(l):
        return {
            'w_ada': w_ada[l], 'b_ada': b_ada[l],
            'mix_pre_g': mix_pre_g[l], 'mix_post_g': mix_post_g[l],
            'ffn_pre_g': ffn_pre_g[l], 'ffn_post_g': ffn_post_g[l],
            'w_in': w_in[l], 'mla_q_norm_g': mla_q_norm_g[l], 'mla_kv_norm_g': mla_kv_norm_g[l],
            'w_mla_uq': w_mla_uq[l], 'w_mla_ukv': w_mla_ukv[l],
            'swa_sink': swa_sink[l], 'na_rpb': na_rpb[l],
            'diff_lq1': diff_lq1[l], 'diff_lk1': diff_lk1[l],
            'diff_lq2': diff_lq2[l], 'diff_lk2': diff_lk2[l], 'diff_subln_g': diff_subln_g[l],
            'w_mix_gate': w_mix_gate[l], 'w_branch': w_branch[l], 'w_out': w_out[l],
            'w_ffn_gate': w_ffn_gate[l], 'w_ffn_up': w_ffn_up[l], 'w_ffn_down': w_ffn_down[l],
        }

    y = x_prompt
    c_ctx_row = c_ctx[None, :]
    per_layer = [[] for _ in range(8)]
    for l in range(DEPTH):
        y, own = trunk_layer(y, c_ctx_row, layer_params(l), lambda_init(l), None)
        for j in range(8):
            per_layer[j].append(own[j])
    new_mla_ckv = jnp.stack(per_layer[0], axis=1)
    new_mla_krope = jnp.stack(per_layer[1], axis=1)
    new_swa_k = jnp.stack(per_layer[2], axis=1)
    new_swa_v = jnp.stack(per_layer[3], axis=1)
    new_na_k = jnp.stack(per_layer[4], axis=1)
    new_na_v = jnp.stack(per_layer[5], axis=1)
    new_diff_k = jnp.stack(per_layer[6], axis=1)
    new_diff_v = jnp.stack(per_layer[7], axis=1)

    z = x_sample
    for l in range(DEPTH):
        cache_l = (cache_mla_ckv[:, l], cache_mla_krope[:, l], cache_swa_k[:, l], cache_swa_v[:, l],
                   cache_na_k[:, l], cache_na_v[:, l], cache_diff_k[:, l], cache_diff_v[:, l])
        z, _ = trunk_layer(z, c, layer_params(l), lambda_init(l), cache_l)

    return (y, z, new_mla_ckv, new_mla_krope, new_swa_k, new_swa_v, new_na_k, new_na_v, new_diff_k, new_diff_v)
```

```python
import functools
import math

import numpy as np
import jax
import jax.numpy as jnp
from jax import lax
from jax.experimental import pallas as pl
from jax.experimental.pallas import tpu as pltpu

D_MODEL = 2048
BATCH = 32
SEQ = 256
DEPTH = 2
DEC_BATCH = 8
DEC_SEQ = 1024
PAST_LEN = 512
GRID_W = 64
HEAD_DIM = 128
ROPE_BASE = 10000.0
EPS = 1e-6
NEG_INF = -1e30

MLA_HEADS = 4
MLA_Q_RANK = 384
MLA_KV_RANK = 128
MLA_NOPE = 128
MLA_ROPE = 64
MLA_V = 128
SWA_HEADS = 4
SWA_KV_HEADS = 2
SWA_WINDOW = 128
NA_HEADS = 4
NA_ROWS = 8
NA_COLS = 16
DIFF_HEADS = 4
DIFF_QK = 64
DIFF_V = 128
N_BRANCH = 4
BRANCH_W = 512
FFN_HIDDEN = 5632

N_CTX_TOK = BATCH * SEQ
N_LAT_TOK = DEC_BATCH * DEC_SEQ
GRID_ROWS = DEC_SEQ // GRID_W

Q_DOWN0, QB0, QC0, QD0 = 0, 384, 896, 1408
Q_WIDTH = 1920
CKV0, KB0, VB0, KC0, VC0, KD0, VD0 = 1920, 2048, 2304, 2560, 3072, 3584, 4096
P_WIDTH = 4608
N_MOD_ROWS = 16

VMEM_LIMIT = 56 * 1024 * 1024
LANES = 128

F32 = jnp.float32
BF16 = jnp.bfloat16


def _lambda_init(l):
    return 0.8 - 0.6 * math.exp(-0.3 * l)


def _rms(x):
    return x * lax.rsqrt(jnp.mean(x * x, axis=-1, keepdims=True) + EPS)


def _dot(a, b):
    return jnp.dot(a, b, preferred_element_type=F32)


def _dot_nt(a, b):
    return lax.dot_general(a, b, (((1,), (1,)), ((), ())), preferred_element_type=F32)


def _attend(pieces, sink=None):
    m = None
    for s, _ in pieces:
        mi = jnp.max(s, axis=-1, keepdims=True)
        m = mi if m is None else jnp.maximum(m, mi)
    if sink is not None:
        m = jnp.maximum(m, sink)
    den = None
    num = None
    for s, v in pieces:
        e = jnp.exp(s - m)
        d = jnp.sum(e, axis=-1, keepdims=True)
        n = _dot(e.astype(BF16), v)
        den = d if den is None else den + d
        num = n if num is None else num + n
    if sink is not None:
        den = den + jnp.exp(sink - m)
    return num / den


def _diff_lambda(lq1, lk1, lq2, lk2, lam_init):
    a = jnp.sum(lq1[...] * lk1[...], axis=-1, keepdims=True)
    b = jnp.sum(lq2[...] * lk2[...], axis=-1, keepdims=True)
    return jnp.exp(a) - jnp.exp(b) + lam_init


def _ada_kernel(c_ref, w_ref, b_ref, o_ref):
    c = c_ref[...]
    s = (c * jax.nn.sigmoid(c)).astype(BF16)
    o_ref[0] = _dot(s, w_ref[0].astype(BF16)) + b_ref[0]


def _ada_call(cvec, w_ada, b_ada):
    tn = 1024
    n = 6 * D_MODEL
    return pl.pallas_call(
        _ada_kernel,
        grid=(DEPTH, n // tn),
        in_specs=[
            pl.BlockSpec((N_MOD_ROWS, D_MODEL), lambda l, j: (0, 0)),
            pl.BlockSpec((1, D_MODEL, tn), lambda l, j: (l, 0, j)),
            pl.BlockSpec((1, 1, tn), lambda l, j: (l, 0, j)),
        ],
        out_specs=pl.BlockSpec((1, N_MOD_ROWS, tn), lambda l, j: (l, 0, j)),
        out_shape=jax.ShapeDtypeStruct((DEPTH, N_MOD_ROWS, n), F32),
        compiler_params=pltpu.CompilerParams(
            dimension_semantics=("arbitrary", "arbitrary"), vmem_limit_bytes=VMEM_LIMIT),
        name="ada_mod",
    )(cvec, w_ada, b_ada.reshape(DEPTH, 1, n))


def _nbr_bias_kernel(rpb_ref, o_ref):
    l = pl.program_id(0)
    h = pl.program_id(1)
    n_dr = 2 * NA_ROWS - 1
    n_dc = 2 * NA_COLS - 1
    base = (l * NA_HEADS + h) * (n_dr * n_dc)
    wq = lax.broadcasted_iota(jnp.int32, (GRID_W, LANES), 0)
    lane = lax.broadcasted_iota(jnp.int32, (GRID_W, LANES), 1)
    upper = lane >= GRID_W
    wk = jnp.where(upper, lane - GRID_W, lane)
    delta = wk - wq + (NA_COLS - 1)
    cs = jnp.clip(wq - NA_COLS // 2, 0, GRID_W - NA_COLS)
    valid = (wk >= cs) & (wk < cs + NA_COLS)
    for dr in range(n_dr - 1):
        acc = jnp.full((GRID_W, LANES), NEG_INF, F32)
        for d in range(n_dc):
            lo = rpb_ref[base + dr * n_dc + d]
            hi = rpb_ref[base + (dr + 1) * n_dc + d]
            acc = jnp.where(delta == d, jnp.where(upper, hi, lo), acc)
        o_ref[0, 0, dr] = jnp.where(valid, acc, NEG_INF)


def _nbr_bias_call(na_rpb):
    n_pairs = 2 * NA_ROWS - 2
    return pl.pallas_call(
        _nbr_bias_kernel,
        grid=(DEPTH, NA_HEADS),
        in_specs=[pl.BlockSpec(memory_space=pltpu.SMEM)],
        out_specs=pl.BlockSpec((1, 1, n_pairs, GRID_W, LANES), lambda l, h: (l, h, 0, 0, 0)),
        out_shape=jax.ShapeDtypeStruct((DEPTH, NA_HEADS, n_pairs, GRID_W, LANES), F32),
        compiler_params=pltpu.CompilerParams(dimension_semantics=("arbitrary", "arbitrary")),
        name="nbr_bias",
    )(na_rpb.reshape(-1))


def _inproj_kernel(x_ref, mod_ref, g_ref, w_ref, wkr_ref, h_ref, p_ref, kr_ref):
    @pl.when(pl.program_id(1) == 0)
    def _():
        m = mod_ref[0]
        h = _rms(x_ref[...]) * g_ref[...] * (1.0 + m[1:2]) + m[0:1]
        hb = h.astype(BF16)
        h_ref[...] = hb
        kr_ref[...] = _dot(hb, wkr_ref[...])

    p_ref[...] = _dot(h_ref[...], w_ref[...]).astype(p_ref.dtype)


def _inproj_call(x, mod, g, w_main, w_kr, seg, p_dtype, name):
    n_tok = x.shape[0]
    tm, tn = 1024, 384
    return pl.pallas_call(
        _inproj_kernel,
        grid=(n_tok // tm, P_WIDTH // tn),
        in_specs=[
            pl.BlockSpec((tm, D_MODEL), lambda i, j: (i, 0)),
            pl.BlockSpec((1, 6, D_MODEL), lambda i, j: (seg(i, tm), 0, 0)),
            pl.BlockSpec((1, D_MODEL), lambda i, j: (0, 0)),
            pl.BlockSpec((D_MODEL, tn), lambda i, j: (0, j)),
            pl.BlockSpec((D_MODEL, LANES), lambda i, j: (0, 0)),
        ],
        out_specs=[
            pl.BlockSpec((tm, D_MODEL), lambda i, j: (i, 0)),
            pl.BlockSpec((tm, tn), lambda i, j: (i, j)),
            pl.BlockSpec((tm, LANES), lambda i, j: (i, 0)),
        ],
        out_shape=[
            jax.ShapeDtypeStruct((n_tok, D_MODEL), BF16),
            jax.ShapeDtypeStruct((n_tok, P_WIDTH), p_dtype),
            jax.ShapeDtypeStruct((n_tok, LANES), F32),
        ],
        compiler_params=pltpu.CompilerParams(
            dimension_semantics=("arbitrary", "arbitrary"), vmem_limit_bytes=VMEM_LIMIT),
        name=name,
    )(x, mod, g, w_main, w_kr)


def _mla_queries(q_down, qn_g, w_uq):
    cq = _rms(q_down) * qn_g[...]
    return _dot(cq.astype(BF16), w_uq[...])


def _ctx_attn_kernel(p_ref, kr_ref, qn_g, w_uq, kvn_g, w_ukv, sink_ref, lq1, lk1, lq2, lk2, subln_g,
                     o_ref, ckv_o, kr_o, kb_o, vb_o, kc_o, vc_o, kd_o, vd_o, *, lam_init):
    qa = _mla_queries(p_ref[:, Q_DOWN0:Q_DOWN0 + MLA_Q_RANK], qn_g, w_uq)
    ckv = _rms(p_ref[:, CKV0:CKV0 + MLA_KV_RANK]) * kvn_g[...]
    ckv_o[...] = ckv
    kr = kr_ref[...]
    kr_o[...] = kr[:, :MLA_ROPE]
    kr_b = kr.astype(BF16)
    kv = _dot(ckv.astype(BF16), w_ukv[...])
    scale_a = (MLA_NOPE + MLA_ROPE) ** -0.5
    for h in range(MLA_HEADS):
        q = jnp.concatenate([qa[:, h * 128:(h + 1) * 128], qa[:, 512 + h * 128:512 + (h + 1) * 128]],
                            axis=1).astype(BF16)
        k = jnp.concatenate([kv[:, h * 256:h * 256 + 128].astype(BF16), kr_b], axis=1)
        v = kv[:, h * 256 + 128:(h + 1) * 256].astype(BF16)
        o = _attend([(_dot_nt(q, k) * scale_a, v)])
        o_ref[:, h * 128:(h + 1) * 128] = o.astype(BF16)

    scale = HEAD_DIM ** -0.5
    kb = p_ref[:, KB0:KB0 + 256]
    vb = p_ref[:, VB0:VB0 + 256]
    kb_o[...] = kb
    vb_o[...] = vb
    g_size = SWA_HEADS // SWA_KV_HEADS
    for h in range(SWA_HEADS):
        g = h // g_size
        q = p_ref[:, QB0 + h * 128:QB0 + (h + 1) * 128].astype(BF16)
        k = kb[:, g * 128:(g + 1) * 128].astype(BF16)
        v = vb[:, g * 128:(g + 1) * 128].astype(BF16)
        o = _attend([(_dot_nt(q, k) * scale, v)], sink=sink_ref[h])
        o_ref[:, 512 + h * 128:512 + (h + 1) * 128] = o.astype(BF16)

    kc = p_ref[:, KC0:KC0 + 512]
    vc = p_ref[:, VC0:VC0 + 512]
    kc_o[...] = kc
    vc_o[...] = vc
    for h in range(NA_HEADS):
        q = p_ref[:, QC0 + h * 128:QC0 + (h + 1) * 128].astype(BF16)
        k = kc[:, h * 128:(h + 1) * 128].astype(BF16)
        v = vc[:, h * 128:(h + 1) * 128].astype(BF16)
        o = _attend([(_dot_nt(q, k) * scale, v)])
        o_ref[:, 1024 + h * 128:1024 + (h + 1) * 128] = o.astype(BF16)

    kd = p_ref[:, KD0:KD0 + 512]
    vd = p_ref[:, VD0:VD0 + 512]
    kd_o[...] = kd
    vd_o[...] = vd
    lam = _diff_lambda(lq1, lk1, lq2, lk2, lam_init)
    scale_d = DIFF_QK ** -0.5
    lo = lax.broadcasted_iota(jnp.int32, (SEQ, 128), 1) < DIFF_QK
    for h in range(DIFF_HEADS):
        q = p_ref[:, QD0 + h * 128:QD0 + (h + 1) * 128]
        q1 = jnp.where(lo, q, 0.0).astype(BF16)
        q2 = jnp.where(lo, 0.0, q).astype(BF16)
        k = kd[:, h * 128:(h + 1) * 128].astype(BF16)
        v = vd[:, h * 128:(h + 1) * 128].astype(BF16)
        o1 = _attend([(_dot_nt(q1, k) * scale_d, v)])
        o2 = _attend([(_dot_nt(q2, k) * scale_d, v)])
        o = _rms(o1 - lam * o2) * subln_g[...] * (1.0 - lam_init)
        o_ref[:, 1536 + h * 128:1536 + (h + 1) * 128] = o.astype(BF16)


def _small_specs(n_grid_axes):
    zero2 = (lambda *a: (0, 0))
    del n_grid_axes
    return [
        pl.BlockSpec((1, MLA_Q_RANK), zero2),
        pl.BlockSpec((MLA_Q_RANK, 1024), zero2),
        pl.BlockSpec((1, MLA_KV_RANK), zero2),
        pl.BlockSpec((MLA_KV_RANK, 1024), zero2),
        pl.BlockSpec(memory_space=pltpu.SMEM),
        pl.BlockSpec((1, DIFF_QK), zero2),
        pl.BlockSpec((1, DIFF_QK), zero2),
        pl.BlockSpec((1, DIFF_QK), zero2),
        pl.BlockSpec((1, DIFF_QK), zero2),
        pl.BlockSpec((1, DIFF_V), zero2),
    ]


def _ctx_attn_call(p, kr, small, lam_init):
    row = lambda b: (b, 0)
    widths = (MLA_KV_RANK, MLA_ROPE, 256, 256, 512, 512, 512, 512)
    return pl.pallas_call(
        functools.partial(_ctx_attn_kernel, lam_init=lam_init),
        grid=(BATCH,),
        in_specs=[pl.BlockSpec((SEQ, P_WIDTH), row), pl.BlockSpec((SEQ, LANES), row)] + _small_specs(1),
        out_specs=[pl.BlockSpec((SEQ, D_MODEL), row)] + [pl.BlockSpec((SEQ, w), row) for w in widths],
        out_shape=[jax.ShapeDtypeStruct((N_CTX_TOK, D_MODEL), BF16)]
        + [jax.ShapeDtypeStruct((N_CTX_TOK, w), F32) for w in widths],
        compiler_params=pltpu.CompilerParams(
            dimension_semantics=("arbitrary",), vmem_limit_bytes=VMEM_LIMIT),
        name="ctx_attn",
    )(p, kr, *small)


LAT_TQ = 256
SWA_PAD = SWA_WINDOW


def _rope_pair64(x, c_ref, sa_ref, sb_ref, rows):
    out = []
    for j in range(x.shape[1] // LANES):
        xj = x[:, j * LANES:(j + 1) * LANES]
        out.append(xj * c_ref[rows, :] + pltpu.roll(xj, 96, 1) * sa_ref[rows, :]
                   + pltpu.roll(xj, 32, 1) * sb_ref[rows, :])
    return jnp.concatenate(out, axis=1) if len(out) > 1 else out[0]


def _rope128(x, c_ref, s_ref, rows):
    out = []
    for j in range(x.shape[1] // LANES):
        xj = x[:, j * LANES:(j + 1) * LANES]
        out.append(xj * c_ref[rows, :] + pltpu.roll(xj, 64, 1) * s_ref[rows, :])
    return jnp.concatenate(out, axis=1) if len(out) > 1 else out[0]


def _lat_attn_kernel(p_ref, kr_ref, c_ckv, c_kr, c_kb, c_vb, c_kc, c_vc, c_kd, c_vd,
                     cos128, sin128, cos64, sina64, sinb64, bias_ref,
                     qn_g, w_uq, kvn_g, w_ukv, sink_ref, lq1, lk1, lq2, lk2, subln_g,
                     o_ref,
                     ka_s, va_s, kbc_s, vbc_s, kbp_s, vbp_s, kcc_s, vcc_s, kdc_s, vdc_s, kdo_s,
                     *, lam_init):
    qi = pl.program_id(1)
    all_rows = slice(None)
    n_own = DEC_SEQ

    @pl.when(qi == 0)
    def _prologue():
        ckv_own = (_rms(p_ref[:, CKV0:CKV0 + MLA_KV_RANK].astype(F32)) * kvn_g[...]).astype(BF16)
        ckv_all = jnp.concatenate([c_ckv[0, 0].astype(BF16), ckv_own], axis=0)
        kr_ctx = jnp.concatenate([c_kr[0, 0], jnp.zeros((PAST_LEN, LANES - MLA_ROPE), F32)], axis=1)
        kr_own = _rope_pair64(kr_ref[...], cos64, sina64, sinb64, all_rows)
        kr_all = jnp.concatenate([kr_ctx, kr_own], axis=0).astype(BF16)
        for h in range(MLA_HEADS):
            kvh = _dot(ckv_all, w_ukv[:, h * 256:(h + 1) * 256])
            ka_s[h, :, 0:128] = kvh[:, 0:128].astype(BF16)
            ka_s[h, :, 128:256] = kr_all
            va_s[:, h * 128:(h + 1) * 128] = kvh[:, 128:256].astype(BF16)
        kbc_s[...] = c_kb[0, 0].astype(BF16)
        vbc_s[...] = c_vb[0, 0].astype(BF16)
        zpad = jnp.zeros((SWA_PAD, 256), BF16)
        kbp_s[0:SWA_PAD, :] = zpad
        kbp_s[SWA_PAD + n_own:, :] = zpad
        vbp_s[0:SWA_PAD, :] = zpad
        vbp_s[SWA_PAD + n_own:, :] = zpad
        kbp_s[SWA_PAD:SWA_PAD + n_own, :] = _rope128(
            p_ref[:, KB0:KB0 + 256].astype(F32), cos128, sin128, all_rows).astype(BF16)
        vbp_s[SWA_PAD:SWA_PAD + n_own, :] = p_ref[:, VB0:VB0 + 256]
        kcc_s[...] = c_kc[0, 0].astype(BF16)
        vcc_s[...] = c_vc[0, 0].astype(BF16)
        kdc_s[...] = c_kd[0, 0].astype(BF16)
        vdc_s[...] = c_vd[0, 0].astype(BF16)
        kdo_s[...] = _rope_pair64(
            p_ref[:, KD0:KD0 + 512].astype(F32), cos64, sina64, sinb64, all_rows).astype(BF16)

    q0 = pl.multiple_of(qi * LAT_TQ, LAT_TQ)
    rows = pl.ds(q0, LAT_TQ)

    qa = _mla_queries(p_ref[rows, Q_DOWN0:Q_DOWN0 + MLA_Q_RANK].astype(F32), qn_g, w_uq)
    qa_r = _rope_pair64(qa[:, 512:1024], cos64, sina64, sinb64, rows)
    scale_a = (MLA_NOPE + MLA_ROPE) ** -0.5
    for h in range(MLA_HEADS):
        q = jnp.concatenate([qa[:, h * 128:(h + 1) * 128], qa_r[:, h * 128:(h + 1) * 128]],
                            axis=1).astype(BF16)
        o = _attend([(_dot_nt(q, ka_s[h]) * scale_a, va_s[:, h * 128:(h + 1) * 128])])
        o_ref[:, h * 128:(h + 1) * 128] = o.astype(BF16)

    scale = HEAD_DIM ** -0.5
    qb = _rope128(p_ref[rows, QB0:QB0 + 512].astype(F32), cos128, sin128, rows).astype(BF16)
    n_loc = LAT_TQ + 2 * SWA_PAD
    qpos = lax.broadcasted_iota(jnp.int32, (LAT_TQ, n_loc), 0)
    kpos = lax.broadcasted_iota(jnp.int32, (LAT_TQ, n_loc), 1) - SWA_PAD
    kabs = kpos + q0
    valid = (jnp.abs(qpos - kpos) <= SWA_WINDOW) & (kabs >= 0) & (kabs < n_own)
    loc_rows = pl.ds(q0, n_loc)
    g_size = SWA_HEADS // SWA_KV_HEADS
    for h in range(SWA_HEADS):
        g = h // g_size
        q = qb[:, h * 128:(h + 1) * 128]
        s_ctx = _dot_nt(q, kbc_s[:, g * 128:(g + 1) * 128]) * scale
        s_loc = jnp.where(valid, _dot_nt(q, kbp_s[loc_rows, g * 128:(g + 1) * 128]) * scale, NEG_INF)
        o = _attend([(s_ctx, vbc_s[:, g * 128:(g + 1) * 128]),
                     (s_loc, vbp_s[loc_rows, g * 128:(g + 1) * 128])], sink=sink_ref[h])
        o_ref[:, 512 + h * 128:512 + (h + 1) * 128] = o.astype(BF16)

    n_win = NA_ROWS * GRID_W
    for t in range(LAT_TQ // GRID_W):
        r = qi * (LAT_TQ // GRID_W) + t
        rs = jnp.clip(r - NA_ROWS // 2, 0, GRID_ROWS - NA_ROWS)
        off = rs - r + (NA_ROWS - 1)
        krows = pl.ds(pl.multiple_of(rs * GRID_W, GRID_W), n_win)
        qrows = pl.ds(pl.multiple_of(q0 + t * GRID_W, GRID_W), GRID_W)
        for h in range(NA_HEADS):
            q = p_ref[qrows, QC0 + h * 128:QC0 + (h + 1) * 128]
            bias = jnp.concatenate([bias_ref[0, h, off + 2 * j] for j in range(NA_ROWS // 2)], axis=1)
            s_loc = _dot_nt(q, p_ref[krows, KC0 + h * 128:KC0 + (h + 1) * 128]) * scale + bias
            s_ctx = _dot_nt(q, kcc_s[:, h * 128:(h + 1) * 128]) * scale
            o = _attend([(s_ctx, vcc_s[:, h * 128:(h + 1) * 128]),
                         (s_loc, p_ref[krows, VC0 + h * 128:VC0 + (h + 1) * 128])])
            o_ref[t * GRID_W:(t + 1) * GRID_W, 1024 + h * 128:1024 + (h + 1) * 128] = o.astype(BF16)

    qd = _rope_pair64(p_ref[rows, QD0:QD0 + 512].astype(F32), cos64, sina64, sinb64, rows)
    lam = _diff_lambda(lq1, lk1, lq2, lk2, lam_init)
    scale_d = DIFF_QK ** -0.5
    lo = lax.broadcasted_iota(jnp.int32, (LAT_TQ, 128), 1) < DIFF_QK
    for h in range(DIFF_HEADS):
        cols = slice(h * 128, (h + 1) * 128)
        q = qd[:, cols]
        v_own = p_ref[:, VD0 + h * 128:VD0 + (h + 1) * 128]
        outs = []
        for qq in (jnp.where(lo, q, 0.0).astype(BF16), jnp.where(lo, 0.0, q).astype(BF16)):
            outs.append(_attend([(_dot_nt(qq, kdc_s[:, cols]) * scale_d, vdc_s[:, cols]),
                                 (_dot_nt(qq, kdo_s[:, cols]) * scale_d, v_own)]))
        o = _rms(outs[0] - lam * outs[1]) * subln_g[...] * (1.0 - lam_init)
        o_ref[:, 1536 + h * 128:1536 + (h + 1) * 128] = o.astype(BF16)


def _lat_attn_call(p, kr, caches, tables, bias, small, layer, lam_init):
    nq = DEC_SEQ // LAT_TQ
    batch_rows = lambda b, q: (b, 0)
    zero2 = lambda b, q: (0, 0)
    once = pl.Buffered(1)
    cache_specs = [pl.BlockSpec((1, 1, PAST_LEN, c.shape[-1]), lambda b, q: (b, layer, 0, 0), pipeline_mode=once)
                   for c in caches]
    table_specs = [pl.BlockSpec((DEC_SEQ, LANES), zero2) for _ in tables]
    n_pairs = 2 * NA_ROWS - 2
    bias_spec = pl.BlockSpec((1, NA_HEADS, n_pairs, GRID_W, LANES), lambda b, q: (layer, 0, 0, 0, 0))
    n_all = PAST_LEN + DEC_SEQ
    n_pad = DEC_SEQ + 2 * SWA_PAD
    scratch = [
        pltpu.VMEM((MLA_HEADS, n_all, 256), BF16),
        pltpu.VMEM((n_all, 512), BF16),
        pltpu.VMEM((PAST_LEN, 256), BF16), pltpu.VMEM((PAST_LEN, 256), BF16),
        pltpu.VMEM((n_pad, 256), BF16), pltpu.VMEM((n_pad, 256), BF16),
        pltpu.VMEM((PAST_LEN, 512), BF16), pltpu.VMEM((PAST_LEN, 512), BF16),
        pltpu.VMEM((PAST_LEN, 512), BF16), pltpu.VMEM((PAST_LEN, 512), BF16),
        pltpu.VMEM((DEC_SEQ, 512), BF16),
    ]
    return pl.pallas_call(
        functools.partial(_lat_attn_kernel, lam_init=lam_init),
        grid=(DEC_BATCH, nq),
        in_specs=[pl.BlockSpec((DEC_SEQ, P_WIDTH), batch_rows, pipeline_mode=once),
                  pl.BlockSpec((DEC_SEQ, LANES), batch_rows, pipeline_mode=once)]
        + cache_specs + table_specs + [bias_spec] + _small_specs(2),
        out_specs=pl.BlockSpec((LAT_TQ, D_MODEL), lambda b, q: (b * nq + q, 0)),
        out_shape=jax.ShapeDtypeStruct((N_LAT_TOK, D_MODEL), BF16),
        scratch_shapes=scratch,
        compiler_params=pltpu.CompilerParams(
            dimension_semantics=("arbitrary", "arbitrary"), vmem_limit_bytes=60 * 1024 * 1024),
        name="lat_attn",
    )(p, kr, *caches, *tables, bias, *small)


def _merge_kernel(h_ref, o_ref, wg_ref, wb_ref, out_ref):
    acc = None
    for i in range(N_BRANCH):
        gate = jax.nn.sigmoid(_dot(h_ref[...], wg_ref[i]))
        term = gate * _dot(o_ref[:, i * BRANCH_W:(i + 1) * BRANCH_W], wb_ref[i])
        acc = term if acc is None else acc + term
    out_ref[...] = acc.astype(BF16)


def _merge_call(h, o, wg, wb, name):
    n_tok = h.shape[0]
    tm, tn = 1024, 512
    return pl.pallas_call(
        _merge_kernel,
        grid=(n_tok // tm, D_MODEL // tn),
        in_specs=[
            pl.BlockSpec((tm, D_MODEL), lambda i, j: (i, 0)),
            pl.BlockSpec((tm, D_MODEL), lambda i, j: (i, 0)),
            pl.BlockSpec((N_BRANCH, D_MODEL, tn), lambda i, j: (0, 0, j)),
            pl.BlockSpec((N_BRANCH, BRANCH_W, tn), lambda i, j: (0, 0, j)),
        ],
        out_specs=pl.BlockSpec((tm, tn), lambda i, j: (i, j)),
        out_shape=jax.ShapeDtypeStruct((n_tok, D_MODEL), BF16),
        compiler_params=pltpu.CompilerParams(
            dimension_semantics=("arbitrary", "arbitrary"), vmem_limit_bytes=VMEM_LIMIT),
        name=name,
    )(h, o, wg, wb)


def _outproj_kernel(mixed_ref, x_ref, mod_ref, g_ref, w_ref, xo_ref):
    out = _dot(mixed_ref[...], w_ref[...])
    m = mod_ref[0]
    xo_ref[...] = x_ref[...] + m[2:3] * (_rms(out) * g_ref[...])


def _outproj_call(mixed, x, mod, g, w, seg, name):
    n_tok = x.shape[0]
    tm = 512
    return pl.pallas_call(
        _outproj_kernel,
        grid=(n_tok // tm,),
        in_specs=[
            pl.BlockSpec((tm, D_MODEL), lambda i: (i, 0)),
            pl.BlockSpec((tm, D_MODEL), lambda i: (i, 0)),
            pl.BlockSpec((1, 6, D_MODEL), lambda i: (seg(i, tm), 0, 0)),
            pl.BlockSpec((1, D_MODEL), lambda i: (0, 0)),
            pl.BlockSpec((D_MODEL, D_MODEL), lambda i: (0, 0)),
        ],
        out_specs=pl.BlockSpec((tm, D_MODEL), lambda i: (i, 0)),
        out_shape=jax.ShapeDtypeStruct((n_tok, D_MODEL), F32),
        compiler_params=pltpu.CompilerParams(
            dimension_semantics=("arbitrary",), vmem_limit_bytes=VMEM_LIMIT),
        name=name,
    )(mixed, x, mod, g, w)


def _ffn_kernel(x_ref, mod_ref, gpre_ref, gpost_ref, wg_ref, wu_ref, wd_ref, xo_ref, h2_s, acc_s):
    j = pl.program_id(1)

    @pl.when(j == 0)
    def _():
        m = mod_ref[0]
        h2 = _rms(x_ref[...]) * gpre_ref[...] * (1.0 + m[4:5]) + m[3:4]
        h2_s[...] = h2.astype(BF16)
        acc_s[...] = jnp.zeros_like(acc_s)

    a = _dot(h2_s[...], wg_ref[...])
    b = _dot(h2_s[...], wu_ref[...])
    t = (a * jax.nn.sigmoid(a) * b).astype(BF16)
    acc_s[...] += _dot(t, wd_ref[...])

    @pl.when(j == pl.num_programs(1) - 1)
    def _():
        m = mod_ref[0]
        xo_ref[...] = x_ref[...] + m[5:6] * (_rms(acc_s[...]) * gpost_ref[...])


def _ffn_call(x, mod, gpre, gpost, wg, wu, wd, seg, name):
    n_tok = x.shape[0]
    tm, th = 512, 512
    return pl.pallas_call(
        _ffn_kernel,
        grid=(n_tok // tm, FFN_HIDDEN // th),
        in_specs=[
            pl.BlockSpec((tm, D_MODEL), lambda i, j: (i, 0)),
            pl.BlockSpec((1, 6, D_MODEL), lambda i, j: (seg(i, tm), 0, 0)),
            pl.BlockSpec((1, D_MODEL), lambda i, j: (0, 0)),
            pl.BlockSpec((1, D_MODEL), lambda i, j: (0, 0)),
            pl.BlockSpec((D_MODEL, th), lambda i, j: (0, j)),
            pl.BlockSpec((D_MODEL, th), lambda i, j: (0, j)),
            pl.BlockSpec((th, D_MODEL), lambda i, j: (j, 0)),
        ],
        out_specs=pl.BlockSpec((tm, D_MODEL), lambda i, j: (i, 0)),
        out_shape=jax.ShapeDtypeStruct((n_tok, D_MODEL), F32),
        scratch_shapes=[pltpu.VMEM((tm, D_MODEL), BF16), pltpu.VMEM((tm, D_MODEL), F32)],
        compiler_params=pltpu.CompilerParams(
            dimension_semantics=("arbitrary", "arbitrary"), vmem_limit_bytes=VMEM_LIMIT),
        name=name,
    )(x, mod, gpre, gpost, wg, wu, wd)


def _rope_tables():
    t = np.arange(DEC_SEQ)
    row = (t // GRID_W).astype(np.float32)
    col = (t % GRID_W).astype(np.float32)

    def half_angles(r):
        quarter = r // 4
        inv = (1.0 / (ROPE_BASE ** (np.arange(quarter, dtype=np.float32) / quarter))).astype(np.float32)
        return np.concatenate([row[:, None] * inv, col[:, None] * inv], axis=-1).astype(np.float32)

    a64 = half_angles(128)
    cos128 = np.concatenate([np.cos(a64), np.cos(a64)], axis=1)
    sin128 = np.concatenate([-np.sin(a64), np.sin(a64)], axis=1)
    a32 = half_angles(64)
    c, s, z = np.cos(a32), np.sin(a32), np.zeros_like(a32)
    cos64 = np.concatenate([c, c, c, c], axis=1)
    sina64 = np.concatenate([-s, z, -s, z], axis=1)
    sinb64 = np.concatenate([z, s, z, s], axis=1)
    return tuple(jnp.asarray(a, F32) for a in (cos128, sin128, cos64, sina64, sinb64))


def _ctx_seg(i, tm):
    del tm
    return i * 0


def _lat_seg(i, tm):
    return 1 + (i * tm) // DEC_SEQ


def kernel(x_prompt, x_sample, cache_mla_ckv, cache_mla_krope, cache_swa_k, cache_swa_v, cache_na_k, cache_na_v, cache_diff_k, cache_diff_v, c, c_ctx, w_ada, b_ada, mix_pre_g, mix_post_g, ffn_pre_g, ffn_post_g, w_in, mla_q_norm_g, mla_kv_norm_g, w_mla_uq, w_mla_ukv, swa_sink, na_rpb, diff_lq1, diff_lk1, diff_lq2, diff_lk2, diff_subln_g, w_mix_gate, w_branch, w_out, w_ffn_gate, w_ffn_up, w_ffn_down):
    cvec = jnp.concatenate([c_ctx[None, :], c, jnp.zeros((N_MOD_ROWS - 1 - DEC_BATCH, D_MODEL), F32)], axis=0)
    mod_all = _ada_call(cvec, w_ada, b_ada).reshape(DEPTH, N_MOD_ROWS, 6, D_MODEL)
    bias_all = _nbr_bias_call(na_rpb)
    tables = _rope_tables()
    caches = (cache_mla_ckv, cache_mla_krope,
              cache_swa_k.reshape(DEC_BATCH, DEPTH, PAST_LEN, 256), cache_swa_v.reshape(DEC_BATCH, DEPTH, PAST_LEN, 256),
              cache_na_k.reshape(DEC_BATCH, DEPTH, PAST_LEN, 512), cache_na_v.reshape(DEC_BATCH, DEPTH, PAST_LEN, 512),
              cache_diff_k.reshape(DEC_BATCH, DEPTH, PAST_LEN, 512), cache_diff_v.reshape(DEC_BATCH, DEPTH, PAST_LEN, 512))

    y = x_prompt.reshape(N_CTX_TOK, D_MODEL)
    z = x_sample.reshape(N_LAT_TOK, D_MODEL)
    new_caches = [[] for _ in range(8)]
    for l in range(DEPTH):
        lam_init = _lambda_init(l)
        mod = mod_all[l]
        wi = w_in[l]
        sp = np.cumsum((0, 384, 128, 64, 512, 256, 256, 512, 512, 512, 512, 512, 512))
        piece = lambda k: wi[:, sp[k]:sp[k + 1]]
        w_main = jnp.concatenate([piece(k) for k in (0, 3, 6, 9, 1, 4, 5, 7, 8, 10, 11)], axis=1).astype(BF16)
        w_kr = jnp.concatenate([piece(2), jnp.zeros((D_MODEL, LANES - MLA_ROPE), F32)], axis=1).astype(BF16)
        uq = w_mla_uq[l].reshape(MLA_Q_RANK, MLA_HEADS, MLA_NOPE + MLA_ROPE)
        uq_rope = jnp.pad(uq[:, :, MLA_NOPE:], ((0, 0), (0, 0), (0, LANES - MLA_ROPE)))
        w_uq = jnp.concatenate([uq[:, :, :MLA_NOPE].reshape(MLA_Q_RANK, 512),
                                uq_rope.reshape(MLA_Q_RANK, 512)], axis=1).astype(BF16)
        small = (mla_q_norm_g[l][None, :], w_uq, mla_kv_norm_g[l][None, :], w_mla_ukv[l].astype(BF16),
                 swa_sink[l], diff_lq1[l][None, :], diff_lk1[l][None, :], diff_lq2[l][None, :],
                 diff_lk2[l][None, :], diff_subln_g[l][None, :])
        wg = w_mix_gate[l].astype(BF16)
        wb = w_branch[l].astype(BF16)
        wo = w_out[l].astype(BF16)
        wfg = w_ffn_gate[l].astype(BF16)
        wfu = w_ffn_up[l].astype(BF16)
        wfd = w_ffn_down[l].astype(BF16)
        g_pre, g_post = mix_pre_g[l][None, :], mix_post_g[l][None, :]
        f_pre, f_post = ffn_pre_g[l][None, :], ffn_post_g[l][None, :]

        h, p, kr = _inproj_call(y, mod, g_pre, w_main, w_kr, _ctx_seg, F32, "ctx_inproj")
        outs = _ctx_attn_call(p, kr, small, lam_init)
        o = outs[0]
        for k in range(8):
            new_caches[k].append(outs[1 + k])
        mixed = _merge_call(h, o, wg, wb, "ctx_merge")
        y = _outproj_call(mixed, y, mod, g_post, wo, _ctx_seg, "ctx_outproj")
        y = _ffn_call(y, mod, f_pre, f_post, wfg, wfu, wfd, _ctx_seg, "ctx_ffn")

        h, p, kr = _inproj_call(z, mod, g_pre, w_main, w_kr, _lat_seg, BF16, "lat_inproj")
        o = _lat_attn_call(p, kr, caches, tables, bias_all, small, l, lam_init)
        mixed = _merge_call(h, o, wg, wb, "lat_merge")
        z = _outproj_call(mixed, z, mod, g_post, wo, _lat_seg, "lat_outproj")
        z = _ffn_call(z, mod, f_pre, f_post, wfg, wfu, wfd, _lat_seg, "lat_ffn")

    def stacked(k, tail):
        a = jnp.stack([new_caches[k][l].reshape((BATCH, SEQ) + tail) for l in range(DEPTH)], axis=1)
        return a

    return (y.reshape(BATCH, SEQ, D_MODEL), z.reshape(DEC_BATCH, DEC_SEQ, D_MODEL),
            stacked(0, (MLA_KV_RANK,)), stacked(1, (MLA_ROPE,)),
            stacked(2, (SWA_KV_HEADS, HEAD_DIM)), stacked(3, (SWA_KV_HEADS, HEAD_DIM)),
            stacked(4, (NA_HEADS, HEAD_DIM)), stacked(5, (NA_HEADS, HEAD_DIM)),
            stacked(6, (DIFF_HEADS, 2 * DIFF_QK)), stacked(7, (DIFF_HEADS, DIFF_V)))
```

```python
import functools
import math

import numpy as np
import jax
import jax.numpy as jnp
from jax import lax
from jax.experimental import pallas as pl
from jax.experimental.pallas import tpu as pltpu

D_MODEL = 2048
BATCH = 32
SEQ = 256
DEPTH = 2
DEC_BATCH = 8
DEC_SEQ = 1024
PAST_LEN = 512
GRID_W = 64
HEAD_DIM = 128
ROPE_BASE = 10000.0
EPS = 1e-6
NEG_INF = -1e30

MLA_HEADS = 4
MLA_Q_RANK = 384
MLA_KV_RANK = 128
MLA_NOPE = 128
MLA_ROPE = 64
MLA_V = 128
SWA_HEADS = 4
SWA_KV_HEADS = 2
SWA_WINDOW = 128
NA_HEADS = 4
NA_ROWS = 8
NA_COLS = 16
DIFF_HEADS = 4
DIFF_QK = 64
DIFF_V = 128
N_BRANCH = 4
BRANCH_W = 512
FFN_HIDDEN = 5632

N_CTX_TOK = BATCH * SEQ
N_LAT_TOK = DEC_BATCH * DEC_SEQ
GRID_ROWS = DEC_SEQ // GRID_W

Q_DOWN0, QB0, QC0, QD0 = 0, 384, 896, 1408
Q_WIDTH = 1920
CKV0, KB0, VB0, KC0, VC0, KD0, VD0 = 1920, 2048, 2304, 2560, 3072, 3584, 4096
P_WIDTH = 4608
N_MOD_ROWS = 16

CACHE_TAILS = ((MLA_KV_RANK,), (MLA_ROPE,), (SWA_KV_HEADS, HEAD_DIM), (SWA_KV_HEADS, HEAD_DIM),
               (NA_HEADS, HEAD_DIM), (NA_HEADS, HEAD_DIM), (DIFF_HEADS, 2 * DIFF_QK), (DIFF_HEADS, DIFF_V))

VMEM_LIMIT = 56 * 1024 * 1024
LANES = 128

F32 = jnp.float32
BF16 = jnp.bfloat16


def _lambda_init(l):
    return 0.8 - 0.6 * math.exp(-0.3 * l)


def _rms(x):
    return x * lax.rsqrt(jnp.mean(x * x, axis=-1, keepdims=True) + EPS)


def _dot(a, b):
    return jnp.dot(a, b, preferred_element_type=F32)


def _dot_nt(a, b):
    return lax.dot_general(a, b, (((1,), (1,)), ((), ())), preferred_element_type=F32)


def _attend(pieces, sink=None):
    m = None
    for s, _ in pieces:
        mi = jnp.max(s, axis=-1, keepdims=True)
        m = mi if m is None else jnp.maximum(m, mi)
    if sink is not None:
        m = jnp.maximum(m, sink)
    den = None
    num = None
    for s, v in pieces:
        e = jnp.exp(s - m)
        d = jnp.sum(e, axis=-1, keepdims=True)
        n = _dot(e.astype(BF16), v)
        den = d if den is None else den + d
        num = n if num is None else num + n
    if sink is not None:
        den = den + jnp.exp(sink - m)
    return num / den


def _diff_lambda(lq1, lk1, lq2, lk2, lam_init):
    a = jnp.sum(lq1[...] * lk1[...], axis=-1, keepdims=True)
    b = jnp.sum(lq2[...] * lk2[...], axis=-1, keepdims=True)
    return jnp.exp(a) - jnp.exp(b) + lam_init


def _ada_kernel(c_ref, w_ref, b_ref, o_ref):
    c = c_ref[...]
    s = (c * jax.nn.sigmoid(c)).astype(BF16)
    o_ref[0] = _dot(s, w_ref[0].astype(BF16)) + b_ref[0]


def _ada_call(cvec, w_ada, b_ada):
    tn = 1024
    n = 6 * D_MODEL
    return pl.pallas_call(
        _ada_kernel,
        grid=(DEPTH, n // tn),
        in_specs=[
            pl.BlockSpec((N_MOD_ROWS, D_MODEL), lambda l, j: (0, 0)),
            pl.BlockSpec((1, D_MODEL, tn), lambda l, j: (l, 0, j)),
            pl.BlockSpec((1, 1, tn), lambda l, j: (l, 0, j)),
        ],
        out_specs=pl.BlockSpec((1, N_MOD_ROWS, tn), lambda l, j: (l, 0, j)),
        out_shape=jax.ShapeDtypeStruct((DEPTH, N_MOD_ROWS, n), F32),
        compiler_params=pltpu.CompilerParams(
            dimension_semantics=("arbitrary", "arbitrary"), vmem_limit_bytes=VMEM_LIMIT),
        name="ada_mod",
    )(cvec, w_ada, b_ada.reshape(DEPTH, 1, n))


def _nbr_bias_kernel(rpb_ref, o_ref):
    l = pl.program_id(0)
    h = pl.program_id(1)
    n_dr = 2 * NA_ROWS - 1
    n_dc = 2 * NA_COLS - 1
    base = (l * NA_HEADS + h) * (n_dr * n_dc)
    wq = lax.broadcasted_iota(jnp.int32, (GRID_W, LANES), 0)
    lane = lax.broadcasted_iota(jnp.int32, (GRID_W, LANES), 1)
    upper = lane >= GRID_W
    wk = jnp.where(upper, lane - GRID_W, lane)
    delta = wk - wq + (NA_COLS - 1)
    cs = jnp.clip(wq - NA_COLS // 2, 0, GRID_W - NA_COLS)
    valid = (wk >= cs) & (wk < cs + NA_COLS)
    for dr in range(n_dr - 1):
        acc = jnp.full((GRID_W, LANES), NEG_INF, F32)
        for d in range(n_dc):
            lo = rpb_ref[base + dr * n_dc + d]
            hi = rpb_ref[base + (dr + 1) * n_dc + d]
            acc = jnp.where(delta == d, jnp.where(upper, hi, lo), acc)
        o_ref[0, 0, dr] = jnp.where(valid, acc, NEG_INF)


def _nbr_bias_call(na_rpb):
    n_pairs = 2 * NA_ROWS - 2
    return pl.pallas_call(
        _nbr_bias_kernel,
        grid=(DEPTH, NA_HEADS),
        in_specs=[pl.BlockSpec(memory_space=pltpu.SMEM)],
        out_specs=pl.BlockSpec((1, 1, n_pairs, GRID_W, LANES), lambda l, h: (l, h, 0, 0, 0)),
        out_shape=jax.ShapeDtypeStruct((DEPTH, NA_HEADS, n_pairs, GRID_W, LANES), F32),
        compiler_params=pltpu.CompilerParams(dimension_semantics=("arbitrary", "arbitrary")),
        name="nbr_bias",
    )(na_rpb.reshape(-1))


def _inproj_kernel(x_ref, mod_ref, g_ref, w_ref, wkr_ref, h_ref, p_ref, kr_ref):
    @pl.when(pl.program_id(1) == 0)
    def _():
        m = mod_ref[0]
        h = _rms(x_ref[...]) * g_ref[...] * (1.0 + m[1:2]) + m[0:1]
        hb = h.astype(BF16)
        h_ref[...] = hb
        kr_ref[...] = _dot(hb, wkr_ref[...])

    p_ref[...] = _dot(h_ref[...], w_ref[...]).astype(p_ref.dtype)


def _inproj_call(x, mod, g, w_main, w_kr, seg, p_dtype, name):
    n_tok = x.shape[0]
    tm, tn = 1024, 768
    return pl.pallas_call(
        _inproj_kernel,
        grid=(n_tok // tm, P_WIDTH // tn),
        in_specs=[
            pl.BlockSpec((tm, D_MODEL), lambda i, j: (i, 0)),
            pl.BlockSpec((1, 6, D_MODEL), lambda i, j: (seg(i, tm), 0, 0)),
            pl.BlockSpec((1, D_MODEL), lambda i, j: (0, 0)),
            pl.BlockSpec((D_MODEL, tn), lambda i, j: (0, j)),
            pl.BlockSpec((D_MODEL, LANES), lambda i, j: (0, 0)),
        ],
        out_specs=[
            pl.BlockSpec((tm, D_MODEL), lambda i, j: (i, 0)),
            pl.BlockSpec((tm, tn), lambda i, j: (i, j)),
            pl.BlockSpec((tm, LANES), lambda i, j: (i, 0)),
        ],
        out_shape=[
            jax.ShapeDtypeStruct((n_tok, D_MODEL), BF16),
            jax.ShapeDtypeStruct((n_tok, P_WIDTH), p_dtype),
            jax.ShapeDtypeStruct((n_tok, LANES), F32),
        ],
        compiler_params=pltpu.CompilerParams(
            dimension_semantics=("arbitrary", "arbitrary"), vmem_limit_bytes=VMEM_LIMIT),
        name=name,
    )(x, mod, g, w_main, w_kr)


def _mla_queries(q_down, qn_g, w_uq):
    cq = _rms(q_down) * qn_g[...]
    return _dot(cq.astype(BF16), w_uq[...])


def _ctx_attn_kernel(*refs, lam_init, n_prev):
    (p_ref, kr_ref, qn_g, w_uq, kvn_g, w_ukv, sink_ref, lq1, lk1, lq2, lk2, subln_g) = refs[:12]
    n_in = 12 + (len(CACHE_TAILS) if n_prev else 0)
    prev = refs[12:n_in]
    o_ref = refs[n_in]
    ckv_o, kr_o, kb_o, vb_o, kc_o, vc_o, kd_o, vd_o = refs[n_in + 1:]
    for src, dst in zip(prev, (ckv_o, kr_o, kb_o, vb_o, kc_o, vc_o, kd_o, vd_o)):
        dst[0, 0:n_prev] = src[0]
    ls = n_prev

    scale_a = (MLA_NOPE + MLA_ROPE) ** -0.5
    qa = _mla_queries(p_ref[:, Q_DOWN0:Q_DOWN0 + MLA_Q_RANK], qn_g, w_uq) * scale_a
    ckv = _rms(p_ref[:, CKV0:CKV0 + MLA_KV_RANK]) * kvn_g[...]
    ckv_o[0, ls] = ckv
    kr = kr_ref[...]
    kr_o[0, ls] = kr[:, :MLA_ROPE]
    kr_b = kr.astype(BF16)
    kv = _dot(ckv.astype(BF16), w_ukv[...])
    for h in range(MLA_HEADS):
        q = jnp.concatenate([qa[:, h * 128:(h + 1) * 128], qa[:, 512 + h * 128:512 + (h + 1) * 128]],
                            axis=1).astype(BF16)
        k = jnp.concatenate([kv[:, h * 256:h * 256 + 128].astype(BF16), kr_b], axis=1)
        v = kv[:, h * 256 + 128:(h + 1) * 256].astype(BF16)
        o = _attend([(_dot_nt(q, k), v)])
        o_ref[:, h * 128:(h + 1) * 128] = o.astype(BF16)

    scale = HEAD_DIM ** -0.5

    def head(col0, h):
        return p_ref[:, col0 + h * 128:col0 + (h + 1) * 128]

    g_size = SWA_HEADS // SWA_KV_HEADS
    for g in range(SWA_KV_HEADS):
        k = head(KB0, g)
        v = head(VB0, g)
        kb_o[0, ls, :, g, :] = k
        vb_o[0, ls, :, g, :] = v
        k = k.astype(BF16)
        v = v.astype(BF16)
        for h in range(g * g_size, (g + 1) * g_size):
            q = (head(QB0, h) * scale).astype(BF16)
            o = _attend([(_dot_nt(q, k), v)], sink=sink_ref[h])
            o_ref[:, 512 + h * 128:512 + (h + 1) * 128] = o.astype(BF16)

    for h in range(NA_HEADS):
        k = head(KC0, h)
        v = head(VC0, h)
        kc_o[0, ls, :, h, :] = k
        vc_o[0, ls, :, h, :] = v
        q = (head(QC0, h) * scale).astype(BF16)
        o = _attend([(_dot_nt(q, k.astype(BF16)), v.astype(BF16))])
        o_ref[:, 1024 + h * 128:1024 + (h + 1) * 128] = o.astype(BF16)

    lam = _diff_lambda(lq1, lk1, lq2, lk2, lam_init)
    scale_d = DIFF_QK ** -0.5
    lo = lax.broadcasted_iota(jnp.int32, (SEQ, 128), 1) < DIFF_QK
    for h in range(DIFF_HEADS):
        k = head(KD0, h)
        v = head(VD0, h)
        kd_o[0, ls, :, h, :] = k
        vd_o[0, ls, :, h, :] = v
        k = k.astype(BF16)
        v = v.astype(BF16)
        q = head(QD0, h) * scale_d
        o1 = _attend([(_dot_nt(jnp.where(lo, q, 0.0).astype(BF16), k), v)])
        o2 = _attend([(_dot_nt(jnp.where(lo, 0.0, q).astype(BF16), k), v)])
        o = _rms(o1 - lam * o2) * subln_g[...] * (1.0 - lam_init)
        o_ref[:, 1536 + h * 128:1536 + (h + 1) * 128] = o.astype(BF16)


def _small_specs():
    zero2 = (lambda *a: (0, 0))
    return [
        pl.BlockSpec((1, MLA_Q_RANK), zero2),
        pl.BlockSpec((MLA_Q_RANK, 1024), zero2),
        pl.BlockSpec((1, MLA_KV_RANK), zero2),
        pl.BlockSpec((MLA_KV_RANK, 1024), zero2),
        pl.BlockSpec(memory_space=pltpu.SMEM),
        pl.BlockSpec((1, DIFF_QK), zero2),
        pl.BlockSpec((1, DIFF_QK), zero2),
        pl.BlockSpec((1, DIFF_QK), zero2),
        pl.BlockSpec((1, DIFF_QK), zero2),
        pl.BlockSpec((1, DIFF_V), zero2),
    ]


def _ctx_attn_call(p, kr, small, prev, lam_init):
    n_prev = prev[0].shape[1] if prev else 0
    row = lambda b: (b, 0)

    def cache_spec(n_layers, tail):
        zeros = (0,) * (len(tail) + 2)
        return pl.BlockSpec((1, n_layers, SEQ) + tail, lambda b: (b,) + zeros)

    return pl.pallas_call(
        functools.partial(_ctx_attn_kernel, lam_init=lam_init, n_prev=n_prev),
        grid=(BATCH,),
        in_specs=[pl.BlockSpec((SEQ, P_WIDTH), row), pl.BlockSpec((SEQ, LANES), row)] + _small_specs()
        + [cache_spec(n_prev, t) for t in (CACHE_TAILS if prev else ())],
        out_specs=[pl.BlockSpec((SEQ, D_MODEL), row)] + [cache_spec(n_prev + 1, t) for t in CACHE_TAILS],
        out_shape=[jax.ShapeDtypeStruct((N_CTX_TOK, D_MODEL), BF16)]
        + [jax.ShapeDtypeStruct((BATCH, n_prev + 1, SEQ) + t, F32) for t in CACHE_TAILS],
        compiler_params=pltpu.CompilerParams(
            dimension_semantics=("arbitrary",), vmem_limit_bytes=VMEM_LIMIT),
        name="ctx_attn",
    )(p, kr, *small, *prev)


LAT_TQ = 256
SWA_PAD = SWA_WINDOW


def _rope_pair64(x, c_ref, sa_ref, sb_ref, rows):
    out = []
    for j in range(x.shape[1] // LANES):
        xj = x[:, j * LANES:(j + 1) * LANES]
        out.append(xj * c_ref[rows, :] + pltpu.roll(xj, 96, 1) * sa_ref[rows, :]
                   + pltpu.roll(xj, 32, 1) * sb_ref[rows, :])
    return jnp.concatenate(out, axis=1) if len(out) > 1 else out[0]


def _rope128(x, c_ref, s_ref, rows):
    out = []
    for j in range(x.shape[1] // LANES):
        xj = x[:, j * LANES:(j + 1) * LANES]
        out.append(xj * c_ref[rows, :] + pltpu.roll(xj, 64, 1) * s_ref[rows, :])
    return jnp.concatenate(out, axis=1) if len(out) > 1 else out[0]


def _lat_attn_kernel(p_ref, kr_ref, c_ckv, c_kr, c_kb, c_vb, c_kc, c_vc, c_kd, c_vd,
                     cos128, sin128, cos64, sina64, sinb64, bias_ref,
                     qn_g, w_uq, kvn_g, w_ukv, sink_ref, lq1, lk1, lq2, lk2, subln_g,
                     o_ref,
                     ka_s, va_s, kbc_s, vbc_s, kbp_s, vbp_s, kcc_s, vcc_s, kdc_s, vdc_s, kdo_s,
                     *, lam_init):
    qi = pl.program_id(1)
    all_rows = slice(None)
    n_own = DEC_SEQ

    @pl.when(qi == 0)
    def _prologue():
        ckv_own = (_rms(p_ref[:, CKV0:CKV0 + MLA_KV_RANK].astype(F32)) * kvn_g[...]).astype(BF16)
        ckv_all = jnp.concatenate([c_ckv[0, 0].astype(BF16), ckv_own], axis=0)
        kr_ctx = jnp.concatenate([c_kr[0, 0], jnp.zeros((PAST_LEN, LANES - MLA_ROPE), F32)], axis=1)
        kr_own = _rope_pair64(kr_ref[...], cos64, sina64, sinb64, all_rows)
        kr_all = jnp.concatenate([kr_ctx, kr_own], axis=0).astype(BF16)
        for h in range(MLA_HEADS):
            kvh = _dot(ckv_all, w_ukv[:, h * 256:(h + 1) * 256])
            ka_s[h, :, 0:128] = kvh[:, 0:128].astype(BF16)
            ka_s[h, :, 128:256] = kr_all
            va_s[:, h * 128:(h + 1) * 128] = kvh[:, 128:256].astype(BF16)
        for g in range(SWA_KV_HEADS):
            kbc_s[:, g * 128:(g + 1) * 128] = c_kb[0, 0, :, g, :].astype(BF16)
            vbc_s[:, g * 128:(g + 1) * 128] = c_vb[0, 0, :, g, :].astype(BF16)
        zpad = jnp.zeros((SWA_PAD, 256), BF16)
        kbp_s[0:SWA_PAD, :] = zpad
        kbp_s[SWA_PAD + n_own:, :] = zpad
        vbp_s[0:SWA_PAD, :] = zpad
        vbp_s[SWA_PAD + n_own:, :] = zpad
        kbp_s[SWA_PAD:SWA_PAD + n_own, :] = _rope128(
            p_ref[:, KB0:KB0 + 256].astype(F32), cos128, sin128, all_rows).astype(BF16)
        vbp_s[SWA_PAD:SWA_PAD + n_own, :] = p_ref[:, VB0:VB0 + 256]
        for h in range(NA_HEADS):
            cols = slice(h * 128, (h + 1) * 128)
            kcc_s[:, cols] = c_kc[0, 0, :, h, :].astype(BF16)
            vcc_s[:, cols] = c_vc[0, 0, :, h, :].astype(BF16)
            kdc_s[:, cols] = c_kd[0, 0, :, h, :].astype(BF16)
            vdc_s[:, cols] = c_vd[0, 0, :, h, :].astype(BF16)
        kdo_s[...] = _rope_pair64(
            p_ref[:, KD0:KD0 + 512].astype(F32), cos64, sina64, sinb64, all_rows).astype(BF16)

    q0 = pl.multiple_of(qi * LAT_TQ, LAT_TQ)
    rows = pl.ds(q0, LAT_TQ)

    scale_a = (MLA_NOPE + MLA_ROPE) ** -0.5
    qa = _mla_queries(p_ref[rows, Q_DOWN0:Q_DOWN0 + MLA_Q_RANK].astype(F32), qn_g, w_uq) * scale_a
    qa_r = _rope_pair64(qa[:, 512:1024], cos64, sina64, sinb64, rows)
    for h in range(MLA_HEADS):
        q = jnp.concatenate([qa[:, h * 128:(h + 1) * 128], qa_r[:, h * 128:(h + 1) * 128]],
                            axis=1).astype(BF16)
        o = _attend([(_dot_nt(q, ka_s[h]), va_s[:, h * 128:(h + 1) * 128])])
        o_ref[:, h * 128:(h + 1) * 128] = o.astype(BF16)

    scale = HEAD_DIM ** -0.5
    qb = (_rope128(p_ref[rows, QB0:QB0 + 512].astype(F32), cos128, sin128, rows) * scale).astype(BF16)
    n_loc = LAT_TQ + 2 * SWA_PAD
    qpos = lax.broadcasted_iota(jnp.int32, (LAT_TQ, n_loc), 0)
    kpos = lax.broadcasted_iota(jnp.int32, (LAT_TQ, n_loc), 1) - SWA_PAD
    kabs = kpos + q0
    valid = (jnp.abs(qpos - kpos) <= SWA_WINDOW) & (kabs >= 0) & (kabs < n_own)
    loc_rows = pl.ds(q0, n_loc)
    g_size = SWA_HEADS // SWA_KV_HEADS
    for h in range(SWA_HEADS):
        g = h // g_size
        q = qb[:, h * 128:(h + 1) * 128]
        s_ctx = _dot_nt(q, kbc_s[:, g * 128:(g + 1) * 128])
        s_loc = jnp.where(valid, _dot_nt(q, kbp_s[loc_rows, g * 128:(g + 1) * 128]), NEG_INF)
        o = _attend([(s_ctx, vbc_s[:, g * 128:(g + 1) * 128]),
                     (s_loc, vbp_s[loc_rows, g * 128:(g + 1) * 128])], sink=sink_ref[h])
        o_ref[:, 512 + h * 128:512 + (h + 1) * 128] = o.astype(BF16)

    n_win = NA_ROWS * GRID_W
    for t in range(LAT_TQ // GRID_W):
        r = qi * (LAT_TQ // GRID_W) + t
        rs = jnp.clip(r - NA_ROWS // 2, 0, GRID_ROWS - NA_ROWS)
        off = rs - r + (NA_ROWS - 1)
        krows = pl.ds(pl.multiple_of(rs * GRID_W, GRID_W), n_win)
        qrows = pl.ds(pl.multiple_of(q0 + t * GRID_W, GRID_W), GRID_W)
        for h in range(NA_HEADS):
            q = (p_ref[qrows, QC0 + h * 128:QC0 + (h + 1) * 128].astype(F32) * scale).astype(BF16)
            bias = jnp.concatenate([bias_ref[0, h, off + 2 * j] for j in range(NA_ROWS // 2)], axis=1)
            s_loc = _dot_nt(q, p_ref[krows, KC0 + h * 128:KC0 + (h + 1) * 128]) + bias
            s_ctx = _dot_nt(q, kcc_s[:, h * 128:(h + 1) * 128])
            o = _attend([(s_ctx, vcc_s[:, h * 128:(h + 1) * 128]),
                         (s_loc, p_ref[krows, VC0 + h * 128:VC0 + (h + 1) * 128])])
            o_ref[t * GRID_W:(t + 1) * GRID_W, 1024 + h * 128:1024 + (h + 1) * 128] = o.astype(BF16)

    scale_d = DIFF_QK ** -0.5
    qd = _rope_pair64(p_ref[rows, QD0:QD0 + 512].astype(F32), cos64, sina64, sinb64, rows) * scale_d
    lam = _diff_lambda(lq1, lk1, lq2, lk2, lam_init)
    lo = lax.broadcasted_iota(jnp.int32, (LAT_TQ, 128), 1) < DIFF_QK
    for h in range(DIFF_HEADS):
        cols = slice(h * 128, (h + 1) * 128)
        q = qd[:, cols]
        v_own = p_ref[:, VD0 + h * 128:VD0 + (h + 1) * 128]
        outs = []
        for qq in (jnp.where(lo, q, 0.0).astype(BF16), jnp.where(lo, 0.0, q).astype(BF16)):
            outs.append(_attend([(_dot_nt(qq, kdc_s[:, cols]), vdc_s[:, cols]),
                                 (_dot_nt(qq, kdo_s[:, cols]), v_own)]))
        o = _rms(outs[0] - lam * outs[1]) * subln_g[...] * (1.0 - lam_init)
        o_ref[:, 1536 + h * 128:1536 + (h + 1) * 128] = o.astype(BF16)


def _lat_attn_call(p, kr, caches, tables, bias, small, layer, lam_init):
    nq = DEC_SEQ // LAT_TQ
    batch_rows = lambda b, q: (b, 0)
    zero2 = lambda b, q: (0, 0)
    once = pl.Buffered(1)
    cache_specs = [pl.BlockSpec((1, 1, PAST_LEN) + t, lambda b, q, z=(0,) * (len(t) + 1): (b, layer) + z,
                                pipeline_mode=once) for t in CACHE_TAILS]
    table_specs = [pl.BlockSpec((DEC_SEQ, LANES), zero2) for _ in tables]
    n_pairs = 2 * NA_ROWS - 2
    bias_spec = pl.BlockSpec((1, NA_HEADS, n_pairs, GRID_W, LANES), lambda b, q: (layer, 0, 0, 0, 0))
    n_all = PAST_LEN + DEC_SEQ
    n_pad = DEC_SEQ + 2 * SWA_PAD
    scratch = [
        pltpu.VMEM((MLA_HEADS, n_all, 256), BF16),
        pltpu.VMEM((n_all, 512), BF16),
        pltpu.VMEM((PAST_LEN, 256), BF16), pltpu.VMEM((PAST_LEN, 256), BF16),
        pltpu.VMEM((n_pad, 256), BF16), pltpu.VMEM((n_pad, 256), BF16),
        pltpu.VMEM((PAST_LEN, 512), BF16), pltpu.VMEM((PAST_LEN, 512), BF16),
        pltpu.VMEM((PAST_LEN, 512), BF16), pltpu.VMEM((PAST_LEN, 512), BF16),
        pltpu.VMEM((DEC_SEQ, 512), BF16),
    ]
    return pl.pallas_call(
        functools.partial(_lat_attn_kernel, lam_init=lam_init),
        grid=(DEC_BATCH, nq),
        in_specs=[pl.BlockSpec((DEC_SEQ, P_WIDTH), batch_rows, pipeline_mode=once),
                  pl.BlockSpec((DEC_SEQ, LANES), batch_rows, pipeline_mode=once)]
        + cache_specs + table_specs + [bias_spec] + _small_specs(),
        out_specs=pl.BlockSpec((LAT_TQ, D_MODEL), lambda b, q: (b * nq + q, 0)),
        out_shape=jax.ShapeDtypeStruct((N_LAT_TOK, D_MODEL), BF16),
        scratch_shapes=scratch,
        compiler_params=pltpu.CompilerParams(
            dimension_semantics=("arbitrary", "arbitrary"), vmem_limit_bytes=60 * 1024 * 1024),
        name="lat_attn",
    )(p, kr, *caches, *tables, bias, *small)


def _merge_kernel(h_ref, o_ref, wg_ref, wb_ref, out_ref):
    acc = None
    for i in range(N_BRANCH):
        gate = jax.nn.sigmoid(_dot(h_ref[...], wg_ref[i]))
        term = gate * _dot(o_ref[:, i * BRANCH_W:(i + 1) * BRANCH_W], wb_ref[i])
        acc = term if acc is None else acc + term
    out_ref[...] = acc.astype(BF16)


def _merge_call(h, o, wg, wb, name):
    n_tok = h.shape[0]
    tm, tn = 1024, 512
    return pl.pallas_call(
        _merge_kernel,
        grid=(n_tok // tm, D_MODEL // tn),
        in_specs=[
            pl.BlockSpec((tm, D_MODEL), lambda i, j: (i, 0)),
            pl.BlockSpec((tm, D_MODEL), lambda i, j: (i, 0)),
            pl.BlockSpec((N_BRANCH, D_MODEL, tn), lambda i, j: (0, 0, j)),
            pl.BlockSpec((N_BRANCH, BRANCH_W, tn), lambda i, j: (0, 0, j)),
        ],
        out_specs=pl.BlockSpec((tm, tn), lambda i, j: (i, j)),
        out_shape=jax.ShapeDtypeStruct((n_tok, D_MODEL), BF16),
        compiler_params=pltpu.CompilerParams(
            dimension_semantics=("arbitrary", "arbitrary"), vmem_limit_bytes=VMEM_LIMIT),
        name=name,
    )(h, o, wg, wb)


def _outproj_kernel(mixed_ref, x_ref, mod_ref, g_ref, gffn_ref, w_ref, xo_ref, h2_ref):
    out = _dot(mixed_ref[...], w_ref[...])
    m = mod_ref[0]
    x = x_ref[...] + m[2:3] * (_rms(out) * g_ref[...])
    xo_ref[...] = x
    h2_ref[...] = (_rms(x) * gffn_ref[...] * (1.0 + m[4:5]) + m[3:4]).astype(BF16)


def _outproj_call(mixed, x, mod, g, g_ffn, w, seg, name):
    n_tok = x.shape[0]
    tm = 512
    return pl.pallas_call(
        _outproj_kernel,
        grid=(n_tok // tm,),
        in_specs=[
            pl.BlockSpec((tm, D_MODEL), lambda i: (i, 0)),
            pl.BlockSpec((tm, D_MODEL), lambda i: (i, 0)),
            pl.BlockSpec((1, 6, D_MODEL), lambda i: (seg(i, tm), 0, 0)),
            pl.BlockSpec((1, D_MODEL), lambda i: (0, 0)),
            pl.BlockSpec((1, D_MODEL), lambda i: (0, 0)),
            pl.BlockSpec((D_MODEL, D_MODEL), lambda i: (0, 0)),
        ],
        out_specs=[pl.BlockSpec((tm, D_MODEL), lambda i: (i, 0)), pl.BlockSpec((tm, D_MODEL), lambda i: (i, 0))],
        out_shape=[jax.ShapeDtypeStruct((n_tok, D_MODEL), F32), jax.ShapeDtypeStruct((n_tok, D_MODEL), BF16)],
        compiler_params=pltpu.CompilerParams(
            dimension_semantics=("arbitrary",), vmem_limit_bytes=VMEM_LIMIT),
        name=name,
    )(mixed, x, mod, g, g_ffn, w)


def _ffn_kernel(x_ref, h2_ref, mod_ref, gpost_ref, wg_ref, wu_ref, wd_ref, xo_ref, acc_s):
    j = pl.program_id(1)

    @pl.when(j == 0)
    def _():
        acc_s[...] = jnp.zeros_like(acc_s)

    a = _dot(h2_ref[...], wg_ref[...])
    b = _dot(h2_ref[...], wu_ref[...])
    t = (a * jax.nn.sigmoid(a) * b).astype(BF16)
    acc_s[...] += _dot(t, wd_ref[...])

    @pl.when(j == pl.num_programs(1) - 1)
    def _():
        m = mod_ref[0]
        xo_ref[...] = x_ref[...] + m[5:6] * (_rms(acc_s[...]) * gpost_ref[...])


def _ffn_call(x, h2, mod, gpost, wg, wu, wd, seg, name):
    n_tok = x.shape[0]
    tm, th = 512, 512
    return pl.pallas_call(
        _ffn_kernel,
        grid=(n_tok // tm, FFN_HIDDEN // th),
        in_specs=[
            pl.BlockSpec((tm, D_MODEL), lambda i, j: (i, 0)),
            pl.BlockSpec((tm, D_MODEL), lambda i, j: (i, 0)),
            pl.BlockSpec((1, 6, D_MODEL), lambda i, j: (seg(i, tm), 0, 0)),
            pl.BlockSpec((1, D_MODEL), lambda i, j: (0, 0)),
            pl.BlockSpec((D_MODEL, th), lambda i, j: (0, j)),
            pl.BlockSpec((D_MODEL, th), lambda i, j: (0, j)),
            pl.BlockSpec((th, D_MODEL), lambda i, j: (j, 0)),
        ],
        out_specs=pl.BlockSpec((tm, D_MODEL), lambda i, j: (i, 0)),
        out_shape=jax.ShapeDtypeStruct((n_tok, D_MODEL), F32),
        scratch_shapes=[pltpu.VMEM((tm, D_MODEL), F32)],
        compiler_params=pltpu.CompilerParams(
            dimension_semantics=("arbitrary", "arbitrary"), vmem_limit_bytes=VMEM_LIMIT),
        name=name,
    )(x, h2, mod, gpost, wg, wu, wd)


def _rope_tables():
    t = np.arange(DEC_SEQ)
    row = (t // GRID_W).astype(np.float32)
    col = (t % GRID_W).astype(np.float32)

    def half_angles(r):
        quarter = r // 4
        inv = (1.0 / (ROPE_BASE ** (np.arange(quarter, dtype=np.float32) / quarter))).astype(np.float32)
        return np.concatenate([row[:, None] * inv, col[:, None] * inv], axis=-1).astype(np.float32)

    a64 = half_angles(128)
    cos128 = np.concatenate([np.cos(a64), np.cos(a64)], axis=1)
    sin128 = np.concatenate([-np.sin(a64), np.sin(a64)], axis=1)
    a32 = half_angles(64)
    c, s, z = np.cos(a32), np.sin(a32), np.zeros_like(a32)
    cos64 = np.concatenate([c, c, c, c], axis=1)
    sina64 = np.concatenate([-s, z, -s, z], axis=1)
    sinb64 = np.concatenate([z, s, z, s], axis=1)
    return tuple(jnp.asarray(a, F32) for a in (cos128, sin128, cos64, sina64, sinb64))


def _ctx_seg(i, tm):
    del tm
    return i * 0


def _lat_seg(i, tm):
    return 1 + (i * tm) // DEC_SEQ


def kernel(x_prompt, x_sample, cache_mla_ckv, cache_mla_krope, cache_swa_k, cache_swa_v, cache_na_k, cache_na_v, cache_diff_k, cache_diff_v, c, c_ctx, w_ada, b_ada, mix_pre_g, mix_post_g, ffn_pre_g, ffn_post_g, w_in, mla_q_norm_g, mla_kv_norm_g, w_mla_uq, w_mla_ukv, swa_sink, na_rpb, diff_lq1, diff_lk1, diff_lq2, diff_lk2, diff_subln_g, w_mix_gate, w_branch, w_out, w_ffn_gate, w_ffn_up, w_ffn_down):
    cvec = jnp.concatenate([c_ctx[None, :], c, jnp.zeros((N_MOD_ROWS - 1 - DEC_BATCH, D_MODEL), F32)], axis=0)
    mod_all = _ada_call(cvec, w_ada, b_ada).reshape(DEPTH, N_MOD_ROWS, 6, D_MODEL)
    bias_all = _nbr_bias_call(na_rpb)
    tables = _rope_tables()
    caches = (cache_mla_ckv, cache_mla_krope, cache_swa_k, cache_swa_v, cache_na_k, cache_na_v,
              cache_diff_k, cache_diff_v)

    y = x_prompt.reshape(N_CTX_TOK, D_MODEL)
    z = x_sample.reshape(N_LAT_TOK, D_MODEL)
    new_caches = ()
    for l in range(DEPTH):
        lam_init = _lambda_init(l)
        mod = mod_all[l]
        wi = w_in[l]
        sp = np.cumsum((0, 384, 128, 64, 512, 256, 256, 512, 512, 512, 512, 512, 512))
        piece = lambda k: wi[:, sp[k]:sp[k + 1]]
        w_main = jnp.concatenate([piece(k) for k in (0, 3, 6, 9, 1, 4, 5, 7, 8, 10, 11)], axis=1).astype(BF16)
        w_kr = jnp.concatenate([piece(2), jnp.zeros((D_MODEL, LANES - MLA_ROPE), F32)], axis=1).astype(BF16)
        uq = w_mla_uq[l].reshape(MLA_Q_RANK, MLA_HEADS, MLA_NOPE + MLA_ROPE)
        uq_rope = jnp.pad(uq[:, :, MLA_NOPE:], ((0, 0), (0, 0), (0, LANES - MLA_ROPE)))
        w_uq = jnp.concatenate([uq[:, :, :MLA_NOPE].reshape(MLA_Q_RANK, 512),
                                uq_rope.reshape(MLA_Q_RANK, 512)], axis=1).astype(BF16)
        small = (mla_q_norm_g[l][None, :], w_uq, mla_kv_norm_g[l][None, :], w_mla_ukv[l].astype(BF16),
                 swa_sink[l], diff_lq1[l][None, :], diff_lk1[l][None, :], diff_lq2[l][None, :],
                 diff_lk2[l][None, :], diff_subln_g[l][None, :])
        wg = w_mix_gate[l].astype(BF16)
        wb = w_branch[l].astype(BF16)
        wo = w_out[l].astype(BF16)
        wfg = w_ffn_gate[l].astype(BF16)
        wfu = w_ffn_up[l].astype(BF16)
        wfd = w_ffn_down[l].astype(BF16)
        g_pre, g_post = mix_pre_g[l][None, :], mix_post_g[l][None, :]
        f_pre, f_post = ffn_pre_g[l][None, :], ffn_post_g[l][None, :]

        h, p, kr = _inproj_call(y, mod, g_pre, w_main, w_kr, _ctx_seg, F32, "ctx_inproj")
        o, *new_caches = _ctx_attn_call(p, kr, small, new_caches, lam_init)
        mixed = _merge_call(h, o, wg, wb, "ctx_merge")
        y, h2 = _outproj_call(mixed, y, mod, g_post, f_pre, wo, _ctx_seg, "ctx_outproj")
        y = _ffn_call(y, h2, mod, f_post, wfg, wfu, wfd, _ctx_seg, "ctx_ffn")

        h, p, kr = _inproj_call(z, mod, g_pre, w_main, w_kr, _lat_seg, BF16, "lat_inproj")
        o = _lat_attn_call(p, kr, caches, tables, bias_all, small, l, lam_init)
        mixed = _merge_call(h, o, wg, wb, "lat_merge")
        z, h2 = _outproj_call(mixed, z, mod, g_post, f_pre, wo, _lat_seg, "lat_outproj")
        z = _ffn_call(z, h2, mod, f_post, wfg, wfu, wfd, _lat_seg, "lat_ffn")

    return (y.reshape(BATCH, SEQ, D_MODEL), z.reshape(DEC_BATCH, DEC_SEQ, D_MODEL), *new_caches)
```

```python
import functools
import math

import numpy as np
import jax
import jax.numpy as jnp
from jax import lax
from jax.experimental import pallas as pl
from jax.experimental.pallas import tpu as pltpu

D_MODEL = 2048
BATCH = 32
SEQ = 256
DEPTH = 2
DEC_BATCH = 8
DEC_SEQ = 1024
PAST_LEN = 512
GRID_W = 64
HEAD_DIM = 128
ROPE_BASE = 10000.0
EPS = 1e-6
NEG_INF = -1e30

MLA_HEADS = 4
MLA_Q_RANK = 384
MLA_KV_RANK = 128
MLA_NOPE = 128
MLA_ROPE = 64
MLA_V = 128
SWA_HEADS = 4
SWA_KV_HEADS = 2
SWA_WINDOW = 128
NA_HEADS = 4
NA_ROWS = 8
NA_COLS = 16
DIFF_HEADS = 4
DIFF_QK = 64
DIFF_V = 128
N_BRANCH = 4
BRANCH_W = 512
FFN_HIDDEN = 5632

N_CTX_TOK = BATCH * SEQ
N_LAT_TOK = DEC_BATCH * DEC_SEQ
GRID_ROWS = DEC_SEQ // GRID_W

Q_DOWN0, CKV0, QB0, KB0, VB0, QC0, KC0, VC0, QD0, KD0, VD0 = (
    0, 384, 512, 1024, 1280, 1536, 2048, 2560, 3072, 3584, 4096)
KROPE_SRC0 = MLA_Q_RANK + MLA_KV_RANK
P_WIDTH = 4608
N_MOD_ROWS = 16

CACHE_TAILS = ((MLA_KV_RANK,), (MLA_ROPE,), (SWA_KV_HEADS, HEAD_DIM), (SWA_KV_HEADS, HEAD_DIM),
               (NA_HEADS, HEAD_DIM), (NA_HEADS, HEAD_DIM), (DIFF_HEADS, 2 * DIFF_QK), (DIFF_HEADS, DIFF_V))

VMEM_LIMIT = 56 * 1024 * 1024
VMEM_LIMIT_BIG = 60 * 1024 * 1024
LANES = 128

F32 = jnp.float32
BF16 = jnp.bfloat16


def _lambda_init(l):
    return 0.8 - 0.6 * math.exp(-0.3 * l)


def _rms(x):
    return x * lax.rsqrt(jnp.mean(x * x, axis=-1, keepdims=True) + EPS)


def _dot(a, b):
    return jnp.dot(a, b, preferred_element_type=F32)


def _dot_nt(a, b):
    return lax.dot_general(a, b, (((1,), (1,)), ((), ())), preferred_element_type=F32)


def _attend(pieces, sink=None):
    m = None
    for s, _ in pieces:
        mi = jnp.max(s, axis=-1, keepdims=True)
        m = mi if m is None else jnp.maximum(m, mi)
    if sink is not None:
        m = jnp.maximum(m, sink)
    den = None
    num = None
    for s, v in pieces:
        e = jnp.exp(s - m)
        d = jnp.sum(e, axis=-1, keepdims=True)
        n = _dot(e.astype(BF16), v)
        den = d if den is None else den + d
        num = n if num is None else num + n
    if sink is not None:
        den = den + jnp.exp(sink - m)
    return num / den


def _diff_lambda(lq1, lk1, lq2, lk2, lam_init):
    a = jnp.sum(lq1[...] * lk1[...], axis=-1, keepdims=True)
    b = jnp.sum(lq2[...] * lk2[...], axis=-1, keepdims=True)
    return jnp.exp(a) - jnp.exp(b) + lam_init


def _ada_kernel(c_ref, w_ref, b_ref, o_ref):
    c = c_ref[...]
    s = (c * jax.nn.sigmoid(c)).astype(BF16)
    o_ref[0] = _dot(s, w_ref[0].astype(BF16)) + b_ref[0]


def _ada_call(cvec, w_ada, b_ada):
    tn = 1024
    n = 6 * D_MODEL
    return pl.pallas_call(
        _ada_kernel,
        grid=(DEPTH, n // tn),
        in_specs=[
            pl.BlockSpec((N_MOD_ROWS, D_MODEL), lambda l, j: (0, 0)),
            pl.BlockSpec((1, D_MODEL, tn), lambda l, j: (l, 0, j)),
            pl.BlockSpec((1, 1, tn), lambda l, j: (l, 0, j)),
        ],
        out_specs=pl.BlockSpec((1, N_MOD_ROWS, tn), lambda l, j: (l, 0, j)),
        out_shape=jax.ShapeDtypeStruct((DEPTH, N_MOD_ROWS, n), F32),
        compiler_params=pltpu.CompilerParams(
            dimension_semantics=("arbitrary", "arbitrary"), vmem_limit_bytes=VMEM_LIMIT),
        name="ada_mod",
    )(cvec, w_ada, b_ada.reshape(DEPTH, 1, n))


def _nbr_bias_kernel(rpb_ref, o_ref):
    l = pl.program_id(0)
    h = pl.program_id(1)
    n_dr = 2 * NA_ROWS - 1
    n_dc = 2 * NA_COLS - 1
    base = (l * NA_HEADS + h) * (n_dr * n_dc)
    wq = lax.broadcasted_iota(jnp.int32, (GRID_W, LANES), 0)
    lane = lax.broadcasted_iota(jnp.int32, (GRID_W, LANES), 1)
    upper = lane >= GRID_W
    wk = jnp.where(upper, lane - GRID_W, lane)
    delta = wk - wq + (NA_COLS - 1)
    cs = jnp.clip(wq - NA_COLS // 2, 0, GRID_W - NA_COLS)
    valid = (wk >= cs) & (wk < cs + NA_COLS)
    for dr in range(n_dr - 1):
        acc = jnp.full((GRID_W, LANES), NEG_INF, F32)
        for d in range(n_dc):
            lo = rpb_ref[base + dr * n_dc + d]
            hi = rpb_ref[base + (dr + 1) * n_dc + d]
            acc = jnp.where(delta == d, jnp.where(upper, hi, lo), acc)
        o_ref[0, 0, dr] = jnp.where(valid, acc, NEG_INF)


def _nbr_bias_call(na_rpb):
    n_pairs = 2 * NA_ROWS - 2
    return pl.pallas_call(
        _nbr_bias_kernel,
        grid=(DEPTH, NA_HEADS),
        in_specs=[pl.BlockSpec(memory_space=pltpu.SMEM)],
        out_specs=pl.BlockSpec((1, 1, n_pairs, GRID_W, LANES), lambda l, h: (l, h, 0, 0, 0)),
        out_shape=jax.ShapeDtypeStruct((DEPTH, NA_HEADS, n_pairs, GRID_W, LANES), F32),
        compiler_params=pltpu.CompilerParams(dimension_semantics=("arbitrary", "arbitrary")),
        name="nbr_bias",
    )(na_rpb.reshape(-1))


def _inproj_kernel(x_ref, mod_ref, g_ref, w_ref, wkr_ref, h_ref, p_ref, kr_ref):
    @pl.when(pl.program_id(1) == 0)
    def _():
        m = mod_ref[0]
        h = _rms(x_ref[...]) * g_ref[...] * (1.0 + m[1:2]) + m[0:1]
        hb = h.astype(BF16)
        h_ref[...] = hb
        kr_ref[...] = _dot(hb, wkr_ref[...])

    p_ref[...] = _dot(h_ref[...], w_ref[...]).astype(p_ref.dtype)


def _inproj_call(x, mod, g, w_main, w_kr, seg, p_dtype, name):
    n_tok = x.shape[0]
    tm, tn = 1024, 768
    return pl.pallas_call(
        _inproj_kernel,
        grid=(n_tok // tm, P_WIDTH // tn),
        in_specs=[
            pl.BlockSpec((tm, D_MODEL), lambda i, j: (i, 0)),
            pl.BlockSpec((1, 6, D_MODEL), lambda i, j: (seg(i, tm), 0, 0)),
            pl.BlockSpec((1, D_MODEL), lambda i, j: (0, 0)),
            pl.BlockSpec((D_MODEL, tn), lambda i, j: (0, j)),
            pl.BlockSpec((D_MODEL, LANES), lambda i, j: (0, 0)),
        ],
        out_specs=[
            pl.BlockSpec((tm, D_MODEL), lambda i, j: (i, 0)),
            pl.BlockSpec((tm, tn), lambda i, j: (i, j)),
            pl.BlockSpec((tm, LANES), lambda i, j: (i, 0)),
        ],
        out_shape=[
            jax.ShapeDtypeStruct((n_tok, D_MODEL), BF16),
            jax.ShapeDtypeStruct((n_tok, P_WIDTH), p_dtype),
            jax.ShapeDtypeStruct((n_tok, LANES), F32),
        ],
        compiler_params=pltpu.CompilerParams(
            dimension_semantics=("arbitrary", "arbitrary"), vmem_limit_bytes=VMEM_LIMIT),
        name=name,
    )(x, mod, g, w_main, w_kr)


def _mla_queries(q_down, qn_g, w_uq):
    cq = _rms(q_down) * qn_g[...]
    return _dot(cq.astype(BF16), w_uq[...])


def _ctx_attn_kernel(*refs, lam_init, n_prev):
    (p_ref, kr_ref, qn_g, w_uq, kvn_g, w_ukv, sink_ref, lq1, lk1, lq2, lk2, subln_g) = refs[:12]
    n_in = 12 + (len(CACHE_TAILS) if n_prev else 0)
    prev = refs[12:n_in]
    o_ref = refs[n_in]
    ckv_o, kr_o, kb_o, vb_o, kc_o, vc_o, kd_o, vd_o = refs[n_in + 1:]
    for src, dst in zip(prev, (ckv_o, kr_o, kb_o, vb_o, kc_o, vc_o, kd_o, vd_o)):
        dst[0, 0:n_prev] = src[0]
    ls = n_prev

    scale_a = (MLA_NOPE + MLA_ROPE) ** -0.5
    qa = _mla_queries(p_ref[:, Q_DOWN0:Q_DOWN0 + MLA_Q_RANK], qn_g, w_uq) * scale_a
    ckv = _rms(p_ref[:, CKV0:CKV0 + MLA_KV_RANK]) * kvn_g[...]
    ckv_o[0, ls] = ckv
    kr = kr_ref[...]
    kr_o[0, ls] = kr[:, :MLA_ROPE]
    kr_b = kr.astype(BF16)
    kv = _dot(ckv.astype(BF16), w_ukv[...])
    for h in range(MLA_HEADS):
        q = jnp.concatenate([qa[:, h * 128:(h + 1) * 128], qa[:, 512 + h * 128:512 + (h + 1) * 128]],
                            axis=1).astype(BF16)
        k = jnp.concatenate([kv[:, h * 256:h * 256 + 128].astype(BF16), kr_b], axis=1)
        v = kv[:, h * 256 + 128:(h + 1) * 256].astype(BF16)
        o = _attend([(_dot_nt(q, k), v)])
        o_ref[:, h * 128:(h + 1) * 128] = o.astype(BF16)

    scale = HEAD_DIM ** -0.5

    def head(col0, h):
        return p_ref[:, col0 + h * 128:col0 + (h + 1) * 128]

    g_size = SWA_HEADS // SWA_KV_HEADS
    for g in range(SWA_KV_HEADS):
        k = head(KB0, g)
        v = head(VB0, g)
        kb_o[0, ls, pl.ds(g, SEQ, stride=SWA_KV_HEADS), :] = k
        vb_o[0, ls, pl.ds(g, SEQ, stride=SWA_KV_HEADS), :] = v
        k = k.astype(BF16)
        v = v.astype(BF16)
        for h in range(g * g_size, (g + 1) * g_size):
            q = (head(QB0, h) * scale).astype(BF16)
            o = _attend([(_dot_nt(q, k), v)], sink=sink_ref[h])
            o_ref[:, 512 + h * 128:512 + (h + 1) * 128] = o.astype(BF16)

    for h in range(NA_HEADS):
        k = head(KC0, h)
        v = head(VC0, h)
        kc_o[0, ls, pl.ds(h, SEQ, stride=NA_HEADS), :] = k
        vc_o[0, ls, pl.ds(h, SEQ, stride=NA_HEADS), :] = v
        q = (head(QC0, h) * scale).astype(BF16)
        o = _attend([(_dot_nt(q, k.astype(BF16)), v.astype(BF16))])
        o_ref[:, 1024 + h * 128:1024 + (h + 1) * 128] = o.astype(BF16)

    lam = _diff_lambda(lq1, lk1, lq2, lk2, lam_init)
    scale_d = DIFF_QK ** -0.5
    lo = lax.broadcasted_iota(jnp.int32, (SEQ, 128), 1) < DIFF_QK
    for h in range(DIFF_HEADS):
        k = head(KD0, h)
        v = head(VD0, h)
        kd_o[0, ls, pl.ds(h, SEQ, stride=DIFF_HEADS), :] = k
        vd_o[0, ls, pl.ds(h, SEQ, stride=DIFF_HEADS), :] = v
        k = k.astype(BF16)
        v = v.astype(BF16)
        q = head(QD0, h) * scale_d
        o1 = _attend([(_dot_nt(jnp.where(lo, q, 0.0).astype(BF16), k), v)])
        o2 = _attend([(_dot_nt(jnp.where(lo, 0.0, q).astype(BF16), k), v)])
        o = _rms(o1 - lam * o2) * subln_g[...] * (1.0 - lam_init)
        o_ref[:, 1536 + h * 128:1536 + (h + 1) * 128] = o.astype(BF16)


def _small_specs():
    zero2 = (lambda *a: (0, 0))
    return [
        pl.BlockSpec((1, MLA_Q_RANK), zero2),
        pl.BlockSpec((MLA_Q_RANK, 1024), zero2),
        pl.BlockSpec((1, MLA_KV_RANK), zero2),
        pl.BlockSpec((MLA_KV_RANK, 1024), zero2),
        pl.BlockSpec(memory_space=pltpu.SMEM),
        pl.BlockSpec((1, DIFF_QK), zero2),
        pl.BlockSpec((1, DIFF_QK), zero2),
        pl.BlockSpec((1, DIFF_QK), zero2),
        pl.BlockSpec((1, DIFF_QK), zero2),
        pl.BlockSpec((1, DIFF_V), zero2),
    ]


def _ctx_attn_call(p, kr, small, prev, lam_init):
    n_prev = prev[0].shape[1] if prev else 0
    row = lambda b: (b, 0)

    def flat(tail):
        return (SEQ * tail[0], tail[1]) if len(tail) == 2 else (SEQ,) + tail

    def cache_spec(n_layers, tail):
        return pl.BlockSpec((1, n_layers) + flat(tail), lambda b: (b, 0, 0, 0))

    return pl.pallas_call(
        functools.partial(_ctx_attn_kernel, lam_init=lam_init, n_prev=n_prev),
        grid=(BATCH,),
        in_specs=[pl.BlockSpec((SEQ, P_WIDTH), row), pl.BlockSpec((SEQ, LANES), row)] + _small_specs()
        + [cache_spec(n_prev, t) for t in (CACHE_TAILS if prev else ())],
        out_specs=[pl.BlockSpec((SEQ, D_MODEL), row)] + [cache_spec(n_prev + 1, t) for t in CACHE_TAILS],
        out_shape=[jax.ShapeDtypeStruct((N_CTX_TOK, D_MODEL), BF16)]
        + [jax.ShapeDtypeStruct((BATCH, n_prev + 1) + flat(t), F32) for t in CACHE_TAILS],
        compiler_params=pltpu.CompilerParams(
            dimension_semantics=("arbitrary",), vmem_limit_bytes=VMEM_LIMIT),
        name="ctx_attn",
    )(p, kr, *small, *prev)


LAT_TQ = 256
SWA_PAD = SWA_WINDOW


def _rope_pair64(x, c_ref, sa_ref, sb_ref, rows):
    out = []
    for j in range(x.shape[1] // LANES):
        xj = x[:, j * LANES:(j + 1) * LANES]
        out.append(xj * c_ref[rows, :] + pltpu.roll(xj, 96, 1) * sa_ref[rows, :]
                   + pltpu.roll(xj, 32, 1) * sb_ref[rows, :])
    return jnp.concatenate(out, axis=1) if len(out) > 1 else out[0]


def _rope128(x, c_ref, s_ref, rows):
    out = []
    for j in range(x.shape[1] // LANES):
        xj = x[:, j * LANES:(j + 1) * LANES]
        out.append(xj * c_ref[rows, :] + pltpu.roll(xj, 64, 1) * s_ref[rows, :])
    return jnp.concatenate(out, axis=1) if len(out) > 1 else out[0]


def _lat_attn_kernel(p_ref, kr_ref, c_ckv, c_kr, c_kb, c_vb, c_kc, c_vc, c_kd, c_vd,
                     cos128, sin128, cos64, sina64, sinb64, bias_ref,
                     qn_g, w_uq, kvn_g, w_ukv, sink_ref, lq1, lk1, lq2, lk2, subln_g,
                     o_ref,
                     ka_s, va_s, kbc_s, vbc_s, kbp_s, vbp_s, kcc_s, vcc_s, kdc_s, vdc_s, kdo_s,
                     *, lam_init):
    def cached_head(ref, h, n_heads):
        return ref[pl.ds(h, PAST_LEN, stride=n_heads), :].astype(BF16)

    qi = pl.program_id(1)
    all_rows = slice(None)
    n_own = DEC_SEQ

    @pl.when(qi == 0)
    def _prologue():
        ckv_own = (_rms(p_ref[:, CKV0:CKV0 + MLA_KV_RANK].astype(F32)) * kvn_g[...]).astype(BF16)
        ckv_all = jnp.concatenate([c_ckv[...].astype(BF16), ckv_own], axis=0)
        kr_ctx = jnp.concatenate([c_kr[...], jnp.zeros((PAST_LEN, LANES - MLA_ROPE), F32)], axis=1)
        kr_own = _rope_pair64(kr_ref[...], cos64, sina64, sinb64, all_rows)
        kr_all = jnp.concatenate([kr_ctx, kr_own], axis=0).astype(BF16)
        for h in range(MLA_HEADS):
            kvh = _dot(ckv_all, w_ukv[:, h * 256:(h + 1) * 256])
            ka_s[h, :, 0:128] = kvh[:, 0:128].astype(BF16)
            ka_s[h, :, 128:256] = kr_all
            va_s[:, h * 128:(h + 1) * 128] = kvh[:, 128:256].astype(BF16)
        for g in range(SWA_KV_HEADS):
            kbc_s[:, g * 128:(g + 1) * 128] = cached_head(c_kb, g, SWA_KV_HEADS)
            vbc_s[:, g * 128:(g + 1) * 128] = cached_head(c_vb, g, SWA_KV_HEADS)
        zpad = jnp.zeros((SWA_PAD, 256), BF16)
        kbp_s[0:SWA_PAD, :] = zpad
        kbp_s[SWA_PAD + n_own:, :] = zpad
        vbp_s[0:SWA_PAD, :] = zpad
        vbp_s[SWA_PAD + n_own:, :] = zpad
        kbp_s[SWA_PAD:SWA_PAD + n_own, :] = _rope128(
            p_ref[:, KB0:KB0 + 256].astype(F32), cos128, sin128, all_rows).astype(BF16)
        vbp_s[SWA_PAD:SWA_PAD + n_own, :] = p_ref[:, VB0:VB0 + 256]
        for h in range(NA_HEADS):
            cols = slice(h * 128, (h + 1) * 128)
            kcc_s[:, cols] = cached_head(c_kc, h, NA_HEADS)
            vcc_s[:, cols] = cached_head(c_vc, h, NA_HEADS)
            kdc_s[:, cols] = cached_head(c_kd, h, DIFF_HEADS)
            vdc_s[:, cols] = cached_head(c_vd, h, DIFF_HEADS)
        kdo_s[...] = _rope_pair64(
            p_ref[:, KD0:KD0 + 512].astype(F32), cos64, sina64, sinb64, all_rows).astype(BF16)

    q0 = pl.multiple_of(qi * LAT_TQ, LAT_TQ)
    rows = pl.ds(q0, LAT_TQ)

    scale_a = (MLA_NOPE + MLA_ROPE) ** -0.5
    qa = _mla_queries(p_ref[rows, Q_DOWN0:Q_DOWN0 + MLA_Q_RANK].astype(F32), qn_g, w_uq) * scale_a
    qa_r = _rope_pair64(qa[:, 512:1024], cos64, sina64, sinb64, rows)
    for h in range(MLA_HEADS):
        q = jnp.concatenate([qa[:, h * 128:(h + 1) * 128], qa_r[:, h * 128:(h + 1) * 128]],
                            axis=1).astype(BF16)
        o = _attend([(_dot_nt(q, ka_s[h]), va_s[:, h * 128:(h + 1) * 128])])
        o_ref[:, h * 128:(h + 1) * 128] = o.astype(BF16)

    scale = HEAD_DIM ** -0.5
    qb = (_rope128(p_ref[rows, QB0:QB0 + 512].astype(F32), cos128, sin128, rows) * scale).astype(BF16)
    n_loc = LAT_TQ + 2 * SWA_PAD
    qpos = lax.broadcasted_iota(jnp.int32, (LAT_TQ, n_loc), 0)
    kpos = lax.broadcasted_iota(jnp.int32, (LAT_TQ, n_loc), 1) - SWA_PAD
    kabs = kpos + q0
    valid = (jnp.abs(qpos - kpos) <= SWA_WINDOW) & (kabs >= 0) & (kabs < n_own)
    loc_rows = pl.ds(q0, n_loc)
    g_size = SWA_HEADS // SWA_KV_HEADS
    for h in range(SWA_HEADS):
        g = h // g_size
        q = qb[:, h * 128:(h + 1) * 128]
        s_ctx = _dot_nt(q, kbc_s[:, g * 128:(g + 1) * 128])
        s_loc = jnp.where(valid, _dot_nt(q, kbp_s[loc_rows, g * 128:(g + 1) * 128]), NEG_INF)
        o = _attend([(s_ctx, vbc_s[:, g * 128:(g + 1) * 128]),
                     (s_loc, vbp_s[loc_rows, g * 128:(g + 1) * 128])], sink=sink_ref[h])
        o_ref[:, 512 + h * 128:512 + (h + 1) * 128] = o.astype(BF16)

    n_win = NA_ROWS * GRID_W
    for t in range(LAT_TQ // GRID_W):
        r = qi * (LAT_TQ // GRID_W) + t
        rs = jnp.clip(r - NA_ROWS // 2, 0, GRID_ROWS - NA_ROWS)
        off = rs - r + (NA_ROWS - 1)
        krows = pl.ds(pl.multiple_of(rs * GRID_W, GRID_W), n_win)
        qrows = pl.ds(pl.multiple_of(q0 + t * GRID_W, GRID_W), GRID_W)
        for h in range(NA_HEADS):
            q = (p_ref[qrows, QC0 + h * 128:QC0 + (h + 1) * 128].astype(F32) * scale).astype(BF16)
            bias = jnp.concatenate([bias_ref[0, h, off + 2 * j] for j in range(NA_ROWS // 2)], axis=1)
            s_loc = _dot_nt(q, p_ref[krows, KC0 + h * 128:KC0 + (h + 1) * 128]) + bias
            s_ctx = _dot_nt(q, kcc_s[:, h * 128:(h + 1) * 128])
            o = _attend([(s_ctx, vcc_s[:, h * 128:(h + 1) * 128]),
                         (s_loc, p_ref[krows, VC0 + h * 128:VC0 + (h + 1) * 128])])
            o_ref[t * GRID_W:(t + 1) * GRID_W, 1024 + h * 128:1024 + (h + 1) * 128] = o.astype(BF16)

    scale_d = DIFF_QK ** -0.5
    qd = _rope_pair64(p_ref[rows, QD0:QD0 + 512].astype(F32), cos64, sina64, sinb64, rows) * scale_d
    lam = _diff_lambda(lq1, lk1, lq2, lk2, lam_init)
    lo = lax.broadcasted_iota(jnp.int32, (LAT_TQ, 128), 1) < DIFF_QK
    for h in range(DIFF_HEADS):
        cols = slice(h * 128, (h + 1) * 128)
        q = qd[:, cols]
        v_own = p_ref[:, VD0 + h * 128:VD0 + (h + 1) * 128]
        outs = []
        for qq in (jnp.where(lo, q, 0.0).astype(BF16), jnp.where(lo, 0.0, q).astype(BF16)):
            outs.append(_attend([(_dot_nt(qq, kdc_s[:, cols]), vdc_s[:, cols]),
                                 (_dot_nt(qq, kdo_s[:, cols]), v_own)]))
        o = _rms(outs[0] - lam * outs[1]) * subln_g[...] * (1.0 - lam_init)
        o_ref[:, 1536 + h * 128:1536 + (h + 1) * 128] = o.astype(BF16)


def _lat_attn_call(p, kr, caches, tables, bias, small, layer, lam_init):
    nq = DEC_SEQ // LAT_TQ
    batch_rows = lambda b, q: (b, 0)
    zero2 = lambda b, q: (0, 0)
    once = pl.Buffered(1)
    cache_args = [a.reshape(a.shape[:2] + (-1, a.shape[-1])) for a in caches]
    cache_specs = [pl.BlockSpec((None, None) + a.shape[2:], lambda b, q: (b, layer, 0, 0), pipeline_mode=once)
                   for a in cache_args]
    table_specs = [pl.BlockSpec((DEC_SEQ, LANES), zero2) for _ in tables]
    n_pairs = 2 * NA_ROWS - 2
    bias_spec = pl.BlockSpec((1, NA_HEADS, n_pairs, GRID_W, LANES), lambda b, q: (layer, 0, 0, 0, 0))
    n_all = PAST_LEN + DEC_SEQ
    n_pad = DEC_SEQ + 2 * SWA_PAD
    scratch = [
        pltpu.VMEM((MLA_HEADS, n_all, 256), BF16),
        pltpu.VMEM((n_all, 512), BF16),
        pltpu.VMEM((PAST_LEN, 256), BF16), pltpu.VMEM((PAST_LEN, 256), BF16),
        pltpu.VMEM((n_pad, 256), BF16), pltpu.VMEM((n_pad, 256), BF16),
        pltpu.VMEM((PAST_LEN, 512), BF16), pltpu.VMEM((PAST_LEN, 512), BF16),
        pltpu.VMEM((PAST_LEN, 512), BF16), pltpu.VMEM((PAST_LEN, 512), BF16),
        pltpu.VMEM((DEC_SEQ, 512), BF16),
    ]
    return pl.pallas_call(
        functools.partial(_lat_attn_kernel, lam_init=lam_init),
        grid=(DEC_BATCH, nq),
        in_specs=[pl.BlockSpec((DEC_SEQ, P_WIDTH), batch_rows, pipeline_mode=once),
                  pl.BlockSpec((DEC_SEQ, LANES), batch_rows, pipeline_mode=once)]
        + cache_specs + table_specs + [bias_spec] + _small_specs(),
        out_specs=pl.BlockSpec((LAT_TQ, D_MODEL), lambda b, q: (b * nq + q, 0)),
        out_shape=jax.ShapeDtypeStruct((N_LAT_TOK, D_MODEL), BF16),
        scratch_shapes=scratch,
        compiler_params=pltpu.CompilerParams(
            dimension_semantics=("arbitrary", "arbitrary"), vmem_limit_bytes=VMEM_LIMIT_BIG),
        name="lat_attn",
    )(p, kr, *cache_args, *tables, bias, *small)


def _merge_kernel(h_ref, o_ref, wg_ref, wb_ref, out_ref):
    acc = None
    for i in range(N_BRANCH):
        gate = jax.nn.sigmoid(_dot(h_ref[...], wg_ref[i]))
        term = gate * _dot(o_ref[:, i * BRANCH_W:(i + 1) * BRANCH_W], wb_ref[i])
        acc = term if acc is None else acc + term
    out_ref[...] = acc.astype(BF16)


def _merge_call(h, o, wg, wb, name):
    n_tok = h.shape[0]
    tm, tn = 1024, 512
    return pl.pallas_call(
        _merge_kernel,
        grid=(n_tok // tm, D_MODEL // tn),
        in_specs=[
            pl.BlockSpec((tm, D_MODEL), lambda i, j: (i, 0)),
            pl.BlockSpec((tm, D_MODEL), lambda i, j: (i, 0)),
            pl.BlockSpec((N_BRANCH, D_MODEL, tn), lambda i, j: (0, 0, j)),
            pl.BlockSpec((N_BRANCH, BRANCH_W, tn), lambda i, j: (0, 0, j)),
        ],
        out_specs=pl.BlockSpec((tm, tn), lambda i, j: (i, j)),
        out_shape=jax.ShapeDtypeStruct((n_tok, D_MODEL), BF16),
        compiler_params=pltpu.CompilerParams(
            dimension_semantics=("arbitrary", "arbitrary"), vmem_limit_bytes=VMEM_LIMIT),
        name=name,
    )(h, o, wg, wb)


def _outproj_kernel(mixed_ref, x_ref, mod_ref, g_ref, gffn_ref, w_ref, xo_ref, h2_ref):
    m = mod_ref[0]
    half = mixed_ref.shape[0] // 2
    for r in (slice(0, half), slice(half, 2 * half)):
        out = _dot(mixed_ref[r, :], w_ref[...])
        x = x_ref[r, :] + m[2:3] * (_rms(out) * g_ref[...])
        xo_ref[r, :] = x
        h2_ref[r, :] = (_rms(x) * gffn_ref[...] * (1.0 + m[4:5]) + m[3:4]).astype(BF16)


def _outproj_call(mixed, x, mod, g, g_ffn, w, seg, name):
    n_tok = x.shape[0]
    tm = 512
    return pl.pallas_call(
        _outproj_kernel,
        grid=(n_tok // tm,),
        in_specs=[
            pl.BlockSpec((tm, D_MODEL), lambda i: (i, 0)),
            pl.BlockSpec((tm, D_MODEL), lambda i: (i, 0)),
            pl.BlockSpec((1, 6, D_MODEL), lambda i: (seg(i, tm), 0, 0)),
            pl.BlockSpec((1, D_MODEL), lambda i: (0, 0)),
            pl.BlockSpec((1, D_MODEL), lambda i: (0, 0)),
            pl.BlockSpec((D_MODEL, D_MODEL), lambda i: (0, 0)),
        ],
        out_specs=[pl.BlockSpec((tm, D_MODEL), lambda i: (i, 0)), pl.BlockSpec((tm, D_MODEL), lambda i: (i, 0))],
        out_shape=[jax.ShapeDtypeStruct((n_tok, D_MODEL), F32), jax.ShapeDtypeStruct((n_tok, D_MODEL), BF16)],
        compiler_params=pltpu.CompilerParams(
            dimension_semantics=("arbitrary",), vmem_limit_bytes=VMEM_LIMIT),
        name=name,
    )(mixed, x, mod, g, g_ffn, w)


def _ffn_kernel(x_ref, h2_ref, mod_ref, gpost_ref, wg_ref, wu_ref, wd_ref, xo_ref):
    j = pl.program_id(1)

    @pl.when(j == 0)
    def _():
        xo_ref[...] = jnp.zeros_like(xo_ref)

    a = _dot(h2_ref[...], wg_ref[...])
    b = _dot(h2_ref[...], wu_ref[...])
    t = (a * jax.nn.sigmoid(a) * b).astype(BF16)
    xo_ref[...] += _dot(t, wd_ref[...])

    @pl.when(j == pl.num_programs(1) - 1)
    def _():
        m = mod_ref[0]
        xo_ref[...] = x_ref[...] + m[5:6] * (_rms(xo_ref[...]) * gpost_ref[...])


def _ffn_call(x, h2, mod, gpost, wg, wu, wd, seg, name):
    n_tok = x.shape[0]
    tm, th = 1024, 512
    return pl.pallas_call(
        _ffn_kernel,
        grid=(n_tok // tm, FFN_HIDDEN // th),
        in_specs=[
            pl.BlockSpec((tm, D_MODEL), lambda i, j: (i, 0), pipeline_mode=pl.Buffered(1)),
            pl.BlockSpec((tm, D_MODEL), lambda i, j: (i, 0)),
            pl.BlockSpec((1, 6, D_MODEL), lambda i, j: (seg(i, tm), 0, 0)),
            pl.BlockSpec((1, D_MODEL), lambda i, j: (0, 0)),
            pl.BlockSpec((D_MODEL, th), lambda i, j: (0, j)),
            pl.BlockSpec((D_MODEL, th), lambda i, j: (0, j)),
            pl.BlockSpec((th, D_MODEL), lambda i, j: (j, 0)),
        ],
        out_specs=pl.BlockSpec((tm, D_MODEL), lambda i, j: (i, 0)),
        out_shape=jax.ShapeDtypeStruct((n_tok, D_MODEL), F32),
        compiler_params=pltpu.CompilerParams(
            dimension_semantics=("arbitrary", "arbitrary"), vmem_limit_bytes=VMEM_LIMIT_BIG),
        name=name,
    )(x, h2, mod, gpost, wg, wu, wd)


def _rope_tables():
    t = np.arange(DEC_SEQ)
    row = (t // GRID_W).astype(np.float32)
    col = (t % GRID_W).astype(np.float32)

    def half_angles(r):
        quarter = r // 4
        inv = (1.0 / (ROPE_BASE ** (np.arange(quarter, dtype=np.float32) / quarter))).astype(np.float32)
        return np.concatenate([row[:, None] * inv, col[:, None] * inv], axis=-1).astype(np.float32)

    a64 = half_angles(128)
    cos128 = np.concatenate([np.cos(a64), np.cos(a64)], axis=1)
    sin128 = np.concatenate([-np.sin(a64), np.sin(a64)], axis=1)
    a32 = half_angles(64)
    c, s, z = np.cos(a32), np.sin(a32), np.zeros_like(a32)
    cos64 = np.concatenate([c, c, c, c], axis=1)
    sina64 = np.concatenate([-s, z, -s, z], axis=1)
    sinb64 = np.concatenate([z, s, z, s], axis=1)
    return tuple(jnp.asarray(a, F32) for a in (cos128, sin128, cos64, sina64, sinb64))


def _ctx_seg(i, tm):
    del tm
    return i * 0


def _lat_seg(i, tm):
    return 1 + (i * tm) // DEC_SEQ


def kernel(x_prompt, x_sample, cache_mla_ckv, cache_mla_krope, cache_swa_k, cache_swa_v, cache_na_k, cache_na_v, cache_diff_k, cache_diff_v, c, c_ctx, w_ada, b_ada, mix_pre_g, mix_post_g, ffn_pre_g, ffn_post_g, w_in, mla_q_norm_g, mla_kv_norm_g, w_mla_uq, w_mla_ukv, swa_sink, na_rpb, diff_lq1, diff_lk1, diff_lq2, diff_lk2, diff_subln_g, w_mix_gate, w_branch, w_out, w_ffn_gate, w_ffn_up, w_ffn_down):
    cvec = jnp.concatenate([c_ctx[None, :], c, jnp.zeros((N_MOD_ROWS - 1 - DEC_BATCH, D_MODEL), F32)], axis=0)
    mod_all = _ada_call(cvec, w_ada, b_ada).reshape(DEPTH, N_MOD_ROWS, 6, D_MODEL)
    bias_all = _nbr_bias_call(na_rpb)
    tables = _rope_tables()
    caches = (cache_mla_ckv, cache_mla_krope, cache_swa_k, cache_swa_v, cache_na_k, cache_na_v,
              cache_diff_k, cache_diff_v)

    y = x_prompt.reshape(N_CTX_TOK, D_MODEL)
    z = x_sample.reshape(N_LAT_TOK, D_MODEL)
    new_caches = ()
    for l in range(DEPTH):
        lam_init = _lambda_init(l)
        mod = mod_all[l]
        wi = w_in[l]
        w_main = jnp.concatenate([wi[:, :KROPE_SRC0], wi[:, KROPE_SRC0 + MLA_ROPE:]], axis=1).astype(BF16)
        w_kr = jnp.pad(wi[:, KROPE_SRC0:KROPE_SRC0 + MLA_ROPE], ((0, 0), (0, LANES - MLA_ROPE))).astype(BF16)
        uq = w_mla_uq[l].reshape(MLA_Q_RANK, MLA_HEADS, MLA_NOPE + MLA_ROPE)
        uq_rope = jnp.pad(uq[:, :, MLA_NOPE:], ((0, 0), (0, 0), (0, LANES - MLA_ROPE)))
        w_uq = jnp.concatenate([uq[:, :, :MLA_NOPE].reshape(MLA_Q_RANK, 512),
                                uq_rope.reshape(MLA_Q_RANK, 512)], axis=1).astype(BF16)
        small = (mla_q_norm_g[l][None, :], w_uq, mla_kv_norm_g[l][None, :], w_mla_ukv[l].astype(BF16),
                 swa_sink[l], diff_lq1[l][None, :], diff_lk1[l][None, :], diff_lq2[l][None, :],
                 diff_lk2[l][None, :], diff_subln_g[l][None, :])
        wg = w_mix_gate[l].astype(BF16)
        wb = w_branch[l].astype(BF16)
        wo = w_out[l].astype(BF16)
        wfg = w_ffn_gate[l].astype(BF16)
        wfu = w_ffn_up[l].astype(BF16)
        wfd = w_ffn_down[l].astype(BF16)
        g_pre, g_post = mix_pre_g[l][None, :], mix_post_g[l][None, :]
        f_pre, f_post = ffn_pre_g[l][None, :], ffn_post_g[l][None, :]

        h, p, kr = _inproj_call(y, mod, g_pre, w_main, w_kr, _ctx_seg, F32, "ctx_inproj")
        o, *new_caches = _ctx_attn_call(p, kr, small, new_caches, lam_init)
        mixed = _merge_call(h, o, wg, wb, "ctx_merge")
        y, h2 = _outproj_call(mixed, y, mod, g_post, f_pre, wo, _ctx_seg, "ctx_outproj")
        y = _ffn_call(y, h2, mod, f_post, wfg, wfu, wfd, _ctx_seg, "ctx_ffn")

        h, p, kr = _inproj_call(z, mod, g_pre, w_main, w_kr, _lat_seg, BF16, "lat_inproj")
        o = _lat_attn_call(p, kr, caches, tables, bias_all, small, l, lam_init)
        mixed = _merge_call(h, o, wg, wb, "lat_merge")
        z, h2 = _outproj_call(mixed, z, mod, g_post, f_pre, wo, _lat_seg, "lat_outproj")
        z = _ffn_call(z, h2, mod, f_post, wfg, wfu, wfd, _lat_seg, "lat_ffn")

    new_caches = [a.reshape((BATCH, DEPTH, SEQ) + t) for a, t in zip(new_caches, CACHE_TAILS)]
    return (y.reshape(BATCH, SEQ, D_MODEL), z.reshape(DEC_BATCH, DEC_SEQ, D_MODEL), *new_caches)
```

```python
import functools
import math

import numpy as np
import jax
import jax.numpy as jnp
from jax import lax
from jax.experimental import pallas as pl
from jax.experimental.pallas import tpu as pltpu

D_MODEL = 2048
BATCH = 32
SEQ = 256
DEPTH = 2
DEC_BATCH = 8
DEC_SEQ = 1024
PAST_LEN = 512
GRID_W = 64
HEAD_DIM = 128
ROPE_BASE = 10000.0
EPS = 1e-6
NEG_INF = -1e30
LOG2E = math.log2(math.e)

MLA_HEADS = 4
MLA_Q_RANK = 384
MLA_KV_RANK = 128
MLA_NOPE = 128
MLA_ROPE = 64
MLA_V = 128
SWA_HEADS = 4
SWA_KV_HEADS = 2
SWA_WINDOW = 128
NA_HEADS = 4
NA_ROWS = 8
NA_COLS = 16
DIFF_HEADS = 4
DIFF_QK = 64
DIFF_V = 128
N_BRANCH = 4
BRANCH_W = 512
FFN_HIDDEN = 5632

N_CTX_TOK = BATCH * SEQ
N_LAT_TOK = DEC_BATCH * DEC_SEQ
GRID_ROWS = DEC_SEQ // GRID_W

Q_DOWN0, CKV0, QB0, KB0, VB0, QC0, KC0, VC0, QD0, KD0, VD0 = (
    0, 384, 512, 1024, 1280, 1536, 2048, 2560, 3072, 3584, 4096)
KROPE_SRC0 = MLA_Q_RANK + MLA_KV_RANK
P_WIDTH = 4608
N_MOD_ROWS = 16

CACHE_TAILS = ((MLA_KV_RANK,), (MLA_ROPE,), (SWA_KV_HEADS, HEAD_DIM), (SWA_KV_HEADS, HEAD_DIM),
               (NA_HEADS, HEAD_DIM), (NA_HEADS, HEAD_DIM), (DIFF_HEADS, 2 * DIFF_QK), (DIFF_HEADS, DIFF_V))

VMEM_LIMIT = 56 * 1024 * 1024
VMEM_LIMIT_BIG = 60 * 1024 * 1024
LANES = 128

F32 = jnp.float32
BF16 = jnp.bfloat16


def _lambda_init(l):
    return 0.8 - 0.6 * math.exp(-0.3 * l)


def _rms(x):
    return x * lax.rsqrt(jnp.mean(x * x, axis=-1, keepdims=True) + EPS)


def _dot(a, b):
    return jnp.dot(a, b, preferred_element_type=F32)


def _dot_nt(a, b):
    return lax.dot_general(a, b, (((1,), (1,)), ((), ())), preferred_element_type=F32)


def _attend(pieces, sink=None):
    m = None
    for s, _ in pieces:
        mi = jnp.max(s, axis=-1, keepdims=True)
        m = mi if m is None else jnp.maximum(m, mi)
    if sink is not None:
        sink = sink * LOG2E
        m = jnp.maximum(m, sink)
    den = None
    num = None
    for s, v in pieces:
        e = jnp.exp2(s - m)
        d = jnp.sum(e, axis=-1, keepdims=True)
        n = _dot(e.astype(BF16), v)
        den = d if den is None else den + d
        num = n if num is None else num + n
    if sink is not None:
        den = den + jnp.exp2(sink - m)
    return num / den


def _diff_lambda(lq1, lk1, lq2, lk2, lam_init):
    a = jnp.sum(lq1[...] * lk1[...], axis=-1, keepdims=True)
    b = jnp.sum(lq2[...] * lk2[...], axis=-1, keepdims=True)
    return jnp.exp(a) - jnp.exp(b) + lam_init


def _ada_kernel(c_ref, w_ref, b_ref, o_ref):
    c = c_ref[...]
    s = (c * jax.nn.sigmoid(c)).astype(BF16)
    o_ref[0] = _dot(s, w_ref[0].astype(BF16)) + b_ref[0]


def _ada_call(cvec, w_ada, b_ada):
    tn = 1024
    n = 6 * D_MODEL
    return pl.pallas_call(
        _ada_kernel,
        grid=(DEPTH, n // tn),
        in_specs=[
            pl.BlockSpec((N_MOD_ROWS, D_MODEL), lambda l, j: (0, 0)),
            pl.BlockSpec((1, D_MODEL, tn), lambda l, j: (l, 0, j)),
            pl.BlockSpec((1, 1, tn), lambda l, j: (l, 0, j)),
        ],
        out_specs=pl.BlockSpec((1, N_MOD_ROWS, tn), lambda l, j: (l, 0, j)),
        out_shape=jax.ShapeDtypeStruct((DEPTH, N_MOD_ROWS, n), F32),
        compiler_params=pltpu.CompilerParams(
            dimension_semantics=("arbitrary", "arbitrary"), vmem_limit_bytes=VMEM_LIMIT),
        name="ada_mod",
    )(cvec, w_ada, b_ada.reshape(DEPTH, 1, n))


def _nbr_bias_kernel(rpb_ref, o_ref):
    l = pl.program_id(0)
    h = pl.program_id(1)
    n_dr = 2 * NA_ROWS - 1
    n_dc = 2 * NA_COLS - 1
    base = (l * NA_HEADS + h) * (n_dr * n_dc)
    wq = lax.broadcasted_iota(jnp.int32, (GRID_W, LANES), 0)
    lane = lax.broadcasted_iota(jnp.int32, (GRID_W, LANES), 1)
    upper = lane >= GRID_W
    wk = jnp.where(upper, lane - GRID_W, lane)
    delta = wk - wq + (NA_COLS - 1)
    cs = jnp.clip(wq - NA_COLS // 2, 0, GRID_W - NA_COLS)
    valid = (wk >= cs) & (wk < cs + NA_COLS)
    for dr in range(n_dr - 1):
        acc = jnp.full((GRID_W, LANES), NEG_INF, F32)
        for d in range(n_dc):
            lo = rpb_ref[base + dr * n_dc + d]
            hi = rpb_ref[base + (dr + 1) * n_dc + d]
            acc = jnp.where(delta == d, jnp.where(upper, hi, lo), acc)
        o_ref[0, 0, dr] = jnp.where(valid, acc * LOG2E, NEG_INF)


def _nbr_bias_call(na_rpb):
    n_pairs = 2 * NA_ROWS - 2
    return pl.pallas_call(
        _nbr_bias_kernel,
        grid=(DEPTH, NA_HEADS),
        in_specs=[pl.BlockSpec(memory_space=pltpu.SMEM)],
        out_specs=pl.BlockSpec((1, 1, n_pairs, GRID_W, LANES), lambda l, h: (l, h, 0, 0, 0)),
        out_shape=jax.ShapeDtypeStruct((DEPTH, NA_HEADS, n_pairs, GRID_W, LANES), F32),
        compiler_params=pltpu.CompilerParams(dimension_semantics=("arbitrary", "arbitrary")),
        name="nbr_bias",
    )(na_rpb.reshape(-1))


def _inproj_kernel(x_ref, mod_ref, g_ref, w_ref, wkr_ref, h_ref, p_ref, kr_ref):
    m = mod_ref[0]
    half = x_ref.shape[0] // 2
    for r in (slice(0, half), slice(half, 2 * half)):
        h = (_rms(x_ref[r, :]) * g_ref[...] * (1.0 + m[1:2]) + m[0:1]).astype(BF16)
        h_ref[r, :] = h
        kr_ref[r, :] = _dot(h, wkr_ref[...])
        p_ref[r, :] = _dot(h, w_ref[...]).astype(p_ref.dtype)


def _inproj_call(x, mod, g, w_main, w_kr, layer, seg, p_dtype, name):
    n_tok = x.shape[0]
    tm = 512
    return pl.pallas_call(
        _inproj_kernel,
        grid=(n_tok // tm,),
        in_specs=[
            pl.BlockSpec((tm, D_MODEL), lambda i: (i, 0)),
            pl.BlockSpec((1, 6, D_MODEL), lambda i: (seg(i, tm), 0, 0)),
            pl.BlockSpec((1, D_MODEL), lambda i: (0, 0)),
            pl.BlockSpec((None, D_MODEL, P_WIDTH), lambda i: (layer, 0, 0)),
            pl.BlockSpec((None, D_MODEL, LANES), lambda i: (layer, 0, 0)),
        ],
        out_specs=[
            pl.BlockSpec((tm, D_MODEL), lambda i: (i, 0)),
            pl.BlockSpec((tm, P_WIDTH), lambda i: (i, 0)),
            pl.BlockSpec((tm, LANES), lambda i: (i, 0)),
        ],
        out_shape=[
            jax.ShapeDtypeStruct((n_tok, D_MODEL), BF16),
            jax.ShapeDtypeStruct((n_tok, P_WIDTH), p_dtype),
            jax.ShapeDtypeStruct((n_tok, LANES), F32),
        ],
        compiler_params=pltpu.CompilerParams(
            dimension_semantics=("arbitrary",), vmem_limit_bytes=VMEM_LIMIT_BIG),
        name=name,
    )(x, mod, g, w_main, w_kr)


def _cast_kernel(x_ref, o_ref):
    o_ref[...] = x_ref[...].astype(BF16)


def _cast_call(w, block_rows, name):
    w2 = w.reshape(-1, w.shape[-1])
    rows, cols = w2.shape
    out = pl.pallas_call(
        _cast_kernel,
        grid=(rows // block_rows,),
        in_specs=[pl.BlockSpec((block_rows, cols), lambda i: (i, 0))],
        out_specs=pl.BlockSpec((block_rows, cols), lambda i: (i, 0)),
        out_shape=jax.ShapeDtypeStruct((rows, cols), BF16),
        compiler_params=pltpu.CompilerParams(dimension_semantics=("arbitrary",), vmem_limit_bytes=VMEM_LIMIT),
        name=name,
    )(w2)
    return out.reshape(w.shape)


def _cast_w_in_kernel(x_ref, main_ref, kr_ref):
    main_ref[:, :KROPE_SRC0] = x_ref[:, :KROPE_SRC0].astype(BF16)
    tail = x_ref[:, KROPE_SRC0:]
    main_ref[:, KROPE_SRC0:] = tail[:, MLA_ROPE:].astype(BF16)
    kr = tail[:, :LANES]
    keep = lax.broadcasted_iota(jnp.int32, kr.shape, 1) < MLA_ROPE
    kr_ref[...] = jnp.where(keep, kr, 0.0).astype(BF16)


def _cast_w_in_call(w_in):
    rows = DEPTH * D_MODEL
    cols = w_in.shape[-1]
    block_rows = 256
    main, kr = pl.pallas_call(
        _cast_w_in_kernel,
        grid=(rows // block_rows,),
        in_specs=[pl.BlockSpec((block_rows, cols), lambda i: (i, 0))],
        out_specs=[pl.BlockSpec((block_rows, P_WIDTH), lambda i: (i, 0)),
                   pl.BlockSpec((block_rows, LANES), lambda i: (i, 0))],
        out_shape=[jax.ShapeDtypeStruct((rows, P_WIDTH), BF16), jax.ShapeDtypeStruct((rows, LANES), BF16)],
        compiler_params=pltpu.CompilerParams(dimension_semantics=("arbitrary",), vmem_limit_bytes=VMEM_LIMIT),
        name="cast_w_in",
    )(w_in.reshape(rows, cols))
    return main.reshape(DEPTH, D_MODEL, P_WIDTH), kr.reshape(DEPTH, D_MODEL, LANES)


def _mla_queries(q_down, qn_g, w_uq):
    cq = _rms(q_down) * qn_g[...]
    return _dot(cq.astype(BF16), w_uq[...])


def _ctx_attn_kernel(*refs, lam_init, n_prev):
    (p_ref, kr_ref, qn_g, w_uq, kvn_g, w_ukv, sink_ref, lq1, lk1, lq2, lk2, subln_g) = refs[:12]
    n_in = 12 + (len(CACHE_TAILS) if n_prev else 0)
    prev = refs[12:n_in]
    o_ref = refs[n_in]
    ckv_o, kr_o, kb_o, vb_o, kc_o, vc_o, kd_o, vd_o = refs[n_in + 1:]
    for src, dst in zip(prev, (ckv_o, kr_o, kb_o, vb_o, kc_o, vc_o, kd_o, vd_o)):
        dst[0, 0:n_prev] = src[0]
    ls = n_prev

    scale_a = LOG2E * (MLA_NOPE + MLA_ROPE) ** -0.5
    qa = _mla_queries(p_ref[:, Q_DOWN0:Q_DOWN0 + MLA_Q_RANK], qn_g, w_uq) * scale_a
    ckv = _rms(p_ref[:, CKV0:CKV0 + MLA_KV_RANK]) * kvn_g[...]
    ckv_o[0, ls] = ckv
    kr = kr_ref[...]
    kr_o[0, ls] = kr[:, :MLA_ROPE]
    kr_b = kr.astype(BF16)
    kv = _dot(ckv.astype(BF16), w_ukv[...])
    for h in range(MLA_HEADS):
        q = jnp.concatenate([qa[:, h * 128:(h + 1) * 128], qa[:, 512 + h * 128:512 + (h + 1) * 128]],
                            axis=1).astype(BF16)
        k = jnp.concatenate([kv[:, h * 256:h * 256 + 128].astype(BF16), kr_b], axis=1)
        v = kv[:, h * 256 + 128:(h + 1) * 256].astype(BF16)
        o = _attend([(_dot_nt(q, k), v)])
        o_ref[:, h * 128:(h + 1) * 128] = o.astype(BF16)

    scale = LOG2E * HEAD_DIM ** -0.5

    def head(col0, h):
        return p_ref[:, col0 + h * 128:col0 + (h + 1) * 128]

    g_size = SWA_HEADS // SWA_KV_HEADS
    for g in range(SWA_KV_HEADS):
        k = head(KB0, g)
        v = head(VB0, g)
        kb_o[0, ls, pl.ds(g, SEQ, stride=SWA_KV_HEADS), :] = k
        vb_o[0, ls, pl.ds(g, SEQ, stride=SWA_KV_HEADS), :] = v
        k = k.astype(BF16)
        v = v.astype(BF16)
        for h in range(g * g_size, (g + 1) * g_size):
            q = (head(QB0, h) * scale).astype(BF16)
            o = _attend([(_dot_nt(q, k), v)], sink=sink_ref[h])
            o_ref[:, 512 + h * 128:512 + (h + 1) * 128] = o.astype(BF16)

    for h in range(NA_HEADS):
        k = head(KC0, h)
        v = head(VC0, h)
        kc_o[0, ls, pl.ds(h, SEQ, stride=NA_HEADS), :] = k
        vc_o[0, ls, pl.ds(h, SEQ, stride=NA_HEADS), :] = v
        q = (head(QC0, h) * scale).astype(BF16)
        o = _attend([(_dot_nt(q, k.astype(BF16)), v.astype(BF16))])
        o_ref[:, 1024 + h * 128:1024 + (h + 1) * 128] = o.astype(BF16)

    lam = _diff_lambda(lq1, lk1, lq2, lk2, lam_init)
    scale_d = LOG2E * DIFF_QK ** -0.5
    lo = lax.broadcasted_iota(jnp.int32, (SEQ, 128), 1) < DIFF_QK
    for h in range(DIFF_HEADS):
        k = head(KD0, h)
        v = head(VD0, h)
        kd_o[0, ls, pl.ds(h, SEQ, stride=DIFF_HEADS), :] = k
        vd_o[0, ls, pl.ds(h, SEQ, stride=DIFF_HEADS), :] = v
        k = k.astype(BF16)
        v = v.astype(BF16)
        q = head(QD0, h) * scale_d
        o1 = _attend([(_dot_nt(jnp.where(lo, q, 0.0).astype(BF16), k), v)])
        o2 = _attend([(_dot_nt(jnp.where(lo, 0.0, q).astype(BF16), k), v)])
        o = _rms(o1 - lam * o2) * subln_g[...] * (1.0 - lam_init)
        o_ref[:, 1536 + h * 128:1536 + (h + 1) * 128] = o.astype(BF16)


def _small_specs():
    zero2 = (lambda *a: (0, 0))
    return [
        pl.BlockSpec((1, MLA_Q_RANK), zero2),
        pl.BlockSpec((MLA_Q_RANK, 1024), zero2),
        pl.BlockSpec((1, MLA_KV_RANK), zero2),
        pl.BlockSpec((MLA_KV_RANK, 1024), zero2),
        pl.BlockSpec(memory_space=pltpu.SMEM),
        pl.BlockSpec((1, DIFF_QK), zero2),
        pl.BlockSpec((1, DIFF_QK), zero2),
        pl.BlockSpec((1, DIFF_QK), zero2),
        pl.BlockSpec((1, DIFF_QK), zero2),
        pl.BlockSpec((1, DIFF_V), zero2),
    ]


def _ctx_attn_call(p, kr, small, prev, lam_init):
    n_prev = prev[0].shape[1] if prev else 0
    row = lambda b: (b, 0)

    def flat(tail):
        return (SEQ * tail[0], tail[1]) if len(tail) == 2 else (SEQ,) + tail

    def cache_spec(n_layers, tail):
        return pl.BlockSpec((1, n_layers) + flat(tail), lambda b: (b, 0, 0, 0))

    return pl.pallas_call(
        functools.partial(_ctx_attn_kernel, lam_init=lam_init, n_prev=n_prev),
        grid=(BATCH,),
        in_specs=[pl.BlockSpec((SEQ, P_WIDTH), row), pl.BlockSpec((SEQ, LANES), row)] + _small_specs()
        + [cache_spec(n_prev, t) for t in (CACHE_TAILS if prev else ())],
        out_specs=[pl.BlockSpec((SEQ, D_MODEL), row)] + [cache_spec(n_prev + 1, t) for t in CACHE_TAILS],
        out_shape=[jax.ShapeDtypeStruct((N_CTX_TOK, D_MODEL), BF16)]
        + [jax.ShapeDtypeStruct((BATCH, n_prev + 1) + flat(t), F32) for t in CACHE_TAILS],
        compiler_params=pltpu.CompilerParams(
            dimension_semantics=("arbitrary",), vmem_limit_bytes=VMEM_LIMIT),
        name="ctx_attn",
    )(p, kr, *small, *prev)


LAT_TQ = 256
SWA_PAD = SWA_WINDOW


def _rope_pair64(x, c_ref, sa_ref, sb_ref, rows):
    out = []
    for j in range(x.shape[1] // LANES):
        xj = x[:, j * LANES:(j + 1) * LANES]
        out.append(xj * c_ref[rows, :] + pltpu.roll(xj, 96, 1) * sa_ref[rows, :]
                   + pltpu.roll(xj, 32, 1) * sb_ref[rows, :])
    return jnp.concatenate(out, axis=1) if len(out) > 1 else out[0]


def _rope128(x, c_ref, s_ref, rows):
    out = []
    for j in range(x.shape[1] // LANES):
        xj = x[:, j * LANES:(j + 1) * LANES]
        out.append(xj * c_ref[rows, :] + pltpu.roll(xj, 64, 1) * s_ref[rows, :])
    return jnp.concatenate(out, axis=1) if len(out) > 1 else out[0]


def _lat_attn_kernel(p_ref, kr_ref, c_ckv, c_kr, c_kb, c_vb, c_kc, c_vc, c_kd, c_vd,
                     cos128, sin128, cos64, sina64, sinb64, bias_ref,
                     qn_g, w_uq, kvn_g, w_ukv, sink_ref, lq1, lk1, lq2, lk2, subln_g,
                     o_ref,
                     ka_s, va_s, kbc_s, vbc_s, kbp_s, vbp_s, kcc_s, vcc_s, kdc_s, vdc_s, kdo_s,
                     *, lam_init):
    def cached_head(ref, h, n_heads):
        return ref[pl.ds(h, PAST_LEN, stride=n_heads), :].astype(BF16)

    qi = pl.program_id(1)
    all_rows = slice(None)
    n_own = DEC_SEQ

    @pl.when(qi == 0)
    def _prologue():
        ckv_own = (_rms(p_ref[:, CKV0:CKV0 + MLA_KV_RANK].astype(F32)) * kvn_g[...]).astype(BF16)
        ckv_all = jnp.concatenate([c_ckv[...].astype(BF16), ckv_own], axis=0)
        kr_ctx = jnp.concatenate([c_kr[...], jnp.zeros((PAST_LEN, LANES - MLA_ROPE), F32)], axis=1)
        kr_own = _rope_pair64(kr_ref[...], cos64, sina64, sinb64, all_rows)
        kr_all = jnp.concatenate([kr_ctx, kr_own], axis=0).astype(BF16)
        for h in range(MLA_HEADS):
            kvh = _dot(ckv_all, w_ukv[:, h * 256:(h + 1) * 256])
            ka_s[h, :, 0:128] = kvh[:, 0:128].astype(BF16)
            ka_s[h, :, 128:256] = kr_all
            va_s[:, h * 128:(h + 1) * 128] = kvh[:, 128:256].astype(BF16)
        for g in range(SWA_KV_HEADS):
            kbc_s[:, g * 128:(g + 1) * 128] = cached_head(c_kb, g, SWA_KV_HEADS)
            vbc_s[:, g * 128:(g + 1) * 128] = cached_head(c_vb, g, SWA_KV_HEADS)
        zpad = jnp.zeros((SWA_PAD, 256), BF16)
        kbp_s[0:SWA_PAD, :] = zpad
        kbp_s[SWA_PAD + n_own:, :] = zpad
        vbp_s[0:SWA_PAD, :] = zpad
        vbp_s[SWA_PAD + n_own:, :] = zpad
        kbp_s[SWA_PAD:SWA_PAD + n_own, :] = _rope128(
            p_ref[:, KB0:KB0 + 256].astype(F32), cos128, sin128, all_rows).astype(BF16)
        vbp_s[SWA_PAD:SWA_PAD + n_own, :] = p_ref[:, VB0:VB0 + 256]
        for h in range(NA_HEADS):
            cols = slice(h * 128, (h + 1) * 128)
            kcc_s[:, cols] = cached_head(c_kc, h, NA_HEADS)
            vcc_s[:, cols] = cached_head(c_vc, h, NA_HEADS)
            kdc_s[:, cols] = cached_head(c_kd, h, DIFF_HEADS)
            vdc_s[:, cols] = cached_head(c_vd, h, DIFF_HEADS)
        kdo_s[...] = _rope_pair64(
            p_ref[:, KD0:KD0 + 512].astype(F32), cos64, sina64, sinb64, all_rows).astype(BF16)

    q0 = pl.multiple_of(qi * LAT_TQ, LAT_TQ)
    rows = pl.ds(q0, LAT_TQ)

    scale_a = LOG2E * (MLA_NOPE + MLA_ROPE) ** -0.5
    qa = _mla_queries(p_ref[rows, Q_DOWN0:Q_DOWN0 + MLA_Q_RANK].astype(F32), qn_g, w_uq) * scale_a
    qa_r = _rope_pair64(qa[:, 512:1024], cos64, sina64, sinb64, rows)
    for h in range(MLA_HEADS):
        q = jnp.concatenate([qa[:, h * 128:(h + 1) * 128], qa_r[:, h * 128:(h + 1) * 128]],
                            axis=1).astype(BF16)
        o = _attend([(_dot_nt(q, ka_s[h]), va_s[:, h * 128:(h + 1) * 128])])
        o_ref[:, h * 128:(h + 1) * 128] = o.astype(BF16)

    scale = LOG2E * HEAD_DIM ** -0.5
    qb = (_rope128(p_ref[rows, QB0:QB0 + 512].astype(F32), cos128, sin128, rows) * scale).astype(BF16)
    n_loc = LAT_TQ + 2 * SWA_PAD
    qpos = lax.broadcasted_iota(jnp.int32, (LAT_TQ, n_loc), 0)
    kpos = lax.broadcasted_iota(jnp.int32, (LAT_TQ, n_loc), 1) - SWA_PAD
    kabs = kpos + q0
    valid = (jnp.abs(qpos - kpos) <= SWA_WINDOW) & (kabs >= 0) & (kabs < n_own)
    loc_rows = pl.ds(q0, n_loc)
    g_size = SWA_HEADS // SWA_KV_HEADS
    for h in range(SWA_HEADS):
        g = h // g_size
        q = qb[:, h * 128:(h + 1) * 128]
        s_ctx = _dot_nt(q, kbc_s[:, g * 128:(g + 1) * 128])
        s_loc = jnp.where(valid, _dot_nt(q, kbp_s[loc_rows, g * 128:(g + 1) * 128]), NEG_INF)
        o = _attend([(s_ctx, vbc_s[:, g * 128:(g + 1) * 128]),
                     (s_loc, vbp_s[loc_rows, g * 128:(g + 1) * 128])], sink=sink_ref[h])
        o_ref[:, 512 + h * 128:512 + (h + 1) * 128] = o.astype(BF16)

    n_win = NA_ROWS * GRID_W
    for t in range(LAT_TQ // GRID_W):
        r = qi * (LAT_TQ // GRID_W) + t
        rs = jnp.clip(r - NA_ROWS // 2, 0, GRID_ROWS - NA_ROWS)
        off = rs - r + (NA_ROWS - 1)
        krows = pl.ds(pl.multiple_of(rs * GRID_W, GRID_W), n_win)
        qrows = pl.ds(pl.multiple_of(q0 + t * GRID_W, GRID_W), GRID_W)
        for h in range(NA_HEADS):
            q = (p_ref[qrows, QC0 + h * 128:QC0 + (h + 1) * 128].astype(F32) * scale).astype(BF16)
            bias = jnp.concatenate([bias_ref[0, h, off + 2 * j] for j in range(NA_ROWS // 2)], axis=1)
            s_loc = _dot_nt(q, p_ref[krows, KC0 + h * 128:KC0 + (h + 1) * 128]) + bias
            s_ctx = _dot_nt(q, kcc_s[:, h * 128:(h + 1) * 128])
            o = _attend([(s_ctx, vcc_s[:, h * 128:(h + 1) * 128]),
                         (s_loc, p_ref[krows, VC0 + h * 128:VC0 + (h + 1) * 128])])
            o_ref[t * GRID_W:(t + 1) * GRID_W, 1024 + h * 128:1024 + (h + 1) * 128] = o.astype(BF16)

    scale_d = LOG2E * DIFF_QK ** -0.5
    qd = _rope_pair64(p_ref[rows, QD0:QD0 + 512].astype(F32), cos64, sina64, sinb64, rows) * scale_d
    lam = _diff_lambda(lq1, lk1, lq2, lk2, lam_init)
    lo = lax.broadcasted_iota(jnp.int32, (LAT_TQ, 128), 1) < DIFF_QK
    for h in range(DIFF_HEADS):
        cols = slice(h * 128, (h + 1) * 128)
        q = qd[:, cols]
        v_own = p_ref[:, VD0 + h * 128:VD0 + (h + 1) * 128]
        outs = []
        for qq in (jnp.where(lo, q, 0.0).astype(BF16), jnp.where(lo, 0.0, q).astype(BF16)):
            outs.append(_attend([(_dot_nt(qq, kdc_s[:, cols]), vdc_s[:, cols]),
                                 (_dot_nt(qq, kdo_s[:, cols]), v_own)]))
        o = _rms(outs[0] - lam * outs[1]) * subln_g[...] * (1.0 - lam_init)
        o_ref[:, 1536 + h * 128:1536 + (h + 1) * 128] = o.astype(BF16)


def _lat_attn_call(p, kr, caches, tables, bias, small, layer, lam_init):
    nq = DEC_SEQ // LAT_TQ
    batch_rows = lambda b, q: (b, 0)
    zero2 = lambda b, q: (0, 0)
    once = pl.Buffered(1)
    cache_args = [a.reshape(a.shape[:2] + (-1, a.shape[-1])) for a in caches]
    cache_specs = [pl.BlockSpec((None, None) + a.shape[2:], lambda b, q: (b, layer, 0, 0), pipeline_mode=once)
                   for a in cache_args]
    table_specs = [pl.BlockSpec((DEC_SEQ, LANES), zero2) for _ in tables]
    n_pairs = 2 * NA_ROWS - 2
    bias_spec = pl.BlockSpec((1, NA_HEADS, n_pairs, GRID_W, LANES), lambda b, q: (layer, 0, 0, 0, 0))
    n_all = PAST_LEN + DEC_SEQ
    n_pad = DEC_SEQ + 2 * SWA_PAD
    scratch = [
        pltpu.VMEM((MLA_HEADS, n_all, 256), BF16),
        pltpu.VMEM((n_all, 512), BF16),
        pltpu.VMEM((PAST_LEN, 256), BF16), pltpu.VMEM((PAST_LEN, 256), BF16),
        pltpu.VMEM((n_pad, 256), BF16), pltpu.VMEM((n_pad, 256), BF16),
        pltpu.VMEM((PAST_LEN, 512), BF16), pltpu.VMEM((PAST_LEN, 512), BF16),
        pltpu.VMEM((PAST_LEN, 512), BF16), pltpu.VMEM((PAST_LEN, 512), BF16),
        pltpu.VMEM((DEC_SEQ, 512), BF16),
    ]
    return pl.pallas_call(
        functools.partial(_lat_attn_kernel, lam_init=lam_init),
        grid=(DEC_BATCH, nq),
        in_specs=[pl.BlockSpec((DEC_SEQ, P_WIDTH), batch_rows, pipeline_mode=once),
                  pl.BlockSpec((DEC_SEQ, LANES), batch_rows, pipeline_mode=once)]
        + cache_specs + table_specs + [bias_spec] + _small_specs(),
        out_specs=pl.BlockSpec((LAT_TQ, D_MODEL), lambda b, q: (b * nq + q, 0)),
        out_shape=jax.ShapeDtypeStruct((N_LAT_TOK, D_MODEL), BF16),
        scratch_shapes=scratch,
        compiler_params=pltpu.CompilerParams(
            dimension_semantics=("arbitrary", "arbitrary"), vmem_limit_bytes=VMEM_LIMIT_BIG),
        name="lat_attn",
    )(p, kr, *cache_args, *tables, bias, *small)


def _merge_kernel(h_ref, o_ref, wg_ref, wb_ref, out_ref):
    acc = None
    for i in range(N_BRANCH):
        gate = jax.nn.sigmoid(_dot(h_ref[...], wg_ref[i]))
        term = gate * _dot(o_ref[:, i * BRANCH_W:(i + 1) * BRANCH_W], wb_ref[i])
        acc = term if acc is None else acc + term
    out_ref[...] = acc.astype(BF16)


def _merge_call(h, o, wg, wb, layer, name):
    n_tok = h.shape[0]
    tm, tn = 1024, 512
    return pl.pallas_call(
        _merge_kernel,
        grid=(n_tok // tm, D_MODEL // tn),
        in_specs=[
            pl.BlockSpec((tm, D_MODEL), lambda i, j: (i, 0)),
            pl.BlockSpec((tm, D_MODEL), lambda i, j: (i, 0)),
            pl.BlockSpec((None, N_BRANCH, D_MODEL, tn), lambda i, j: (layer, 0, 0, j)),
            pl.BlockSpec((None, N_BRANCH, BRANCH_W, tn), lambda i, j: (layer, 0, 0, j)),
        ],
        out_specs=pl.BlockSpec((tm, tn), lambda i, j: (i, j)),
        out_shape=jax.ShapeDtypeStruct((n_tok, D_MODEL), BF16),
        compiler_params=pltpu.CompilerParams(
            dimension_semantics=("arbitrary", "arbitrary"), vmem_limit_bytes=VMEM_LIMIT),
        name=name,
    )(h, o, wg, wb)


def _outproj_kernel(mixed_ref, x_ref, mod_ref, g_ref, gffn_ref, w_ref, xo_ref, h2_ref):
    m = mod_ref[0]
    half = mixed_ref.shape[0] // 2
    for r in (slice(0, half), slice(half, 2 * half)):
        out = _dot(mixed_ref[r, :], w_ref[...])
        x = x_ref[r, :] + m[2:3] * (_rms(out) * g_ref[...])
        xo_ref[r, :] = x
        h2_ref[r, :] = (_rms(x) * gffn_ref[...] * (1.0 + m[4:5]) + m[3:4]).astype(BF16)


def _outproj_call(mixed, x, mod, g, g_ffn, w, layer, seg, name):
    n_tok = x.shape[0]
    tm = 512
    return pl.pallas_call(
        _outproj_kernel,
        grid=(n_tok // tm,),
        in_specs=[
            pl.BlockSpec((tm, D_MODEL), lambda i: (i, 0)),
            pl.BlockSpec((tm, D_MODEL), lambda i: (i, 0)),
            pl.BlockSpec((1, 6, D_MODEL), lambda i: (seg(i, tm), 0, 0)),
            pl.BlockSpec((1, D_MODEL), lambda i: (0, 0)),
            pl.BlockSpec((1, D_MODEL), lambda i: (0, 0)),
            pl.BlockSpec((None, D_MODEL, D_MODEL), lambda i: (layer, 0, 0)),
        ],
        out_specs=[pl.BlockSpec((tm, D_MODEL), lambda i: (i, 0)), pl.BlockSpec((tm, D_MODEL), lambda i: (i, 0))],
        out_shape=[jax.ShapeDtypeStruct((n_tok, D_MODEL), F32), jax.ShapeDtypeStruct((n_tok, D_MODEL), BF16)],
        compiler_params=pltpu.CompilerParams(
            dimension_semantics=("arbitrary",), vmem_limit_bytes=VMEM_LIMIT),
        name=name,
    )(mixed, x, mod, g, g_ffn, w)


def _ffn_kernel(x_ref, h2_ref, mod_ref, gpost_ref, wg_ref, wu_ref, wd_ref, xo_ref, acc_s):
    j = pl.program_id(1)

    @pl.when(j == 0)
    def _():
        acc_s[...] = jnp.zeros_like(acc_s)

    a = _dot(h2_ref[...], wg_ref[...])
    b = _dot(h2_ref[...], wu_ref[...])
    t = (a * jax.nn.sigmoid(a) * b).astype(BF16)
    acc_s[...] += _dot(t, wd_ref[...])

    @pl.when(j == pl.num_programs(1) - 1)
    def _():
        m = mod_ref[0]
        xo_ref[...] = x_ref[...] + m[5:6] * (_rms(acc_s[...]) * gpost_ref[...])


def _ffn_call(x, h2, mod, gpost, wg, wu, wd, layer, seg, name):
    n_tok = x.shape[0]
    tm, th = 512, 512
    return pl.pallas_call(
        _ffn_kernel,
        grid=(n_tok // tm, FFN_HIDDEN // th),
        in_specs=[
            pl.BlockSpec((tm, D_MODEL), lambda i, j: (i, 0)),
            pl.BlockSpec((tm, D_MODEL), lambda i, j: (i, 0)),
            pl.BlockSpec((1, 6, D_MODEL), lambda i, j: (seg(i, tm), 0, 0)),
            pl.BlockSpec((1, D_MODEL), lambda i, j: (0, 0)),
            pl.BlockSpec((None, D_MODEL, th), lambda i, j: (layer, 0, j)),
            pl.BlockSpec((None, D_MODEL, th), lambda i, j: (layer, 0, j)),
            pl.BlockSpec((None, th, D_MODEL), lambda i, j: (layer, j, 0)),
        ],
        out_specs=pl.BlockSpec((tm, D_MODEL), lambda i, j: (i, 0)),
        out_shape=jax.ShapeDtypeStruct((n_tok, D_MODEL), F32),
        scratch_shapes=[pltpu.VMEM((tm, D_MODEL), F32)],
        compiler_params=pltpu.CompilerParams(
            dimension_semantics=("arbitrary", "arbitrary"), vmem_limit_bytes=VMEM_LIMIT),
        name=name,
    )(x, h2, mod, gpost, wg, wu, wd)


def _rope_tables():
    t = np.arange(DEC_SEQ)
    row = (t // GRID_W).astype(np.float32)
    col = (t % GRID_W).astype(np.float32)

    def half_angles(r):
        quarter = r // 4
        inv = (1.0 / (ROPE_BASE ** (np.arange(quarter, dtype=np.float32) / quarter))).astype(np.float32)
        return np.concatenate([row[:, None] * inv, col[:, None] * inv], axis=-1).astype(np.float32)

    a64 = half_angles(128)
    cos128 = np.concatenate([np.cos(a64), np.cos(a64)], axis=1)
    sin128 = np.concatenate([-np.sin(a64), np.sin(a64)], axis=1)
    a32 = half_angles(64)
    c, s, z = np.cos(a32), np.sin(a32), np.zeros_like(a32)
    cos64 = np.concatenate([c, c, c, c], axis=1)
    sina64 = np.concatenate([-s, z, -s, z], axis=1)
    sinb64 = np.concatenate([z, s, z, s], axis=1)
    return tuple(jnp.asarray(a, F32) for a in (cos128, sin128, cos64, sina64, sinb64))


def _ctx_seg(i, tm):
    del tm
    return i * 0


def _lat_seg(i, tm):
    return 1 + (i * tm) // DEC_SEQ


def kernel(x_prompt, x_sample, cache_mla_ckv, cache_mla_krope, cache_swa_k, cache_swa_v, cache_na_k, cache_na_v, cache_diff_k, cache_diff_v, c, c_ctx, w_ada, b_ada, mix_pre_g, mix_post_g, ffn_pre_g, ffn_post_g, w_in, mla_q_norm_g, mla_kv_norm_g, w_mla_uq, w_mla_ukv, swa_sink, na_rpb, diff_lq1, diff_lk1, diff_lq2, diff_lk2, diff_subln_g, w_mix_gate, w_branch, w_out, w_ffn_gate, w_ffn_up, w_ffn_down):
    cvec = jnp.concatenate([c_ctx[None, :], c, jnp.zeros((N_MOD_ROWS - 1 - DEC_BATCH, D_MODEL), F32)], axis=0)
    mod_all = _ada_call(cvec, w_ada, b_ada).reshape(DEPTH, N_MOD_ROWS, 6, D_MODEL)
    bias_all = _nbr_bias_call(na_rpb)
    tables = _rope_tables()
    caches = (cache_mla_ckv, cache_mla_krope, cache_swa_k, cache_swa_v, cache_na_k, cache_na_v,
              cache_diff_k, cache_diff_v)

    w_main, w_kr = _cast_w_in_call(w_in)
    wg = _cast_call(w_mix_gate, 512, "cast_w_gate")
    wb = _cast_call(w_branch, 512, "cast_w_branch")
    wo = _cast_call(w_out, 512, "cast_w_out")
    wfg = _cast_call(w_ffn_gate, 256, "cast_w_ffn_gate")
    wfu = _cast_call(w_ffn_up, 256, "cast_w_ffn_up")
    wfd = _cast_call(w_ffn_down, 512, "cast_w_ffn_down")

    y = x_prompt.reshape(N_CTX_TOK, D_MODEL)
    z = x_sample.reshape(N_LAT_TOK, D_MODEL)
    new_caches = ()
    for l in range(DEPTH):
        lam_init = _lambda_init(l)
        mod = mod_all[l]
        uq = w_mla_uq[l].reshape(MLA_Q_RANK, MLA_HEADS, MLA_NOPE + MLA_ROPE)
        uq_rope = jnp.pad(uq[:, :, MLA_NOPE:], ((0, 0), (0, 0), (0, LANES - MLA_ROPE)))
        w_uq = jnp.concatenate([uq[:, :, :MLA_NOPE].reshape(MLA_Q_RANK, 512),
                                uq_rope.reshape(MLA_Q_RANK, 512)], axis=1).astype(BF16)
        small = (mla_q_norm_g[l][None, :], w_uq, mla_kv_norm_g[l][None, :], w_mla_ukv[l].astype(BF16),
                 swa_sink[l], diff_lq1[l][None, :], diff_lk1[l][None, :], diff_lq2[l][None, :],
                 diff_lk2[l][None, :], diff_subln_g[l][None, :])
        g_pre, g_post = mix_pre_g[l][None, :], mix_post_g[l][None, :]
        f_pre, f_post = ffn_pre_g[l][None, :], ffn_post_g[l][None, :]

        h, p, kr = _inproj_call(y, mod, g_pre, w_main, w_kr, l, _ctx_seg, F32, "ctx_inproj")
        o, *new_caches = _ctx_attn_call(p, kr, small, new_caches, lam_init)
        mixed = _merge_call(h, o, wg, wb, l, "ctx_merge")
        y, h2 = _outproj_call(mixed, y, mod, g_post, f_pre, wo, l, _ctx_seg, "ctx_outproj")
        y = _ffn_call(y, h2, mod, f_post, wfg, wfu, wfd, l, _ctx_seg, "ctx_ffn")

        h, p, kr = _inproj_call(z, mod, g_pre, w_main, w_kr, l, _lat_seg, BF16, "lat_inproj")
        o = _lat_attn_call(p, kr, caches, tables, bias_all, small, l, lam_init)
        mixed = _merge_call(h, o, wg, wb, l, "lat_merge")
        z, h2 = _outproj_call(mixed, z, mod, g_post, f_pre, wo, l, _lat_seg, "lat_outproj")
        z = _ffn_call(z, h2, mod, f_post, wfg, wfu, wfd, l, _lat_seg, "lat_ffn")

    new_caches = [a.reshape((BATCH, DEPTH, SEQ) + t) for a, t in zip(new_caches, CACHE_TAILS)]
    return (y.reshape(BATCH, SEQ, D_MODEL), z.reshape(DEC_BATCH, DEC_SEQ, D_MODEL), *new_caches)
```

```python
import functools
import math

import numpy as np
import jax
import jax.numpy as jnp
from jax import lax
from jax.experimental import pallas as pl
from jax.experimental.pallas import tpu as pltpu

D_MODEL = 2048
BATCH = 32
SEQ = 256
DEPTH = 2
DEC_BATCH = 8
DEC_SEQ = 1024
PAST_LEN = 512
GRID_W = 64
HEAD_DIM = 128
ROPE_BASE = 10000.0
EPS = 1e-6
NEG_INF = -1e30
LOG2E = math.log2(math.e)

MLA_HEADS = 4
MLA_Q_RANK = 384
MLA_KV_RANK = 128
MLA_NOPE = 128
MLA_ROPE = 64
MLA_V = 128
SWA_HEADS = 4
SWA_KV_HEADS = 2
SWA_WINDOW = 128
NA_HEADS = 4
NA_ROWS = 8
NA_COLS = 16
DIFF_HEADS = 4
DIFF_QK = 64
DIFF_V = 128
N_BRANCH = 4
BRANCH_W = 512
FFN_HIDDEN = 5632

N_CTX_TOK = BATCH * SEQ
N_LAT_TOK = DEC_BATCH * DEC_SEQ
GRID_ROWS = DEC_SEQ // GRID_W

Q_DOWN0, CKV0, QB0, KB0, VB0, QC0, KC0, VC0, QD0, KD0, VD0 = (
    0, 384, 512, 1024, 1280, 1536, 2048, 2560, 3072, 3584, 4096)
KROPE_SRC0 = MLA_Q_RANK + MLA_KV_RANK
P_WIDTH = 4608
N_MOD_ROWS = 16

CACHE_TAILS = ((MLA_KV_RANK,), (MLA_ROPE,), (SWA_KV_HEADS, HEAD_DIM), (SWA_KV_HEADS, HEAD_DIM),
               (NA_HEADS, HEAD_DIM), (NA_HEADS, HEAD_DIM), (DIFF_HEADS, 2 * DIFF_QK), (DIFF_HEADS, DIFF_V))

VMEM_LIMIT = 56 * 1024 * 1024
VMEM_LIMIT_BIG = 60 * 1024 * 1024
LANES = 128

F32 = jnp.float32
BF16 = jnp.bfloat16


def _lambda_init(l):
    return 0.8 - 0.6 * math.exp(-0.3 * l)


def _rms(x):
    return x * lax.rsqrt(jnp.mean(x * x, axis=-1, keepdims=True) + EPS)


def _dot(a, b):
    return jnp.dot(a, b, preferred_element_type=F32)


def _dot_nt(a, b):
    return lax.dot_general(a, b, (((1,), (1,)), ((), ())), preferred_element_type=F32)


def _attend(pieces, sink=None):
    m = None
    for s, _ in pieces:
        mi = jnp.max(s, axis=-1, keepdims=True)
        m = mi if m is None else jnp.maximum(m, mi)
    if sink is not None:
        sink = sink * LOG2E
        m = jnp.maximum(m, sink)
    den = None
    num = None
    for s, v in pieces:
        e = jnp.exp2(s - m)
        d = jnp.sum(e, axis=-1, keepdims=True)
        n = _dot(e.astype(BF16), v)
        den = d if den is None else den + d
        num = n if num is None else num + n
    if sink is not None:
        den = den + jnp.exp2(sink - m)
    return num / den


def _diff_attend(pieces1, pieces2, values, lam):
    def exps(pieces):
        m = None
        for s in pieces:
            mi = jnp.max(s, axis=-1, keepdims=True)
            m = mi if m is None else jnp.maximum(m, mi)
        es = [jnp.exp2(s - m) for s in pieces]
        den = None
        for e in es:
            d = jnp.sum(e, axis=-1, keepdims=True)
            den = d if den is None else den + d
        return es, den

    es1, den1 = exps(pieces1)
    es2, den2 = exps(pieces2)
    c1 = 1.0 / den1
    c2 = lam / den2
    out = None
    for e1, e2, v in zip(es1, es2, values):
        n = _dot((e1 * c1 - e2 * c2).astype(BF16), v)
        out = n if out is None else out + n
    return out


def _diff_lambda(lq1, lk1, lq2, lk2, lam_init):
    a = jnp.sum(lq1[...] * lk1[...], axis=-1, keepdims=True)
    b = jnp.sum(lq2[...] * lk2[...], axis=-1, keepdims=True)
    return jnp.exp(a) - jnp.exp(b) + lam_init


def _ada_kernel(c_ref, w_ref, b_ref, o_ref):
    c = c_ref[...]
    s = (c * jax.nn.sigmoid(c)).astype(BF16)
    o_ref[0] = _dot(s, w_ref[0].astype(BF16)) + b_ref[0]


def _ada_call(cvec, w_ada, b_ada):
    tn = 1024
    n = 6 * D_MODEL
    return pl.pallas_call(
        _ada_kernel,
        grid=(DEPTH, n // tn),
        in_specs=[
            pl.BlockSpec((N_MOD_ROWS, D_MODEL), lambda l, j: (0, 0)),
            pl.BlockSpec((1, D_MODEL, tn), lambda l, j: (l, 0, j)),
            pl.BlockSpec((1, 1, tn), lambda l, j: (l, 0, j)),
        ],
        out_specs=pl.BlockSpec((1, N_MOD_ROWS, tn), lambda l, j: (l, 0, j)),
        out_shape=jax.ShapeDtypeStruct((DEPTH, N_MOD_ROWS, n), F32),
        compiler_params=pltpu.CompilerParams(
            dimension_semantics=("arbitrary", "arbitrary"), vmem_limit_bytes=VMEM_LIMIT),
        name="ada_mod",
    )(cvec, w_ada, b_ada.reshape(DEPTH, 1, n))


NBR_PAIRS = 2 * NA_ROWS


def _nbr_bias_kernel(rpb_ref, o_ref):
    l = pl.program_id(0)
    h = pl.program_id(1)
    n_dr = 2 * NA_ROWS - 1
    n_dc = 2 * NA_COLS - 1
    base = (l * NA_HEADS + h) * (n_dr * n_dc)
    wq = lax.broadcasted_iota(jnp.int32, (GRID_W, LANES), 0)
    lane = lax.broadcasted_iota(jnp.int32, (GRID_W, LANES), 1)
    upper = lane >= GRID_W
    wk = jnp.where(upper, lane - GRID_W, lane)
    delta = wk - wq + (NA_COLS - 1)
    cs = jnp.clip(wq - NA_COLS // 2, 0, GRID_W - NA_COLS)
    valid = (wk >= cs) & (wk < cs + NA_COLS)
    for e in range(NBR_PAIRS):
        dr_lo, dr_hi = max(e - 1, 0), min(e, n_dr - 1)
        acc = jnp.full((GRID_W, LANES), NEG_INF, F32)
        for d in range(n_dc):
            lo = rpb_ref[base + dr_lo * n_dc + d]
            hi = rpb_ref[base + dr_hi * n_dc + d]
            acc = jnp.where(delta == d, jnp.where(upper, hi, lo), acc)
        o_ref[0, 0, e] = jnp.where(valid, acc * LOG2E, NEG_INF)


def _nbr_bias_call(na_rpb):
    n_pairs = NBR_PAIRS
    return pl.pallas_call(
        _nbr_bias_kernel,
        grid=(DEPTH, NA_HEADS),
        in_specs=[pl.BlockSpec(memory_space=pltpu.SMEM)],
        out_specs=pl.BlockSpec((1, 1, n_pairs, GRID_W, LANES), lambda l, h: (l, h, 0, 0, 0)),
        out_shape=jax.ShapeDtypeStruct((DEPTH, NA_HEADS, n_pairs, GRID_W, LANES), F32),
        compiler_params=pltpu.CompilerParams(dimension_semantics=("arbitrary", "arbitrary")),
        name="nbr_bias",
    )(na_rpb.reshape(-1))


def _inproj_kernel(x_ref, mod_ref, g_ref, w_ref, wkr_ref, h_ref, p_ref, kr_ref):
    m = mod_ref[0]
    half = x_ref.shape[0] // 2
    for r in (slice(0, half), slice(half, 2 * half)):
        h = (_rms(x_ref[r, :]) * g_ref[...] * (1.0 + m[1:2]) + m[0:1]).astype(BF16)
        h_ref[r, :] = h
        kr_ref[r, :] = _dot(h, wkr_ref[...])
        p_ref[r, :] = _dot(h, w_ref[...]).astype(p_ref.dtype)


def _inproj_call(x, mod, g, w_main, w_kr, layer, seg, p_dtype, name):
    n_tok = x.shape[0]
    tm = 512
    return pl.pallas_call(
        _inproj_kernel,
        grid=(n_tok // tm,),
        in_specs=[
            pl.BlockSpec((tm, D_MODEL), lambda i: (i, 0)),
            pl.BlockSpec((1, 6, D_MODEL), lambda i: (seg(i, tm), 0, 0)),
            pl.BlockSpec((1, D_MODEL), lambda i: (0, 0)),
            pl.BlockSpec((None, D_MODEL, P_WIDTH), lambda i: (layer, 0, 0)),
            pl.BlockSpec((None, D_MODEL, LANES), lambda i: (layer, 0, 0)),
        ],
        out_specs=[
            pl.BlockSpec((tm, D_MODEL), lambda i: (i, 0)),
            pl.BlockSpec((tm, P_WIDTH), lambda i: (i, 0)),
            pl.BlockSpec((tm, LANES), lambda i: (i, 0)),
        ],
        out_shape=[
            jax.ShapeDtypeStruct((n_tok, D_MODEL), BF16),
            jax.ShapeDtypeStruct((n_tok, P_WIDTH), p_dtype),
            jax.ShapeDtypeStruct((n_tok, LANES), F32),
        ],
        compiler_params=pltpu.CompilerParams(
            dimension_semantics=("arbitrary",), vmem_limit_bytes=VMEM_LIMIT_BIG),
        name=name,
    )(x, mod, g, w_main, w_kr)


def _cast_kernel(x_ref, o_ref):
    o_ref[...] = x_ref[...].astype(BF16)


def _cast_call(w, block_rows, name):
    w2 = w.reshape(-1, w.shape[-1])
    rows, cols = w2.shape
    out = pl.pallas_call(
        _cast_kernel,
        grid=(rows // block_rows,),
        in_specs=[pl.BlockSpec((block_rows, cols), lambda i: (i, 0))],
        out_specs=pl.BlockSpec((block_rows, cols), lambda i: (i, 0)),
        out_shape=jax.ShapeDtypeStruct((rows, cols), BF16),
        compiler_params=pltpu.CompilerParams(dimension_semantics=("arbitrary",), vmem_limit_bytes=VMEM_LIMIT),
        name=name,
    )(w2)
    return out.reshape(w.shape)


def _cast_w_in_kernel(x_ref, main_ref, kr_ref):
    main_ref[:, :KROPE_SRC0] = x_ref[:, :KROPE_SRC0].astype(BF16)
    tail = x_ref[:, KROPE_SRC0:]
    main_ref[:, KROPE_SRC0:] = tail[:, MLA_ROPE:].astype(BF16)
    kr = tail[:, :LANES]
    keep = lax.broadcasted_iota(jnp.int32, kr.shape, 1) < MLA_ROPE
    kr_ref[...] = jnp.where(keep, kr, 0.0).astype(BF16)


def _cast_w_in_call(w_in):
    rows = DEPTH * D_MODEL
    cols = w_in.shape[-1]
    block_rows = 256
    main, kr = pl.pallas_call(
        _cast_w_in_kernel,
        grid=(rows // block_rows,),
        in_specs=[pl.BlockSpec((block_rows, cols), lambda i: (i, 0))],
        out_specs=[pl.BlockSpec((block_rows, P_WIDTH), lambda i: (i, 0)),
                   pl.BlockSpec((block_rows, LANES), lambda i: (i, 0))],
        out_shape=[jax.ShapeDtypeStruct((rows, P_WIDTH), BF16), jax.ShapeDtypeStruct((rows, LANES), BF16)],
        compiler_params=pltpu.CompilerParams(dimension_semantics=("arbitrary",), vmem_limit_bytes=VMEM_LIMIT),
        name="cast_w_in",
    )(w_in.reshape(rows, cols))
    return main.reshape(DEPTH, D_MODEL, P_WIDTH), kr.reshape(DEPTH, D_MODEL, LANES)


def _mla_queries(q_down, qn_g, w_uq):
    cq = _rms(q_down) * qn_g[...]
    return _dot(cq.astype(BF16), w_uq[...])


def _ctx_attn_kernel(*refs, lam_init, n_prev):
    (p_ref, kr_ref, qn_g, w_uq, kvn_g, w_ukv, sink_ref, lq1, lk1, lq2, lk2, subln_g) = refs[:12]
    n_in = 12 + (len(CACHE_TAILS) if n_prev else 0)
    prev = refs[12:n_in]
    o_ref = refs[n_in]
    ckv_o, kr_o, kb_o, vb_o, kc_o, vc_o, kd_o, vd_o = refs[n_in + 1:]
    for src, dst in zip(prev, (ckv_o, kr_o, kb_o, vb_o, kc_o, vc_o, kd_o, vd_o)):
        dst[0, 0:n_prev] = src[0]
    ls = n_prev

    scale_a = LOG2E * (MLA_NOPE + MLA_ROPE) ** -0.5
    qa = _mla_queries(p_ref[:, Q_DOWN0:Q_DOWN0 + MLA_Q_RANK], qn_g, w_uq) * scale_a
    ckv = _rms(p_ref[:, CKV0:CKV0 + MLA_KV_RANK]) * kvn_g[...]
    ckv_o[0, ls] = ckv
    kr = kr_ref[...]
    kr_o[0, ls] = kr[:, :MLA_ROPE]
    kr_b = kr.astype(BF16)
    kv = _dot(ckv.astype(BF16), w_ukv[...])
    for h in range(MLA_HEADS):
        q = jnp.concatenate([qa[:, h * 128:(h + 1) * 128], qa[:, 512 + h * 128:512 + (h + 1) * 128]],
                            axis=1).astype(BF16)
        k = jnp.concatenate([kv[:, h * 256:h * 256 + 128].astype(BF16), kr_b], axis=1)
        v = kv[:, h * 256 + 128:(h + 1) * 256].astype(BF16)
        o = _attend([(_dot_nt(q, k), v)])
        o_ref[:, h * 128:(h + 1) * 128] = o.astype(BF16)

    scale = LOG2E * HEAD_DIM ** -0.5

    def head(col0, h):
        return p_ref[:, col0 + h * 128:col0 + (h + 1) * 128]

    g_size = SWA_HEADS // SWA_KV_HEADS
    for g in range(SWA_KV_HEADS):
        k = head(KB0, g)
        v = head(VB0, g)
        kb_o[0, ls, pl.ds(g, SEQ, stride=SWA_KV_HEADS), :] = k
        vb_o[0, ls, pl.ds(g, SEQ, stride=SWA_KV_HEADS), :] = v
        k = k.astype(BF16)
        v = v.astype(BF16)
        for h in range(g * g_size, (g + 1) * g_size):
            q = (head(QB0, h) * scale).astype(BF16)
            o = _attend([(_dot_nt(q, k), v)], sink=sink_ref[h])
            o_ref[:, 512 + h * 128:512 + (h + 1) * 128] = o.astype(BF16)

    for h in range(NA_HEADS):
        k = head(KC0, h)
        v = head(VC0, h)
        kc_o[0, ls, pl.ds(h, SEQ, stride=NA_HEADS), :] = k
        vc_o[0, ls, pl.ds(h, SEQ, stride=NA_HEADS), :] = v
        q = (head(QC0, h) * scale).astype(BF16)
        o = _attend([(_dot_nt(q, k.astype(BF16)), v.astype(BF16))])
        o_ref[:, 1024 + h * 128:1024 + (h + 1) * 128] = o.astype(BF16)

    lam = _diff_lambda(lq1, lk1, lq2, lk2, lam_init)
    scale_d = LOG2E * DIFF_QK ** -0.5
    lo = lax.broadcasted_iota(jnp.int32, (SEQ, 128), 1) < DIFF_QK
    for h in range(DIFF_HEADS):
        k = head(KD0, h)
        v = head(VD0, h)
        kd_o[0, ls, pl.ds(h, SEQ, stride=DIFF_HEADS), :] = k
        vd_o[0, ls, pl.ds(h, SEQ, stride=DIFF_HEADS), :] = v
        k = k.astype(BF16)
        v = v.astype(BF16)
        q = head(QD0, h) * scale_d
        o1 = _attend([(_dot_nt(jnp.where(lo, q, 0.0).astype(BF16), k), v)])
        o2 = _attend([(_dot_nt(jnp.where(lo, 0.0, q).astype(BF16), k), v)])
        o = _rms(o1 - lam * o2) * subln_g[...] * (1.0 - lam_init)
        o_ref[:, 1536 + h * 128:1536 + (h + 1) * 128] = o.astype(BF16)


def _small_specs():
    zero2 = (lambda *a: (0, 0))
    return [
        pl.BlockSpec((1, MLA_Q_RANK), zero2),
        pl.BlockSpec((MLA_Q_RANK, 1024), zero2),
        pl.BlockSpec((1, MLA_KV_RANK), zero2),
        pl.BlockSpec((MLA_KV_RANK, 1024), zero2),
        pl.BlockSpec(memory_space=pltpu.SMEM),
        pl.BlockSpec((1, DIFF_QK), zero2),
        pl.BlockSpec((1, DIFF_QK), zero2),
        pl.BlockSpec((1, DIFF_QK), zero2),
        pl.BlockSpec((1, DIFF_QK), zero2),
        pl.BlockSpec((1, DIFF_V), zero2),
    ]


def _ctx_attn_call(p, kr, small, prev, lam_init):
    n_prev = prev[0].shape[1] if prev else 0
    row = lambda b: (b, 0)

    def flat(tail):
        return (SEQ * tail[0], tail[1]) if len(tail) == 2 else (SEQ,) + tail

    def cache_spec(n_layers, tail):
        return pl.BlockSpec((1, n_layers) + flat(tail), lambda b: (b, 0, 0, 0))

    return pl.pallas_call(
        functools.partial(_ctx_attn_kernel, lam_init=lam_init, n_prev=n_prev),
        grid=(BATCH,),
        in_specs=[pl.BlockSpec((SEQ, P_WIDTH), row), pl.BlockSpec((SEQ, LANES), row)] + _small_specs()
        + [cache_spec(n_prev, t) for t in (CACHE_TAILS if prev else ())],
        out_specs=[pl.BlockSpec((SEQ, D_MODEL), row)] + [cache_spec(n_prev + 1, t) for t in CACHE_TAILS],
        out_shape=[jax.ShapeDtypeStruct((N_CTX_TOK, D_MODEL), BF16)]
        + [jax.ShapeDtypeStruct((BATCH, n_prev + 1) + flat(t), F32) for t in CACHE_TAILS],
        compiler_params=pltpu.CompilerParams(
            dimension_semantics=("arbitrary",), vmem_limit_bytes=VMEM_LIMIT),
        name="ctx_attn",
    )(p, kr, *small, *prev)


LAT_TQ = 256
SWA_PAD = SWA_WINDOW
NBR_WIN_ROWS = 12


def _rope_pair64(x, c_ref, sa_ref, sb_ref, rows):
    out = []
    for j in range(x.shape[1] // LANES):
        xj = x[:, j * LANES:(j + 1) * LANES]
        out.append(xj * c_ref[rows, :] + pltpu.roll(xj, 96, 1) * sa_ref[rows, :]
                   + pltpu.roll(xj, 32, 1) * sb_ref[rows, :])
    return jnp.concatenate(out, axis=1) if len(out) > 1 else out[0]


def _rope128(x, c_ref, s_ref, rows):
    out = []
    for j in range(x.shape[1] // LANES):
        xj = x[:, j * LANES:(j + 1) * LANES]
        out.append(xj * c_ref[rows, :] + pltpu.roll(xj, 64, 1) * s_ref[rows, :])
    return jnp.concatenate(out, axis=1) if len(out) > 1 else out[0]


def _lat_attn_kernel(p_ref, kr_ref, c_ckv, c_kr, c_kb, c_vb, c_kc, c_vc, c_kd, c_vd,
                     cos128, sin128, cos64, sina64, sinb64, bias_ref,
                     qn_g, w_uq, kvn_g, w_ukv, sink_ref, lq1, lk1, lq2, lk2, subln_g,
                     o_ref,
                     ka_s, va_s, kbc_s, vbc_s, kbp_s, vbp_s, kcc_s, vcc_s, kdc_s, vdc_s, kdo_s,
                     *, lam_init):
    def cached_head(ref, h, n_heads):
        return ref[pl.ds(h, PAST_LEN, stride=n_heads), :].astype(BF16)

    qi = pl.program_id(1)
    all_rows = slice(None)
    n_own = DEC_SEQ

    @pl.when(qi == 0)
    def _prologue():
        ckv_own = (_rms(p_ref[:, CKV0:CKV0 + MLA_KV_RANK].astype(F32)) * kvn_g[...]).astype(BF16)
        ckv_all = jnp.concatenate([c_ckv[...].astype(BF16), ckv_own], axis=0)
        kr_ctx = jnp.concatenate([c_kr[...], jnp.zeros((PAST_LEN, LANES - MLA_ROPE), F32)], axis=1)
        kr_own = _rope_pair64(kr_ref[...], cos64, sina64, sinb64, all_rows)
        kr_all = jnp.concatenate([kr_ctx, kr_own], axis=0).astype(BF16)
        for h in range(MLA_HEADS):
            kvh = _dot(ckv_all, w_ukv[:, h * 256:(h + 1) * 256])
            ka_s[h, :, 0:128] = kvh[:, 0:128].astype(BF16)
            ka_s[h, :, 128:256] = kr_all
            va_s[:, h * 128:(h + 1) * 128] = kvh[:, 128:256].astype(BF16)
        for g in range(SWA_KV_HEADS):
            kbc_s[:, g * 128:(g + 1) * 128] = cached_head(c_kb, g, SWA_KV_HEADS)
            vbc_s[:, g * 128:(g + 1) * 128] = cached_head(c_vb, g, SWA_KV_HEADS)
        zpad = jnp.zeros((SWA_PAD, 256), BF16)
        kbp_s[0:SWA_PAD, :] = zpad
        kbp_s[SWA_PAD + n_own:, :] = zpad
        vbp_s[0:SWA_PAD, :] = zpad
        vbp_s[SWA_PAD + n_own:, :] = zpad
        kbp_s[SWA_PAD:SWA_PAD + n_own, :] = _rope128(
            p_ref[:, KB0:KB0 + 256].astype(F32), cos128, sin128, all_rows).astype(BF16)
        vbp_s[SWA_PAD:SWA_PAD + n_own, :] = p_ref[:, VB0:VB0 + 256]
        for h in range(NA_HEADS):
            cols = slice(h * 128, (h + 1) * 128)
            kcc_s[:, cols] = cached_head(c_kc, h, NA_HEADS)
            vcc_s[:, cols] = cached_head(c_vc, h, NA_HEADS)
            kdc_s[:, cols] = cached_head(c_kd, h, DIFF_HEADS)
            vdc_s[:, cols] = cached_head(c_vd, h, DIFF_HEADS)
        kdo_s[...] = _rope_pair64(
            p_ref[:, KD0:KD0 + 512].astype(F32), cos64, sina64, sinb64, all_rows).astype(BF16)

    q0 = pl.multiple_of(qi * LAT_TQ, LAT_TQ)
    rows = pl.ds(q0, LAT_TQ)

    scale_a = LOG2E * (MLA_NOPE + MLA_ROPE) ** -0.5
    qa = _mla_queries(p_ref[rows, Q_DOWN0:Q_DOWN0 + MLA_Q_RANK].astype(F32), qn_g, w_uq) * scale_a
    qa_r = _rope_pair64(qa[:, 512:1024], cos64, sina64, sinb64, rows)
    for h in range(MLA_HEADS):
        q = jnp.concatenate([qa[:, h * 128:(h + 1) * 128], qa_r[:, h * 128:(h + 1) * 128]],
                            axis=1).astype(BF16)
        o = _attend([(_dot_nt(q, ka_s[h]), va_s[:, h * 128:(h + 1) * 128])])
        o_ref[:, h * 128:(h + 1) * 128] = o.astype(BF16)

    scale = LOG2E * HEAD_DIM ** -0.5
    qb = (_rope128(p_ref[rows, QB0:QB0 + 512].astype(F32), cos128, sin128, rows) * scale).astype(BF16)
    n_loc = LAT_TQ + 2 * SWA_PAD
    qpos = lax.broadcasted_iota(jnp.int32, (LAT_TQ, n_loc), 0)
    kpos = lax.broadcasted_iota(jnp.int32, (LAT_TQ, n_loc), 1) - SWA_PAD
    kabs = kpos + q0
    valid = (jnp.abs(qpos - kpos) <= SWA_WINDOW) & (kabs >= 0) & (kabs < n_own)
    loc_rows = pl.ds(q0, n_loc)
    g_size = SWA_HEADS // SWA_KV_HEADS
    for h in range(SWA_HEADS):
        g = h // g_size
        q = qb[:, h * 128:(h + 1) * 128]
        s_ctx = _dot_nt(q, kbc_s[:, g * 128:(g + 1) * 128])
        s_loc = jnp.where(valid, _dot_nt(q, kbp_s[loc_rows, g * 128:(g + 1) * 128]), NEG_INF)
        o = _attend([(s_ctx, vbc_s[:, g * 128:(g + 1) * 128]),
                     (s_loc, vbp_s[loc_rows, g * 128:(g + 1) * 128])], sink=sink_ref[h])
        o_ref[:, 512 + h * 128:512 + (h + 1) * 128] = o.astype(BF16)

    q_rows = LAT_TQ // GRID_W
    ws = jnp.clip(qi * q_rows - NA_ROWS // 2, 0, GRID_ROWS - NBR_WIN_ROWS)
    ws = (ws // 2) * 2
    krows = pl.ds(pl.multiple_of(ws * GRID_W, 2 * GRID_W), NBR_WIN_ROWS * GRID_W)
    lower = lax.broadcasted_iota(jnp.int32, (GRID_W, LANES), 1) < GRID_W
    for h in range(NA_HEADS):
        cols = slice(h * 128, (h + 1) * 128)
        bias_rows = []
        for t in range(q_rows):
            r = qi * q_rows + t
            off = jnp.clip(r - NA_ROWS // 2, 0, GRID_ROWS - NA_ROWS) - r + (NA_ROWS - 1)
            blocks = []
            for j in range(NBR_WIN_ROWS // 2):
                dr = ws + 2 * j - r + (NA_ROWS - 1)
                ok_lo = ((dr >= off) & (dr < off + NA_ROWS)).astype(jnp.int32)
                ok_hi = ((dr + 1 >= off) & (dr + 1 < off + NA_ROWS)).astype(jnp.int32)
                ok = jnp.where(lower, ok_lo, ok_hi) > 0
                blk = bias_ref[0, h, jnp.clip(dr + 1, 0, NBR_PAIRS - 1)]
                blocks.append(jnp.where(ok, blk, NEG_INF))
            bias_rows.append(jnp.concatenate(blocks, axis=1))
        bias = jnp.concatenate(bias_rows, axis=0)
        q = (p_ref[rows, QC0 + h * 128:QC0 + (h + 1) * 128].astype(F32) * scale).astype(BF16)
        s_loc = _dot_nt(q, p_ref[krows, KC0 + h * 128:KC0 + (h + 1) * 128]) + bias
        s_ctx = _dot_nt(q, kcc_s[:, cols])
        o = _attend([(s_ctx, vcc_s[:, cols]), (s_loc, p_ref[krows, VC0 + h * 128:VC0 + (h + 1) * 128])])
        o_ref[:, 1024 + h * 128:1024 + (h + 1) * 128] = o.astype(BF16)

    scale_d = LOG2E * DIFF_QK ** -0.5
    qd = _rope_pair64(p_ref[rows, QD0:QD0 + 512].astype(F32), cos64, sina64, sinb64, rows) * scale_d
    lam = _diff_lambda(lq1, lk1, lq2, lk2, lam_init)
    lo = lax.broadcasted_iota(jnp.int32, (LAT_TQ, 128), 1) < DIFF_QK
    for h in range(DIFF_HEADS):
        cols = slice(h * 128, (h + 1) * 128)
        q = qd[:, cols]
        v_own = p_ref[:, VD0 + h * 128:VD0 + (h + 1) * 128]
        q1 = jnp.where(lo, q, 0.0).astype(BF16)
        q2 = jnp.where(lo, 0.0, q).astype(BF16)
        o = _diff_attend([_dot_nt(q1, kdc_s[:, cols]), _dot_nt(q1, kdo_s[:, cols])],
                         [_dot_nt(q2, kdc_s[:, cols]), _dot_nt(q2, kdo_s[:, cols])],
                         [vdc_s[:, cols], v_own], lam)
        o = _rms(o) * subln_g[...] * (1.0 - lam_init)
        o_ref[:, 1536 + h * 128:1536 + (h + 1) * 128] = o.astype(BF16)


def _lat_attn_call(p, kr, caches, tables, bias, small, layer, lam_init):
    nq = DEC_SEQ // LAT_TQ
    batch_rows = lambda b, q: (b, 0)
    zero2 = lambda b, q: (0, 0)
    once = pl.Buffered(1)
    cache_args = [a.reshape(a.shape[:2] + (-1, a.shape[-1])) for a in caches]
    cache_specs = [pl.BlockSpec((None, None) + a.shape[2:], lambda b, q: (b, layer, 0, 0), pipeline_mode=once)
                   for a in cache_args]
    table_specs = [pl.BlockSpec((DEC_SEQ, LANES), zero2) for _ in tables]
    bias_spec = pl.BlockSpec((1, NA_HEADS, NBR_PAIRS, GRID_W, LANES), lambda b, q: (layer, 0, 0, 0, 0))
    n_all = PAST_LEN + DEC_SEQ
    n_pad = DEC_SEQ + 2 * SWA_PAD
    scratch = [
        pltpu.VMEM((MLA_HEADS, n_all, 256), BF16),
        pltpu.VMEM((n_all, 512), BF16),
        pltpu.VMEM((PAST_LEN, 256), BF16), pltpu.VMEM((PAST_LEN, 256), BF16),
        pltpu.VMEM((n_pad, 256), BF16), pltpu.VMEM((n_pad, 256), BF16),
        pltpu.VMEM((PAST_LEN, 512), BF16), pltpu.VMEM((PAST_LEN, 512), BF16),
        pltpu.VMEM((PAST_LEN, 512), BF16), pltpu.VMEM((PAST_LEN, 512), BF16),
        pltpu.VMEM((DEC_SEQ, 512), BF16),
    ]
    return pl.pallas_call(
        functools.partial(_lat_attn_kernel, lam_init=lam_init),
        grid=(DEC_BATCH, nq),
        in_specs=[pl.BlockSpec((DEC_SEQ, P_WIDTH), batch_rows, pipeline_mode=once),
                  pl.BlockSpec((DEC_SEQ, LANES), batch_rows, pipeline_mode=once)]
        + cache_specs + table_specs + [bias_spec] + _small_specs(),
        out_specs=pl.BlockSpec((LAT_TQ, D_MODEL), lambda b, q: (b * nq + q, 0)),
        out_shape=jax.ShapeDtypeStruct((N_LAT_TOK, D_MODEL), BF16),
        scratch_shapes=scratch,
        compiler_params=pltpu.CompilerParams(
            dimension_semantics=("arbitrary", "arbitrary"), vmem_limit_bytes=VMEM_LIMIT_BIG),
        name="lat_attn",
    )(p, kr, *cache_args, *tables, bias, *small)


def _merge_kernel(h_ref, o_ref, wg_ref, wb_ref, out_ref):
    acc = None
    for i in range(N_BRANCH):
        gate = jax.nn.sigmoid(_dot(h_ref[...], wg_ref[i]))
        term = gate * _dot(o_ref[:, i * BRANCH_W:(i + 1) * BRANCH_W], wb_ref[i])
        acc = term if acc is None else acc + term
    out_ref[...] = acc.astype(BF16)


def _merge_call(h, o, wg, wb, layer, name):
    n_tok = h.shape[0]
    tm, tn = 1024, 512
    return pl.pallas_call(
        _merge_kernel,
        grid=(n_tok // tm, D_MODEL // tn),
        in_specs=[
            pl.BlockSpec((tm, D_MODEL), lambda i, j: (i, 0)),
            pl.BlockSpec((tm, D_MODEL), lambda i, j: (i, 0)),
            pl.BlockSpec((None, N_BRANCH, D_MODEL, tn), lambda i, j: (layer, 0, 0, j)),
            pl.BlockSpec((None, N_BRANCH, BRANCH_W, tn), lambda i, j: (layer, 0, 0, j)),
        ],
        out_specs=pl.BlockSpec((tm, tn), lambda i, j: (i, j)),
        out_shape=jax.ShapeDtypeStruct((n_tok, D_MODEL), BF16),
        compiler_params=pltpu.CompilerParams(
            dimension_semantics=("arbitrary", "arbitrary"), vmem_limit_bytes=VMEM_LIMIT),
        name=name,
    )(h, o, wg, wb)


def _outproj_kernel(mixed_ref, x_ref, mod_ref, g_ref, gffn_ref, w_ref, xo_ref, h2_ref):
    m = mod_ref[0]
    half = mixed_ref.shape[0] // 2
    for r in (slice(0, half), slice(half, 2 * half)):
        out = _dot(mixed_ref[r, :], w_ref[...])
        x = x_ref[r, :] + m[2:3] * (_rms(out) * g_ref[...])
        xo_ref[r, :] = x
        h2_ref[r, :] = (_rms(x) * gffn_ref[...] * (1.0 + m[4:5]) + m[3:4]).astype(BF16)


def _outproj_call(mixed, x, mod, g, g_ffn, w, layer, seg, name):
    n_tok = x.shape[0]
    tm = 512
    return pl.pallas_call(
        _outproj_kernel,
        grid=(n_tok // tm,),
        in_specs=[
            pl.BlockSpec((tm, D_MODEL), lambda i: (i, 0)),
            pl.BlockSpec((tm, D_MODEL), lambda i: (i, 0)),
            pl.BlockSpec((1, 6, D_MODEL), lambda i: (seg(i, tm), 0, 0)),
            pl.BlockSpec((1, D_MODEL), lambda i: (0, 0)),
            pl.BlockSpec((1, D_MODEL), lambda i: (0, 0)),
            pl.BlockSpec((None, D_MODEL, D_MODEL), lambda i: (layer, 0, 0)),
        ],
        out_specs=[pl.BlockSpec((tm, D_MODEL), lambda i: (i, 0)), pl.BlockSpec((tm, D_MODEL), lambda i: (i, 0))],
        out_shape=[jax.ShapeDtypeStruct((n_tok, D_MODEL), F32), jax.ShapeDtypeStruct((n_tok, D_MODEL), BF16)],
        compiler_params=pltpu.CompilerParams(
            dimension_semantics=("arbitrary",), vmem_limit_bytes=VMEM_LIMIT),
        name=name,
    )(mixed, x, mod, g, g_ffn, w)


def _ffn_kernel(x_ref, h2_ref, mod_ref, gpost_ref, wg_ref, wu_ref, wd_ref, xo_ref, acc_s):
    j = pl.program_id(1)

    @pl.when(j == 0)
    def _():
        acc_s[...] = jnp.zeros_like(acc_s)

    a = _dot(h2_ref[...], wg_ref[...])
    b = _dot(h2_ref[...], wu_ref[...])
    t = (a * jax.nn.sigmoid(a) * b).astype(BF16)
    acc_s[...] += _dot(t, wd_ref[...])

    @pl.when(j == pl.num_programs(1) - 1)
    def _():
        m = mod_ref[0]
        xo_ref[...] = x_ref[...] + m[5:6] * (_rms(acc_s[...]) * gpost_ref[...])


def _ffn_call(x, h2, mod, gpost, wg, wu, wd, layer, seg, name):
    n_tok = x.shape[0]
    tm, th = 512, 512
    return pl.pallas_call(
        _ffn_kernel,
        grid=(n_tok // tm, FFN_HIDDEN // th),
        in_specs=[
            pl.BlockSpec((tm, D_MODEL), lambda i, j: (i, 0)),
            pl.BlockSpec((tm, D_MODEL), lambda i, j: (i, 0)),
            pl.BlockSpec((1, 6, D_MODEL), lambda i, j: (seg(i, tm), 0, 0)),
            pl.BlockSpec((1, D_MODEL), lambda i, j: (0, 0)),
            pl.BlockSpec((None, D_MODEL, th), lambda i, j: (layer, 0, j)),
            pl.BlockSpec((None, D_MODEL, th), lambda i, j: (layer, 0, j)),
            pl.BlockSpec((None, th, D_MODEL), lambda i, j: (layer, j, 0)),
        ],
        out_specs=pl.BlockSpec((tm, D_MODEL), lambda i, j: (i, 0)),
        out_shape=jax.ShapeDtypeStruct((n_tok, D_MODEL), F32),
        scratch_shapes=[pltpu.VMEM((tm, D_MODEL), F32)],
        compiler_params=pltpu.CompilerParams(
            dimension_semantics=("arbitrary", "arbitrary"), vmem_limit_bytes=VMEM_LIMIT),
        name=name,
    )(x, h2, mod, gpost, wg, wu, wd)


def _rope_tables():
    t = np.arange(DEC_SEQ)
    row = (t // GRID_W).astype(np.float32)
    col = (t % GRID_W).astype(np.float32)

    def half_angles(r):
        quarter = r // 4
        inv = (1.0 / (ROPE_BASE ** (np.arange(quarter, dtype=np.float32) / quarter))).astype(np.float32)
        return np.concatenate([row[:, None] * inv, col[:, None] * inv], axis=-1).astype(np.float32)

    a64 = half_angles(128)
    cos128 = np.concatenate([np.cos(a64), np.cos(a64)], axis=1)
    sin128 = np.concatenate([-np.sin(a64), np.sin(a64)], axis=1)
    a32 = half_angles(64)
    c, s, z = np.cos(a32), np.sin(a32), np.zeros_like(a32)
    cos64 = np.concatenate([c, c, c, c], axis=1)
    sina64 = np.concatenate([-s, z, -s, z], axis=1)
    sinb64 = np.concatenate([z, s, z, s], axis=1)
    return tuple(jnp.asarray(a, F32) for a in (cos128, sin128, cos64, sina64, sinb64))


def _ctx_seg(i, tm):
    del tm
    return i * 0


def _lat_seg(i, tm):
    return 1 + (i * tm) // DEC_SEQ


def kernel(x_prompt, x_sample, cache_mla_ckv, cache_mla_krope, cache_swa_k, cache_swa_v, cache_na_k, cache_na_v, cache_diff_k, cache_diff_v, c, c_ctx, w_ada, b_ada, mix_pre_g, mix_post_g, ffn_pre_g, ffn_post_g, w_in, mla_q_norm_g, mla_kv_norm_g, w_mla_uq, w_mla_ukv, swa_sink, na_rpb, diff_lq1, diff_lk1, diff_lq2, diff_lk2, diff_subln_g, w_mix_gate, w_branch, w_out, w_ffn_gate, w_ffn_up, w_ffn_down):
    cvec = jnp.concatenate([c_ctx[None, :], c, jnp.zeros((N_MOD_ROWS - 1 - DEC_BATCH, D_MODEL), F32)], axis=0)
    mod_all = _ada_call(cvec, w_ada, b_ada).reshape(DEPTH, N_MOD_ROWS, 6, D_MODEL)
    bias_all = _nbr_bias_call(na_rpb)
    tables = _rope_tables()
    caches = (cache_mla_ckv, cache_mla_krope, cache_swa_k, cache_swa_v, cache_na_k, cache_na_v,
              cache_diff_k, cache_diff_v)

    w_main, w_kr = _cast_w_in_call(w_in)
    wg = _cast_call(w_mix_gate, 512, "cast_w_gate")
    wb = _cast_call(w_branch, 512, "cast_w_branch")
    wo = _cast_call(w_out, 512, "cast_w_out")
    wfg = _cast_call(w_ffn_gate, 256, "cast_w_ffn_gate")
    wfu = _cast_call(w_ffn_up, 256, "cast_w_ffn_up")
    wfd = _cast_call(w_ffn_down, 512, "cast_w_ffn_down")

    y = x_prompt.reshape(N_CTX_TOK, D_MODEL)
    z = x_sample.reshape(N_LAT_TOK, D_MODEL)
    new_caches = ()
    for l in range(DEPTH):
        lam_init = _lambda_init(l)
        mod = mod_all[l]
        uq = w_mla_uq[l].reshape(MLA_Q_RANK, MLA_HEADS, MLA_NOPE + MLA_ROPE)
        uq_rope = jnp.pad(uq[:, :, MLA_NOPE:], ((0, 0), (0, 0), (0, LANES - MLA_ROPE)))
        w_uq = jnp.concatenate([uq[:, :, :MLA_NOPE].reshape(MLA_Q_RANK, 512),
                                uq_rope.reshape(MLA_Q_RANK, 512)], axis=1).astype(BF16)
        small = (mla_q_norm_g[l][None, :], w_uq, mla_kv_norm_g[l][None, :], w_mla_ukv[l].astype(BF16),
                 swa_sink[l], diff_lq1[l][None, :], diff_lk1[l][None, :], diff_lq2[l][None, :],
                 diff_lk2[l][None, :], diff_subln_g[l][None, :])
        g_pre, g_post = mix_pre_g[l][None, :], mix_post_g[l][None, :]
        f_pre, f_post = ffn_pre_g[l][None, :], ffn_post_g[l][None, :]

        h, p, kr = _inproj_call(y, mod, g_pre, w_main, w_kr, l, _ctx_seg, F32, "ctx_inproj")
        o, *new_caches = _ctx_attn_call(p, kr, small, new_caches, lam_init)
        mixed = _merge_call(h, o, wg, wb, l, "ctx_merge")
        y, h2 = _outproj_call(mixed, y, mod, g_post, f_pre, wo, l, _ctx_seg, "ctx_outproj")
        y = _ffn_call(y, h2, mod, f_post, wfg, wfu, wfd, l, _ctx_seg, "ctx_ffn")

        h, p, kr = _inproj_call(z, mod, g_pre, w_main, w_kr, l, _lat_seg, BF16, "lat_inproj")
        o = _lat_attn_call(p, kr, caches, tables, bias_all, small, l, lam_init)
        mixed = _merge_call(h, o, wg, wb, l, "lat_merge")
        z, h2 = _outproj_call(mixed, z, mod, g_post, f_pre, wo, l, _lat_seg, "lat_outproj")
        z = _ffn_call(z, h2, mod, f_post, wfg, wfu, wfd, l, _lat_seg, "lat_ffn")

    new_caches = [a.reshape((BATCH, DEPTH, SEQ) + t) for a, t in zip(new_caches, CACHE_TAILS)]
    return (y.reshape(BATCH, SEQ, D_MODEL), z.reshape(DEC_BATCH, DEC_SEQ, D_MODEL), *new_caches)
```

```python
import functools
import math

import numpy as np
import jax
import jax.numpy as jnp
from jax import lax
from jax.experimental import pallas as pl
from jax.experimental.pallas import tpu as pltpu

D_MODEL = 2048
BATCH = 32
SEQ = 256
DEPTH = 2
DEC_BATCH = 8
DEC_SEQ = 1024
PAST_LEN = 512
GRID_W = 64
HEAD_DIM = 128
ROPE_BASE = 10000.0
EPS = 1e-6
NEG_INF = -1e30
LOG2E = math.log2(math.e)

MLA_HEADS = 4
MLA_Q_RANK = 384
MLA_KV_RANK = 128
MLA_NOPE = 128
MLA_ROPE = 64
MLA_V = 128
SWA_HEADS = 4
SWA_KV_HEADS = 2
SWA_WINDOW = 128
NA_HEADS = 4
NA_ROWS = 8
NA_COLS = 16
DIFF_HEADS = 4
DIFF_QK = 64
DIFF_V = 128
N_BRANCH = 4
BRANCH_W = 512
FFN_HIDDEN = 5632

N_CTX_TOK = BATCH * SEQ
N_LAT_TOK = DEC_BATCH * DEC_SEQ
GRID_ROWS = DEC_SEQ // GRID_W

Q_DOWN0, CKV0, QB0, KB0, VB0, QC0, KC0, VC0, QD0, KD0, VD0 = (
    0, 384, 512, 1024, 1280, 1536, 2048, 2560, 3072, 3584, 4096)
KROPE_SRC0 = MLA_Q_RANK + MLA_KV_RANK
P_WIDTH = 4608
N_MOD_ROWS = 16

CACHE_TAILS = ((MLA_KV_RANK,), (MLA_ROPE,), (SWA_KV_HEADS, HEAD_DIM), (SWA_KV_HEADS, HEAD_DIM),
               (NA_HEADS, HEAD_DIM), (NA_HEADS, HEAD_DIM), (DIFF_HEADS, 2 * DIFF_QK), (DIFF_HEADS, DIFF_V))

VMEM_LIMIT = 56 * 1024 * 1024
VMEM_LIMIT_BIG = 60 * 1024 * 1024
LANES = 128

F32 = jnp.float32
BF16 = jnp.bfloat16


def _lambda_init(l):
    return 0.8 - 0.6 * math.exp(-0.3 * l)


def _rms(x):
    return x * lax.rsqrt(jnp.mean(x * x, axis=-1, keepdims=True) + EPS)


def _dot(a, b):
    return jnp.dot(a, b, preferred_element_type=F32)


def _dot_nt(a, b):
    return lax.dot_general(a, b, (((1,), (1,)), ((), ())), preferred_element_type=F32)


def _attend(pieces, sink=None):
    m = None
    for s, _ in pieces:
        mi = jnp.max(s, axis=-1, keepdims=True)
        m = mi if m is None else jnp.maximum(m, mi)
    if sink is not None:
        sink = sink * LOG2E
        m = jnp.maximum(m, sink)
    den = None
    num = None
    for s, v in pieces:
        e = jnp.exp2(s - m)
        d = jnp.sum(e, axis=-1, keepdims=True)
        n = _dot(e.astype(BF16), v)
        den = d if den is None else den + d
        num = n if num is None else num + n
    if sink is not None:
        den = den + jnp.exp2(sink - m)
    return num / den


def _diff_attend(pieces1, pieces2, values, lam):
    def exps(pieces):
        m = None
        for s in pieces:
            mi = jnp.max(s, axis=-1, keepdims=True)
            m = mi if m is None else jnp.maximum(m, mi)
        es = [jnp.exp2(s - m) for s in pieces]
        den = None
        for e in es:
            d = jnp.sum(e, axis=-1, keepdims=True)
            den = d if den is None else den + d
        return es, den

    es1, den1 = exps(pieces1)
    es2, den2 = exps(pieces2)
    c1 = 1.0 / den1
    c2 = lam / den2
    out = None
    for e1, e2, v in zip(es1, es2, values):
        n = _dot((e1 * c1 - e2 * c2).astype(BF16), v)
        out = n if out is None else out + n
    return out


def _diff_lambda(lq1, lk1, lq2, lk2, lam_init):
    a = jnp.sum(lq1[...] * lk1[...], axis=-1, keepdims=True)
    b = jnp.sum(lq2[...] * lk2[...], axis=-1, keepdims=True)
    return jnp.exp(a) - jnp.exp(b) + lam_init


def _ada_kernel(c_ref, w_ref, b_ref, o_ref):
    c = c_ref[...]
    s = (c * jax.nn.sigmoid(c)).astype(BF16)
    o_ref[0] = _dot(s, w_ref[0].astype(BF16)) + b_ref[0]


def _ada_call(cvec, w_ada, b_ada):
    tn = 1024
    n = 6 * D_MODEL
    return pl.pallas_call(
        _ada_kernel,
        grid=(DEPTH, n // tn),
        in_specs=[
            pl.BlockSpec((N_MOD_ROWS, D_MODEL), lambda l, j: (0, 0)),
            pl.BlockSpec((1, D_MODEL, tn), lambda l, j: (l, 0, j)),
            pl.BlockSpec((1, 1, tn), lambda l, j: (l, 0, j)),
        ],
        out_specs=pl.BlockSpec((1, N_MOD_ROWS, tn), lambda l, j: (l, 0, j)),
        out_shape=jax.ShapeDtypeStruct((DEPTH, N_MOD_ROWS, n), F32),
        compiler_params=pltpu.CompilerParams(
            dimension_semantics=("arbitrary", "arbitrary"), vmem_limit_bytes=VMEM_LIMIT),
        name="ada_mod",
    )(cvec, w_ada, b_ada.reshape(DEPTH, 1, n))


NBR_PAIRS = 2 * NA_ROWS


def _nbr_bias_kernel(rpb_ref, o_ref):
    l = pl.program_id(0)
    h = pl.program_id(1)
    n_dr = 2 * NA_ROWS - 1
    n_dc = 2 * NA_COLS - 1
    base = (l * NA_HEADS + h) * (n_dr * n_dc)
    wq = lax.broadcasted_iota(jnp.int32, (GRID_W, LANES), 0)
    lane = lax.broadcasted_iota(jnp.int32, (GRID_W, LANES), 1)
    upper = lane >= GRID_W
    wk = jnp.where(upper, lane - GRID_W, lane)
    delta = wk - wq + (NA_COLS - 1)
    cs = jnp.clip(wq - NA_COLS // 2, 0, GRID_W - NA_COLS)
    valid = (wk >= cs) & (wk < cs + NA_COLS)
    for e in range(NBR_PAIRS):
        dr_lo, dr_hi = max(e - 1, 0), min(e, n_dr - 1)
        acc = jnp.full((GRID_W, LANES), NEG_INF, F32)
        for d in range(n_dc):
            lo = rpb_ref[base + dr_lo * n_dc + d]
            hi = rpb_ref[base + dr_hi * n_dc + d]
            acc = jnp.where(delta == d, jnp.where(upper, hi, lo), acc)
        o_ref[0, 0, e] = jnp.where(valid, acc * LOG2E, NEG_INF)


def _nbr_bias_call(na_rpb):
    n_pairs = NBR_PAIRS
    return pl.pallas_call(
        _nbr_bias_kernel,
        grid=(DEPTH, NA_HEADS),
        in_specs=[pl.BlockSpec(memory_space=pltpu.SMEM)],
        out_specs=pl.BlockSpec((1, 1, n_pairs, GRID_W, LANES), lambda l, h: (l, h, 0, 0, 0)),
        out_shape=jax.ShapeDtypeStruct((DEPTH, NA_HEADS, n_pairs, GRID_W, LANES), F32),
        compiler_params=pltpu.CompilerParams(dimension_semantics=("arbitrary", "arbitrary")),
        name="nbr_bias",
    )(na_rpb.reshape(-1))


def _inproj_kernel(x_ref, mod_ref, g_ref, w_ref, wkr_ref, h_ref, p_ref, kr_ref):
    m = mod_ref[0]
    half = x_ref.shape[0] // 2
    for r in (slice(0, half), slice(half, 2 * half)):
        h = (_rms(x_ref[r, :]) * g_ref[...] * (1.0 + m[1:2]) + m[0:1]).astype(BF16)
        h_ref[r, :] = h
        kr_ref[r, :] = _dot_nt(h, wkr_ref[...])
        p_ref[r, :] = _dot_nt(h, w_ref[...]).astype(p_ref.dtype)


def _inproj_call(x, mod, g, w_main, w_kr, layer, seg, p_dtype, name):
    n_tok = x.shape[0]
    tm = 512
    return pl.pallas_call(
        _inproj_kernel,
        grid=(n_tok // tm,),
        in_specs=[
            pl.BlockSpec((tm, D_MODEL), lambda i: (i, 0)),
            pl.BlockSpec((1, 6, D_MODEL), lambda i: (seg(i, tm), 0, 0)),
            pl.BlockSpec((1, D_MODEL), lambda i: (0, 0)),
            pl.BlockSpec((None, P_WIDTH, D_MODEL), lambda i: (layer, 0, 0)),
            pl.BlockSpec((None, LANES, D_MODEL), lambda i: (layer, 0, 0)),
        ],
        out_specs=[
            pl.BlockSpec((tm, D_MODEL), lambda i: (i, 0)),
            pl.BlockSpec((tm, P_WIDTH), lambda i: (i, 0)),
            pl.BlockSpec((tm, LANES), lambda i: (i, 0)),
        ],
        out_shape=[
            jax.ShapeDtypeStruct((n_tok, D_MODEL), BF16),
            jax.ShapeDtypeStruct((n_tok, P_WIDTH), p_dtype),
            jax.ShapeDtypeStruct((n_tok, LANES), F32),
        ],
        compiler_params=pltpu.CompilerParams(
            dimension_semantics=("arbitrary",), vmem_limit_bytes=VMEM_LIMIT_BIG),
        name=name,
    )(x, mod, g, w_main, w_kr)


def _cast_kernel(x_ref, o_ref):
    o_ref[...] = x_ref[...].astype(BF16)


def _cast_call(w, block_rows, name):
    w2 = w.reshape(-1, w.shape[-1])
    rows, cols = w2.shape
    out = pl.pallas_call(
        _cast_kernel,
        grid=(rows // block_rows,),
        in_specs=[pl.BlockSpec((block_rows, cols), lambda i: (i, 0))],
        out_specs=pl.BlockSpec((block_rows, cols), lambda i: (i, 0)),
        out_shape=jax.ShapeDtypeStruct((rows, cols), BF16),
        compiler_params=pltpu.CompilerParams(dimension_semantics=("arbitrary",), vmem_limit_bytes=VMEM_LIMIT),
        name=name,
    )(w2)
    return out.reshape(w.shape)


def _cast_w_in_kernel(x_ref, xkr_ref, main_ref, kr_ref):
    main_ref[...] = x_ref[0].astype(BF16)

    @pl.when(pl.program_id(1) == 0)
    def _():
        kr_ref[:MLA_ROPE, :] = xkr_ref[...].astype(BF16)
        kr_ref[MLA_ROPE:, :] = jnp.zeros((LANES - MLA_ROPE, D_MODEL), BF16)


def _cast_w_in_call(w_in):
    w_t = jnp.swapaxes(w_in, 1, 2)
    block_rows = 512
    src_row = lambda l, j: (l, pl.multiple_of(j * block_rows + jnp.where(j * block_rows >= KROPE_SRC0, MLA_ROPE, 0), MLA_ROPE), 0)
    return pl.pallas_call(
        _cast_w_in_kernel,
        grid=(DEPTH, P_WIDTH // block_rows),
        in_specs=[pl.BlockSpec((pl.Element(1), pl.Element(block_rows), pl.Element(D_MODEL)), src_row),
                  pl.BlockSpec((None, MLA_ROPE, D_MODEL), lambda l, j: (l, KROPE_SRC0 // MLA_ROPE, 0))],
        out_specs=[pl.BlockSpec((None, block_rows, D_MODEL), lambda l, j: (l, j, 0)),
                   pl.BlockSpec((None, LANES, D_MODEL), lambda l, j: (l, 0, 0))],
        out_shape=[jax.ShapeDtypeStruct((DEPTH, P_WIDTH, D_MODEL), BF16),
                   jax.ShapeDtypeStruct((DEPTH, LANES, D_MODEL), BF16)],
        compiler_params=pltpu.CompilerParams(
            dimension_semantics=("arbitrary", "arbitrary"), vmem_limit_bytes=VMEM_LIMIT),
        name="cast_w_in",
    )(w_t, w_t)


def _mla_queries(q_down, qn_g, w_uq):
    cq = _rms(q_down) * qn_g[...]
    return _dot(cq.astype(BF16), w_uq[...])


def _ctx_attn_kernel(*refs, lam_init, first):
    (p_ref, kr_ref, qn_g, w_uq, kvn_g, w_ukv, sink_ref, lq1, lk1, lq2, lk2, subln_g) = refs[:12]
    n_in = 12 + (0 if first else len(CACHE_TAILS))
    o_ref = refs[n_in]
    ckv_o, kr_o, kb_o, vb_o, kc_o, vc_o, kd_o, vd_o = refs[n_in + 1:]
    ls = 0
    if first:
        for dst in (ckv_o, kr_o, kb_o, vb_o, kc_o, vc_o, kd_o, vd_o):
            dst[0, 1:] = jnp.zeros(dst.shape[1:], F32)[1:]

    scale_a = LOG2E * (MLA_NOPE + MLA_ROPE) ** -0.5
    qa = _mla_queries(p_ref[:, Q_DOWN0:Q_DOWN0 + MLA_Q_RANK], qn_g, w_uq) * scale_a
    ckv = _rms(p_ref[:, CKV0:CKV0 + MLA_KV_RANK]) * kvn_g[...]
    ckv_o[0, ls] = ckv
    kr = kr_ref[...]
    kr_o[0, ls] = kr[:, :MLA_ROPE]
    kr_b = kr.astype(BF16)
    kv = _dot(ckv.astype(BF16), w_ukv[...])
    for h in range(MLA_HEADS):
        q = jnp.concatenate([qa[:, h * 128:(h + 1) * 128], qa[:, 512 + h * 128:512 + (h + 1) * 128]],
                            axis=1).astype(BF16)
        k = jnp.concatenate([kv[:, h * 256:h * 256 + 128].astype(BF16), kr_b], axis=1)
        v = kv[:, h * 256 + 128:(h + 1) * 256].astype(BF16)
        o = _attend([(_dot_nt(q, k), v)])
        o_ref[:, h * 128:(h + 1) * 128] = o.astype(BF16)

    scale = LOG2E * HEAD_DIM ** -0.5

    def head(col0, h):
        return p_ref[:, col0 + h * 128:col0 + (h + 1) * 128]

    g_size = SWA_HEADS // SWA_KV_HEADS
    for g in range(SWA_KV_HEADS):
        k = head(KB0, g)
        v = head(VB0, g)
        kb_o[0, ls, pl.ds(g, SEQ, stride=SWA_KV_HEADS), :] = k
        vb_o[0, ls, pl.ds(g, SEQ, stride=SWA_KV_HEADS), :] = v
        k = k.astype(BF16)
        v = v.astype(BF16)
        for h in range(g * g_size, (g + 1) * g_size):
            q = (head(QB0, h) * scale).astype(BF16)
            o = _attend([(_dot_nt(q, k), v)], sink=sink_ref[h])
            o_ref[:, 512 + h * 128:512 + (h + 1) * 128] = o.astype(BF16)

    for h in range(NA_HEADS):
        k = head(KC0, h)
        v = head(VC0, h)
        kc_o[0, ls, pl.ds(h, SEQ, stride=NA_HEADS), :] = k
        vc_o[0, ls, pl.ds(h, SEQ, stride=NA_HEADS), :] = v
        q = (head(QC0, h) * scale).astype(BF16)
        o = _attend([(_dot_nt(q, k.astype(BF16)), v.astype(BF16))])
        o_ref[:, 1024 + h * 128:1024 + (h + 1) * 128] = o.astype(BF16)

    lam = _diff_lambda(lq1, lk1, lq2, lk2, lam_init)
    scale_d = LOG2E * DIFF_QK ** -0.5
    lo = lax.broadcasted_iota(jnp.int32, (SEQ, 128), 1) < DIFF_QK
    for h in range(DIFF_HEADS):
        k = head(KD0, h)
        v = head(VD0, h)
        kd_o[0, ls, pl.ds(h, SEQ, stride=DIFF_HEADS), :] = k
        vd_o[0, ls, pl.ds(h, SEQ, stride=DIFF_HEADS), :] = v
        k = k.astype(BF16)
        v = v.astype(BF16)
        q = head(QD0, h) * scale_d
        o1 = _attend([(_dot_nt(jnp.where(lo, q, 0.0).astype(BF16), k), v)])
        o2 = _attend([(_dot_nt(jnp.where(lo, 0.0, q).astype(BF16), k), v)])
        o = _rms(o1 - lam * o2) * subln_g[...] * (1.0 - lam_init)
        o_ref[:, 1536 + h * 128:1536 + (h + 1) * 128] = o.astype(BF16)


def _small_specs():
    zero2 = (lambda *a: (0, 0))
    return [
        pl.BlockSpec((1, MLA_Q_RANK), zero2),
        pl.BlockSpec((MLA_Q_RANK, 1024), zero2),
        pl.BlockSpec((1, MLA_KV_RANK), zero2),
        pl.BlockSpec((MLA_KV_RANK, 1024), zero2),
        pl.BlockSpec(memory_space=pltpu.SMEM),
        pl.BlockSpec((1, DIFF_QK), zero2),
        pl.BlockSpec((1, DIFF_QK), zero2),
        pl.BlockSpec((1, DIFF_QK), zero2),
        pl.BlockSpec((1, DIFF_QK), zero2),
        pl.BlockSpec((1, DIFF_V), zero2),
    ]


def _ctx_attn_call(p, kr, small, prev, layer, lam_init):
    first = not prev
    row = lambda b: (b, 0)

    def flat(tail):
        return (SEQ * tail[0], tail[1]) if len(tail) == 2 else (SEQ,) + tail

    def cache_spec(tail):
        if first:
            return pl.BlockSpec((1, DEPTH) + flat(tail), lambda b: (b, 0, 0, 0))
        return pl.BlockSpec((1, 1) + flat(tail), lambda b: (b, layer, 0, 0))

    n_fixed = 12
    return pl.pallas_call(
        functools.partial(_ctx_attn_kernel, lam_init=lam_init, first=first),
        grid=(BATCH,),
        in_specs=[pl.BlockSpec((SEQ, P_WIDTH), row), pl.BlockSpec((SEQ, LANES), row)] + _small_specs()
        + [pl.BlockSpec(memory_space=pl.ANY) for _ in prev],
        out_specs=[pl.BlockSpec((SEQ, D_MODEL), row)] + [cache_spec(t) for t in CACHE_TAILS],
        out_shape=[jax.ShapeDtypeStruct((N_CTX_TOK, D_MODEL), BF16)]
        + [jax.ShapeDtypeStruct((BATCH, DEPTH) + flat(t), F32) for t in CACHE_TAILS],
        input_output_aliases={n_fixed + k: 1 + k for k in range(len(prev))},
        compiler_params=pltpu.CompilerParams(
            dimension_semantics=("arbitrary",), vmem_limit_bytes=VMEM_LIMIT),
        name="ctx_attn",
    )(p, kr, *small, *prev)


LAT_TQ = 256
SWA_PAD = SWA_WINDOW
NBR_WIN_ROWS = 12


def _rope_pair64(x, c_ref, sa_ref, sb_ref, rows):
    out = []
    for j in range(x.shape[1] // LANES):
        xj = x[:, j * LANES:(j + 1) * LANES]
        out.append(xj * c_ref[rows, :] + pltpu.roll(xj, 96, 1) * sa_ref[rows, :]
                   + pltpu.roll(xj, 32, 1) * sb_ref[rows, :])
    return jnp.concatenate(out, axis=1) if len(out) > 1 else out[0]


def _rope128(x, c_ref, s_ref, rows):
    out = []
    for j in range(x.shape[1] // LANES):
        xj = x[:, j * LANES:(j + 1) * LANES]
        out.append(xj * c_ref[rows, :] + pltpu.roll(xj, 64, 1) * s_ref[rows, :])
    return jnp.concatenate(out, axis=1) if len(out) > 1 else out[0]


def _lat_attn_kernel(p_ref, kr_ref, c_ckv, c_kr, c_kb, c_vb, c_kc, c_vc, c_kd, c_vd,
                     cos128, sin128, cos64, sina64, sinb64, bias_ref,
                     qn_g, w_uq, kvn_g, w_ukv, sink_ref, lq1, lk1, lq2, lk2, subln_g,
                     o_ref,
                     ka_s, va_s, kbc_s, vbc_s, kbp_s, vbp_s, kcc_s, vcc_s, kdc_s, vdc_s, kdo_s,
                     *, lam_init):
    def cached_head(ref, h, n_heads):
        return ref[pl.ds(h, PAST_LEN, stride=n_heads), :].astype(BF16)

    qi = pl.program_id(1)
    all_rows = slice(None)
    n_own = DEC_SEQ

    @pl.when(qi == 0)
    def _prologue():
        ckv_own = (_rms(p_ref[:, CKV0:CKV0 + MLA_KV_RANK].astype(F32)) * kvn_g[...]).astype(BF16)
        ckv_all = jnp.concatenate([c_ckv[...].astype(BF16), ckv_own], axis=0)
        kr_ctx = jnp.concatenate([c_kr[...], jnp.zeros((PAST_LEN, LANES - MLA_ROPE), F32)], axis=1)
        kr_own = _rope_pair64(kr_ref[...], cos64, sina64, sinb64, all_rows)
        kr_all = jnp.concatenate([kr_ctx, kr_own], axis=0).astype(BF16)
        for h in range(MLA_HEADS):
            kvh = _dot(ckv_all, w_ukv[:, h * 256:(h + 1) * 256])
            ka_s[h, :, 0:128] = kvh[:, 0:128].astype(BF16)
            ka_s[h, :, 128:256] = kr_all
            va_s[:, h * 128:(h + 1) * 128] = kvh[:, 128:256].astype(BF16)
        for g in range(SWA_KV_HEADS):
            kbc_s[:, g * 128:(g + 1) * 128] = cached_head(c_kb, g, SWA_KV_HEADS)
            vbc_s[:, g * 128:(g + 1) * 128] = cached_head(c_vb, g, SWA_KV_HEADS)
        zpad = jnp.zeros((SWA_PAD, 256), BF16)
        kbp_s[0:SWA_PAD, :] = zpad
        kbp_s[SWA_PAD + n_own:, :] = zpad
        vbp_s[0:SWA_PAD, :] = zpad
        vbp_s[SWA_PAD + n_own:, :] = zpad
        kbp_s[SWA_PAD:SWA_PAD + n_own, :] = _rope128(
            p_ref[:, KB0:KB0 + 256].astype(F32), cos128, sin128, all_rows).astype(BF16)
        vbp_s[SWA_PAD:SWA_PAD + n_own, :] = p_ref[:, VB0:VB0 + 256]
        for h in range(NA_HEADS):
            cols = slice(h * 128, (h + 1) * 128)
            kcc_s[:, cols] = cached_head(c_kc, h, NA_HEADS)
            vcc_s[:, cols] = cached_head(c_vc, h, NA_HEADS)
            kdc_s[:, cols] = cached_head(c_kd, h, DIFF_HEADS)
            vdc_s[:, cols] = cached_head(c_vd, h, DIFF_HEADS)
        kdo_s[...] = _rope_pair64(
            p_ref[:, KD0:KD0 + 512].astype(F32), cos64, sina64, sinb64, all_rows).astype(BF16)

    q0 = pl.multiple_of(qi * LAT_TQ, LAT_TQ)
    rows = pl.ds(q0, LAT_TQ)

    scale_a = LOG2E * (MLA_NOPE + MLA_ROPE) ** -0.5
    qa = _mla_queries(p_ref[rows, Q_DOWN0:Q_DOWN0 + MLA_Q_RANK].astype(F32), qn_g, w_uq) * scale_a
    qa_r = _rope_pair64(qa[:, 512:1024], cos64, sina64, sinb64, rows)
    for h in range(MLA_HEADS):
        q = jnp.concatenate([qa[:, h * 128:(h + 1) * 128], qa_r[:, h * 128:(h + 1) * 128]],
                            axis=1).astype(BF16)
        o = _attend([(_dot_nt(q, ka_s[h]), va_s[:, h * 128:(h + 1) * 128])])
        o_ref[:, h * 128:(h + 1) * 128] = o.astype(BF16)

    scale = LOG2E * HEAD_DIM ** -0.5
    qb = (_rope128(p_ref[rows, QB0:QB0 + 512].astype(F32), cos128, sin128, rows) * scale).astype(BF16)
    n_loc = LAT_TQ + 2 * SWA_PAD
    qpos = lax.broadcasted_iota(jnp.int32, (LAT_TQ, n_loc), 0)
    kpos = lax.broadcasted_iota(jnp.int32, (LAT_TQ, n_loc), 1) - SWA_PAD
    kabs = kpos + q0
    valid = (jnp.abs(qpos - kpos) <= SWA_WINDOW) & (kabs >= 0) & (kabs < n_own)
    loc_rows = pl.ds(q0, n_loc)
    g_size = SWA_HEADS // SWA_KV_HEADS
    for h in range(SWA_HEADS):
        g = h // g_size
        q = qb[:, h * 128:(h + 1) * 128]
        s_ctx = _dot_nt(q, kbc_s[:, g * 128:(g + 1) * 128])
        s_loc = jnp.where(valid, _dot_nt(q, kbp_s[loc_rows, g * 128:(g + 1) * 128]), NEG_INF)
        o = _attend([(s_ctx, vbc_s[:, g * 128:(g + 1) * 128]),
                     (s_loc, vbp_s[loc_rows, g * 128:(g + 1) * 128])], sink=sink_ref[h])
        o_ref[:, 512 + h * 128:512 + (h + 1) * 128] = o.astype(BF16)

    q_rows = LAT_TQ // GRID_W
    ws = jnp.clip(qi * q_rows - NA_ROWS // 2, 0, GRID_ROWS - NBR_WIN_ROWS)
    ws = (ws // 2) * 2
    krows = pl.ds(pl.multiple_of(ws * GRID_W, 2 * GRID_W), NBR_WIN_ROWS * GRID_W)
    lower = lax.broadcasted_iota(jnp.int32, (GRID_W, LANES), 1) < GRID_W
    for h in range(NA_HEADS):
        cols = slice(h * 128, (h + 1) * 128)
        bias_rows = []
        for t in range(q_rows):
            r = qi * q_rows + t
            off = jnp.clip(r - NA_ROWS // 2, 0, GRID_ROWS - NA_ROWS) - r + (NA_ROWS - 1)
            blocks = []
            for j in range(NBR_WIN_ROWS // 2):
                dr = ws + 2 * j - r + (NA_ROWS - 1)
                ok_lo = ((dr >= off) & (dr < off + NA_ROWS)).astype(jnp.int32)
                ok_hi = ((dr + 1 >= off) & (dr + 1 < off + NA_ROWS)).astype(jnp.int32)
                ok = jnp.where(lower, ok_lo, ok_hi) > 0
                blk = bias_ref[0, h, jnp.clip(dr + 1, 0, NBR_PAIRS - 1)]
                blocks.append(jnp.where(ok, blk, NEG_INF))
            bias_rows.append(jnp.concatenate(blocks, axis=1))
        bias = jnp.concatenate(bias_rows, axis=0)
        q = (p_ref[rows, QC0 + h * 128:QC0 + (h + 1) * 128].astype(F32) * scale).astype(BF16)
        s_loc = _dot_nt(q, p_ref[krows, KC0 + h * 128:KC0 + (h + 1) * 128]) + bias
        s_ctx = _dot_nt(q, kcc_s[:, cols])
        o = _attend([(s_ctx, vcc_s[:, cols]), (s_loc, p_ref[krows, VC0 + h * 128:VC0 + (h + 1) * 128])])
        o_ref[:, 1024 + h * 128:1024 + (h + 1) * 128] = o.astype(BF16)

    scale_d = LOG2E * DIFF_QK ** -0.5
    qd = _rope_pair64(p_ref[rows, QD0:QD0 + 512].astype(F32), cos64, sina64, sinb64, rows) * scale_d
    lam = _diff_lambda(lq1, lk1, lq2, lk2, lam_init)
    lo = lax.broadcasted_iota(jnp.int32, (LAT_TQ, 128), 1) < DIFF_QK
    for h in range(DIFF_HEADS):
        cols = slice(h * 128, (h + 1) * 128)
        q = qd[:, cols]
        v_own = p_ref[:, VD0 + h * 128:VD0 + (h + 1) * 128]
        q1 = jnp.where(lo, q, 0.0).astype(BF16)
        q2 = jnp.where(lo, 0.0, q).astype(BF16)
        o = _diff_attend([_dot_nt(q1, kdc_s[:, cols]), _dot_nt(q1, kdo_s[:, cols])],
                         [_dot_nt(q2, kdc_s[:, cols]), _dot_nt(q2, kdo_s[:, cols])],
                         [vdc_s[:, cols], v_own], lam)
        o = _rms(o) * subln_g[...] * (1.0 - lam_init)
        o_ref[:, 1536 + h * 128:1536 + (h + 1) * 128] = o.astype(BF16)


def _lat_attn_call(p, kr, caches, tables, bias, small, layer, lam_init):
    nq = DEC_SEQ // LAT_TQ
    batch_rows = lambda b, q: (b, 0)
    zero2 = lambda b, q: (0, 0)
    once = pl.Buffered(1)
    cache_args = [a.reshape(a.shape[:2] + (-1, a.shape[-1])) for a in caches]
    cache_specs = [pl.BlockSpec((None, None) + a.shape[2:], lambda b, q: (b, layer, 0, 0), pipeline_mode=once)
                   for a in cache_args]
    table_specs = [pl.BlockSpec((DEC_SEQ, LANES), zero2) for _ in tables]
    bias_spec = pl.BlockSpec((1, NA_HEADS, NBR_PAIRS, GRID_W, LANES), lambda b, q: (layer, 0, 0, 0, 0))
    n_all = PAST_LEN + DEC_SEQ
    n_pad = DEC_SEQ + 2 * SWA_PAD
    scratch = [
        pltpu.VMEM((MLA_HEADS, n_all, 256), BF16),
        pltpu.VMEM((n_all, 512), BF16),
        pltpu.VMEM((PAST_LEN, 256), BF16), pltpu.VMEM((PAST_LEN, 256), BF16),
        pltpu.VMEM((n_pad, 256), BF16), pltpu.VMEM((n_pad, 256), BF16),
        pltpu.VMEM((PAST_LEN, 512), BF16), pltpu.VMEM((PAST_LEN, 512), BF16),
        pltpu.VMEM((PAST_LEN, 512), BF16), pltpu.VMEM((PAST_LEN, 512), BF16),
        pltpu.VMEM((DEC_SEQ, 512), BF16),
    ]
    return pl.pallas_call(
        functools.partial(_lat_attn_kernel, lam_init=lam_init),
        grid=(DEC_BATCH, nq),
        in_specs=[pl.BlockSpec((DEC_SEQ, P_WIDTH), batch_rows, pipeline_mode=once),
                  pl.BlockSpec((DEC_SEQ, LANES), batch_rows, pipeline_mode=once)]
        + cache_specs + table_specs + [bias_spec] + _small_specs(),
        out_specs=pl.BlockSpec((LAT_TQ, D_MODEL), lambda b, q: (b * nq + q, 0)),
        out_shape=jax.ShapeDtypeStruct((N_LAT_TOK, D_MODEL), BF16),
        scratch_shapes=scratch,
        compiler_params=pltpu.CompilerParams(
            dimension_semantics=("arbitrary", "arbitrary"), vmem_limit_bytes=VMEM_LIMIT_BIG),
        name="lat_attn",
    )(p, kr, *cache_args, *tables, bias, *small)


def _merge_kernel(h_ref, o_ref, wg_ref, wb_ref, out_ref):
    acc = None
    for i in range(N_BRANCH):
        gate = jax.nn.sigmoid(_dot(h_ref[...], wg_ref[i]))
        term = gate * _dot(o_ref[:, i * BRANCH_W:(i + 1) * BRANCH_W], wb_ref[i])
        acc = term if acc is None else acc + term
    out_ref[...] = acc.astype(BF16)


def _merge_call(h, o, wg, wb, layer, name):
    n_tok = h.shape[0]
    tm, tn = 1024, 512
    return pl.pallas_call(
        _merge_kernel,
        grid=(n_tok // tm, D_MODEL // tn),
        in_specs=[
            pl.BlockSpec((tm, D_MODEL), lambda i, j: (i, 0)),
            pl.BlockSpec((tm, D_MODEL), lambda i, j: (i, 0)),
            pl.BlockSpec((None, N_BRANCH, D_MODEL, tn), lambda i, j: (layer, 0, 0, j)),
            pl.BlockSpec((None, N_BRANCH, BRANCH_W, tn), lambda i, j: (layer, 0, 0, j)),
        ],
        out_specs=pl.BlockSpec((tm, tn), lambda i, j: (i, j)),
        out_shape=jax.ShapeDtypeStruct((n_tok, D_MODEL), BF16),
        compiler_params=pltpu.CompilerParams(
            dimension_semantics=("arbitrary", "arbitrary"), vmem_limit_bytes=VMEM_LIMIT),
        name=name,
    )(h, o, wg, wb)


def _outproj_kernel(mixed_ref, x_ref, mod_ref, g_ref, gffn_ref, w_ref, xo_ref, h2_ref):
    m = mod_ref[0]
    half = mixed_ref.shape[0] // 2
    for r in (slice(0, half), slice(half, 2 * half)):
        out = _dot(mixed_ref[r, :], w_ref[...])
        x = x_ref[r, :] + m[2:3] * (_rms(out) * g_ref[...])
        xo_ref[r, :] = x
        h2_ref[r, :] = (_rms(x) * gffn_ref[...] * (1.0 + m[4:5]) + m[3:4]).astype(BF16)


def _outproj_call(mixed, x, mod, g, g_ffn, w, layer, seg, name):
    n_tok = x.shape[0]
    tm = 512
    return pl.pallas_call(
        _outproj_kernel,
        grid=(n_tok // tm,),
        in_specs=[
            pl.BlockSpec((tm, D_MODEL), lambda i: (i, 0)),
            pl.BlockSpec((tm, D_MODEL), lambda i: (i, 0)),
            pl.BlockSpec((1, 6, D_MODEL), lambda i: (seg(i, tm), 0, 0)),
            pl.BlockSpec((1, D_MODEL), lambda i: (0, 0)),
            pl.BlockSpec((1, D_MODEL), lambda i: (0, 0)),
            pl.BlockSpec((None, D_MODEL, D_MODEL), lambda i: (layer, 0, 0)),
        ],
        out_specs=[pl.BlockSpec((tm, D_MODEL), lambda i: (i, 0)), pl.BlockSpec((tm, D_MODEL), lambda i: (i, 0))],
        out_shape=[jax.ShapeDtypeStruct((n_tok, D_MODEL), F32), jax.ShapeDtypeStruct((n_tok, D_MODEL), BF16)],
        compiler_params=pltpu.CompilerParams(
            dimension_semantics=("arbitrary",), vmem_limit_bytes=VMEM_LIMIT),
        name=name,
    )(mixed, x, mod, g, g_ffn, w)


def _ffn_kernel(x_ref, h2_ref, mod_ref, gpost_ref, wg_ref, wu_ref, wd_ref, xo_ref, acc_s):
    j = pl.program_id(1)

    @pl.when(j == 0)
    def _():
        acc_s[...] = jnp.zeros_like(acc_s)

    a = _dot(h2_ref[...], wg_ref[...])
    b = _dot(h2_ref[...], wu_ref[...])
    t = (a * jax.nn.sigmoid(a) * b).astype(BF16)
    acc_s[...] += _dot(t, wd_ref[...])

    @pl.when(j == pl.num_programs(1) - 1)
    def _():
        m = mod_ref[0]
        xo_ref[...] = x_ref[...] + m[5:6] * (_rms(acc_s[...]) * gpost_ref[...])


def _ffn_call(x, h2, mod, gpost, wg, wu, wd, layer, seg, name):
    n_tok = x.shape[0]
    tm, th = 512, 512
    return pl.pallas_call(
        _ffn_kernel,
        grid=(n_tok // tm, FFN_HIDDEN // th),
        in_specs=[
            pl.BlockSpec((tm, D_MODEL), lambda i, j: (i, 0)),
            pl.BlockSpec((tm, D_MODEL), lambda i, j: (i, 0)),
            pl.BlockSpec((1, 6, D_MODEL), lambda i, j: (seg(i, tm), 0, 0)),
            pl.BlockSpec((1, D_MODEL), lambda i, j: (0, 0)),
            pl.BlockSpec((None, D_MODEL, th), lambda i, j: (layer, 0, j)),
            pl.BlockSpec((None, D_MODEL, th), lambda i, j: (layer, 0, j)),
            pl.BlockSpec((None, th, D_MODEL), lambda i, j: (layer, j, 0)),
        ],
        out_specs=pl.BlockSpec((tm, D_MODEL), lambda i, j: (i, 0)),
        out_shape=jax.ShapeDtypeStruct((n_tok, D_MODEL), F32),
        scratch_shapes=[pltpu.VMEM((tm, D_MODEL), F32)],
        compiler_params=pltpu.CompilerParams(
            dimension_semantics=("arbitrary", "arbitrary"), vmem_limit_bytes=VMEM_LIMIT),
        name=name,
    )(x, h2, mod, gpost, wg, wu, wd)


def _rope_tables():
    t = np.arange(DEC_SEQ)
    row = (t // GRID_W).astype(np.float32)
    col = (t % GRID_W).astype(np.float32)

    def half_angles(r):
        quarter = r // 4
        inv = (1.0 / (ROPE_BASE ** (np.arange(quarter, dtype=np.float32) / quarter))).astype(np.float32)
        return np.concatenate([row[:, None] * inv, col[:, None] * inv], axis=-1).astype(np.float32)

    a64 = half_angles(128)
    cos128 = np.concatenate([np.cos(a64), np.cos(a64)], axis=1)
    sin128 = np.concatenate([-np.sin(a64), np.sin(a64)], axis=1)
    a32 = half_angles(64)
    c, s, z = np.cos(a32), np.sin(a32), np.zeros_like(a32)
    cos64 = np.concatenate([c, c, c, c], axis=1)
    sina64 = np.concatenate([-s, z, -s, z], axis=1)
    sinb64 = np.concatenate([z, s, z, s], axis=1)
    return tuple(jnp.asarray(a, F32) for a in (cos128, sin128, cos64, sina64, sinb64))


def _ctx_seg(i, tm):
    del tm
    return i * 0


def _lat_seg(i, tm):
    return 1 + (i * tm) // DEC_SEQ


def kernel(x_prompt, x_sample, cache_mla_ckv, cache_mla_krope, cache_swa_k, cache_swa_v, cache_na_k, cache_na_v, cache_diff_k, cache_diff_v, c, c_ctx, w_ada, b_ada, mix_pre_g, mix_post_g, ffn_pre_g, ffn_post_g, w_in, mla_q_norm_g, mla_kv_norm_g, w_mla_uq, w_mla_ukv, swa_sink, na_rpb, diff_lq1, diff_lk1, diff_lq2, diff_lk2, diff_subln_g, w_mix_gate, w_branch, w_out, w_ffn_gate, w_ffn_up, w_ffn_down):
    cvec = jnp.concatenate([c_ctx[None, :], c, jnp.zeros((N_MOD_ROWS - 1 - DEC_BATCH, D_MODEL), F32)], axis=0)
    mod_all = _ada_call(cvec, w_ada, b_ada).reshape(DEPTH, N_MOD_ROWS, 6, D_MODEL)
    bias_all = _nbr_bias_call(na_rpb)
    tables = _rope_tables()
    caches = (cache_mla_ckv, cache_mla_krope, cache_swa_k, cache_swa_v, cache_na_k, cache_na_v,
              cache_diff_k, cache_diff_v)

    w_main, w_kr = _cast_w_in_call(w_in)
    wg = _cast_call(w_mix_gate, 512, "cast_w_gate")
    wb = _cast_call(w_branch, 512, "cast_w_branch")
    wo = _cast_call(w_out, 512, "cast_w_out")
    wfg = _cast_call(w_ffn_gate, 256, "cast_w_ffn_gate")
    wfu = _cast_call(w_ffn_up, 256, "cast_w_ffn_up")
    wfd = _cast_call(w_ffn_down, 512, "cast_w_ffn_down")

    y = x_prompt.reshape(N_CTX_TOK, D_MODEL)
    z = x_sample.reshape(N_LAT_TOK, D_MODEL)
    new_caches = ()
    for l in range(DEPTH):
        lam_init = _lambda_init(l)
        mod = mod_all[l]
        uq = w_mla_uq[l].reshape(MLA_Q_RANK, MLA_HEADS, MLA_NOPE + MLA_ROPE)
        uq_rope = jnp.pad(uq[:, :, MLA_NOPE:], ((0, 0), (0, 0), (0, LANES - MLA_ROPE)))
        w_uq = jnp.concatenate([uq[:, :, :MLA_NOPE].reshape(MLA_Q_RANK, 512),
                                uq_rope.reshape(MLA_Q_RANK, 512)], axis=1).astype(BF16)
        small = (mla_q_norm_g[l][None, :], w_uq, mla_kv_norm_g[l][None, :], w_mla_ukv[l].astype(BF16),
                 swa_sink[l], diff_lq1[l][None, :], diff_lk1[l][None, :], diff_lq2[l][None, :],
                 diff_lk2[l][None, :], diff_subln_g[l][None, :])
        g_pre, g_post = mix_pre_g[l][None, :], mix_post_g[l][None, :]
        f_pre, f_post = ffn_pre_g[l][None, :], ffn_post_g[l][None, :]

        h, p, kr = _inproj_call(y, mod, g_pre, w_main, w_kr, l, _ctx_seg, F32, "ctx_inproj")
        o, *new_caches = _ctx_attn_call(p, kr, small, new_caches, l, lam_init)
        mixed = _merge_call(h, o, wg, wb, l, "ctx_merge")
        y, h2 = _outproj_call(mixed, y, mod, g_post, f_pre, wo, l, _ctx_seg, "ctx_outproj")
        y = _ffn_call(y, h2, mod, f_post, wfg, wfu, wfd, l, _ctx_seg, "ctx_ffn")

        h, p, kr = _inproj_call(z, mod, g_pre, w_main, w_kr, l, _lat_seg, BF16, "lat_inproj")
        o = _lat_attn_call(p, kr, caches, tables, bias_all, small, l, lam_init)
        mixed = _merge_call(h, o, wg, wb, l, "lat_merge")
        z, h2 = _outproj_call(mixed, z, mod, g_post, f_pre, wo, l, _lat_seg, "lat_outproj")
        z = _ffn_call(z, h2, mod, f_post, wfg, wfu, wfd, l, _lat_seg, "lat_ffn")

    new_caches = [a.reshape((BATCH, DEPTH, SEQ) + t) for a, t in zip(new_caches, CACHE_TAILS)]
    return (y.reshape(BATCH, SEQ, D_MODEL), z.reshape(DEC_BATCH, DEC_SEQ, D_MODEL), *new_caches)
```

```python
import functools
import math

import numpy as np
import jax
import jax.numpy as jnp
from jax import lax
from jax.experimental import pallas as pl
from jax.experimental.pallas import tpu as pltpu

D_MODEL = 2048
BATCH = 32
SEQ = 256
DEPTH = 2
DEC_BATCH = 8
DEC_SEQ = 1024
PAST_LEN = 512
GRID_W = 64
HEAD_DIM = 128
ROPE_BASE = 10000.0
EPS = 1e-6
NEG_INF = -1e30
LOG2E = math.log2(math.e)

MLA_HEADS = 4
MLA_Q_RANK = 384
MLA_KV_RANK = 128
MLA_NOPE = 128
MLA_ROPE = 64
MLA_V = 128
SWA_HEADS = 4
SWA_KV_HEADS = 2
SWA_WINDOW = 128
NA_HEADS = 4
NA_ROWS = 8
NA_COLS = 16
DIFF_HEADS = 4
DIFF_QK = 64
DIFF_V = 128
N_BRANCH = 4
BRANCH_W = 512
FFN_HIDDEN = 5632

N_CTX_TOK = BATCH * SEQ
N_LAT_TOK = DEC_BATCH * DEC_SEQ
GRID_ROWS = DEC_SEQ // GRID_W

Q_DOWN0, CKV0, QB0, KB0, VB0, QC0, KC0, VC0, QD0, KD0, VD0 = (
    0, 384, 512, 1024, 1280, 1536, 2048, 2560, 3072, 3584, 4096)
KROPE_SRC0 = MLA_Q_RANK + MLA_KV_RANK
P_WIDTH = 4608
N_MOD_ROWS = 16

CACHE_TAILS = ((MLA_KV_RANK,), (MLA_ROPE,), (SWA_KV_HEADS, HEAD_DIM), (SWA_KV_HEADS, HEAD_DIM),
               (NA_HEADS, HEAD_DIM), (NA_HEADS, HEAD_DIM), (DIFF_HEADS, 2 * DIFF_QK), (DIFF_HEADS, DIFF_V))

VMEM_LIMIT = 56 * 1024 * 1024
VMEM_LIMIT_BIG = 60 * 1024 * 1024
LANES = 128

F32 = jnp.float32
BF16 = jnp.bfloat16


def _lambda_init(l):
    return 0.8 - 0.6 * math.exp(-0.3 * l)


def _rms(x):
    return x * lax.rsqrt(jnp.mean(x * x, axis=-1, keepdims=True) + EPS)


def _dot(a, b):
    return jnp.dot(a, b, preferred_element_type=F32)


def _dot_nt(a, b):
    return lax.dot_general(a, b, (((1,), (1,)), ((), ())), preferred_element_type=F32)


def _attend(pieces, sink=None):
    m = None
    for s, _ in pieces:
        mi = jnp.max(s, axis=-1, keepdims=True)
        m = mi if m is None else jnp.maximum(m, mi)
    if sink is not None:
        sink = sink * LOG2E
        m = jnp.maximum(m, sink)
    den = None
    num = None
    for s, v in pieces:
        e = jnp.exp2(s - m)
        d = jnp.sum(e, axis=-1, keepdims=True)
        n = _dot(e.astype(BF16), v)
        den = d if den is None else den + d
        num = n if num is None else num + n
    if sink is not None:
        den = den + jnp.exp2(sink - m)
    return num / den


def _diff_attend(pieces1, pieces2, values, lam):
    def exps(pieces):
        m = None
        for s in pieces:
            mi = jnp.max(s, axis=-1, keepdims=True)
            m = mi if m is None else jnp.maximum(m, mi)
        es = [jnp.exp2(s - m) for s in pieces]
        den = None
        for e in es:
            d = jnp.sum(e, axis=-1, keepdims=True)
            den = d if den is None else den + d
        return es, den

    es1, den1 = exps(pieces1)
    es2, den2 = exps(pieces2)
    c1 = 1.0 / den1
    c2 = lam / den2
    out = None
    for e1, e2, v in zip(es1, es2, values):
        n = _dot((e1 * c1 - e2 * c2).astype(BF16), v)
        out = n if out is None else out + n
    return out


def _diff_lambda(lq1, lk1, lq2, lk2, lam_init):
    a = jnp.sum(lq1[...] * lk1[...], axis=-1, keepdims=True)
    b = jnp.sum(lq2[...] * lk2[...], axis=-1, keepdims=True)
    return jnp.exp(a) - jnp.exp(b) + lam_init


def _ada_kernel(c_ref, w_ref, b_ref, o_ref):
    c = c_ref[...]
    s = (c * jax.nn.sigmoid(c)).astype(BF16)
    o_ref[0] = _dot(s, w_ref[0].astype(BF16)) + b_ref[0]


def _ada_call(cvec, w_ada, b_ada):
    tn = 1024
    n = 6 * D_MODEL
    return pl.pallas_call(
        _ada_kernel,
        grid=(DEPTH, n // tn),
        in_specs=[
            pl.BlockSpec((N_MOD_ROWS, D_MODEL), lambda l, j: (0, 0)),
            pl.BlockSpec((1, D_MODEL, tn), lambda l, j: (l, 0, j)),
            pl.BlockSpec((1, 1, tn), lambda l, j: (l, 0, j)),
        ],
        out_specs=pl.BlockSpec((1, N_MOD_ROWS, tn), lambda l, j: (l, 0, j)),
        out_shape=jax.ShapeDtypeStruct((DEPTH, N_MOD_ROWS, n), F32),
        compiler_params=pltpu.CompilerParams(
            dimension_semantics=("arbitrary", "arbitrary"), vmem_limit_bytes=VMEM_LIMIT),
        name="ada_mod",
    )(cvec, w_ada, b_ada.reshape(DEPTH, 1, n))


NBR_PAIRS = 2 * NA_ROWS


def _nbr_bias_kernel(rpb_ref, o_ref):
    l = pl.program_id(0)
    h = pl.program_id(1)
    n_dr = 2 * NA_ROWS - 1
    n_dc = 2 * NA_COLS - 1
    base = (l * NA_HEADS + h) * (n_dr * n_dc)
    wq = lax.broadcasted_iota(jnp.int32, (GRID_W, LANES), 0)
    lane = lax.broadcasted_iota(jnp.int32, (GRID_W, LANES), 1)
    upper = lane >= GRID_W
    wk = jnp.where(upper, lane - GRID_W, lane)
    delta = wk - wq + (NA_COLS - 1)
    cs = jnp.clip(wq - NA_COLS // 2, 0, GRID_W - NA_COLS)
    valid = (wk >= cs) & (wk < cs + NA_COLS)
    for e in range(NBR_PAIRS):
        dr_lo, dr_hi = max(e - 1, 0), min(e, n_dr - 1)
        acc = jnp.full((GRID_W, LANES), NEG_INF, F32)
        for d in range(n_dc):
            lo = rpb_ref[base + dr_lo * n_dc + d]
            hi = rpb_ref[base + dr_hi * n_dc + d]
            acc = jnp.where(delta == d, jnp.where(upper, hi, lo), acc)
        o_ref[0, 0, e] = jnp.where(valid, acc * LOG2E, NEG_INF)


def _nbr_bias_call(na_rpb):
    n_pairs = NBR_PAIRS
    return pl.pallas_call(
        _nbr_bias_kernel,
        grid=(DEPTH, NA_HEADS),
        in_specs=[pl.BlockSpec(memory_space=pltpu.SMEM)],
        out_specs=pl.BlockSpec((1, 1, n_pairs, GRID_W, LANES), lambda l, h: (l, h, 0, 0, 0)),
        out_shape=jax.ShapeDtypeStruct((DEPTH, NA_HEADS, n_pairs, GRID_W, LANES), F32),
        compiler_params=pltpu.CompilerParams(dimension_semantics=("arbitrary", "arbitrary")),
        name="nbr_bias",
    )(na_rpb.reshape(-1))


def _inproj_kernel(x_ref, mod_ref, g_ref, w_ref, wkr_ref, h_ref, p_ref, kr_ref):
    m = mod_ref[0]
    half = x_ref.shape[0] // 2
    for r in (slice(0, half), slice(half, 2 * half)):
        h = (_rms(x_ref[r, :]) * g_ref[...] * (1.0 + m[1:2]) + m[0:1]).astype(BF16)
        h_ref[r, :] = h
        kr_ref[r, :] = _dot_nt(h, wkr_ref[...])
        p_ref[r, :] = _dot_nt(h, w_ref[...]).astype(p_ref.dtype)


def _inproj_call(x, mod, g, w_main, w_kr, layer, seg, p_dtype, name):
    n_tok = x.shape[0]
    tm = 512
    return pl.pallas_call(
        _inproj_kernel,
        grid=(n_tok // tm,),
        in_specs=[
            pl.BlockSpec((tm, D_MODEL), lambda i: (i, 0)),
            pl.BlockSpec((1, 6, D_MODEL), lambda i: (seg(i, tm), 0, 0)),
            pl.BlockSpec((1, D_MODEL), lambda i: (0, 0)),
            pl.BlockSpec((None, P_WIDTH, D_MODEL), lambda i: (layer, 0, 0)),
            pl.BlockSpec((None, LANES, D_MODEL), lambda i: (layer, 0, 0)),
        ],
        out_specs=[
            pl.BlockSpec((tm, D_MODEL), lambda i: (i, 0)),
            pl.BlockSpec((tm, P_WIDTH), lambda i: (i, 0)),
            pl.BlockSpec((tm, LANES), lambda i: (i, 0)),
        ],
        out_shape=[
            jax.ShapeDtypeStruct((n_tok, D_MODEL), BF16),
            jax.ShapeDtypeStruct((n_tok, P_WIDTH), p_dtype),
            jax.ShapeDtypeStruct((n_tok, LANES), F32),
        ],
        compiler_params=pltpu.CompilerParams(
            dimension_semantics=("arbitrary",), vmem_limit_bytes=VMEM_LIMIT_BIG),
        name=name,
    )(x, mod, g, w_main, w_kr)


def _cast_kernel(x_ref, o_ref):
    o_ref[...] = x_ref[...].astype(BF16)


def _cast_call(w, block_rows, name):
    w2 = w.reshape(-1, w.shape[-1])
    rows, cols = w2.shape[0] // w.shape[0], w2.shape[1]
    out = pl.pallas_call(
        _cast_kernel,
        grid=(rows // block_rows,),
        in_specs=[pl.BlockSpec((block_rows, cols), lambda i: (i, 0))],
        out_specs=pl.BlockSpec((block_rows, cols), lambda i: (i, 0)),
        out_shape=jax.ShapeDtypeStruct((rows, cols), BF16),
        compiler_params=pltpu.CompilerParams(dimension_semantics=("arbitrary",), vmem_limit_bytes=VMEM_LIMIT),
        name=name,
    )(w2)
    return out.reshape((1,) + w.shape[1:])


def _side_specs(side, n_grid):
    in_specs, out_specs, out_shape = [], [], []
    for w, layer, block, index in side:
        if n_grid == 1:
            in_specs.append(pl.BlockSpec((None,) + block, lambda i, f=index, l=layer: (l,) + f(i)))
            out_specs.append(pl.BlockSpec((None,) + block, lambda i, f=index: (0,) + f(i)))
        else:
            in_specs.append(pl.BlockSpec((None,) + block, lambda i, j, f=index, l=layer: (l,) + f(i, j)))
            out_specs.append(pl.BlockSpec((None,) + block, lambda i, j, f=index: (0,) + f(i, j)))
        out_shape.append(jax.ShapeDtypeStruct((1,) + w.shape[1:], BF16))
    return in_specs, out_specs, out_shape


def _side_cast(srcs, dsts):
    for src, dst in zip(srcs, dsts):
        dst[...] = src[...].astype(BF16)


def _cast_w_in_kernel(x_ref, xkr_ref, main_ref, kr_ref):
    main_ref[...] = x_ref[0].astype(BF16)

    @pl.when(pl.program_id(1) == 0)
    def _():
        kr_ref[:MLA_ROPE, :] = xkr_ref[...].astype(BF16)
        kr_ref[MLA_ROPE:, :] = jnp.zeros((LANES - MLA_ROPE, D_MODEL), BF16)


def _cast_w_in_call(w_in):
    w_t = jnp.swapaxes(w_in, 1, 2)
    block_rows = 512
    src_row = lambda l, j: (l, pl.multiple_of(j * block_rows + jnp.where(j * block_rows >= KROPE_SRC0, MLA_ROPE, 0), MLA_ROPE), 0)
    return pl.pallas_call(
        _cast_w_in_kernel,
        grid=(DEPTH, P_WIDTH // block_rows),
        in_specs=[pl.BlockSpec((pl.Element(1), pl.Element(block_rows), pl.Element(D_MODEL)), src_row),
                  pl.BlockSpec((None, MLA_ROPE, D_MODEL), lambda l, j: (l, KROPE_SRC0 // MLA_ROPE, 0))],
        out_specs=[pl.BlockSpec((None, block_rows, D_MODEL), lambda l, j: (l, j, 0)),
                   pl.BlockSpec((None, LANES, D_MODEL), lambda l, j: (l, 0, 0))],
        out_shape=[jax.ShapeDtypeStruct((DEPTH, P_WIDTH, D_MODEL), BF16),
                   jax.ShapeDtypeStruct((DEPTH, LANES, D_MODEL), BF16)],
        compiler_params=pltpu.CompilerParams(
            dimension_semantics=("arbitrary", "arbitrary"), vmem_limit_bytes=VMEM_LIMIT),
        name="cast_w_in",
    )(w_t, w_t)


def _mla_queries(q_down, qn_g, w_uq):
    cq = _rms(q_down) * qn_g[...]
    return _dot(cq.astype(BF16), w_uq[...])


def _ctx_attn_kernel(*refs, lam_init, first):
    (p_ref, kr_ref, qn_g, w_uq, kvn_g, w_ukv, sink_ref, lq1, lk1, lq2, lk2, subln_g) = refs[:12]
    n_in = 12 + (0 if first else len(CACHE_TAILS))
    o_ref = refs[n_in]
    ckv_o, kr_o, kb_o, vb_o, kc_o, vc_o, kd_o, vd_o = refs[n_in + 1:]
    ls = 0
    if first:
        for dst in (ckv_o, kr_o, kb_o, vb_o, kc_o, vc_o, kd_o, vd_o):
            dst[0, 1:] = jnp.zeros(dst.shape[1:], F32)[1:]

    scale_a = LOG2E * (MLA_NOPE + MLA_ROPE) ** -0.5
    qa = _mla_queries(p_ref[:, Q_DOWN0:Q_DOWN0 + MLA_Q_RANK], qn_g, w_uq) * scale_a
    ckv = _rms(p_ref[:, CKV0:CKV0 + MLA_KV_RANK]) * kvn_g[...]
    ckv_o[0, ls] = ckv
    kr = kr_ref[...]
    kr_o[0, ls] = kr[:, :MLA_ROPE]
    kr_b = kr.astype(BF16)
    kv = _dot(ckv.astype(BF16), w_ukv[...])
    for h in range(MLA_HEADS):
        q = jnp.concatenate([qa[:, h * 128:(h + 1) * 128], qa[:, 512 + h * 128:512 + (h + 1) * 128]],
                            axis=1).astype(BF16)
        k = jnp.concatenate([kv[:, h * 256:h * 256 + 128].astype(BF16), kr_b], axis=1)
        v = kv[:, h * 256 + 128:(h + 1) * 256].astype(BF16)
        o = _attend([(_dot_nt(q, k), v)])
        o_ref[:, h * 128:(h + 1) * 128] = o.astype(BF16)

    scale = LOG2E * HEAD_DIM ** -0.5

    def head(col0, h):
        return p_ref[:, col0 + h * 128:col0 + (h + 1) * 128]

    g_size = SWA_HEADS // SWA_KV_HEADS
    for g in range(SWA_KV_HEADS):
        k = head(KB0, g)
        v = head(VB0, g)
        kb_o[0, ls, pl.ds(g, SEQ, stride=SWA_KV_HEADS), :] = k
        vb_o[0, ls, pl.ds(g, SEQ, stride=SWA_KV_HEADS), :] = v
        k = k.astype(BF16)
        v = v.astype(BF16)
        for h in range(g * g_size, (g + 1) * g_size):
            q = (head(QB0, h) * scale).astype(BF16)
            o = _attend([(_dot_nt(q, k), v)], sink=sink_ref[h])
            o_ref[:, 512 + h * 128:512 + (h + 1) * 128] = o.astype(BF16)

    for h in range(NA_HEADS):
        k = head(KC0, h)
        v = head(VC0, h)
        kc_o[0, ls, pl.ds(h, SEQ, stride=NA_HEADS), :] = k
        vc_o[0, ls, pl.ds(h, SEQ, stride=NA_HEADS), :] = v
        q = (head(QC0, h) * scale).astype(BF16)
        o = _attend([(_dot_nt(q, k.astype(BF16)), v.astype(BF16))])
        o_ref[:, 1024 + h * 128:1024 + (h + 1) * 128] = o.astype(BF16)

    lam = _diff_lambda(lq1, lk1, lq2, lk2, lam_init)
    scale_d = LOG2E * DIFF_QK ** -0.5
    lo = lax.broadcasted_iota(jnp.int32, (SEQ, 128), 1) < DIFF_QK
    for h in range(DIFF_HEADS):
        k = head(KD0, h)
        v = head(VD0, h)
        kd_o[0, ls, pl.ds(h, SEQ, stride=DIFF_HEADS), :] = k
        vd_o[0, ls, pl.ds(h, SEQ, stride=DIFF_HEADS), :] = v
        k = k.astype(BF16)
        v = v.astype(BF16)
        q = head(QD0, h) * scale_d
        o1 = _attend([(_dot_nt(jnp.where(lo, q, 0.0).astype(BF16), k), v)])
        o2 = _attend([(_dot_nt(jnp.where(lo, 0.0, q).astype(BF16), k), v)])
        o = _rms(o1 - lam * o2) * subln_g[...] * (1.0 - lam_init)
        o_ref[:, 1536 + h * 128:1536 + (h + 1) * 128] = o.astype(BF16)


def _small_specs():
    zero2 = (lambda *a: (0, 0))
    return [
        pl.BlockSpec((1, MLA_Q_RANK), zero2),
        pl.BlockSpec((MLA_Q_RANK, 1024), zero2),
        pl.BlockSpec((1, MLA_KV_RANK), zero2),
        pl.BlockSpec((MLA_KV_RANK, 1024), zero2),
        pl.BlockSpec(memory_space=pltpu.SMEM),
        pl.BlockSpec((1, DIFF_QK), zero2),
        pl.BlockSpec((1, DIFF_QK), zero2),
        pl.BlockSpec((1, DIFF_QK), zero2),
        pl.BlockSpec((1, DIFF_QK), zero2),
        pl.BlockSpec((1, DIFF_V), zero2),
    ]


def _ctx_attn_call(p, kr, small, prev, layer, lam_init):
    first = not prev
    row = lambda b: (b, 0)

    def flat(tail):
        return (SEQ * tail[0], tail[1]) if len(tail) == 2 else (SEQ,) + tail

    def cache_spec(tail):
        if first:
            return pl.BlockSpec((1, DEPTH) + flat(tail), lambda b: (b, 0, 0, 0))
        return pl.BlockSpec((1, 1) + flat(tail), lambda b: (b, layer, 0, 0))

    n_fixed = 12
    return pl.pallas_call(
        functools.partial(_ctx_attn_kernel, lam_init=lam_init, first=first),
        grid=(BATCH,),
        in_specs=[pl.BlockSpec((SEQ, P_WIDTH), row), pl.BlockSpec((SEQ, LANES), row)] + _small_specs()
        + [pl.BlockSpec(memory_space=pl.ANY) for _ in prev],
        out_specs=[pl.BlockSpec((SEQ, D_MODEL), row)] + [cache_spec(t) for t in CACHE_TAILS],
        out_shape=[jax.ShapeDtypeStruct((N_CTX_TOK, D_MODEL), BF16)]
        + [jax.ShapeDtypeStruct((BATCH, DEPTH) + flat(t), F32) for t in CACHE_TAILS],
        input_output_aliases={n_fixed + k: 1 + k for k in range(len(prev))},
        compiler_params=pltpu.CompilerParams(
            dimension_semantics=("arbitrary",), vmem_limit_bytes=VMEM_LIMIT),
        name="ctx_attn",
    )(p, kr, *small, *prev)


LAT_TQ = 256
SWA_PAD = SWA_WINDOW
NBR_WIN_ROWS = 12


def _rope_pair64(x, c_ref, sa_ref, sb_ref, rows):
    out = []
    for j in range(x.shape[1] // LANES):
        xj = x[:, j * LANES:(j + 1) * LANES]
        out.append(xj * c_ref[rows, :] + pltpu.roll(xj, 96, 1) * sa_ref[rows, :]
                   + pltpu.roll(xj, 32, 1) * sb_ref[rows, :])
    return jnp.concatenate(out, axis=1) if len(out) > 1 else out[0]


def _rope128(x, c_ref, s_ref, rows):
    out = []
    for j in range(x.shape[1] // LANES):
        xj = x[:, j * LANES:(j + 1) * LANES]
        out.append(xj * c_ref[rows, :] + pltpu.roll(xj, 64, 1) * s_ref[rows, :])
    return jnp.concatenate(out, axis=1) if len(out) > 1 else out[0]


def _lat_attn_kernel(p_ref, kr_ref, c_ckv, c_kr, c_kb, c_vb, c_kc, c_vc, c_kd, c_vd,
                     cos128, sin128, cos64, sina64, sinb64, bias_ref,
                     qn_g, w_uq, kvn_g, w_ukv, sink_ref, lq1, lk1, lq2, lk2, subln_g,
                     o_ref,
                     ka_s, va_s, kbc_s, vbc_s, kbp_s, vbp_s, kcc_s, vcc_s, kdc_s, vdc_s, kdo_s,
                     *, lam_init):
    def cached_head(ref, h, n_heads):
        return ref[pl.ds(h, PAST_LEN, stride=n_heads), :].astype(BF16)

    qi = pl.program_id(1)
    all_rows = slice(None)
    n_own = DEC_SEQ

    @pl.when(qi == 0)
    def _prologue():
        ckv_own = (_rms(p_ref[:, CKV0:CKV0 + MLA_KV_RANK].astype(F32)) * kvn_g[...]).astype(BF16)
        ckv_all = jnp.concatenate([c_ckv[...].astype(BF16), ckv_own], axis=0)
        kr_ctx = jnp.concatenate([c_kr[...], jnp.zeros((PAST_LEN, LANES - MLA_ROPE), F32)], axis=1)
        kr_own = _rope_pair64(kr_ref[...], cos64, sina64, sinb64, all_rows)
        kr_all = jnp.concatenate([kr_ctx, kr_own], axis=0).astype(BF16)
        for h in range(MLA_HEADS):
            kvh = _dot(ckv_all, w_ukv[:, h * 256:(h + 1) * 256])
            ka_s[h, :, 0:128] = kvh[:, 0:128].astype(BF16)
            ka_s[h, :, 128:256] = kr_all
            va_s[:, h * 128:(h + 1) * 128] = kvh[:, 128:256].astype(BF16)
        for g in range(SWA_KV_HEADS):
            kbc_s[:, g * 128:(g + 1) * 128] = cached_head(c_kb, g, SWA_KV_HEADS)
            vbc_s[:, g * 128:(g + 1) * 128] = cached_head(c_vb, g, SWA_KV_HEADS)
        zpad = jnp.zeros((SWA_PAD, 256), BF16)
        kbp_s[0:SWA_PAD, :] = zpad
        kbp_s[SWA_PAD + n_own:, :] = zpad
        vbp_s[0:SWA_PAD, :] = zpad
        vbp_s[SWA_PAD + n_own:, :] = zpad
        kbp_s[SWA_PAD:SWA_PAD + n_own, :] = _rope128(
            p_ref[:, KB0:KB0 + 256].astype(F32), cos128, sin128, all_rows).astype(BF16)
        vbp_s[SWA_PAD:SWA_PAD + n_own, :] = p_ref[:, VB0:VB0 + 256]
        for h in range(NA_HEADS):
            cols = slice(h * 128, (h + 1) * 128)
            kcc_s[:, cols] = cached_head(c_kc, h, NA_HEADS)
            vcc_s[:, cols] = cached_head(c_vc, h, NA_HEADS)
            kdc_s[:, cols] = cached_head(c_kd, h, DIFF_HEADS)
            vdc_s[:, cols] = cached_head(c_vd, h, DIFF_HEADS)
        kdo_s[...] = _rope_pair64(
            p_ref[:, KD0:KD0 + 512].astype(F32), cos64, sina64, sinb64, all_rows).astype(BF16)

    q0 = pl.multiple_of(qi * LAT_TQ, LAT_TQ)
    rows = pl.ds(q0, LAT_TQ)

    scale_a = LOG2E * (MLA_NOPE + MLA_ROPE) ** -0.5
    qa = _mla_queries(p_ref[rows, Q_DOWN0:Q_DOWN0 + MLA_Q_RANK].astype(F32), qn_g, w_uq) * scale_a
    qa_r = _rope_pair64(qa[:, 512:1024], cos64, sina64, sinb64, rows)
    for h in range(MLA_HEADS):
        q = jnp.concatenate([qa[:, h * 128:(h + 1) * 128], qa_r[:, h * 128:(h + 1) * 128]],
                            axis=1).astype(BF16)
        o = _attend([(_dot_nt(q, ka_s[h]), va_s[:, h * 128:(h + 1) * 128])])
        o_ref[:, h * 128:(h + 1) * 128] = o.astype(BF16)

    scale = LOG2E * HEAD_DIM ** -0.5
    qb = (_rope128(p_ref[rows, QB0:QB0 + 512].astype(F32), cos128, sin128, rows) * scale).astype(BF16)
    n_loc = LAT_TQ + 2 * SWA_PAD
    qpos = lax.broadcasted_iota(jnp.int32, (LAT_TQ, n_loc), 0)
    kpos = lax.broadcasted_iota(jnp.int32, (LAT_TQ, n_loc), 1) - SWA_PAD
    kabs = kpos + q0
    valid = (jnp.abs(qpos - kpos) <= SWA_WINDOW) & (kabs >= 0) & (kabs < n_own)
    loc_rows = pl.ds(q0, n_loc)
    g_size = SWA_HEADS // SWA_KV_HEADS
    for h in range(SWA_HEADS):
        g = h // g_size
        q = qb[:, h * 128:(h + 1) * 128]
        s_ctx = _dot_nt(q, kbc_s[:, g * 128:(g + 1) * 128])
        s_loc = jnp.where(valid, _dot_nt(q, kbp_s[loc_rows, g * 128:(g + 1) * 128]), NEG_INF)
        o = _attend([(s_ctx, vbc_s[:, g * 128:(g + 1) * 128]),
                     (s_loc, vbp_s[loc_rows, g * 128:(g + 1) * 128])], sink=sink_ref[h])
        o_ref[:, 512 + h * 128:512 + (h + 1) * 128] = o.astype(BF16)

    q_rows = LAT_TQ // GRID_W
    ws = jnp.clip(qi * q_rows - NA_ROWS // 2, 0, GRID_ROWS - NBR_WIN_ROWS)
    ws = (ws // 2) * 2
    krows = pl.ds(pl.multiple_of(ws * GRID_W, 2 * GRID_W), NBR_WIN_ROWS * GRID_W)
    lower = lax.broadcasted_iota(jnp.int32, (GRID_W, LANES), 1) < GRID_W
    for h in range(NA_HEADS):
        cols = slice(h * 128, (h + 1) * 128)
        bias_rows = []
        for t in range(q_rows):
            r = qi * q_rows + t
            off = jnp.clip(r - NA_ROWS // 2, 0, GRID_ROWS - NA_ROWS) - r + (NA_ROWS - 1)
            blocks = []
            for j in range(NBR_WIN_ROWS // 2):
                dr = ws + 2 * j - r + (NA_ROWS - 1)
                ok_lo = ((dr >= off) & (dr < off + NA_ROWS)).astype(jnp.int32)
                ok_hi = ((dr + 1 >= off) & (dr + 1 < off + NA_ROWS)).astype(jnp.int32)
                ok = jnp.where(lower, ok_lo, ok_hi) > 0
                blk = bias_ref[0, h, jnp.clip(dr + 1, 0, NBR_PAIRS - 1)]
                blocks.append(jnp.where(ok, blk, NEG_INF))
            bias_rows.append(jnp.concatenate(blocks, axis=1))
        bias = jnp.concatenate(bias_rows, axis=0)
        q = (p_ref[rows, QC0 + h * 128:QC0 + (h + 1) * 128].astype(F32) * scale).astype(BF16)
        s_loc = _dot_nt(q, p_ref[krows, KC0 + h * 128:KC0 + (h + 1) * 128]) + bias
        s_ctx = _dot_nt(q, kcc_s[:, cols])
        o = _attend([(s_ctx, vcc_s[:, cols]), (s_loc, p_ref[krows, VC0 + h * 128:VC0 + (h + 1) * 128])])
        o_ref[:, 1024 + h * 128:1024 + (h + 1) * 128] = o.astype(BF16)

    scale_d = LOG2E * DIFF_QK ** -0.5
    qd = _rope_pair64(p_ref[rows, QD0:QD0 + 512].astype(F32), cos64, sina64, sinb64, rows) * scale_d
    lam = _diff_lambda(lq1, lk1, lq2, lk2, lam_init)
    lo = lax.broadcasted_iota(jnp.int32, (LAT_TQ, 128), 1) < DIFF_QK
    for h in range(DIFF_HEADS):
        cols = slice(h * 128, (h + 1) * 128)
        q = qd[:, cols]
        v_own = p_ref[:, VD0 + h * 128:VD0 + (h + 1) * 128]
        q1 = jnp.where(lo, q, 0.0).astype(BF16)
        q2 = jnp.where(lo, 0.0, q).astype(BF16)
        o = _diff_attend([_dot_nt(q1, kdc_s[:, cols]), _dot_nt(q1, kdo_s[:, cols])],
                         [_dot_nt(q2, kdc_s[:, cols]), _dot_nt(q2, kdo_s[:, cols])],
                         [vdc_s[:, cols], v_own], lam)
        o = _rms(o) * subln_g[...] * (1.0 - lam_init)
        o_ref[:, 1536 + h * 128:1536 + (h + 1) * 128] = o.astype(BF16)


def _lat_attn_call(p, kr, caches, tables, bias, small, layer, lam_init):
    nq = DEC_SEQ // LAT_TQ
    batch_rows = lambda b, q: (b, 0)
    zero2 = lambda b, q: (0, 0)
    once = pl.Buffered(1)
    cache_args = [a.reshape(a.shape[:2] + (-1, a.shape[-1])) for a in caches]
    cache_specs = [pl.BlockSpec((None, None) + a.shape[2:], lambda b, q: (b, layer, 0, 0), pipeline_mode=once)
                   for a in cache_args]
    table_specs = [pl.BlockSpec((DEC_SEQ, LANES), zero2) for _ in tables]
    bias_spec = pl.BlockSpec((1, NA_HEADS, NBR_PAIRS, GRID_W, LANES), lambda b, q: (layer, 0, 0, 0, 0))
    n_all = PAST_LEN + DEC_SEQ
    n_pad = DEC_SEQ + 2 * SWA_PAD
    scratch = [
        pltpu.VMEM((MLA_HEADS, n_all, 256), BF16),
        pltpu.VMEM((n_all, 512), BF16),
        pltpu.VMEM((PAST_LEN, 256), BF16), pltpu.VMEM((PAST_LEN, 256), BF16),
        pltpu.VMEM((n_pad, 256), BF16), pltpu.VMEM((n_pad, 256), BF16),
        pltpu.VMEM((PAST_LEN, 512), BF16), pltpu.VMEM((PAST_LEN, 512), BF16),
        pltpu.VMEM((PAST_LEN, 512), BF16), pltpu.VMEM((PAST_LEN, 512), BF16),
        pltpu.VMEM((DEC_SEQ, 512), BF16),
    ]
    return pl.pallas_call(
        functools.partial(_lat_attn_kernel, lam_init=lam_init),
        grid=(DEC_BATCH, nq),
        in_specs=[pl.BlockSpec((DEC_SEQ, P_WIDTH), batch_rows, pipeline_mode=once),
                  pl.BlockSpec((DEC_SEQ, LANES), batch_rows, pipeline_mode=once)]
        + cache_specs + table_specs + [bias_spec] + _small_specs(),
        out_specs=pl.BlockSpec((LAT_TQ, D_MODEL), lambda b, q: (b * nq + q, 0)),
        out_shape=jax.ShapeDtypeStruct((N_LAT_TOK, D_MODEL), BF16),
        scratch_shapes=scratch,
        compiler_params=pltpu.CompilerParams(
            dimension_semantics=("arbitrary", "arbitrary"), vmem_limit_bytes=VMEM_LIMIT_BIG),
        name="lat_attn",
    )(p, kr, *cache_args, *tables, bias, *small)


def _merge_kernel(h_ref, o_ref, wg_ref, wb_ref, *rest):
    n_side = (len(rest) - 1) // 2
    out_ref = rest[n_side]
    _side_cast(rest[:n_side], rest[n_side + 1:])
    acc = None
    for i in range(N_BRANCH):
        gate = jax.nn.sigmoid(_dot(h_ref[...], wg_ref[i]))
        term = gate * _dot(o_ref[:, i * BRANCH_W:(i + 1) * BRANCH_W], wb_ref[i])
        acc = term if acc is None else acc + term
    out_ref[...] = acc.astype(BF16)


def _merge_call(h, o, wg, wb, name, side_weights=None):
    n_tok = h.shape[0]
    tm, tn = 1024, 512
    grid = (n_tok // tm, D_MODEL // tn)
    side = []
    if side_weights is not None:
        w_gate, w_br, layer = side_weights
        side = [(w_gate, layer, (N_BRANCH, D_MODEL // grid[0], tn), lambda i, j: (0, i, j)),
                (w_br, layer, (N_BRANCH, BRANCH_W // grid[0], tn), lambda i, j: (0, i, j))]
    s_in, s_out, s_shape = _side_specs(side, 2)
    return pl.pallas_call(
        _merge_kernel,
        grid=grid,
        in_specs=[
            pl.BlockSpec((tm, D_MODEL), lambda i, j: (i, 0)),
            pl.BlockSpec((tm, D_MODEL), lambda i, j: (i, 0)),
            pl.BlockSpec((None, N_BRANCH, D_MODEL, tn), lambda i, j: (0, 0, 0, j)),
            pl.BlockSpec((None, N_BRANCH, BRANCH_W, tn), lambda i, j: (0, 0, 0, j)),
        ] + s_in,
        out_specs=[pl.BlockSpec((tm, tn), lambda i, j: (i, j))] + s_out,
        out_shape=[jax.ShapeDtypeStruct((n_tok, D_MODEL), BF16)] + s_shape,
        compiler_params=pltpu.CompilerParams(
            dimension_semantics=("arbitrary", "arbitrary"), vmem_limit_bytes=VMEM_LIMIT),
        name=name,
    )(h, o, wg, wb, *[w for w, _, _, _ in side])


def _outproj_kernel(mixed_ref, x_ref, mod_ref, g_ref, gffn_ref, w_ref, *rest):
    n_side = (len(rest) - 2) // 2
    xo_ref, h2_ref = rest[n_side:n_side + 2]
    _side_cast(rest[:n_side], rest[n_side + 2:])
    m = mod_ref[0]
    half = mixed_ref.shape[0] // 2
    for r in (slice(0, half), slice(half, 2 * half)):
        out = _dot(mixed_ref[r, :], w_ref[...])
        x = x_ref[r, :] + m[2:3] * (_rms(out) * g_ref[...])
        xo_ref[r, :] = x
        h2_ref[r, :] = (_rms(x) * gffn_ref[...] * (1.0 + m[4:5]) + m[3:4]).astype(BF16)


def _outproj_call(mixed, x, mod, g, g_ffn, w, seg, name, side_weights=None):
    n_tok = x.shape[0]
    tm = 512
    n_steps = n_tok // tm
    side = []
    if side_weights is not None:
        w_o, layer = side_weights
        side = [(w_o, layer, (D_MODEL // n_steps, D_MODEL), lambda i: (i, 0))]
    s_in, s_out, s_shape = _side_specs(side, 1)
    return pl.pallas_call(
        _outproj_kernel,
        grid=(n_steps,),
        in_specs=[
            pl.BlockSpec((tm, D_MODEL), lambda i: (i, 0)),
            pl.BlockSpec((tm, D_MODEL), lambda i: (i, 0)),
            pl.BlockSpec((1, 6, D_MODEL), lambda i: (seg(i, tm), 0, 0)),
            pl.BlockSpec((1, D_MODEL), lambda i: (0, 0)),
            pl.BlockSpec((1, D_MODEL), lambda i: (0, 0)),
            pl.BlockSpec((None, D_MODEL, D_MODEL), lambda i: (0, 0, 0)),
        ] + s_in,
        out_specs=[pl.BlockSpec((tm, D_MODEL), lambda i: (i, 0)), pl.BlockSpec((tm, D_MODEL), lambda i: (i, 0))]
        + s_out,
        out_shape=[jax.ShapeDtypeStruct((n_tok, D_MODEL), F32), jax.ShapeDtypeStruct((n_tok, D_MODEL), BF16)]
        + s_shape,
        compiler_params=pltpu.CompilerParams(
            dimension_semantics=("arbitrary",), vmem_limit_bytes=VMEM_LIMIT),
        name=name,
    )(mixed, x, mod, g, g_ffn, w, *[w_ for w_, _, _, _ in side])


def _ffn_kernel(x_ref, h2_ref, mod_ref, gpost_ref, wg_ref, wu_ref, wd_ref, *rest):
    n_side = (len(rest) - 2) // 2
    xo_ref, acc_s = rest[n_side], rest[-1]
    _side_cast(rest[:n_side], rest[n_side + 1:-1])
    j = pl.program_id(1)

    @pl.when(j == 0)
    def _():
        acc_s[...] = jnp.zeros_like(acc_s)

    a = _dot(h2_ref[...], wg_ref[...])
    b = _dot(h2_ref[...], wu_ref[...])
    t = (a * jax.nn.sigmoid(a) * b).astype(BF16)
    acc_s[...] += _dot(t, wd_ref[...])

    @pl.when(j == pl.num_programs(1) - 1)
    def _():
        m = mod_ref[0]
        xo_ref[...] = x_ref[...] + m[5:6] * (_rms(acc_s[...]) * gpost_ref[...])


def _ffn_call(x, h2, mod, gpost, wg, wu, wd, seg, name, side_weights=None):
    n_tok = x.shape[0]
    tm, th = 512, 512
    grid = (n_tok // tm, FFN_HIDDEN // th)
    side = []
    if side_weights is not None:
        w_g, w_u, w_d, layer = side_weights
        rows = D_MODEL // grid[0]
        side = [(w_g, layer, (rows, th), lambda i, j: (i, j)), (w_u, layer, (rows, th), lambda i, j: (i, j)),
                (w_d, layer, (th, rows), lambda i, j: (j, i))]
    s_in, s_out, s_shape = _side_specs(side, 2)
    return pl.pallas_call(
        _ffn_kernel,
        grid=grid,
        in_specs=[
            pl.BlockSpec((tm, D_MODEL), lambda i, j: (i, 0)),
            pl.BlockSpec((tm, D_MODEL), lambda i, j: (i, 0)),
            pl.BlockSpec((1, 6, D_MODEL), lambda i, j: (seg(i, tm), 0, 0)),
            pl.BlockSpec((1, D_MODEL), lambda i, j: (0, 0)),
            pl.BlockSpec((None, D_MODEL, th), lambda i, j: (0, 0, j)),
            pl.BlockSpec((None, D_MODEL, th), lambda i, j: (0, 0, j)),
            pl.BlockSpec((None, th, D_MODEL), lambda i, j: (0, j, 0)),
        ] + s_in,
        out_specs=[pl.BlockSpec((tm, D_MODEL), lambda i, j: (i, 0))] + s_out,
        out_shape=[jax.ShapeDtypeStruct((n_tok, D_MODEL), F32)] + s_shape,
        scratch_shapes=[pltpu.VMEM((tm, D_MODEL), F32)],
        compiler_params=pltpu.CompilerParams(
            dimension_semantics=("arbitrary", "arbitrary"), vmem_limit_bytes=VMEM_LIMIT),
        name=name,
    )(x, h2, mod, gpost, wg, wu, wd, *[w for w, _, _, _ in side])


def _rope_tables():
    t = np.arange(DEC_SEQ)
    row = (t // GRID_W).astype(np.float32)
    col = (t % GRID_W).astype(np.float32)

    def half_angles(r):
        quarter = r // 4
        inv = (1.0 / (ROPE_BASE ** (np.arange(quarter, dtype=np.float32) / quarter))).astype(np.float32)
        return np.concatenate([row[:, None] * inv, col[:, None] * inv], axis=-1).astype(np.float32)

    a64 = half_angles(128)
    cos128 = np.concatenate([np.cos(a64), np.cos(a64)], axis=1)
    sin128 = np.concatenate([-np.sin(a64), np.sin(a64)], axis=1)
    a32 = half_angles(64)
    c, s, z = np.cos(a32), np.sin(a32), np.zeros_like(a32)
    cos64 = np.concatenate([c, c, c, c], axis=1)
    sina64 = np.concatenate([-s, z, -s, z], axis=1)
    sinb64 = np.concatenate([z, s, z, s], axis=1)
    return tuple(jnp.asarray(a, F32) for a in (cos128, sin128, cos64, sina64, sinb64))


def _ctx_seg(i, tm):
    del tm
    return i * 0


def _lat_seg(i, tm):
    return 1 + (i * tm) // DEC_SEQ


def kernel(x_prompt, x_sample, cache_mla_ckv, cache_mla_krope, cache_swa_k, cache_swa_v, cache_na_k, cache_na_v, cache_diff_k, cache_diff_v, c, c_ctx, w_ada, b_ada, mix_pre_g, mix_post_g, ffn_pre_g, ffn_post_g, w_in, mla_q_norm_g, mla_kv_norm_g, w_mla_uq, w_mla_ukv, swa_sink, na_rpb, diff_lq1, diff_lk1, diff_lq2, diff_lk2, diff_subln_g, w_mix_gate, w_branch, w_out, w_ffn_gate, w_ffn_up, w_ffn_down):
    cvec = jnp.concatenate([c_ctx[None, :], c, jnp.zeros((N_MOD_ROWS - 1 - DEC_BATCH, D_MODEL), F32)], axis=0)
    mod_all = _ada_call(cvec, w_ada, b_ada).reshape(DEPTH, N_MOD_ROWS, 6, D_MODEL)
    bias_all = _nbr_bias_call(na_rpb)
    tables = _rope_tables()
    caches = (cache_mla_ckv, cache_mla_krope, cache_swa_k, cache_swa_v, cache_na_k, cache_na_v,
              cache_diff_k, cache_diff_v)

    w_main, w_kr = _cast_w_in_call(w_in)
    wg = _cast_call(w_mix_gate, 512, "cast_w_gate")
    wb = _cast_call(w_branch, 512, "cast_w_branch")
    wo = _cast_call(w_out, 512, "cast_w_out")
    wfg = _cast_call(w_ffn_gate, 256, "cast_w_ffn_gate")
    wfu = _cast_call(w_ffn_up, 256, "cast_w_ffn_up")
    wfd = _cast_call(w_ffn_down, 512, "cast_w_ffn_down")

    y = x_prompt.reshape(N_CTX_TOK, D_MODEL)
    z = x_sample.reshape(N_LAT_TOK, D_MODEL)
    new_caches = ()
    for l in range(DEPTH):
        lam_init = _lambda_init(l)
        mod = mod_all[l]
        uq = w_mla_uq[l].reshape(MLA_Q_RANK, MLA_HEADS, MLA_NOPE + MLA_ROPE)
        uq_rope = jnp.pad(uq[:, :, MLA_NOPE:], ((0, 0), (0, 0), (0, LANES - MLA_ROPE)))
        w_uq = jnp.concatenate([uq[:, :, :MLA_NOPE].reshape(MLA_Q_RANK, 512),
                                uq_rope.reshape(MLA_Q_RANK, 512)], axis=1).astype(BF16)
        small = (mla_q_norm_g[l][None, :], w_uq, mla_kv_norm_g[l][None, :], w_mla_ukv[l].astype(BF16),
                 swa_sink[l], diff_lq1[l][None, :], diff_lk1[l][None, :], diff_lq2[l][None, :],
                 diff_lk2[l][None, :], diff_subln_g[l][None, :])
        g_pre, g_post = mix_pre_g[l][None, :], mix_post_g[l][None, :]
        f_pre, f_post = ffn_pre_g[l][None, :], ffn_post_g[l][None, :]

        nxt = l + 1 if l + 1 < DEPTH else None
        h, p, kr = _inproj_call(y, mod, g_pre, w_main, w_kr, l, _ctx_seg, F32, "ctx_inproj")
        o, *new_caches = _ctx_attn_call(p, kr, small, new_caches, l, lam_init)
        mixed, *nxt_merge = _merge_call(
            h, o, wg, wb, "ctx_merge", None if nxt is None else (w_mix_gate, w_branch, nxt))
        y, h2, *nxt_out = _outproj_call(
            mixed, y, mod, g_post, f_pre, wo, _ctx_seg, "ctx_outproj", None if nxt is None else (w_out, nxt))
        y, *nxt_ffn = _ffn_call(
            y, h2, mod, f_post, wfg, wfu, wfd, _ctx_seg, "ctx_ffn",
            None if nxt is None else (w_ffn_gate, w_ffn_up, w_ffn_down, nxt))

        h, p, kr = _inproj_call(z, mod, g_pre, w_main, w_kr, l, _lat_seg, BF16, "lat_inproj")
        o = _lat_attn_call(p, kr, caches, tables, bias_all, small, l, lam_init)
        mixed, = _merge_call(h, o, wg, wb, "lat_merge")
        z, h2 = _outproj_call(mixed, z, mod, g_post, f_pre, wo, _lat_seg, "lat_outproj")
        z, = _ffn_call(z, h2, mod, f_post, wfg, wfu, wfd, _lat_seg, "lat_ffn")
        if nxt is not None:
            (wg, wb), (wo,), (wfg, wfu, wfd) = nxt_merge, nxt_out, nxt_ffn

    new_caches = [a.reshape((BATCH, DEPTH, SEQ) + t) for a, t in zip(new_caches, CACHE_TAILS)]
    return (y.reshape(BATCH, SEQ, D_MODEL), z.reshape(DEC_BATCH, DEC_SEQ, D_MODEL), *new_caches)
```

```python
import functools
import math

import numpy as np
import jax
import jax.numpy as jnp
from jax import lax
from jax.experimental import pallas as pl
from jax.experimental.pallas import tpu as pltpu

D_MODEL = 2048
BATCH = 32
SEQ = 256
DEPTH = 2
DEC_BATCH = 8
DEC_SEQ = 1024
PAST_LEN = 512
GRID_W = 64
HEAD_DIM = 128
ROPE_BASE = 10000.0
EPS = 1e-6
NEG_INF = -1e30
LOG2E = math.log2(math.e)

MLA_HEADS = 4
MLA_Q_RANK = 384
MLA_KV_RANK = 128
MLA_NOPE = 128
MLA_ROPE = 64
MLA_V = 128
SWA_HEADS = 4
SWA_KV_HEADS = 2
SWA_WINDOW = 128
NA_HEADS = 4
NA_ROWS = 8
NA_COLS = 16
DIFF_HEADS = 4
DIFF_QK = 64
DIFF_V = 128
N_BRANCH = 4
BRANCH_W = 512
FFN_HIDDEN = 5632

N_CTX_TOK = BATCH * SEQ
N_LAT_TOK = DEC_BATCH * DEC_SEQ
GRID_ROWS = DEC_SEQ // GRID_W

Q_DOWN0, CKV0, QB0, KB0, VB0, QC0, KC0, VC0, QD0, KD0, VD0 = (
    0, 384, 512, 1024, 1280, 1536, 2048, 2560, 3072, 3584, 4096)
KROPE_SRC0 = MLA_Q_RANK + MLA_KV_RANK
P_WIDTH = 4608
MLA_QN_W = MLA_HEADS * MLA_NOPE
MLA_KV_W = MLA_NOPE + MLA_V
SWA_KV_W = SWA_KV_HEADS * HEAD_DIM
N_MOD_ROWS = 16

CACHE_TAILS = ((MLA_KV_RANK,), (MLA_ROPE,), (SWA_KV_HEADS, HEAD_DIM), (SWA_KV_HEADS, HEAD_DIM),
               (NA_HEADS, HEAD_DIM), (NA_HEADS, HEAD_DIM), (DIFF_HEADS, 2 * DIFF_QK), (DIFF_HEADS, DIFF_V))

VMEM_LIMIT = 56 * 1024 * 1024
VMEM_LIMIT_BIG = 62 * 1024 * 1024
LANES = 128

F32 = jnp.float32
BF16 = jnp.bfloat16


def _lambda_init(l):
    return 0.8 - 0.6 * math.exp(-0.3 * l)


def _rms(x):
    return x * lax.rsqrt(jnp.mean(x * x, axis=-1, keepdims=True) + EPS)


def _head_cols(h, base=0):
    return slice(base + h * HEAD_DIM, base + (h + 1) * HEAD_DIM)


def _dot(a, b):
    return jnp.dot(a, b, preferred_element_type=F32)


def _dot_nt(a, b):
    return lax.dot_general(a, b, (((1,), (1,)), ((), ())), preferred_element_type=F32)


def _attend(pieces, sink=None):
    m = None
    for s, _ in pieces:
        mi = jnp.max(s, axis=-1, keepdims=True)
        m = mi if m is None else jnp.maximum(m, mi)
    if sink is not None:
        sink = sink * LOG2E
        m = jnp.maximum(m, sink)
    den = None
    num = None
    for s, v in pieces:
        e = jnp.exp2(s - m)
        d = jnp.sum(e, axis=-1, keepdims=True)
        n = _dot(e.astype(BF16), v)
        den = d if den is None else den + d
        num = n if num is None else num + n
    if sink is not None:
        den = den + jnp.exp2(sink - m)
    return num / den


def _diff_attend(pieces1, pieces2, values, lam):
    def exps(pieces):
        m = None
        for s in pieces:
            mi = jnp.max(s, axis=-1, keepdims=True)
            m = mi if m is None else jnp.maximum(m, mi)
        es = [jnp.exp2(s - m) for s in pieces]
        den = None
        for e in es:
            d = jnp.sum(e, axis=-1, keepdims=True)
            den = d if den is None else den + d
        return es, den

    es1, den1 = exps(pieces1)
    es2, den2 = exps(pieces2)
    ratio = lam * den1 / den2
    out = None
    for e1, e2, v in zip(es1, es2, values):
        n = _dot((e1 - e2 * ratio).astype(BF16), v)
        out = n if out is None else out + n
    return out / den1


def _diff_lambda(lq1, lk1, lq2, lk2, lam_init):
    a = jnp.sum(lq1[...] * lk1[...], axis=-1, keepdims=True)
    b = jnp.sum(lq2[...] * lk2[...], axis=-1, keepdims=True)
    return jnp.exp(a) - jnp.exp(b) + lam_init


def _ada_kernel(c_ref, w_ref, b_ref, o_ref):
    c = c_ref[...]
    s = (c * jax.nn.sigmoid(c)).astype(BF16)
    o_ref[0] = _dot(s, w_ref[0].astype(BF16)) + b_ref[0]


def _ada_call(cvec, w_ada, b_ada):
    tn = 1024
    n = 6 * D_MODEL
    return pl.pallas_call(
        _ada_kernel,
        grid=(DEPTH, n // tn),
        in_specs=[
            pl.BlockSpec((N_MOD_ROWS, D_MODEL), lambda l, j: (0, 0)),
            pl.BlockSpec((1, D_MODEL, tn), lambda l, j: (l, 0, j)),
            pl.BlockSpec((1, 1, tn), lambda l, j: (l, 0, j)),
        ],
        out_specs=pl.BlockSpec((1, N_MOD_ROWS, tn), lambda l, j: (l, 0, j)),
        out_shape=jax.ShapeDtypeStruct((DEPTH, N_MOD_ROWS, n), F32),
        compiler_params=pltpu.CompilerParams(
            dimension_semantics=("arbitrary", "arbitrary"), vmem_limit_bytes=VMEM_LIMIT),
        name="ada_mod",
    )(cvec, w_ada, b_ada.reshape(DEPTH, 1, n))


NBR_PAIRS = 2 * NA_ROWS


def _nbr_bias_kernel(rpb_ref, o_ref):
    l = pl.program_id(0)
    h = pl.program_id(1)
    n_dr = 2 * NA_ROWS - 1
    n_dc = 2 * NA_COLS - 1
    base = (l * NA_HEADS + h) * (n_dr * n_dc)
    wq = lax.broadcasted_iota(jnp.int32, (GRID_W, LANES), 0)
    lane = lax.broadcasted_iota(jnp.int32, (GRID_W, LANES), 1)
    upper = lane >= GRID_W
    wk = jnp.where(upper, lane - GRID_W, lane)
    delta = wk - wq + (NA_COLS - 1)
    cs = jnp.clip(wq - NA_COLS // 2, 0, GRID_W - NA_COLS)
    valid = (wk >= cs) & (wk < cs + NA_COLS)
    for e in range(NBR_PAIRS):
        dr_lo, dr_hi = max(e - 1, 0), min(e, n_dr - 1)
        acc = jnp.full((GRID_W, LANES), NEG_INF, F32)
        for d in range(n_dc):
            lo = rpb_ref[base + dr_lo * n_dc + d]
            hi = rpb_ref[base + dr_hi * n_dc + d]
            acc = jnp.where(delta == d, jnp.where(upper, hi, lo), acc)
        o_ref[0, 0, e] = jnp.where(valid, acc * LOG2E, NEG_INF)


def _nbr_bias_call(na_rpb):
    n_pairs = NBR_PAIRS
    return pl.pallas_call(
        _nbr_bias_kernel,
        grid=(DEPTH, NA_HEADS),
        in_specs=[pl.BlockSpec(memory_space=pltpu.SMEM)],
        out_specs=pl.BlockSpec((1, 1, n_pairs, GRID_W, LANES), lambda l, h: (l, h, 0, 0, 0)),
        out_shape=jax.ShapeDtypeStruct((DEPTH, NA_HEADS, n_pairs, GRID_W, LANES), F32),
        compiler_params=pltpu.CompilerParams(dimension_semantics=("arbitrary", "arbitrary")),
        name="nbr_bias",
    )(na_rpb.reshape(-1))


def _inproj_kernel(x_ref, mod_ref, g_ref, w_ref, wkr_ref, h_ref, p_ref, kr_ref):
    m = mod_ref[0]
    half = x_ref.shape[0] // 2
    for r in (slice(0, half), slice(half, 2 * half)):
        h = (_rms(x_ref[r, :]) * g_ref[...] * (1.0 + m[1:2]) + m[0:1]).astype(BF16)
        h_ref[r, :] = h
        kr_ref[r, :] = _dot_nt(h, wkr_ref[...])
        p_ref[r, :] = _dot_nt(h, w_ref[...]).astype(p_ref.dtype)


def _inproj_call(x, mod, g, w_main, w_kr, layer, seg, p_dtype, name):
    n_tok = x.shape[0]
    tm = 512
    return pl.pallas_call(
        _inproj_kernel,
        grid=(n_tok // tm,),
        in_specs=[
            pl.BlockSpec((tm, D_MODEL), lambda i: (i, 0)),
            pl.BlockSpec((1, 6, D_MODEL), lambda i: (seg(i, tm), 0, 0)),
            pl.BlockSpec((1, D_MODEL), lambda i: (0, 0)),
            pl.BlockSpec((None, P_WIDTH, D_MODEL), lambda i: (layer, 0, 0)),
            pl.BlockSpec((None, LANES, D_MODEL), lambda i: (layer, 0, 0)),
        ],
        out_specs=[
            pl.BlockSpec((tm, D_MODEL), lambda i: (i, 0)),
            pl.BlockSpec((tm, P_WIDTH), lambda i: (i, 0)),
            pl.BlockSpec((tm, LANES), lambda i: (i, 0)),
        ],
        out_shape=[
            jax.ShapeDtypeStruct((n_tok, D_MODEL), BF16),
            jax.ShapeDtypeStruct((n_tok, P_WIDTH), p_dtype),
            jax.ShapeDtypeStruct((n_tok, LANES), F32),
        ],
        compiler_params=pltpu.CompilerParams(
            dimension_semantics=("arbitrary",), vmem_limit_bytes=VMEM_LIMIT_BIG),
        name=name,
    )(x, mod, g, w_main, w_kr)


def _cast_kernel(x_ref, o_ref):
    o_ref[...] = x_ref[...].astype(BF16)


def _cast_call(w, block_rows, name):
    w2 = w.reshape(-1, w.shape[-1])
    rows, cols = w2.shape[0] // w.shape[0], w2.shape[1]
    out = pl.pallas_call(
        _cast_kernel,
        grid=(rows // block_rows,),
        in_specs=[pl.BlockSpec((block_rows, cols), lambda i: (i, 0))],
        out_specs=pl.BlockSpec((block_rows, cols), lambda i: (i, 0)),
        out_shape=jax.ShapeDtypeStruct((rows, cols), BF16),
        compiler_params=pltpu.CompilerParams(dimension_semantics=("arbitrary",), vmem_limit_bytes=VMEM_LIMIT),
        name=name,
    )(w2)
    return out.reshape((1,) + w.shape[1:])


def _side_specs(side, n_grid):
    in_specs, out_specs, out_shape = [], [], []
    for w, layer, block, index in side:
        if n_grid == 1:
            in_specs.append(pl.BlockSpec((None,) + block, lambda i, f=index, l=layer: (l,) + f(i)))
            out_specs.append(pl.BlockSpec((None,) + block, lambda i, f=index: (0,) + f(i)))
        else:
            in_specs.append(pl.BlockSpec((None,) + block, lambda i, j, f=index, l=layer: (l,) + f(i, j)))
            out_specs.append(pl.BlockSpec((None,) + block, lambda i, j, f=index: (0,) + f(i, j)))
        out_shape.append(jax.ShapeDtypeStruct((1,) + w.shape[1:], BF16))
    return in_specs, out_specs, out_shape


def _side_cast(srcs, dsts):
    for src, dst in zip(srcs, dsts):
        dst[...] = src[...].astype(BF16)


def _cast_w_in_kernel(x_ref, xkr_ref, main_ref, kr_ref):
    main_ref[...] = x_ref[0].astype(BF16)

    @pl.when(pl.program_id(1) == 0)
    def _():
        kr_ref[:MLA_ROPE, :] = xkr_ref[...].astype(BF16)
        kr_ref[MLA_ROPE:, :] = jnp.zeros((LANES - MLA_ROPE, D_MODEL), BF16)


def _cast_w_in_call(w_in):
    w_t = jnp.swapaxes(w_in, 1, 2)
    block_rows = 512
    src_row = lambda l, j: (l, pl.multiple_of(j * block_rows + jnp.where(j * block_rows >= KROPE_SRC0, MLA_ROPE, 0), MLA_ROPE), 0)
    return pl.pallas_call(
        _cast_w_in_kernel,
        grid=(DEPTH, P_WIDTH // block_rows),
        in_specs=[pl.BlockSpec((pl.Element(1), pl.Element(block_rows), pl.Element(D_MODEL)), src_row),
                  pl.BlockSpec((None, MLA_ROPE, D_MODEL), lambda l, j: (l, KROPE_SRC0 // MLA_ROPE, 0))],
        out_specs=[pl.BlockSpec((None, block_rows, D_MODEL), lambda l, j: (l, j, 0)),
                   pl.BlockSpec((None, LANES, D_MODEL), lambda l, j: (l, 0, 0))],
        out_shape=[jax.ShapeDtypeStruct((DEPTH, P_WIDTH, D_MODEL), BF16),
                   jax.ShapeDtypeStruct((DEPTH, LANES, D_MODEL), BF16)],
        compiler_params=pltpu.CompilerParams(
            dimension_semantics=("arbitrary", "arbitrary"), vmem_limit_bytes=VMEM_LIMIT),
        name="cast_w_in",
    )(w_t, w_t)


def _mla_queries(q_down, qn_g, w_uq):
    cq = _rms(q_down) * qn_g[...]
    return _dot(cq.astype(BF16), w_uq[...])


def _ctx_attn_kernel(*refs, lam_init, first):
    (p_ref, kr_ref, qn_g, w_uq, kvn_g, w_ukv, sink_ref, lq1, lk1, lq2, lk2, subln_g) = refs[:12]
    n_in = 12 + (0 if first else len(CACHE_TAILS))
    o_ref = refs[n_in]
    ckv_o, kr_o, kb_o, vb_o, kc_o, vc_o, kd_o, vd_o = refs[n_in + 1:]
    ls = 0
    if first:
        for dst in (ckv_o, kr_o, kb_o, vb_o, kc_o, vc_o, kd_o, vd_o):
            dst[0, 1:] = jnp.zeros(dst.shape[1:], F32)[1:]

    scale_a = LOG2E * (MLA_NOPE + MLA_ROPE) ** -0.5
    qa = _mla_queries(p_ref[:, Q_DOWN0:Q_DOWN0 + MLA_Q_RANK], qn_g, w_uq) * scale_a
    ckv = _rms(p_ref[:, CKV0:CKV0 + MLA_KV_RANK]) * kvn_g[...]
    ckv_o[0, ls] = ckv
    kr = kr_ref[...]
    kr_o[0, ls] = kr[:, :MLA_ROPE]
    kr_b = kr.astype(BF16)
    kv = _dot(ckv.astype(BF16), w_ukv[...])
    for h in range(MLA_HEADS):
        q = jnp.concatenate([qa[:, _head_cols(h)], qa[:, _head_cols(h, MLA_QN_W)]],
                            axis=1).astype(BF16)
        kv0 = h * MLA_KV_W
        k = jnp.concatenate([kv[:, kv0:kv0 + MLA_NOPE].astype(BF16), kr_b], axis=1)
        v = kv[:, kv0 + MLA_NOPE:kv0 + MLA_KV_W].astype(BF16)
        o = _attend([(_dot_nt(q, k), v)])
        o_ref[:, _head_cols(h)] = o.astype(BF16)

    scale = LOG2E * HEAD_DIM ** -0.5

    def head(col0, h):
        return p_ref[:, _head_cols(h, col0)]

    g_size = SWA_HEADS // SWA_KV_HEADS
    for g in range(SWA_KV_HEADS):
        k = head(KB0, g)
        v = head(VB0, g)
        kb_o[0, ls, pl.ds(g, SEQ, stride=SWA_KV_HEADS), :] = k
        vb_o[0, ls, pl.ds(g, SEQ, stride=SWA_KV_HEADS), :] = v
        k = k.astype(BF16)
        v = v.astype(BF16)
        for h in range(g * g_size, (g + 1) * g_size):
            q = (head(QB0, h) * scale).astype(BF16)
            o = _attend([(_dot_nt(q, k), v)], sink=sink_ref[h])
            o_ref[:, _head_cols(h, BRANCH_W)] = o.astype(BF16)

    for h in range(NA_HEADS):
        k = head(KC0, h)
        v = head(VC0, h)
        kc_o[0, ls, pl.ds(h, SEQ, stride=NA_HEADS), :] = k
        vc_o[0, ls, pl.ds(h, SEQ, stride=NA_HEADS), :] = v
        q = (head(QC0, h) * scale).astype(BF16)
        o = _attend([(_dot_nt(q, k.astype(BF16)), v.astype(BF16))])
        o_ref[:, _head_cols(h, 2 * BRANCH_W)] = o.astype(BF16)

    lam = _diff_lambda(lq1, lk1, lq2, lk2, lam_init)
    scale_d = LOG2E * DIFF_QK ** -0.5
    lo = lax.broadcasted_iota(jnp.int32, (SEQ, 2 * DIFF_QK), 1) < DIFF_QK
    for h in range(DIFF_HEADS):
        k = head(KD0, h)
        v = head(VD0, h)
        kd_o[0, ls, pl.ds(h, SEQ, stride=DIFF_HEADS), :] = k
        vd_o[0, ls, pl.ds(h, SEQ, stride=DIFF_HEADS), :] = v
        k = k.astype(BF16)
        v = v.astype(BF16)
        q = head(QD0, h) * scale_d
        o1 = _attend([(_dot_nt(jnp.where(lo, q, 0.0).astype(BF16), k), v)])
        o2 = _attend([(_dot_nt(jnp.where(lo, 0.0, q).astype(BF16), k), v)])
        o = _rms(o1 - lam * o2) * subln_g[...] * (1.0 - lam_init)
        o_ref[:, _head_cols(h, 3 * BRANCH_W)] = o.astype(BF16)


def _small_specs():
    zero2 = (lambda *a: (0, 0))
    return [
        pl.BlockSpec((1, MLA_Q_RANK), zero2),
        pl.BlockSpec((MLA_Q_RANK, 2 * MLA_QN_W), zero2),
        pl.BlockSpec((1, MLA_KV_RANK), zero2),
        pl.BlockSpec((MLA_KV_RANK, MLA_HEADS * MLA_KV_W), zero2),
        pl.BlockSpec(memory_space=pltpu.SMEM),
        pl.BlockSpec((1, DIFF_QK), zero2),
        pl.BlockSpec((1, DIFF_QK), zero2),
        pl.BlockSpec((1, DIFF_QK), zero2),
        pl.BlockSpec((1, DIFF_QK), zero2),
        pl.BlockSpec((1, DIFF_V), zero2),
    ]


def _ctx_attn_call(p, kr, small, prev, layer, lam_init):
    first = not prev
    row = lambda b: (b, 0)

    def flat(tail):
        return (SEQ * tail[0], tail[1]) if len(tail) == 2 else (SEQ,) + tail

    def cache_spec(tail):
        if first:
            return pl.BlockSpec((1, DEPTH) + flat(tail), lambda b: (b, 0, 0, 0))
        return pl.BlockSpec((1, 1) + flat(tail), lambda b: (b, layer, 0, 0))

    n_fixed = 12
    return pl.pallas_call(
        functools.partial(_ctx_attn_kernel, lam_init=lam_init, first=first),
        grid=(BATCH,),
        in_specs=[pl.BlockSpec((SEQ, P_WIDTH), row), pl.BlockSpec((SEQ, LANES), row)] + _small_specs()
        + [pl.BlockSpec(memory_space=pl.ANY) for _ in prev],
        out_specs=[pl.BlockSpec((SEQ, D_MODEL), row)] + [cache_spec(t) for t in CACHE_TAILS],
        out_shape=[jax.ShapeDtypeStruct((N_CTX_TOK, D_MODEL), BF16)]
        + [jax.ShapeDtypeStruct((BATCH, DEPTH) + flat(t), F32) for t in CACHE_TAILS],
        input_output_aliases={n_fixed + k: 1 + k for k in range(len(prev))},
        compiler_params=pltpu.CompilerParams(
            dimension_semantics=("arbitrary",), vmem_limit_bytes=VMEM_LIMIT),
        name="ctx_attn",
    )(p, kr, *small, *prev)


LAT_TQ = 256
SWA_PAD = SWA_WINDOW
NBR_WIN_ROWS = 12


def _rope_pair64(x, c_ref, sa_ref, sb_ref, rows):
    out = []
    for j in range(x.shape[1] // LANES):
        xj = x[:, j * LANES:(j + 1) * LANES]
        out.append(xj * c_ref[rows, :] + pltpu.roll(xj, 96, 1) * sa_ref[rows, :]
                   + pltpu.roll(xj, 32, 1) * sb_ref[rows, :])
    return jnp.concatenate(out, axis=1) if len(out) > 1 else out[0]


def _rope128(x, c_ref, s_ref, rows):
    out = []
    for j in range(x.shape[1] // LANES):
        xj = x[:, j * LANES:(j + 1) * LANES]
        out.append(xj * c_ref[rows, :] + pltpu.roll(xj, 64, 1) * s_ref[rows, :])
    return jnp.concatenate(out, axis=1) if len(out) > 1 else out[0]


def _lat_attn_kernel(p_ref, kr_ref, c_ckv, c_kr, c_kb, c_vb, c_kc, c_vc, c_kd, c_vd,
                     cos128, sin128, cos64, sina64, sinb64, bias_ref,
                     qn_g, w_uq, kvn_g, w_ukv, sink_ref, lq1, lk1, lq2, lk2, subln_g,
                     o_ref,
                     ka_s, va_s, kbc_s, vbc_s, kbp_s, vbp_s, kcc_s, vcc_s, kdc_s, vdc_s, kdo_s,
                     *, lam_init):
    def cached_head(ref, h, n_heads):
        return ref[pl.ds(h, PAST_LEN, stride=n_heads), :].astype(BF16)

    qi = pl.program_id(1)
    all_rows = slice(None)
    n_own = DEC_SEQ

    @pl.when(qi == 0)
    def _prologue():
        ckv_own = (_rms(p_ref[:, CKV0:CKV0 + MLA_KV_RANK].astype(F32)) * kvn_g[...]).astype(BF16)
        ckv_all = jnp.concatenate([c_ckv[...].astype(BF16), ckv_own], axis=0)
        kr_ctx = jnp.concatenate([c_kr[...], jnp.zeros((PAST_LEN, LANES - MLA_ROPE), F32)], axis=1)
        kr_own = _rope_pair64(kr_ref[...], cos64, sina64, sinb64, all_rows)
        kr_all = jnp.concatenate([kr_ctx, kr_own], axis=0).astype(BF16)
        for h in range(MLA_HEADS):
            kvh = _dot(ckv_all, w_ukv[:, h * MLA_KV_W:(h + 1) * MLA_KV_W])
            ka_s[h, :, 0:MLA_NOPE] = kvh[:, 0:MLA_NOPE].astype(BF16)
            ka_s[h, :, MLA_NOPE:MLA_NOPE + LANES] = kr_all
            va_s[:, _head_cols(h)] = kvh[:, MLA_NOPE:MLA_KV_W].astype(BF16)
        for g in range(SWA_KV_HEADS):
            kbc_s[:, _head_cols(g)] = cached_head(c_kb, g, SWA_KV_HEADS)
            vbc_s[:, _head_cols(g)] = cached_head(c_vb, g, SWA_KV_HEADS)
        zpad = jnp.zeros((SWA_PAD, SWA_KV_W), BF16)
        kbp_s[0:SWA_PAD, :] = zpad
        kbp_s[SWA_PAD + n_own:, :] = zpad
        vbp_s[0:SWA_PAD, :] = zpad
        vbp_s[SWA_PAD + n_own:, :] = zpad
        kbp_s[SWA_PAD:SWA_PAD + n_own, :] = _rope128(
            p_ref[:, KB0:KB0 + SWA_KV_W].astype(F32), cos128, sin128, all_rows).astype(BF16)
        vbp_s[SWA_PAD:SWA_PAD + n_own, :] = p_ref[:, VB0:VB0 + SWA_KV_W]
        for h in range(NA_HEADS):
            cols = _head_cols(h)
            kcc_s[:, cols] = cached_head(c_kc, h, NA_HEADS)
            vcc_s[:, cols] = cached_head(c_vc, h, NA_HEADS)
            kdc_s[:, cols] = cached_head(c_kd, h, DIFF_HEADS)
            vdc_s[:, cols] = cached_head(c_vd, h, DIFF_HEADS)
        kdo_s[...] = _rope_pair64(
            p_ref[:, KD0:KD0 + BRANCH_W].astype(F32), cos64, sina64, sinb64, all_rows).astype(BF16)

    q0 = pl.multiple_of(qi * LAT_TQ, LAT_TQ)
    rows = pl.ds(q0, LAT_TQ)

    scale_a = LOG2E * (MLA_NOPE + MLA_ROPE) ** -0.5
    qa = _mla_queries(p_ref[rows, Q_DOWN0:Q_DOWN0 + MLA_Q_RANK].astype(F32), qn_g, w_uq) * scale_a
    qa_r = _rope_pair64(qa[:, MLA_QN_W:2 * MLA_QN_W], cos64, sina64, sinb64, rows)
    for h in range(MLA_HEADS):
        q = jnp.concatenate([qa[:, _head_cols(h)], qa_r[:, _head_cols(h)]],
                            axis=1).astype(BF16)
        o = _attend([(_dot_nt(q, ka_s[h]), va_s[:, _head_cols(h)])])
        o_ref[:, _head_cols(h)] = o.astype(BF16)

    scale = LOG2E * HEAD_DIM ** -0.5
    qb = (_rope128(p_ref[rows, QB0:QB0 + BRANCH_W].astype(F32), cos128, sin128, rows) * scale).astype(BF16)
    n_loc = LAT_TQ + 2 * SWA_PAD
    qpos = lax.broadcasted_iota(jnp.int32, (LAT_TQ, n_loc), 0)
    kpos = lax.broadcasted_iota(jnp.int32, (LAT_TQ, n_loc), 1) - SWA_PAD
    kabs = kpos + q0
    valid = (jnp.abs(qpos - kpos) <= SWA_WINDOW) & (kabs >= 0) & (kabs < n_own)
    loc_rows = pl.ds(q0, n_loc)
    g_size = SWA_HEADS // SWA_KV_HEADS
    for h in range(SWA_HEADS):
        g = h // g_size
        q = qb[:, _head_cols(h)]
        s_ctx = _dot_nt(q, kbc_s[:, _head_cols(g)])
        s_loc = jnp.where(valid, _dot_nt(q, kbp_s[loc_rows, _head_cols(g)]), NEG_INF)
        o = _attend([(s_ctx, vbc_s[:, _head_cols(g)]),
                     (s_loc, vbp_s[loc_rows, _head_cols(g)])], sink=sink_ref[h])
        o_ref[:, _head_cols(h, BRANCH_W)] = o.astype(BF16)

    q_rows = LAT_TQ // GRID_W
    ws = jnp.clip(qi * q_rows - NA_ROWS // 2, 0, GRID_ROWS - NBR_WIN_ROWS)
    ws = (ws // 2) * 2
    krows = pl.ds(pl.multiple_of(ws * GRID_W, 2 * GRID_W), NBR_WIN_ROWS * GRID_W)
    lower = lax.broadcasted_iota(jnp.int32, (GRID_W, LANES), 1) < GRID_W
    for h in range(NA_HEADS):
        cols = _head_cols(h)
        bias_rows = []
        for t in range(q_rows):
            r = qi * q_rows + t
            off = jnp.clip(r - NA_ROWS // 2, 0, GRID_ROWS - NA_ROWS) - r + (NA_ROWS - 1)
            blocks = []
            for j in range(NBR_WIN_ROWS // 2):
                dr = ws + 2 * j - r + (NA_ROWS - 1)
                ok_lo = ((dr >= off) & (dr < off + NA_ROWS)).astype(jnp.int32)
                ok_hi = ((dr + 1 >= off) & (dr + 1 < off + NA_ROWS)).astype(jnp.int32)
                ok = jnp.where(lower, ok_lo, ok_hi) > 0
                blk = bias_ref[0, h, jnp.clip(dr + 1, 0, NBR_PAIRS - 1)]
                blocks.append(jnp.where(ok, blk, NEG_INF))
            bias_rows.append(jnp.concatenate(blocks, axis=1))
        bias = jnp.concatenate(bias_rows, axis=0)
        q = (p_ref[rows, _head_cols(h, QC0)].astype(F32) * scale).astype(BF16)
        s_loc = _dot_nt(q, p_ref[krows, _head_cols(h, KC0)]) + bias
        s_ctx = _dot_nt(q, kcc_s[:, cols])
        o = _attend([(s_ctx, vcc_s[:, cols]), (s_loc, p_ref[krows, _head_cols(h, VC0)])])
        o_ref[:, _head_cols(h, 2 * BRANCH_W)] = o.astype(BF16)

    scale_d = LOG2E * DIFF_QK ** -0.5
    qd = _rope_pair64(p_ref[rows, QD0:QD0 + BRANCH_W].astype(F32), cos64, sina64, sinb64, rows) * scale_d
    lam = _diff_lambda(lq1, lk1, lq2, lk2, lam_init)
    lo = lax.broadcasted_iota(jnp.int32, (LAT_TQ, 2 * DIFF_QK), 1) < DIFF_QK
    for h in range(DIFF_HEADS):
        cols = _head_cols(h)
        q = qd[:, cols]
        v_own = p_ref[:, _head_cols(h, VD0)]
        q1 = jnp.where(lo, q, 0.0).astype(BF16)
        q2 = jnp.where(lo, 0.0, q).astype(BF16)
        o = _diff_attend([_dot_nt(q1, kdc_s[:, cols]), _dot_nt(q1, kdo_s[:, cols])],
                         [_dot_nt(q2, kdc_s[:, cols]), _dot_nt(q2, kdo_s[:, cols])],
                         [vdc_s[:, cols], v_own], lam)
        o = _rms(o) * subln_g[...] * (1.0 - lam_init)
        o_ref[:, _head_cols(h, 3 * BRANCH_W)] = o.astype(BF16)


def _lat_attn_call(p, kr, caches, tables, bias, small, layer, lam_init):
    nq = DEC_SEQ // LAT_TQ
    batch_rows = lambda b, q: (b, 0)
    zero2 = lambda b, q: (0, 0)
    once = pl.Buffered(1)
    cache_args = [a.reshape(a.shape[:2] + (-1, a.shape[-1])) for a in caches]
    cache_specs = [pl.BlockSpec((None, None) + a.shape[2:], lambda b, q: (b, layer, 0, 0)) for a in cache_args]
    table_specs = [pl.BlockSpec((DEC_SEQ, LANES), zero2) for _ in tables]
    bias_spec = pl.BlockSpec((1, NA_HEADS, NBR_PAIRS, GRID_W, LANES), lambda b, q: (layer, 0, 0, 0, 0))
    n_all = PAST_LEN + DEC_SEQ
    n_pad = DEC_SEQ + 2 * SWA_PAD
    scratch = [
        pltpu.VMEM((MLA_HEADS, n_all, MLA_NOPE + LANES), BF16),
        pltpu.VMEM((n_all, BRANCH_W), BF16),
        pltpu.VMEM((PAST_LEN, SWA_KV_W), BF16), pltpu.VMEM((PAST_LEN, SWA_KV_W), BF16),
        pltpu.VMEM((n_pad, SWA_KV_W), BF16), pltpu.VMEM((n_pad, SWA_KV_W), BF16),
        pltpu.VMEM((PAST_LEN, BRANCH_W), BF16), pltpu.VMEM((PAST_LEN, BRANCH_W), BF16),
        pltpu.VMEM((PAST_LEN, BRANCH_W), BF16), pltpu.VMEM((PAST_LEN, BRANCH_W), BF16),
        pltpu.VMEM((DEC_SEQ, BRANCH_W), BF16),
    ]
    return pl.pallas_call(
        functools.partial(_lat_attn_kernel, lam_init=lam_init),
        grid=(DEC_BATCH, nq),
        in_specs=[pl.BlockSpec((DEC_SEQ, P_WIDTH), batch_rows, pipeline_mode=once),
                  pl.BlockSpec((DEC_SEQ, LANES), batch_rows, pipeline_mode=once)]
        + cache_specs + table_specs + [bias_spec] + _small_specs(),
        out_specs=pl.BlockSpec((LAT_TQ, D_MODEL), lambda b, q: (b * nq + q, 0)),
        out_shape=jax.ShapeDtypeStruct((N_LAT_TOK, D_MODEL), BF16),
        scratch_shapes=scratch,
        compiler_params=pltpu.CompilerParams(
            dimension_semantics=("arbitrary", "arbitrary"), vmem_limit_bytes=VMEM_LIMIT_BIG),
        name="lat_attn",
    )(p, kr, *cache_args, *tables, bias, *small)


def _merge_kernel(h_ref, o_ref, wg_ref, wb_ref, *rest):
    n_side = (len(rest) - 1) // 2
    out_ref = rest[n_side]
    _side_cast(rest[:n_side], rest[n_side + 1:])
    acc = None
    for i in range(N_BRANCH):
        gate = jax.nn.sigmoid(_dot(h_ref[...], wg_ref[i]))
        term = gate * _dot(o_ref[:, i * BRANCH_W:(i + 1) * BRANCH_W], wb_ref[i])
        acc = term if acc is None else acc + term
    out_ref[...] = acc.astype(BF16)


def _merge_call(h, o, wg, wb, name, side_weights=()):
    n_tok = h.shape[0]
    tm, tn = 1024, 512
    grid = (n_tok // tm, D_MODEL // tn)
    side = []
    for w, layer in side_weights:
        lead = w.shape[1:-2]
        block = lead + (w.shape[-2] // grid[0], w.shape[-1] // grid[1])
        side.append((w, layer, block, lambda i, j, z=(0,) * len(lead): z + (i, j)))
    s_in, s_out, s_shape = _side_specs(side, 2)
    return pl.pallas_call(
        _merge_kernel,
        grid=grid,
        in_specs=[
            pl.BlockSpec((tm, D_MODEL), lambda i, j: (i, 0)),
            pl.BlockSpec((tm, D_MODEL), lambda i, j: (i, 0)),
            pl.BlockSpec((None, N_BRANCH, D_MODEL, tn), lambda i, j: (0, 0, 0, j)),
            pl.BlockSpec((None, N_BRANCH, BRANCH_W, tn), lambda i, j: (0, 0, 0, j)),
        ] + s_in,
        out_specs=[pl.BlockSpec((tm, tn), lambda i, j: (i, j))] + s_out,
        out_shape=[jax.ShapeDtypeStruct((n_tok, D_MODEL), BF16)] + s_shape,
        compiler_params=pltpu.CompilerParams(
            dimension_semantics=("arbitrary", "arbitrary"), vmem_limit_bytes=VMEM_LIMIT),
        name=name,
    )(h, o, wg, wb, *[w for w, _, _, _ in side])


def _outproj_kernel(mixed_ref, x_ref, mod_ref, g_ref, gffn_ref, w_ref, *rest):
    n_side = (len(rest) - 2) // 2
    xo_ref, h2_ref = rest[n_side:n_side + 2]
    _side_cast(rest[:n_side], rest[n_side + 2:])
    m = mod_ref[0]
    half = mixed_ref.shape[0] // 2
    for r in (slice(0, half), slice(half, 2 * half)):
        out = _dot(mixed_ref[r, :], w_ref[...])
        x = x_ref[r, :] + m[2:3] * (_rms(out) * g_ref[...])
        xo_ref[r, :] = x
        h2_ref[r, :] = (_rms(x) * gffn_ref[...] * (1.0 + m[4:5]) + m[3:4]).astype(BF16)


def _outproj_call(mixed, x, mod, g, g_ffn, w, seg, name, side_weights=()):
    n_tok = x.shape[0]
    tm = 512
    n_steps = n_tok // tm
    side = [(w_, layer, (w_.shape[1] // n_steps, w_.shape[2]), lambda i: (i, 0)) for w_, layer in side_weights]
    s_in, s_out, s_shape = _side_specs(side, 1)
    return pl.pallas_call(
        _outproj_kernel,
        grid=(n_steps,),
        in_specs=[
            pl.BlockSpec((tm, D_MODEL), lambda i: (i, 0)),
            pl.BlockSpec((tm, D_MODEL), lambda i: (i, 0)),
            pl.BlockSpec((1, 6, D_MODEL), lambda i: (seg(i, tm), 0, 0)),
            pl.BlockSpec((1, D_MODEL), lambda i: (0, 0)),
            pl.BlockSpec((1, D_MODEL), lambda i: (0, 0)),
            pl.BlockSpec((None, D_MODEL, D_MODEL), lambda i: (0, 0, 0)),
        ] + s_in,
        out_specs=[pl.BlockSpec((tm, D_MODEL), lambda i: (i, 0)), pl.BlockSpec((tm, D_MODEL), lambda i: (i, 0))]
        + s_out,
        out_shape=[jax.ShapeDtypeStruct((n_tok, D_MODEL), F32), jax.ShapeDtypeStruct((n_tok, D_MODEL), BF16)]
        + s_shape,
        compiler_params=pltpu.CompilerParams(
            dimension_semantics=("arbitrary",), vmem_limit_bytes=VMEM_LIMIT),
        name=name,
    )(mixed, x, mod, g, g_ffn, w, *[w_ for w_, _, _, _ in side])


def _ffn_kernel(x_ref, h2_ref, mod_ref, gpost_ref, wg_ref, wu_ref, wd_ref, xo_ref, acc_s):
    j = pl.program_id(1)

    @pl.when(j == 0)
    def _():
        acc_s[...] = jnp.zeros_like(acc_s)

    a = _dot(h2_ref[...], wg_ref[...])
    b = _dot(h2_ref[...], wu_ref[...])
    t = (a * jax.nn.sigmoid(a) * b).astype(BF16)
    acc_s[...] += _dot(t, wd_ref[...])

    @pl.when(j == pl.num_programs(1) - 1)
    def _():
        m = mod_ref[0]
        xo_ref[...] = x_ref[...] + m[5:6] * (_rms(acc_s[...]) * gpost_ref[...])


def _ffn_call(x, h2, mod, gpost, wg, wu, wd, seg, name):
    n_tok = x.shape[0]
    tm, th = 512, 512
    return pl.pallas_call(
        _ffn_kernel,
        grid=(n_tok // tm, FFN_HIDDEN // th),
        in_specs=[
            pl.BlockSpec((tm, D_MODEL), lambda i, j: (i, 0)),
            pl.BlockSpec((tm, D_MODEL), lambda i, j: (i, 0)),
            pl.BlockSpec((1, 6, D_MODEL), lambda i, j: (seg(i, tm), 0, 0)),
            pl.BlockSpec((1, D_MODEL), lambda i, j: (0, 0)),
            pl.BlockSpec((None, D_MODEL, th), lambda i, j: (0, 0, j)),
            pl.BlockSpec((None, D_MODEL, th), lambda i, j: (0, 0, j)),
            pl.BlockSpec((None, th, D_MODEL), lambda i, j: (0, j, 0)),
        ],
        out_specs=pl.BlockSpec((tm, D_MODEL), lambda i, j: (i, 0)),
        out_shape=jax.ShapeDtypeStruct((n_tok, D_MODEL), F32),
        scratch_shapes=[pltpu.VMEM((tm, D_MODEL), F32)],
        compiler_params=pltpu.CompilerParams(
            dimension_semantics=("arbitrary", "arbitrary"), vmem_limit_bytes=VMEM_LIMIT),
        name=name,
    )(x, h2, mod, gpost, wg, wu, wd)


def _rope_tables():
    t = np.arange(DEC_SEQ)
    row = (t // GRID_W).astype(np.float32)
    col = (t % GRID_W).astype(np.float32)

    def half_angles(r):
        quarter = r // 4
        inv = (1.0 / (ROPE_BASE ** (np.arange(quarter, dtype=np.float32) / quarter))).astype(np.float32)
        return np.concatenate([row[:, None] * inv, col[:, None] * inv], axis=-1).astype(np.float32)

    a64 = half_angles(HEAD_DIM)
    cos128 = np.concatenate([np.cos(a64), np.cos(a64)], axis=1)
    sin128 = np.concatenate([-np.sin(a64), np.sin(a64)], axis=1)
    a32 = half_angles(MLA_ROPE)
    c, s, z = np.cos(a32), np.sin(a32), np.zeros_like(a32)
    cos64 = np.concatenate([c, c, c, c], axis=1)
    sina64 = np.concatenate([-s, z, -s, z], axis=1)
    sinb64 = np.concatenate([z, s, z, s], axis=1)
    return tuple(jnp.asarray(a, F32) for a in (cos128, sin128, cos64, sina64, sinb64))


def _ctx_seg(i, tm):
    del tm
    return i * 0


def _lat_seg(i, tm):
    return 1 + (i * tm) // DEC_SEQ


def kernel(x_prompt, x_sample, cache_mla_ckv, cache_mla_krope, cache_swa_k, cache_swa_v, cache_na_k, cache_na_v, cache_diff_k, cache_diff_v, c, c_ctx, w_ada, b_ada, mix_pre_g, mix_post_g, ffn_pre_g, ffn_post_g, w_in, mla_q_norm_g, mla_kv_norm_g, w_mla_uq, w_mla_ukv, swa_sink, na_rpb, diff_lq1, diff_lk1, diff_lq2, diff_lk2, diff_subln_g, w_mix_gate, w_branch, w_out, w_ffn_gate, w_ffn_up, w_ffn_down):
    cvec = jnp.concatenate([c_ctx[None, :], c, jnp.zeros((N_MOD_ROWS - 1 - DEC_BATCH, D_MODEL), F32)], axis=0)
    mod_all = _ada_call(cvec, w_ada, b_ada).reshape(DEPTH, N_MOD_ROWS, 6, D_MODEL)
    bias_all = _nbr_bias_call(na_rpb)
    tables = _rope_tables()
    caches = (cache_mla_ckv, cache_mla_krope, cache_swa_k, cache_swa_v, cache_na_k, cache_na_v,
              cache_diff_k, cache_diff_v)

    w_main, w_kr = _cast_w_in_call(w_in)
    wg = _cast_call(w_mix_gate, 512, "cast_w_gate")
    wb = _cast_call(w_branch, 512, "cast_w_branch")
    wo = _cast_call(w_out, 512, "cast_w_out")
    wfg = _cast_call(w_ffn_gate, 256, "cast_w_ffn_gate")
    wfu = _cast_call(w_ffn_up, 256, "cast_w_ffn_up")
    wfd = _cast_call(w_ffn_down, 512, "cast_w_ffn_down")

    y = x_prompt.reshape(N_CTX_TOK, D_MODEL)
    z = x_sample.reshape(N_LAT_TOK, D_MODEL)
    new_caches = ()
    for l in range(DEPTH):
        lam_init = _lambda_init(l)
        mod = mod_all[l]
        uq = w_mla_uq[l].reshape(MLA_Q_RANK, MLA_HEADS, MLA_NOPE + MLA_ROPE)
        uq_rope = jnp.pad(uq[:, :, MLA_NOPE:], ((0, 0), (0, 0), (0, LANES - MLA_ROPE)))
        w_uq = jnp.concatenate([uq[:, :, :MLA_NOPE].reshape(MLA_Q_RANK, MLA_QN_W),
                                uq_rope.reshape(MLA_Q_RANK, MLA_QN_W)], axis=1).astype(BF16)
        small = (mla_q_norm_g[l][None, :], w_uq, mla_kv_norm_g[l][None, :], w_mla_ukv[l].astype(BF16),
                 swa_sink[l], diff_lq1[l][None, :], diff_lk1[l][None, :], diff_lq2[l][None, :],
                 diff_lk2[l][None, :], diff_subln_g[l][None, :])
        g_pre, g_post = mix_pre_g[l][None, :], mix_post_g[l][None, :]
        f_pre, f_post = ffn_pre_g[l][None, :], ffn_post_g[l][None, :]

        nxt = l + 1 if l + 1 < DEPTH else None
        side = (lambda *ws: ()) if nxt is None else (lambda *ws: tuple((w, nxt) for w in ws))

        h, p, kr = _inproj_call(y, mod, g_pre, w_main, w_kr, l, _ctx_seg, F32, "ctx_inproj")
        o, *new_caches = _ctx_attn_call(p, kr, small, new_caches, l, lam_init)
        mixed, *nxt_a = _merge_call(h, o, wg, wb, "ctx_merge", side(w_mix_gate, w_branch))
        y, h2, *nxt_b = _outproj_call(mixed, y, mod, g_post, f_pre, wo, _ctx_seg, "ctx_outproj",
                                      side(w_out, w_ffn_gate))
        y = _ffn_call(y, h2, mod, f_post, wfg, wfu, wfd, _ctx_seg, "ctx_ffn")

        h, p, kr = _inproj_call(z, mod, g_pre, w_main, w_kr, l, _lat_seg, BF16, "lat_inproj")
        o = _lat_attn_call(p, kr, caches, tables, bias_all, small, l, lam_init)
        mixed, *nxt_c = _merge_call(h, o, wg, wb, "lat_merge", side(w_ffn_down))
        z, h2, *nxt_d = _outproj_call(mixed, z, mod, g_post, f_pre, wo, _lat_seg, "lat_outproj", side(w_ffn_up))
        z = _ffn_call(z, h2, mod, f_post, wfg, wfu, wfd, _lat_seg, "lat_ffn")
        if nxt is not None:
            (wg, wb), (wo, wfg), (wfd,), (wfu,) = nxt_a, nxt_b, nxt_c, nxt_d

    new_caches = [a.reshape((BATCH, DEPTH, SEQ) + t) for a, t in zip(new_caches, CACHE_TAILS)]
    return (y.reshape(BATCH, SEQ, D_MODEL), z.reshape(DEC_BATCH, DEC_SEQ, D_MODEL), *new_caches)
```

```python
import functools
import math

import numpy as np
import jax
import jax.numpy as jnp
from jax import lax
from jax.experimental import pallas as pl
from jax.experimental.pallas import tpu as pltpu

D_MODEL = 2048
BATCH = 32
SEQ = 256
DEPTH = 2
DEC_BATCH = 8
DEC_SEQ = 1024
PAST_LEN = 512
GRID_W = 64
HEAD_DIM = 128
ROPE_BASE = 10000.0
EPS = 1e-6
NEG_INF = -1e30
LOG2E = math.log2(math.e)

MLA_HEADS = 4
MLA_Q_RANK = 384
MLA_KV_RANK = 128
MLA_NOPE = 128
MLA_ROPE = 64
MLA_V = 128
SWA_HEADS = 4
SWA_KV_HEADS = 2
SWA_WINDOW = 128
NA_HEADS = 4
NA_ROWS = 8
NA_COLS = 16
DIFF_HEADS = 4
DIFF_QK = 64
DIFF_V = 128
N_BRANCH = 4
BRANCH_W = 512
FFN_HIDDEN = 5632

N_CTX_TOK = BATCH * SEQ
N_LAT_TOK = DEC_BATCH * DEC_SEQ
GRID_ROWS = DEC_SEQ // GRID_W

Q_DOWN0, CKV0, QB0, KB0, VB0, QC0, KC0, VC0, QD0, KD0, VD0 = (
    0, 384, 512, 1024, 1280, 1536, 2048, 2560, 3072, 3584, 4096)
KROPE_SRC0 = MLA_Q_RANK + MLA_KV_RANK
P_WIDTH = 4608
MLA_QN_W = MLA_HEADS * MLA_NOPE
MLA_KV_W = MLA_NOPE + MLA_V
SWA_KV_W = SWA_KV_HEADS * HEAD_DIM
N_MOD_ROWS = 16

CACHE_TAILS = ((MLA_KV_RANK,), (MLA_ROPE,), (SWA_KV_HEADS, HEAD_DIM), (SWA_KV_HEADS, HEAD_DIM),
               (NA_HEADS, HEAD_DIM), (NA_HEADS, HEAD_DIM), (DIFF_HEADS, 2 * DIFF_QK), (DIFF_HEADS, DIFF_V))

VMEM_LIMIT = 56 * 1024 * 1024
VMEM_LIMIT_BIG = 62 * 1024 * 1024
LANES = 128

F32 = jnp.float32
BF16 = jnp.bfloat16


def _lambda_init(l):
    return 0.8 - 0.6 * math.exp(-0.3 * l)


def _rms(x):
    return x * lax.rsqrt(jnp.mean(x * x, axis=-1, keepdims=True) + EPS)


def _head_cols(h, base=0):
    return slice(base + h * HEAD_DIM, base + (h + 1) * HEAD_DIM)


def _dot(a, b):
    return jnp.dot(a, b, preferred_element_type=F32)


def _dot_nt(a, b):
    return lax.dot_general(a, b, (((1,), (1,)), ((), ())), preferred_element_type=F32)


def _attend(pieces, sink=None):
    m = None
    for s, _ in pieces:
        mi = jnp.max(s, axis=-1, keepdims=True)
        m = mi if m is None else jnp.maximum(m, mi)
    if sink is not None:
        sink = sink * LOG2E
        m = jnp.maximum(m, sink)
    den = None
    num = None
    for s, v in pieces:
        e = jnp.exp2(s - m)
        d = jnp.sum(e, axis=-1, keepdims=True)
        n = _dot(e.astype(BF16), v)
        den = d if den is None else den + d
        num = n if num is None else num + n
    if sink is not None:
        den = den + jnp.exp2(sink - m)
    return num / den


def _diff_attend(pieces1, pieces2, values, lam):
    def exps(pieces):
        m = None
        for s in pieces:
            mi = jnp.max(s, axis=-1, keepdims=True)
            m = mi if m is None else jnp.maximum(m, mi)
        es = [jnp.exp2(s - m) for s in pieces]
        den = None
        for e in es:
            d = jnp.sum(e, axis=-1, keepdims=True)
            den = d if den is None else den + d
        return es, den

    es1, den1 = exps(pieces1)
    es2, den2 = exps(pieces2)
    ratio = lam * den1 / den2
    out = None
    for e1, e2, v in zip(es1, es2, values):
        n = _dot((e1 - e2 * ratio).astype(BF16), v)
        out = n if out is None else out + n
    return out / den1


def _diff_lambda(lq1, lk1, lq2, lk2, lam_init):
    a = jnp.sum(lq1[...] * lk1[...], axis=-1, keepdims=True)
    b = jnp.sum(lq2[...] * lk2[...], axis=-1, keepdims=True)
    return jnp.exp(a) - jnp.exp(b) + lam_init


def _ada_kernel(c_ref, w_ref, b_ref, o_ref):
    c = c_ref[...]
    s = (c * jax.nn.sigmoid(c)).astype(BF16)
    o_ref[0] = _dot(s, w_ref[0].astype(BF16)) + b_ref[0]


def _ada_call(cvec, w_ada, b_ada):
    tn = 1024
    n = 6 * D_MODEL
    return pl.pallas_call(
        _ada_kernel,
        grid=(DEPTH, n // tn),
        in_specs=[
            pl.BlockSpec((N_MOD_ROWS, D_MODEL), lambda l, j: (0, 0)),
            pl.BlockSpec((1, D_MODEL, tn), lambda l, j: (l, 0, j)),
            pl.BlockSpec((1, 1, tn), lambda l, j: (l, 0, j)),
        ],
        out_specs=pl.BlockSpec((1, N_MOD_ROWS, tn), lambda l, j: (l, 0, j)),
        out_shape=jax.ShapeDtypeStruct((DEPTH, N_MOD_ROWS, n), F32),
        compiler_params=pltpu.CompilerParams(
            dimension_semantics=("arbitrary", "arbitrary"), vmem_limit_bytes=VMEM_LIMIT),
        name="ada_mod",
    )(cvec, w_ada, b_ada.reshape(DEPTH, 1, n))


NBR_PAIRS = 2 * NA_ROWS


def _nbr_bias_kernel(rpb_ref, o_ref):
    l = pl.program_id(0)
    h = pl.program_id(1)
    n_dr = 2 * NA_ROWS - 1
    n_dc = 2 * NA_COLS - 1
    base = (l * NA_HEADS + h) * (n_dr * n_dc)
    wq = lax.broadcasted_iota(jnp.int32, (GRID_W, LANES), 0)
    lane = lax.broadcasted_iota(jnp.int32, (GRID_W, LANES), 1)
    upper = lane >= GRID_W
    wk = jnp.where(upper, lane - GRID_W, lane)
    delta = wk - wq + (NA_COLS - 1)
    cs = jnp.clip(wq - NA_COLS // 2, 0, GRID_W - NA_COLS)
    valid = (wk >= cs) & (wk < cs + NA_COLS)
    for e in range(NBR_PAIRS):
        dr_lo, dr_hi = max(e - 1, 0), min(e, n_dr - 1)
        acc = jnp.full((GRID_W, LANES), NEG_INF, F32)
        for d in range(n_dc):
            lo = rpb_ref[base + dr_lo * n_dc + d]
            hi = rpb_ref[base + dr_hi * n_dc + d]
            acc = jnp.where(delta == d, jnp.where(upper, hi, lo), acc)
        o_ref[0, 0, e] = jnp.where(valid, acc * LOG2E, NEG_INF)


def _nbr_bias_call(na_rpb):
    n_pairs = NBR_PAIRS
    return pl.pallas_call(
        _nbr_bias_kernel,
        grid=(DEPTH, NA_HEADS),
        in_specs=[pl.BlockSpec(memory_space=pltpu.SMEM)],
        out_specs=pl.BlockSpec((1, 1, n_pairs, GRID_W, LANES), lambda l, h: (l, h, 0, 0, 0)),
        out_shape=jax.ShapeDtypeStruct((DEPTH, NA_HEADS, n_pairs, GRID_W, LANES), F32),
        compiler_params=pltpu.CompilerParams(dimension_semantics=("arbitrary", "arbitrary")),
        name="nbr_bias",
    )(na_rpb.reshape(-1))


def _inproj_kernel(x_ref, mod_ref, g_ref, w_ref, wkr_ref, h_ref, p_ref, kr_ref):
    m = mod_ref[0]
    half = x_ref.shape[0] // 2
    for r in (slice(0, half), slice(half, 2 * half)):
        h = (_rms(x_ref[r, :]) * g_ref[...] * (1.0 + m[1:2]) + m[0:1]).astype(BF16)
        h_ref[r, :] = h
        kr_ref[r, :] = _dot_nt(h, wkr_ref[...])
        p_ref[r, :] = _dot_nt(h, w_ref[...]).astype(p_ref.dtype)


def _inproj_call(x, mod, g, w_main, w_kr, layer, seg, p_dtype, name):
    n_tok = x.shape[0]
    tm = 512
    return pl.pallas_call(
        _inproj_kernel,
        grid=(n_tok // tm,),
        in_specs=[
            pl.BlockSpec((tm, D_MODEL), lambda i: (i, 0)),
            pl.BlockSpec((1, 6, D_MODEL), lambda i: (seg(i, tm), 0, 0)),
            pl.BlockSpec((1, D_MODEL), lambda i: (0, 0)),
            pl.BlockSpec((None, P_WIDTH, D_MODEL), lambda i: (layer, 0, 0)),
            pl.BlockSpec((None, LANES, D_MODEL), lambda i: (layer, 0, 0)),
        ],
        out_specs=[
            pl.BlockSpec((tm, D_MODEL), lambda i: (i, 0)),
            pl.BlockSpec((tm, P_WIDTH), lambda i: (i, 0)),
            pl.BlockSpec((tm, LANES), lambda i: (i, 0)),
        ],
        out_shape=[
            jax.ShapeDtypeStruct((n_tok, D_MODEL), BF16),
            jax.ShapeDtypeStruct((n_tok, P_WIDTH), p_dtype),
            jax.ShapeDtypeStruct((n_tok, LANES), F32),
        ],
        compiler_params=pltpu.CompilerParams(
            dimension_semantics=("arbitrary",), vmem_limit_bytes=VMEM_LIMIT_BIG),
        name=name,
    )(x, mod, g, w_main, w_kr)


def _cast_kernel(x_ref, o_ref):
    o_ref[...] = x_ref[...].astype(BF16)


def _cast_call(w, block_rows, name):
    w2 = w.reshape(-1, w.shape[-1])
    rows, cols = w2.shape[0] // w.shape[0], w2.shape[1]
    out = pl.pallas_call(
        _cast_kernel,
        grid=(rows // block_rows,),
        in_specs=[pl.BlockSpec((block_rows, cols), lambda i: (i, 0))],
        out_specs=pl.BlockSpec((block_rows, cols), lambda i: (i, 0)),
        out_shape=jax.ShapeDtypeStruct((rows, cols), BF16),
        compiler_params=pltpu.CompilerParams(dimension_semantics=("arbitrary",), vmem_limit_bytes=VMEM_LIMIT),
        name=name,
    )(w2)
    return out.reshape((1,) + w.shape[1:])


def _side_specs(side, n_grid):
    in_specs, out_specs, out_shape = [], [], []
    for w, layer, block, index in side:
        if n_grid == 1:
            in_specs.append(pl.BlockSpec((None,) + block, lambda i, f=index, l=layer: (l,) + f(i)))
            out_specs.append(pl.BlockSpec((None,) + block, lambda i, f=index: (0,) + f(i)))
        else:
            in_specs.append(pl.BlockSpec((None,) + block, lambda i, j, f=index, l=layer: (l,) + f(i, j)))
            out_specs.append(pl.BlockSpec((None,) + block, lambda i, j, f=index: (0,) + f(i, j)))
        out_shape.append(jax.ShapeDtypeStruct((1,) + w.shape[1:], BF16))
    return in_specs, out_specs, out_shape


def _side_cast(srcs, dsts):
    for src, dst in zip(srcs, dsts):
        dst[...] = src[...].astype(BF16)


def _cast_w_in_kernel(x_ref, xkr_ref, main_ref, kr_ref):
    main_ref[...] = x_ref[0].astype(BF16)

    @pl.when(pl.program_id(1) == 0)
    def _():
        kr_ref[:MLA_ROPE, :] = xkr_ref[...].astype(BF16)
        kr_ref[MLA_ROPE:, :] = jnp.zeros((LANES - MLA_ROPE, D_MODEL), BF16)


def _cast_w_in_call(w_in):
    w_t = jnp.swapaxes(w_in, 1, 2)
    block_rows = 512
    src_row = lambda l, j: (l, pl.multiple_of(j * block_rows + jnp.where(j * block_rows >= KROPE_SRC0, MLA_ROPE, 0), MLA_ROPE), 0)
    return pl.pallas_call(
        _cast_w_in_kernel,
        grid=(DEPTH, P_WIDTH // block_rows),
        in_specs=[pl.BlockSpec((pl.Element(1), pl.Element(block_rows), pl.Element(D_MODEL)), src_row),
                  pl.BlockSpec((None, MLA_ROPE, D_MODEL), lambda l, j: (l, KROPE_SRC0 // MLA_ROPE, 0))],
        out_specs=[pl.BlockSpec((None, block_rows, D_MODEL), lambda l, j: (l, j, 0)),
                   pl.BlockSpec((None, LANES, D_MODEL), lambda l, j: (l, 0, 0))],
        out_shape=[jax.ShapeDtypeStruct((DEPTH, P_WIDTH, D_MODEL), BF16),
                   jax.ShapeDtypeStruct((DEPTH, LANES, D_MODEL), BF16)],
        compiler_params=pltpu.CompilerParams(
            dimension_semantics=("arbitrary", "arbitrary"), vmem_limit_bytes=VMEM_LIMIT),
        name="cast_w_in",
    )(w_t, w_t)


def _mla_queries(q_down, qn_g, w_uq):
    cq = _rms(q_down) * qn_g[...]
    return _dot(cq.astype(BF16), w_uq[...])


def _ctx_attn_kernel(*refs, lam_init, first):
    (p_ref, kr_ref, qn_g, w_uq, kvn_g, w_ukv, sink_ref, lq1, lk1, lq2, lk2, subln_g) = refs[:12]
    n_in = 12 + (0 if first else len(CACHE_TAILS))
    o_ref = refs[n_in]
    ckv_o, kr_o, kb_o, vb_o, kc_o, vc_o, kd_o, vd_o = refs[n_in + 1:]
    ls = 0
    if first:
        for dst in (ckv_o, kr_o, kb_o, vb_o, kc_o, vc_o, kd_o, vd_o):
            dst[0, 1:] = jnp.zeros(dst.shape[1:], F32)[1:]

    scale_a = LOG2E * (MLA_NOPE + MLA_ROPE) ** -0.5
    qa = _mla_queries(p_ref[:, Q_DOWN0:Q_DOWN0 + MLA_Q_RANK], qn_g, w_uq) * scale_a
    ckv = _rms(p_ref[:, CKV0:CKV0 + MLA_KV_RANK]) * kvn_g[...]
    ckv_o[0, ls] = ckv
    kr = kr_ref[...]
    kr_o[0, ls] = kr[:, :MLA_ROPE]
    kr_b = kr.astype(BF16)
    kv = _dot(ckv.astype(BF16), w_ukv[...])
    for h in range(MLA_HEADS):
        q = jnp.concatenate([qa[:, _head_cols(h)], qa[:, _head_cols(h, MLA_QN_W)]],
                            axis=1).astype(BF16)
        kv0 = h * MLA_KV_W
        k = jnp.concatenate([kv[:, kv0:kv0 + MLA_NOPE].astype(BF16), kr_b], axis=1)
        v = kv[:, kv0 + MLA_NOPE:kv0 + MLA_KV_W].astype(BF16)
        o = _attend([(_dot_nt(q, k), v)])
        o_ref[:, _head_cols(h)] = o.astype(BF16)

    scale = LOG2E * HEAD_DIM ** -0.5

    def head(col0, h):
        return p_ref[:, _head_cols(h, col0)]

    g_size = SWA_HEADS // SWA_KV_HEADS
    for g in range(SWA_KV_HEADS):
        k = head(KB0, g)
        v = head(VB0, g)
        kb_o[0, ls, pl.ds(g, SEQ, stride=SWA_KV_HEADS), :] = k
        vb_o[0, ls, pl.ds(g, SEQ, stride=SWA_KV_HEADS), :] = v
        k = k.astype(BF16)
        v = v.astype(BF16)
        for h in range(g * g_size, (g + 1) * g_size):
            q = (head(QB0, h) * scale).astype(BF16)
            o = _attend([(_dot_nt(q, k), v)], sink=sink_ref[h])
            o_ref[:, _head_cols(h, BRANCH_W)] = o.astype(BF16)

    for h in range(NA_HEADS):
        k = head(KC0, h)
        v = head(VC0, h)
        kc_o[0, ls, pl.ds(h, SEQ, stride=NA_HEADS), :] = k
        vc_o[0, ls, pl.ds(h, SEQ, stride=NA_HEADS), :] = v
        q = (head(QC0, h) * scale).astype(BF16)
        o = _attend([(_dot_nt(q, k.astype(BF16)), v.astype(BF16))])
        o_ref[:, _head_cols(h, 2 * BRANCH_W)] = o.astype(BF16)

    lam = _diff_lambda(lq1, lk1, lq2, lk2, lam_init)
    scale_d = LOG2E * DIFF_QK ** -0.5
    lo = lax.broadcasted_iota(jnp.int32, (SEQ, 2 * DIFF_QK), 1) < DIFF_QK
    for h in range(DIFF_HEADS):
        k = head(KD0, h)
        v = head(VD0, h)
        kd_o[0, ls, pl.ds(h, SEQ, stride=DIFF_HEADS), :] = k
        vd_o[0, ls, pl.ds(h, SEQ, stride=DIFF_HEADS), :] = v
        k = k.astype(BF16)
        v = v.astype(BF16)
        q = head(QD0, h) * scale_d
        o1 = _attend([(_dot_nt(jnp.where(lo, q, 0.0).astype(BF16), k), v)])
        o2 = _attend([(_dot_nt(jnp.where(lo, 0.0, q).astype(BF16), k), v)])
        o = _rms(o1 - lam * o2) * subln_g[...] * (1.0 - lam_init)
        o_ref[:, _head_cols(h, 3 * BRANCH_W)] = o.astype(BF16)


def _small_specs():
    zero2 = (lambda *a: (0, 0))
    return [
        pl.BlockSpec((1, MLA_Q_RANK), zero2),
        pl.BlockSpec((MLA_Q_RANK, 2 * MLA_QN_W), zero2),
        pl.BlockSpec((1, MLA_KV_RANK), zero2),
        pl.BlockSpec((MLA_KV_RANK, MLA_HEADS * MLA_KV_W), zero2),
        pl.BlockSpec(memory_space=pltpu.SMEM),
        pl.BlockSpec((1, DIFF_QK), zero2),
        pl.BlockSpec((1, DIFF_QK), zero2),
        pl.BlockSpec((1, DIFF_QK), zero2),
        pl.BlockSpec((1, DIFF_QK), zero2),
        pl.BlockSpec((1, DIFF_V), zero2),
    ]


def _ctx_attn_call(p, kr, small, prev, layer, lam_init):
    first = not prev
    row = lambda b: (b, 0)

    def flat(tail):
        return (SEQ * tail[0], tail[1]) if len(tail) == 2 else (SEQ,) + tail

    def cache_spec(tail):
        if first:
            return pl.BlockSpec((1, DEPTH) + flat(tail), lambda b: (b, 0, 0, 0))
        return pl.BlockSpec((1, 1) + flat(tail), lambda b: (b, layer, 0, 0))

    n_fixed = 12
    return pl.pallas_call(
        functools.partial(_ctx_attn_kernel, lam_init=lam_init, first=first),
        grid=(BATCH,),
        in_specs=[pl.BlockSpec((SEQ, P_WIDTH), row), pl.BlockSpec((SEQ, LANES), row)] + _small_specs()
        + [pl.BlockSpec(memory_space=pl.ANY) for _ in prev],
        out_specs=[pl.BlockSpec((SEQ, D_MODEL), row)] + [cache_spec(t) for t in CACHE_TAILS],
        out_shape=[jax.ShapeDtypeStruct((N_CTX_TOK, D_MODEL), BF16)]
        + [jax.ShapeDtypeStruct((BATCH, DEPTH) + flat(t), F32) for t in CACHE_TAILS],
        input_output_aliases={n_fixed + k: 1 + k for k in range(len(prev))},
        compiler_params=pltpu.CompilerParams(
            dimension_semantics=("arbitrary",), vmem_limit_bytes=VMEM_LIMIT),
        name="ctx_attn",
    )(p, kr, *small, *prev)


LAT_TQ = 256
SWA_PAD = SWA_WINDOW
NBR_WIN_ROWS = 12


def _rope_pair64(x, c_ref, sa_ref, sb_ref, rows):
    out = []
    for j in range(x.shape[1] // LANES):
        xj = x[:, j * LANES:(j + 1) * LANES]
        out.append(xj * c_ref[rows, :] + pltpu.roll(xj, 96, 1) * sa_ref[rows, :]
                   + pltpu.roll(xj, 32, 1) * sb_ref[rows, :])
    return jnp.concatenate(out, axis=1) if len(out) > 1 else out[0]


def _rope128(x, c_ref, s_ref, rows):
    out = []
    for j in range(x.shape[1] // LANES):
        xj = x[:, j * LANES:(j + 1) * LANES]
        out.append(xj * c_ref[rows, :] + pltpu.roll(xj, 64, 1) * s_ref[rows, :])
    return jnp.concatenate(out, axis=1) if len(out) > 1 else out[0]


def _lat_attn_kernel(p_ref, kr_ref, c_ckv, c_kr, c_kb, c_vb, c_kc, c_vc, c_kd, c_vd,
                     cos128, sin128, cos64, sina64, sinb64, bias_ref,
                     qn_g, w_uq, kvn_g, w_ukv, sink_ref, lq1, lk1, lq2, lk2, subln_g,
                     o_ref,
                     ka_s, va_s, kbc_s, vbc_s, kbp_s, vbp_s, kcc_s, vcc_s, kdc_s, vdc_s, kdo_s,
                     *, lam_init):
    def cached_head(ref, h, n_heads):
        return ref[pl.ds(h, PAST_LEN, stride=n_heads), :].astype(BF16)

    qi = pl.program_id(1)
    all_rows = slice(None)
    n_own = DEC_SEQ

    @pl.when(qi == 0)
    def _prologue():
        ckv_own = (_rms(p_ref[:, CKV0:CKV0 + MLA_KV_RANK].astype(F32)) * kvn_g[...]).astype(BF16)
        ckv_all = jnp.concatenate([c_ckv[...].astype(BF16), ckv_own], axis=0)
        kr_ctx = jnp.concatenate([c_kr[...], jnp.zeros((PAST_LEN, LANES - MLA_ROPE), F32)], axis=1)
        kr_own = _rope_pair64(kr_ref[...], cos64, sina64, sinb64, all_rows)
        kr_all = jnp.concatenate([kr_ctx, kr_own], axis=0).astype(BF16)
        for h in range(MLA_HEADS):
            kvh = _dot(ckv_all, w_ukv[:, h * MLA_KV_W:(h + 1) * MLA_KV_W])
            ka_s[h, :, 0:MLA_NOPE] = kvh[:, 0:MLA_NOPE].astype(BF16)
            ka_s[h, :, MLA_NOPE:MLA_NOPE + LANES] = kr_all
            va_s[:, _head_cols(h)] = kvh[:, MLA_NOPE:MLA_KV_W].astype(BF16)
        for g in range(SWA_KV_HEADS):
            kbc_s[:, _head_cols(g)] = cached_head(c_kb, g, SWA_KV_HEADS)
            vbc_s[:, _head_cols(g)] = cached_head(c_vb, g, SWA_KV_HEADS)
        zpad = jnp.zeros((SWA_PAD, SWA_KV_W), BF16)
        kbp_s[0:SWA_PAD, :] = zpad
        kbp_s[SWA_PAD + n_own:, :] = zpad
        vbp_s[0:SWA_PAD, :] = zpad
        vbp_s[SWA_PAD + n_own:, :] = zpad
        kbp_s[SWA_PAD:SWA_PAD + n_own, :] = _rope128(
            p_ref[:, KB0:KB0 + SWA_KV_W].astype(F32), cos128, sin128, all_rows).astype(BF16)
        vbp_s[SWA_PAD:SWA_PAD + n_own, :] = p_ref[:, VB0:VB0 + SWA_KV_W]
        for h in range(NA_HEADS):
            cols = _head_cols(h)
            kcc_s[:, cols] = cached_head(c_kc, h, NA_HEADS)
            vcc_s[:, cols] = cached_head(c_vc, h, NA_HEADS)
            kdc_s[:, cols] = cached_head(c_kd, h, DIFF_HEADS)
            vdc_s[:, cols] = cached_head(c_vd, h, DIFF_HEADS)
        kdo_s[...] = _rope_pair64(
            p_ref[:, KD0:KD0 + BRANCH_W].astype(F32), cos64, sina64, sinb64, all_rows).astype(BF16)

    q0 = pl.multiple_of(qi * LAT_TQ, LAT_TQ)
    rows = pl.ds(q0, LAT_TQ)

    scale_a = LOG2E * (MLA_NOPE + MLA_ROPE) ** -0.5
    qa = _mla_queries(p_ref[rows, Q_DOWN0:Q_DOWN0 + MLA_Q_RANK].astype(F32), qn_g, w_uq) * scale_a
    qa_r = _rope_pair64(qa[:, MLA_QN_W:2 * MLA_QN_W], cos64, sina64, sinb64, rows)
    for h in range(MLA_HEADS):
        q = jnp.concatenate([qa[:, _head_cols(h)], qa_r[:, _head_cols(h)]],
                            axis=1).astype(BF16)
        o = _attend([(_dot_nt(q, ka_s[h]), va_s[:, _head_cols(h)])])
        o_ref[:, _head_cols(h)] = o.astype(BF16)

    scale = LOG2E * HEAD_DIM ** -0.5
    qb = (_rope128(p_ref[rows, QB0:QB0 + BRANCH_W].astype(F32), cos128, sin128, rows) * scale).astype(BF16)
    n_loc = LAT_TQ + 2 * SWA_PAD
    qpos = lax.broadcasted_iota(jnp.int32, (LAT_TQ, n_loc), 0)
    kpos = lax.broadcasted_iota(jnp.int32, (LAT_TQ, n_loc), 1) - SWA_PAD
    kabs = kpos + q0
    valid = (jnp.abs(qpos - kpos) <= SWA_WINDOW) & (kabs >= 0) & (kabs < n_own)
    loc_rows = pl.ds(q0, n_loc)
    g_size = SWA_HEADS // SWA_KV_HEADS
    for h in range(SWA_HEADS):
        g = h // g_size
        q = qb[:, _head_cols(h)]
        s_ctx = _dot_nt(q, kbc_s[:, _head_cols(g)])
        s_loc = jnp.where(valid, _dot_nt(q, kbp_s[loc_rows, _head_cols(g)]), NEG_INF)
        o = _attend([(s_ctx, vbc_s[:, _head_cols(g)]),
                     (s_loc, vbp_s[loc_rows, _head_cols(g)])], sink=sink_ref[h])
        o_ref[:, _head_cols(h, BRANCH_W)] = o.astype(BF16)

    q_rows = LAT_TQ // GRID_W
    ws = jnp.clip(qi * q_rows - NA_ROWS // 2, 0, GRID_ROWS - NBR_WIN_ROWS)
    ws = (ws // 2) * 2
    krows = pl.ds(pl.multiple_of(ws * GRID_W, 2 * GRID_W), NBR_WIN_ROWS * GRID_W)
    lower = lax.broadcasted_iota(jnp.int32, (GRID_W, LANES), 1) < GRID_W
    for h in range(NA_HEADS):
        cols = _head_cols(h)
        bias_rows = []
        for t in range(q_rows):
            r = qi * q_rows + t
            off = jnp.clip(r - NA_ROWS // 2, 0, GRID_ROWS - NA_ROWS) - r + (NA_ROWS - 1)
            blocks = []
            for j in range(NBR_WIN_ROWS // 2):
                dr = ws + 2 * j - r + (NA_ROWS - 1)
                ok_lo = ((dr >= off) & (dr < off + NA_ROWS)).astype(jnp.int32)
                ok_hi = ((dr + 1 >= off) & (dr + 1 < off + NA_ROWS)).astype(jnp.int32)
                ok = jnp.where(lower, ok_lo, ok_hi) > 0
                blk = bias_ref[0, h, jnp.clip(dr + 1, 0, NBR_PAIRS - 1)]
                blocks.append(jnp.where(ok, blk, NEG_INF))
            bias_rows.append(jnp.concatenate(blocks, axis=1))
        bias = jnp.concatenate(bias_rows, axis=0)
        q = (p_ref[rows, _head_cols(h, QC0)].astype(F32) * scale).astype(BF16)
        s_loc = _dot_nt(q, p_ref[krows, _head_cols(h, KC0)]) + bias
        s_ctx = _dot_nt(q, kcc_s[:, cols])
        o = _attend([(s_ctx, vcc_s[:, cols]), (s_loc, p_ref[krows, _head_cols(h, VC0)])])
        o_ref[:, _head_cols(h, 2 * BRANCH_W)] = o.astype(BF16)

    scale_d = LOG2E * DIFF_QK ** -0.5
    qd = _rope_pair64(p_ref[rows, QD0:QD0 + BRANCH_W].astype(F32), cos64, sina64, sinb64, rows) * scale_d
    lam = _diff_lambda(lq1, lk1, lq2, lk2, lam_init)
    lo = lax.broadcasted_iota(jnp.int32, (LAT_TQ, 2 * DIFF_QK), 1) < DIFF_QK
    for h in range(DIFF_HEADS):
        cols = _head_cols(h)
        q = qd[:, cols]
        v_own = p_ref[:, _head_cols(h, VD0)]
        q1 = jnp.where(lo, q, 0.0).astype(BF16)
        q2 = jnp.where(lo, 0.0, q).astype(BF16)
        o = _diff_attend([_dot_nt(q1, kdc_s[:, cols]), _dot_nt(q1, kdo_s[:, cols])],
                         [_dot_nt(q2, kdc_s[:, cols]), _dot_nt(q2, kdo_s[:, cols])],
                         [vdc_s[:, cols], v_own], lam)
        o = _rms(o) * subln_g[...] * (1.0 - lam_init)
        o_ref[:, _head_cols(h, 3 * BRANCH_W)] = o.astype(BF16)


def _lat_attn_call(p, kr, caches, tables, bias, small, layer, lam_init):
    nq = DEC_SEQ // LAT_TQ
    batch_rows = lambda b, q: (b, 0)
    zero2 = lambda b, q: (0, 0)
    once = pl.Buffered(1)
    cache_args = [a.reshape(a.shape[:2] + (-1, a.shape[-1])) for a in caches]
    cache_specs = [pl.BlockSpec((None, None) + a.shape[2:], lambda b, q: (b, layer, 0, 0)) for a in cache_args]
    table_specs = [pl.BlockSpec((DEC_SEQ, LANES), zero2) for _ in tables]
    bias_spec = pl.BlockSpec((1, NA_HEADS, NBR_PAIRS, GRID_W, LANES), lambda b, q: (layer, 0, 0, 0, 0))
    n_all = PAST_LEN + DEC_SEQ
    n_pad = DEC_SEQ + 2 * SWA_PAD
    scratch = [
        pltpu.VMEM((MLA_HEADS, n_all, MLA_NOPE + LANES), BF16),
        pltpu.VMEM((n_all, BRANCH_W), BF16),
        pltpu.VMEM((PAST_LEN, SWA_KV_W), BF16), pltpu.VMEM((PAST_LEN, SWA_KV_W), BF16),
        pltpu.VMEM((n_pad, SWA_KV_W), BF16), pltpu.VMEM((n_pad, SWA_KV_W), BF16),
        pltpu.VMEM((PAST_LEN, BRANCH_W), BF16), pltpu.VMEM((PAST_LEN, BRANCH_W), BF16),
        pltpu.VMEM((PAST_LEN, BRANCH_W), BF16), pltpu.VMEM((PAST_LEN, BRANCH_W), BF16),
        pltpu.VMEM((DEC_SEQ, BRANCH_W), BF16),
    ]
    return pl.pallas_call(
        functools.partial(_lat_attn_kernel, lam_init=lam_init),
        grid=(DEC_BATCH, nq),
        in_specs=[pl.BlockSpec((DEC_SEQ, P_WIDTH), batch_rows, pipeline_mode=once),
                  pl.BlockSpec((DEC_SEQ, LANES), batch_rows, pipeline_mode=once)]
        + cache_specs + table_specs + [bias_spec] + _small_specs(),
        out_specs=pl.BlockSpec((LAT_TQ, D_MODEL), lambda b, q: (b * nq + q, 0)),
        out_shape=jax.ShapeDtypeStruct((N_LAT_TOK, D_MODEL), BF16),
        scratch_shapes=scratch,
        compiler_params=pltpu.CompilerParams(
            dimension_semantics=("arbitrary", "arbitrary"), vmem_limit_bytes=VMEM_LIMIT_BIG),
        name="lat_attn",
    )(p, kr, *cache_args, *tables, bias, *small)


def _merge_kernel(h_ref, o_ref, wg_ref, wb_ref, *rest):
    n_side = (len(rest) - 1) // 2
    out_ref = rest[n_side]
    _side_cast(rest[:n_side], rest[n_side + 1:])
    acc = None
    for i in range(N_BRANCH):
        gate = jax.nn.sigmoid(_dot(h_ref[...], wg_ref[i]))
        term = gate * _dot(o_ref[:, i * BRANCH_W:(i + 1) * BRANCH_W], wb_ref[i])
        acc = term if acc is None else acc + term
    out_ref[...] = acc.astype(BF16)


def _merge_call(h, o, wg, wb, name, side_weights=()):
    n_tok = h.shape[0]
    tm, tn = 1024, 512
    grid = (n_tok // tm, D_MODEL // tn)
    side = []
    for w, layer in side_weights:
        lead = w.shape[1:-2]
        block = lead + (w.shape[-2] // grid[0], w.shape[-1] // grid[1])
        side.append((w, layer, block, lambda i, j, z=(0,) * len(lead): z + (i, j)))
    s_in, s_out, s_shape = _side_specs(side, 2)
    return pl.pallas_call(
        _merge_kernel,
        grid=grid,
        in_specs=[
            pl.BlockSpec((tm, D_MODEL), lambda i, j: (i, 0)),
            pl.BlockSpec((tm, D_MODEL), lambda i, j: (i, 0)),
            pl.BlockSpec((None, N_BRANCH, D_MODEL, tn), lambda i, j: (0, 0, 0, j)),
            pl.BlockSpec((None, N_BRANCH, BRANCH_W, tn), lambda i, j: (0, 0, 0, j)),
        ] + s_in,
        out_specs=[pl.BlockSpec((tm, tn), lambda i, j: (i, j))] + s_out,
        out_shape=[jax.ShapeDtypeStruct((n_tok, D_MODEL), BF16)] + s_shape,
        compiler_params=pltpu.CompilerParams(
            dimension_semantics=("arbitrary", "arbitrary"), vmem_limit_bytes=VMEM_LIMIT),
        name=name,
    )(h, o, wg, wb, *[w for w, _, _, _ in side])


def _outproj_kernel(mixed_ref, x_ref, mod_ref, g_ref, gffn_ref, w_ref, *rest):
    n_side = (len(rest) - 2) // 2
    xo_ref, h2_ref = rest[n_side:n_side + 2]
    _side_cast(rest[:n_side], rest[n_side + 2:])
    m = mod_ref[0]
    half = mixed_ref.shape[0] // 2
    for r in (slice(0, half), slice(half, 2 * half)):
        out = _dot(mixed_ref[r, :], w_ref[...])
        x = x_ref[r, :] + m[2:3] * (_rms(out) * g_ref[...])
        xo_ref[r, :] = x
        h2_ref[r, :] = (_rms(x) * gffn_ref[...] * (1.0 + m[4:5]) + m[3:4]).astype(BF16)


def _outproj_call(mixed, x, mod, g, g_ffn, w, seg, name, side_weights=()):
    n_tok = x.shape[0]
    tm = 512
    n_steps = n_tok // tm
    side = [(w_, layer, (w_.shape[1] // n_steps, w_.shape[2]), lambda i: (i, 0)) for w_, layer in side_weights]
    s_in, s_out, s_shape = _side_specs(side, 1)
    return pl.pallas_call(
        _outproj_kernel,
        grid=(n_steps,),
        in_specs=[
            pl.BlockSpec((tm, D_MODEL), lambda i: (i, 0)),
            pl.BlockSpec((tm, D_MODEL), lambda i: (i, 0)),
            pl.BlockSpec((1, 6, D_MODEL), lambda i: (seg(i, tm), 0, 0)),
            pl.BlockSpec((1, D_MODEL), lambda i: (0, 0)),
            pl.BlockSpec((1, D_MODEL), lambda i: (0, 0)),
            pl.BlockSpec((None, D_MODEL, D_MODEL), lambda i: (0, 0, 0)),
        ] + s_in,
        out_specs=[pl.BlockSpec((tm, D_MODEL), lambda i: (i, 0)), pl.BlockSpec((tm, D_MODEL), lambda i: (i, 0))]
        + s_out,
        out_shape=[jax.ShapeDtypeStruct((n_tok, D_MODEL), F32), jax.ShapeDtypeStruct((n_tok, D_MODEL), BF16)]
        + s_shape,
        compiler_params=pltpu.CompilerParams(
            dimension_semantics=("arbitrary",), vmem_limit_bytes=VMEM_LIMIT),
        name=name,
    )(mixed, x, mod, g, g_ffn, w, *[w_ for w_, _, _, _ in side])


def _ffn_kernel(x_ref, h2_ref, mod_ref, gpost_ref, wg_ref, wu_ref, wd_ref, xo_ref, acc_s, *, th, th_last):
    j = pl.program_id(1)
    last = pl.num_programs(1) - 1

    @pl.when(j == 0)
    def _():
        acc_s[...] = jnp.zeros_like(acc_s)

    def step(width):
        a = _dot(h2_ref[...], wg_ref[:, :width])
        b = _dot(h2_ref[...], wu_ref[:, :width])
        t = (a * jax.nn.sigmoid(a) * b).astype(BF16)
        acc_s[...] += _dot(t, wd_ref[:width, :])

    @pl.when(j < last)
    def _():
        step(th)

    @pl.when(j == last)
    def _():
        step(th_last)
        m = mod_ref[0]
        xo_ref[...] = x_ref[...] + m[5:6] * (_rms(acc_s[...]) * gpost_ref[...])


def _ffn_call(x, h2, mod, gpost, wg, wu, wd, seg, name):
    n_tok = x.shape[0]
    tm, th = 512, 1024
    n_h = pl.cdiv(FFN_HIDDEN, th)
    return pl.pallas_call(
        functools.partial(_ffn_kernel, th=th, th_last=FFN_HIDDEN - (n_h - 1) * th),
        grid=(n_tok // tm, n_h),
        in_specs=[
            pl.BlockSpec((tm, D_MODEL), lambda i, j: (i, 0)),
            pl.BlockSpec((tm, D_MODEL), lambda i, j: (i, 0)),
            pl.BlockSpec((1, 6, D_MODEL), lambda i, j: (seg(i, tm), 0, 0)),
            pl.BlockSpec((1, D_MODEL), lambda i, j: (0, 0)),
            pl.BlockSpec((None, D_MODEL, th), lambda i, j: (0, 0, j)),
            pl.BlockSpec((None, D_MODEL, th), lambda i, j: (0, 0, j)),
            pl.BlockSpec((None, th, D_MODEL), lambda i, j: (0, j, 0)),
        ],
        out_specs=pl.BlockSpec((tm, D_MODEL), lambda i, j: (i, 0)),
        out_shape=jax.ShapeDtypeStruct((n_tok, D_MODEL), F32),
        scratch_shapes=[pltpu.VMEM((tm, D_MODEL), F32)],
        compiler_params=pltpu.CompilerParams(
            dimension_semantics=("arbitrary", "arbitrary"), vmem_limit_bytes=VMEM_LIMIT_BIG),
        name=name,
    )(x, h2, mod, gpost, wg, wu, wd)


def _rope_tables():
    t = np.arange(DEC_SEQ)
    row = (t // GRID_W).astype(np.float32)
    col = (t % GRID_W).astype(np.float32)

    def half_angles(r):
        quarter = r // 4
        inv = (1.0 / (ROPE_BASE ** (np.arange(quarter, dtype=np.float32) / quarter))).astype(np.float32)
        return np.concatenate([row[:, None] * inv, col[:, None] * inv], axis=-1).astype(np.float32)

    a64 = half_angles(HEAD_DIM)
    cos128 = np.concatenate([np.cos(a64), np.cos(a64)], axis=1)
    sin128 = np.concatenate([-np.sin(a64), np.sin(a64)], axis=1)
    a32 = half_angles(MLA_ROPE)
    c, s, z = np.cos(a32), np.sin(a32), np.zeros_like(a32)
    cos64 = np.concatenate([c, c, c, c], axis=1)
    sina64 = np.concatenate([-s, z, -s, z], axis=1)
    sinb64 = np.concatenate([z, s, z, s], axis=1)
    return tuple(jnp.asarray(a, F32) for a in (cos128, sin128, cos64, sina64, sinb64))


def _ctx_seg(i, tm):
    del tm
    return i * 0


def _lat_seg(i, tm):
    return 1 + (i * tm) // DEC_SEQ


def kernel(x_prompt, x_sample, cache_mla_ckv, cache_mla_krope, cache_swa_k, cache_swa_v, cache_na_k, cache_na_v, cache_diff_k, cache_diff_v, c, c_ctx, w_ada, b_ada, mix_pre_g, mix_post_g, ffn_pre_g, ffn_post_g, w_in, mla_q_norm_g, mla_kv_norm_g, w_mla_uq, w_mla_ukv, swa_sink, na_rpb, diff_lq1, diff_lk1, diff_lq2, diff_lk2, diff_subln_g, w_mix_gate, w_branch, w_out, w_ffn_gate, w_ffn_up, w_ffn_down):
    cvec = jnp.concatenate([c_ctx[None, :], c, jnp.zeros((N_MOD_ROWS - 1 - DEC_BATCH, D_MODEL), F32)], axis=0)
    mod_all = _ada_call(cvec, w_ada, b_ada).reshape(DEPTH, N_MOD_ROWS, 6, D_MODEL)
    bias_all = _nbr_bias_call(na_rpb)
    tables = _rope_tables()
    caches = (cache_mla_ckv, cache_mla_krope, cache_swa_k, cache_swa_v, cache_na_k, cache_na_v,
              cache_diff_k, cache_diff_v)

    w_main, w_kr = _cast_w_in_call(w_in)
    wg = _cast_call(w_mix_gate, 512, "cast_w_gate")
    wb = _cast_call(w_branch, 512, "cast_w_branch")
    wo = _cast_call(w_out, 512, "cast_w_out")
    wfg = _cast_call(w_ffn_gate, 256, "cast_w_ffn_gate")
    wfu = _cast_call(w_ffn_up, 256, "cast_w_ffn_up")
    wfd = _cast_call(w_ffn_down, 512, "cast_w_ffn_down")

    y = x_prompt.reshape(N_CTX_TOK, D_MODEL)
    z = x_sample.reshape(N_LAT_TOK, D_MODEL)
    new_caches = ()
    for l in range(DEPTH):
        lam_init = _lambda_init(l)
        mod = mod_all[l]
        uq = w_mla_uq[l].reshape(MLA_Q_RANK, MLA_HEADS, MLA_NOPE + MLA_ROPE)
        uq_rope = jnp.pad(uq[:, :, MLA_NOPE:], ((0, 0), (0, 0), (0, LANES - MLA_ROPE)))
        w_uq = jnp.concatenate([uq[:, :, :MLA_NOPE].reshape(MLA_Q_RANK, MLA_QN_W),
                                uq_rope.reshape(MLA_Q_RANK, MLA_QN_W)], axis=1).astype(BF16)
        small = (mla_q_norm_g[l][None, :], w_uq, mla_kv_norm_g[l][None, :], w_mla_ukv[l].astype(BF16),
                 swa_sink[l], diff_lq1[l][None, :], diff_lk1[l][None, :], diff_lq2[l][None, :],
                 diff_lk2[l][None, :], diff_subln_g[l][None, :])
        g_pre, g_post = mix_pre_g[l][None, :], mix_post_g[l][None, :]
        f_pre, f_post = ffn_pre_g[l][None, :], ffn_post_g[l][None, :]

        nxt = l + 1 if l + 1 < DEPTH else None
        side = (lambda *ws: ()) if nxt is None else (lambda *ws: tuple((w, nxt) for w in ws))

        h, p, kr = _inproj_call(y, mod, g_pre, w_main, w_kr, l, _ctx_seg, F32, "ctx_inproj")
        o, *new_caches = _ctx_attn_call(p, kr, small, new_caches, l, lam_init)
        mixed, *nxt_a = _merge_call(h, o, wg, wb, "ctx_merge", side(w_mix_gate, w_branch))
        y, h2, *nxt_b = _outproj_call(mixed, y, mod, g_post, f_pre, wo, _ctx_seg, "ctx_outproj",
                                      side(w_out, w_ffn_gate))
        y = _ffn_call(y, h2, mod, f_post, wfg, wfu, wfd, _ctx_seg, "ctx_ffn")

        h, p, kr = _inproj_call(z, mod, g_pre, w_main, w_kr, l, _lat_seg, BF16, "lat_inproj")
        o = _lat_attn_call(p, kr, caches, tables, bias_all, small, l, lam_init)
        mixed, *nxt_c = _merge_call(h, o, wg, wb, "lat_merge", side(w_ffn_down))
        z, h2, *nxt_d = _outproj_call(mixed, z, mod, g_post, f_pre, wo, _lat_seg, "lat_outproj", side(w_ffn_up))
        z = _ffn_call(z, h2, mod, f_post, wfg, wfu, wfd, _lat_seg, "lat_ffn")
        if nxt is not None:
            (wg, wb), (wo, wfg), (wfd,), (wfu,) = nxt_a, nxt_b, nxt_c, nxt_d

    new_caches = [a.reshape((BATCH, DEPTH, SEQ) + t) for a, t in zip(new_caches, CACHE_TAILS)]
    return (y.reshape(BATCH, SEQ, D_MODEL), z.reshape(DEC_BATCH, DEC_SEQ, D_MODEL), *new_caches)
```

```python
import functools
import math

import numpy as np
import jax
import jax.numpy as jnp
from jax import lax
from jax.experimental import pallas as pl
from jax.experimental.pallas import tpu as pltpu

D_MODEL = 2048
BATCH = 32
SEQ = 256
DEPTH = 2
DEC_BATCH = 8
DEC_SEQ = 1024
PAST_LEN = 512
GRID_W = 64
HEAD_DIM = 128
ROPE_BASE = 10000.0
EPS = 1e-6
NEG_INF = -1e30
LOG2E = math.log2(math.e)

MLA_HEADS = 4
MLA_Q_RANK = 384
MLA_KV_RANK = 128
MLA_NOPE = 128
MLA_ROPE = 64
MLA_V = 128
SWA_HEADS = 4
SWA_KV_HEADS = 2
SWA_WINDOW = 128
NA_HEADS = 4
NA_ROWS = 8
NA_COLS = 16
DIFF_HEADS = 4
DIFF_QK = 64
DIFF_V = 128
N_BRANCH = 4
BRANCH_W = 512
FFN_HIDDEN = 5632

N_CTX_TOK = BATCH * SEQ
N_LAT_TOK = DEC_BATCH * DEC_SEQ
GRID_ROWS = DEC_SEQ // GRID_W

Q_DOWN0, CKV0, QB0, KB0, VB0, QC0, KC0, VC0, QD0, KD0, VD0 = (
    0, 384, 512, 1024, 1280, 1536, 2048, 2560, 3072, 3584, 4096)
KROPE_SRC0 = MLA_Q_RANK + MLA_KV_RANK
P_WIDTH = 4608
MLA_QN_W = MLA_HEADS * MLA_NOPE
MLA_KV_W = MLA_NOPE + MLA_V
SWA_KV_W = SWA_KV_HEADS * HEAD_DIM
N_MOD_ROWS = 16

CACHE_TAILS = ((MLA_KV_RANK,), (MLA_ROPE,), (SWA_KV_HEADS, HEAD_DIM), (SWA_KV_HEADS, HEAD_DIM),
               (NA_HEADS, HEAD_DIM), (NA_HEADS, HEAD_DIM), (DIFF_HEADS, 2 * DIFF_QK), (DIFF_HEADS, DIFF_V))

VMEM_LIMIT = 56 * 1024 * 1024
VMEM_LIMIT_BIG = 62 * 1024 * 1024
LANES = 128

F32 = jnp.float32
BF16 = jnp.bfloat16


def _lambda_init(l):
    return 0.8 - 0.6 * math.exp(-0.3 * l)


def _rms(x):
    return x * lax.rsqrt(jnp.mean(x * x, axis=-1, keepdims=True) + EPS)


def _head_cols(h, base=0):
    return slice(base + h * HEAD_DIM, base + (h + 1) * HEAD_DIM)


def _dot(a, b):
    return jnp.dot(a, b, preferred_element_type=F32)


def _dot_nt(a, b):
    return lax.dot_general(a, b, (((1,), (1,)), ((), ())), preferred_element_type=F32)


def _attend(pieces, sink=None):
    m = None
    for s, _ in pieces:
        mi = jnp.max(s, axis=-1, keepdims=True)
        m = mi if m is None else jnp.maximum(m, mi)
    if sink is not None:
        sink = sink * LOG2E
        m = jnp.maximum(m, sink)
    den = None
    num = None
    for s, v in pieces:
        e = jnp.exp2(s - m)
        d = jnp.sum(e, axis=-1, keepdims=True)
        n = _dot(e.astype(BF16), v)
        den = d if den is None else den + d
        num = n if num is None else num + n
    if sink is not None:
        den = den + jnp.exp2(sink - m)
    return num / den


def _diff_attend(pieces1, pieces2, values, lam):
    def exps(pieces):
        m = None
        for s in pieces:
            mi = jnp.max(s, axis=-1, keepdims=True)
            m = mi if m is None else jnp.maximum(m, mi)
        es = [jnp.exp2(s - m) for s in pieces]
        den = None
        for e in es:
            d = jnp.sum(e, axis=-1, keepdims=True)
            den = d if den is None else den + d
        return es, den

    es1, den1 = exps(pieces1)
    es2, den2 = exps(pieces2)
    ratio = lam * den1 / den2
    out = None
    for e1, e2, v in zip(es1, es2, values):
        n = _dot((e1 - e2 * ratio).astype(BF16), v)
        out = n if out is None else out + n
    return out / den1


def _diff_lambda(lq1, lk1, lq2, lk2, lam_init):
    a = jnp.sum(lq1[...] * lk1[...], axis=-1, keepdims=True)
    b = jnp.sum(lq2[...] * lk2[...], axis=-1, keepdims=True)
    return jnp.exp(a) - jnp.exp(b) + lam_init


def _ada_kernel(c_ref, w_ref, b_ref, o_ref):
    c = c_ref[...]
    s = (c * jax.nn.sigmoid(c)).astype(BF16)
    o_ref[0] = _dot(s, w_ref[0].astype(BF16)) + b_ref[0]


def _ada_call(cvec, w_ada, b_ada):
    tn = 1024
    n = 6 * D_MODEL
    return pl.pallas_call(
        _ada_kernel,
        grid=(DEPTH, n // tn),
        in_specs=[
            pl.BlockSpec((N_MOD_ROWS, D_MODEL), lambda l, j: (0, 0)),
            pl.BlockSpec((1, D_MODEL, tn), lambda l, j: (l, 0, j)),
            pl.BlockSpec((1, 1, tn), lambda l, j: (l, 0, j)),
        ],
        out_specs=pl.BlockSpec((1, N_MOD_ROWS, tn), lambda l, j: (l, 0, j)),
        out_shape=jax.ShapeDtypeStruct((DEPTH, N_MOD_ROWS, n), F32),
        compiler_params=pltpu.CompilerParams(
            dimension_semantics=("arbitrary", "arbitrary"), vmem_limit_bytes=VMEM_LIMIT),
        name="ada_mod",
    )(cvec, w_ada, b_ada.reshape(DEPTH, 1, n))


NBR_PAIRS = 2 * NA_ROWS


def _nbr_bias_kernel(rpb_ref, o_ref):
    l = pl.program_id(0)
    h = pl.program_id(1)
    n_dr = 2 * NA_ROWS - 1
    n_dc = 2 * NA_COLS - 1
    base = (l * NA_HEADS + h) * (n_dr * n_dc)
    wq = lax.broadcasted_iota(jnp.int32, (GRID_W, LANES), 0)
    lane = lax.broadcasted_iota(jnp.int32, (GRID_W, LANES), 1)
    upper = lane >= GRID_W
    wk = jnp.where(upper, lane - GRID_W, lane)
    delta = wk - wq + (NA_COLS - 1)
    cs = jnp.clip(wq - NA_COLS // 2, 0, GRID_W - NA_COLS)
    valid = (wk >= cs) & (wk < cs + NA_COLS)
    for e in range(NBR_PAIRS):
        dr_lo, dr_hi = max(e - 1, 0), min(e, n_dr - 1)
        acc = jnp.full((GRID_W, LANES), NEG_INF, F32)
        for d in range(n_dc):
            lo = rpb_ref[base + dr_lo * n_dc + d]
            hi = rpb_ref[base + dr_hi * n_dc + d]
            acc = jnp.where(delta == d, jnp.where(upper, hi, lo), acc)
        o_ref[0, 0, e] = jnp.where(valid, acc * LOG2E, NEG_INF)


def _nbr_bias_call(na_rpb):
    n_pairs = NBR_PAIRS
    return pl.pallas_call(
        _nbr_bias_kernel,
        grid=(DEPTH, NA_HEADS),
        in_specs=[pl.BlockSpec(memory_space=pltpu.SMEM)],
        out_specs=pl.BlockSpec((1, 1, n_pairs, GRID_W, LANES), lambda l, h: (l, h, 0, 0, 0)),
        out_shape=jax.ShapeDtypeStruct((DEPTH, NA_HEADS, n_pairs, GRID_W, LANES), F32),
        compiler_params=pltpu.CompilerParams(dimension_semantics=("arbitrary", "arbitrary")),
        name="nbr_bias",
    )(na_rpb.reshape(-1))


def _inproj_kernel(x_ref, mod_ref, g_ref, w_ref, wkr_ref, h_ref, p_ref, kr_ref):
    m = mod_ref[0]
    half = x_ref.shape[0] // 2
    for r in (slice(0, half), slice(half, 2 * half)):
        h = (_rms(x_ref[r, :]) * g_ref[...] * (1.0 + m[1:2]) + m[0:1]).astype(BF16)
        h_ref[r, :] = h
        kr_ref[r, :] = _dot_nt(h, wkr_ref[...])
        p_ref[r, :] = _dot_nt(h, w_ref[...]).astype(p_ref.dtype)


def _inproj_call(x, mod, g, w_main, w_kr, layer, seg, p_dtype, name):
    n_tok = x.shape[0]
    tm = 512
    return pl.pallas_call(
        _inproj_kernel,
        grid=(n_tok // tm,),
        in_specs=[
            pl.BlockSpec((tm, D_MODEL), lambda i: (i, 0)),
            pl.BlockSpec((1, 6, D_MODEL), lambda i: (seg(i, tm), 0, 0)),
            pl.BlockSpec((1, D_MODEL), lambda i: (0, 0)),
            pl.BlockSpec((None, P_WIDTH, D_MODEL), lambda i: (layer, 0, 0)),
            pl.BlockSpec((None, LANES, D_MODEL), lambda i: (layer, 0, 0)),
        ],
        out_specs=[
            pl.BlockSpec((tm, D_MODEL), lambda i: (i, 0)),
            pl.BlockSpec((tm, P_WIDTH), lambda i: (i, 0)),
            pl.BlockSpec((tm, LANES), lambda i: (i, 0)),
        ],
        out_shape=[
            jax.ShapeDtypeStruct((n_tok, D_MODEL), BF16),
            jax.ShapeDtypeStruct((n_tok, P_WIDTH), p_dtype),
            jax.ShapeDtypeStruct((n_tok, LANES), F32),
        ],
        compiler_params=pltpu.CompilerParams(
            dimension_semantics=("arbitrary",), vmem_limit_bytes=VMEM_LIMIT_BIG),
        name=name,
    )(x, mod, g, w_main, w_kr)


def _cast_kernel(x_ref, o_ref):
    o_ref[...] = x_ref[...].astype(BF16)


def _cast_call(w, block_rows, name):
    w2 = w.reshape(-1, w.shape[-1])
    rows, cols = w2.shape[0] // w.shape[0], w2.shape[1]
    out = pl.pallas_call(
        _cast_kernel,
        grid=(rows // block_rows,),
        in_specs=[pl.BlockSpec((block_rows, cols), lambda i: (i, 0))],
        out_specs=pl.BlockSpec((block_rows, cols), lambda i: (i, 0)),
        out_shape=jax.ShapeDtypeStruct((rows, cols), BF16),
        compiler_params=pltpu.CompilerParams(dimension_semantics=("arbitrary",), vmem_limit_bytes=VMEM_LIMIT),
        name=name,
    )(w2)
    return out.reshape((1,) + w.shape[1:])


def _side_specs(side, n_grid):
    in_specs, out_specs, out_shape = [], [], []
    for w, layer, block, index in side:
        if n_grid == 1:
            in_specs.append(pl.BlockSpec((None,) + block, lambda i, f=index, l=layer: (l,) + f(i)))
            out_specs.append(pl.BlockSpec((None,) + block, lambda i, f=index: (0,) + f(i)))
        else:
            in_specs.append(pl.BlockSpec((None,) + block, lambda i, j, f=index, l=layer: (l,) + f(i, j)))
            out_specs.append(pl.BlockSpec((None,) + block, lambda i, j, f=index: (0,) + f(i, j)))
        out_shape.append(jax.ShapeDtypeStruct((1,) + w.shape[1:], BF16))
    return in_specs, out_specs, out_shape


def _side_cast(srcs, dsts):
    for src, dst in zip(srcs, dsts):
        dst[...] = src[...].astype(BF16)


def _cast_w_in_kernel(x_ref, xkr_ref, main_ref, kr_ref):
    main_ref[...] = x_ref[0].astype(BF16)

    @pl.when(pl.program_id(1) == 0)
    def _():
        kr_ref[:MLA_ROPE, :] = xkr_ref[...].astype(BF16)
        kr_ref[MLA_ROPE:, :] = jnp.zeros((LANES - MLA_ROPE, D_MODEL), BF16)


def _cast_w_in_call(w_in):
    w_t = jnp.swapaxes(w_in, 1, 2)
    block_rows = 512
    src_row = lambda l, j: (l, pl.multiple_of(j * block_rows + jnp.where(j * block_rows >= KROPE_SRC0, MLA_ROPE, 0), MLA_ROPE), 0)
    return pl.pallas_call(
        _cast_w_in_kernel,
        grid=(DEPTH, P_WIDTH // block_rows),
        in_specs=[pl.BlockSpec((pl.Element(1), pl.Element(block_rows), pl.Element(D_MODEL)), src_row),
                  pl.BlockSpec((None, MLA_ROPE, D_MODEL), lambda l, j: (l, KROPE_SRC0 // MLA_ROPE, 0))],
        out_specs=[pl.BlockSpec((None, block_rows, D_MODEL), lambda l, j: (l, j, 0)),
                   pl.BlockSpec((None, LANES, D_MODEL), lambda l, j: (l, 0, 0))],
        out_shape=[jax.ShapeDtypeStruct((DEPTH, P_WIDTH, D_MODEL), BF16),
                   jax.ShapeDtypeStruct((DEPTH, LANES, D_MODEL), BF16)],
        compiler_params=pltpu.CompilerParams(
            dimension_semantics=("arbitrary", "arbitrary"), vmem_limit_bytes=VMEM_LIMIT),
        name="cast_w_in",
    )(w_t, w_t)


def _mla_queries(q_down, qn_g, w_uq):
    cq = _rms(q_down) * qn_g[...]
    return _dot(cq.astype(BF16), w_uq[...])


def _ctx_attn_kernel(*refs, lam_init, first):
    (p_ref, kr_ref, qn_g, w_uq, kvn_g, w_ukv, sink_ref, lq1, lk1, lq2, lk2, subln_g) = refs[:12]
    n_in = 12 + (0 if first else len(CACHE_TAILS))
    o_ref = refs[n_in]
    ckv_o, kr_o, kb_o, vb_o, kc_o, vc_o, kd_o, vd_o = refs[n_in + 1:]
    ls = 0
    if first:
        for dst in (ckv_o, kr_o, kb_o, vb_o, kc_o, vc_o, kd_o, vd_o):
            dst[0, 1:] = jnp.zeros(dst.shape[1:], F32)[1:]

    scale_a = LOG2E * (MLA_NOPE + MLA_ROPE) ** -0.5
    qa = _mla_queries(p_ref[:, Q_DOWN0:Q_DOWN0 + MLA_Q_RANK], qn_g, w_uq) * scale_a
    ckv = _rms(p_ref[:, CKV0:CKV0 + MLA_KV_RANK]) * kvn_g[...]
    ckv_o[0, ls] = ckv
    kr = kr_ref[...]
    kr_o[0, ls] = kr[:, :MLA_ROPE]
    kr_b = kr.astype(BF16)
    kv = _dot(ckv.astype(BF16), w_ukv[...])
    for h in range(MLA_HEADS):
        q = jnp.concatenate([qa[:, _head_cols(h)], qa[:, _head_cols(h, MLA_QN_W)]],
                            axis=1).astype(BF16)
        kv0 = h * MLA_KV_W
        k = jnp.concatenate([kv[:, kv0:kv0 + MLA_NOPE].astype(BF16), kr_b], axis=1)
        v = kv[:, kv0 + MLA_NOPE:kv0 + MLA_KV_W].astype(BF16)
        o = _attend([(_dot_nt(q, k), v)])
        o_ref[:, _head_cols(h)] = o.astype(BF16)

    scale = LOG2E * HEAD_DIM ** -0.5

    def head(col0, h):
        return p_ref[:, _head_cols(h, col0)]

    g_size = SWA_HEADS // SWA_KV_HEADS
    for g in range(SWA_KV_HEADS):
        k = head(KB0, g)
        v = head(VB0, g)
        kb_o[0, ls, pl.ds(g, SEQ, stride=SWA_KV_HEADS), :] = k
        vb_o[0, ls, pl.ds(g, SEQ, stride=SWA_KV_HEADS), :] = v
        k = k.astype(BF16)
        v = v.astype(BF16)
        for h in range(g * g_size, (g + 1) * g_size):
            q = (head(QB0, h) * scale).astype(BF16)
            o = _attend([(_dot_nt(q, k), v)], sink=sink_ref[h])
            o_ref[:, _head_cols(h, BRANCH_W)] = o.astype(BF16)

    for h in range(NA_HEADS):
        k = head(KC0, h)
        v = head(VC0, h)
        kc_o[0, ls, pl.ds(h, SEQ, stride=NA_HEADS), :] = k
        vc_o[0, ls, pl.ds(h, SEQ, stride=NA_HEADS), :] = v
        q = (head(QC0, h) * scale).astype(BF16)
        o = _attend([(_dot_nt(q, k.astype(BF16)), v.astype(BF16))])
        o_ref[:, _head_cols(h, 2 * BRANCH_W)] = o.astype(BF16)

    lam = _diff_lambda(lq1, lk1, lq2, lk2, lam_init)
    scale_d = LOG2E * DIFF_QK ** -0.5
    lo = lax.broadcasted_iota(jnp.int32, (SEQ, 2 * DIFF_QK), 1) < DIFF_QK
    for h in range(DIFF_HEADS):
        k = head(KD0, h)
        v = head(VD0, h)
        kd_o[0, ls, pl.ds(h, SEQ, stride=DIFF_HEADS), :] = k
        vd_o[0, ls, pl.ds(h, SEQ, stride=DIFF_HEADS), :] = v
        k = k.astype(BF16)
        v = v.astype(BF16)
        q = head(QD0, h) * scale_d
        o1 = _attend([(_dot_nt(jnp.where(lo, q, 0.0).astype(BF16), k), v)])
        o2 = _attend([(_dot_nt(jnp.where(lo, 0.0, q).astype(BF16), k), v)])
        o = _rms(o1 - lam * o2) * subln_g[...] * (1.0 - lam_init)
        o_ref[:, _head_cols(h, 3 * BRANCH_W)] = o.astype(BF16)


def _small_specs():
    zero2 = (lambda *a: (0, 0))
    return [
        pl.BlockSpec((1, MLA_Q_RANK), zero2),
        pl.BlockSpec((MLA_Q_RANK, 2 * MLA_QN_W), zero2),
        pl.BlockSpec((1, MLA_KV_RANK), zero2),
        pl.BlockSpec((MLA_KV_RANK, MLA_HEADS * MLA_KV_W), zero2),
        pl.BlockSpec(memory_space=pltpu.SMEM),
        pl.BlockSpec((1, DIFF_QK), zero2),
        pl.BlockSpec((1, DIFF_QK), zero2),
        pl.BlockSpec((1, DIFF_QK), zero2),
        pl.BlockSpec((1, DIFF_QK), zero2),
        pl.BlockSpec((1, DIFF_V), zero2),
    ]


def _ctx_attn_call(p, kr, small, prev, layer, lam_init):
    first = not prev
    row = lambda b: (b, 0)

    def flat(tail):
        return (SEQ * tail[0], tail[1]) if len(tail) == 2 else (SEQ,) + tail

    def cache_spec(tail):
        if first:
            return pl.BlockSpec((1, DEPTH) + flat(tail), lambda b: (b, 0, 0, 0))
        return pl.BlockSpec((1, 1) + flat(tail), lambda b: (b, layer, 0, 0))

    n_fixed = 12
    return pl.pallas_call(
        functools.partial(_ctx_attn_kernel, lam_init=lam_init, first=first),
        grid=(BATCH,),
        in_specs=[pl.BlockSpec((SEQ, P_WIDTH), row), pl.BlockSpec((SEQ, LANES), row)] + _small_specs()
        + [pl.BlockSpec(memory_space=pl.ANY) for _ in prev],
        out_specs=[pl.BlockSpec((SEQ, D_MODEL), row)] + [cache_spec(t) for t in CACHE_TAILS],
        out_shape=[jax.ShapeDtypeStruct((N_CTX_TOK, D_MODEL), BF16)]
        + [jax.ShapeDtypeStruct((BATCH, DEPTH) + flat(t), F32) for t in CACHE_TAILS],
        input_output_aliases={n_fixed + k: 1 + k for k in range(len(prev))},
        compiler_params=pltpu.CompilerParams(
            dimension_semantics=("arbitrary",), vmem_limit_bytes=VMEM_LIMIT),
        name="ctx_attn",
    )(p, kr, *small, *prev)


LAT_TQ = 256
SWA_PAD = SWA_WINDOW
NBR_WIN_ROWS = 12


def _rope_pair64(x, c_ref, sa_ref, sb_ref, rows):
    out = []
    for j in range(x.shape[1] // LANES):
        xj = x[:, j * LANES:(j + 1) * LANES]
        out.append(xj * c_ref[rows, :] + pltpu.roll(xj, 96, 1) * sa_ref[rows, :]
                   + pltpu.roll(xj, 32, 1) * sb_ref[rows, :])
    return jnp.concatenate(out, axis=1) if len(out) > 1 else out[0]


def _rope128(x, c_ref, s_ref, rows):
    out = []
    for j in range(x.shape[1] // LANES):
        xj = x[:, j * LANES:(j + 1) * LANES]
        out.append(xj * c_ref[rows, :] + pltpu.roll(xj, 64, 1) * s_ref[rows, :])
    return jnp.concatenate(out, axis=1) if len(out) > 1 else out[0]


def _lat_attn_kernel(p_ref, kr_ref, c_ckv, c_kr, c_kb, c_vb, c_kc, c_vc, c_kd, c_vd,
                     cos128, sin128, cos64, sina64, sinb64, bias_ref,
                     qn_g, w_uq, kvn_g, w_ukv, sink_ref, lq1, lk1, lq2, lk2, subln_g,
                     o_ref,
                     ka_s, va_s, kbc_s, vbc_s, kbp_s, vbp_s, kcc_s, vcc_s, kdc_s, vdc_s, kdo_s,
                     *, lam_init):
    def cached_head(ref, h, n_heads):
        return ref[pl.ds(h, PAST_LEN, stride=n_heads), :].astype(BF16)

    qi = pl.program_id(1)
    all_rows = slice(None)
    n_own = DEC_SEQ

    @pl.when(qi == 0)
    def _prologue():
        ckv_own = (_rms(p_ref[:, CKV0:CKV0 + MLA_KV_RANK].astype(F32)) * kvn_g[...]).astype(BF16)
        ckv_all = jnp.concatenate([c_ckv[...].astype(BF16), ckv_own], axis=0)
        kr_ctx = jnp.concatenate([c_kr[...], jnp.zeros((PAST_LEN, LANES - MLA_ROPE), F32)], axis=1)
        kr_own = _rope_pair64(kr_ref[...], cos64, sina64, sinb64, all_rows)
        kr_all = jnp.concatenate([kr_ctx, kr_own], axis=0).astype(BF16)
        for h in range(MLA_HEADS):
            kvh = _dot(ckv_all, w_ukv[:, h * MLA_KV_W:(h + 1) * MLA_KV_W])
            ka_s[h, :, 0:MLA_NOPE] = kvh[:, 0:MLA_NOPE].astype(BF16)
            ka_s[h, :, MLA_NOPE:MLA_NOPE + LANES] = kr_all
            va_s[:, _head_cols(h)] = kvh[:, MLA_NOPE:MLA_KV_W].astype(BF16)
        for g in range(SWA_KV_HEADS):
            kbc_s[:, _head_cols(g)] = cached_head(c_kb, g, SWA_KV_HEADS)
            vbc_s[:, _head_cols(g)] = cached_head(c_vb, g, SWA_KV_HEADS)
        zpad = jnp.zeros((SWA_PAD, SWA_KV_W), BF16)
        kbp_s[0:SWA_PAD, :] = zpad
        kbp_s[SWA_PAD + n_own:, :] = zpad
        vbp_s[0:SWA_PAD, :] = zpad
        vbp_s[SWA_PAD + n_own:, :] = zpad
        kbp_s[SWA_PAD:SWA_PAD + n_own, :] = _rope128(
            p_ref[:, KB0:KB0 + SWA_KV_W].astype(F32), cos128, sin128, all_rows).astype(BF16)
        vbp_s[SWA_PAD:SWA_PAD + n_own, :] = p_ref[:, VB0:VB0 + SWA_KV_W]
        for h in range(NA_HEADS):
            cols = _head_cols(h)
            kcc_s[:, cols] = cached_head(c_kc, h, NA_HEADS)
            vcc_s[:, cols] = cached_head(c_vc, h, NA_HEADS)
            kdc_s[:, cols] = cached_head(c_kd, h, DIFF_HEADS)
            vdc_s[:, cols] = cached_head(c_vd, h, DIFF_HEADS)
        kdo_s[...] = _rope_pair64(
            p_ref[:, KD0:KD0 + BRANCH_W].astype(F32), cos64, sina64, sinb64, all_rows).astype(BF16)

    q0 = pl.multiple_of(qi * LAT_TQ, LAT_TQ)
    rows = pl.ds(q0, LAT_TQ)

    scale_a = LOG2E * (MLA_NOPE + MLA_ROPE) ** -0.5
    qa = _mla_queries(p_ref[rows, Q_DOWN0:Q_DOWN0 + MLA_Q_RANK].astype(F32), qn_g, w_uq) * scale_a
    qa_r = _rope_pair64(qa[:, MLA_QN_W:2 * MLA_QN_W], cos64, sina64, sinb64, rows)
    for h in range(MLA_HEADS):
        q = jnp.concatenate([qa[:, _head_cols(h)], qa_r[:, _head_cols(h)]],
                            axis=1).astype(BF16)
        o = _attend([(_dot_nt(q, ka_s[h]), va_s[:, _head_cols(h)])])
        o_ref[:, _head_cols(h)] = o.astype(BF16)

    scale = LOG2E * HEAD_DIM ** -0.5
    qb = (_rope128(p_ref[rows, QB0:QB0 + BRANCH_W].astype(F32), cos128, sin128, rows) * scale).astype(BF16)
    n_loc = LAT_TQ + 2 * SWA_PAD
    qpos = lax.broadcasted_iota(jnp.int32, (LAT_TQ, n_loc), 0)
    kpos = lax.broadcasted_iota(jnp.int32, (LAT_TQ, n_loc), 1) - SWA_PAD
    kabs = kpos + q0
    valid = (jnp.abs(qpos - kpos) <= SWA_WINDOW) & (kabs >= 0) & (kabs < n_own)
    loc_rows = pl.ds(q0, n_loc)
    g_size = SWA_HEADS // SWA_KV_HEADS
    for h in range(SWA_HEADS):
        g = h // g_size
        q = qb[:, _head_cols(h)]
        s_ctx = _dot_nt(q, kbc_s[:, _head_cols(g)])
        s_loc = jnp.where(valid, _dot_nt(q, kbp_s[loc_rows, _head_cols(g)]), NEG_INF)
        o = _attend([(s_ctx, vbc_s[:, _head_cols(g)]),
                     (s_loc, vbp_s[loc_rows, _head_cols(g)])], sink=sink_ref[h])
        o_ref[:, _head_cols(h, BRANCH_W)] = o.astype(BF16)

    q_rows = LAT_TQ // GRID_W
    ws = jnp.clip(qi * q_rows - NA_ROWS // 2, 0, GRID_ROWS - NBR_WIN_ROWS)
    ws = (ws // 2) * 2
    krows = pl.ds(pl.multiple_of(ws * GRID_W, 2 * GRID_W), NBR_WIN_ROWS * GRID_W)
    lower = lax.broadcasted_iota(jnp.int32, (GRID_W, LANES), 1) < GRID_W
    for h in range(NA_HEADS):
        cols = _head_cols(h)
        bias_rows = []
        for t in range(q_rows):
            r = qi * q_rows + t
            off = jnp.clip(r - NA_ROWS // 2, 0, GRID_ROWS - NA_ROWS) - r + (NA_ROWS - 1)
            blocks = []
            for j in range(NBR_WIN_ROWS // 2):
                dr = ws + 2 * j - r + (NA_ROWS - 1)
                ok_lo = ((dr >= off) & (dr < off + NA_ROWS)).astype(jnp.int32)
                ok_hi = ((dr + 1 >= off) & (dr + 1 < off + NA_ROWS)).astype(jnp.int32)
                ok = jnp.where(lower, ok_lo, ok_hi) > 0
                blk = bias_ref[0, h, jnp.clip(dr + 1, 0, NBR_PAIRS - 1)]
                blocks.append(jnp.where(ok, blk, NEG_INF))
            bias_rows.append(jnp.concatenate(blocks, axis=1))
        bias = jnp.concatenate(bias_rows, axis=0)
        q = (p_ref[rows, _head_cols(h, QC0)].astype(F32) * scale).astype(BF16)
        s_loc = _dot_nt(q, p_ref[krows, _head_cols(h, KC0)]) + bias
        s_ctx = _dot_nt(q, kcc_s[:, cols])
        o = _attend([(s_ctx, vcc_s[:, cols]), (s_loc, p_ref[krows, _head_cols(h, VC0)])])
        o_ref[:, _head_cols(h, 2 * BRANCH_W)] = o.astype(BF16)

    scale_d = LOG2E * DIFF_QK ** -0.5
    qd = _rope_pair64(p_ref[rows, QD0:QD0 + BRANCH_W].astype(F32), cos64, sina64, sinb64, rows) * scale_d
    lam = _diff_lambda(lq1, lk1, lq2, lk2, lam_init)
    lo = lax.broadcasted_iota(jnp.int32, (LAT_TQ, 2 * DIFF_QK), 1) < DIFF_QK
    for h in range(DIFF_HEADS):
        cols = _head_cols(h)
        q = qd[:, cols]
        v_own = p_ref[:, _head_cols(h, VD0)]
        q1 = jnp.where(lo, q, 0.0).astype(BF16)
        q2 = jnp.where(lo, 0.0, q).astype(BF16)
        o = _diff_attend([_dot_nt(q1, kdc_s[:, cols]), _dot_nt(q1, kdo_s[:, cols])],
                         [_dot_nt(q2, kdc_s[:, cols]), _dot_nt(q2, kdo_s[:, cols])],
                         [vdc_s[:, cols], v_own], lam)
        o = _rms(o) * subln_g[...] * (1.0 - lam_init)
        o_ref[:, _head_cols(h, 3 * BRANCH_W)] = o.astype(BF16)


def _lat_attn_call(p, kr, caches, tables, bias, small, layer, lam_init):
    nq = DEC_SEQ // LAT_TQ
    batch_rows = lambda b, q: (b, 0)
    zero2 = lambda b, q: (0, 0)
    once = pl.Buffered(1)
    cache_args = [a.reshape(a.shape[:2] + (-1, a.shape[-1])) for a in caches]
    cache_specs = [pl.BlockSpec((None, None) + a.shape[2:], lambda b, q: (b, layer, 0, 0)) for a in cache_args]
    table_specs = [pl.BlockSpec((DEC_SEQ, LANES), zero2) for _ in tables]
    bias_spec = pl.BlockSpec((1, NA_HEADS, NBR_PAIRS, GRID_W, LANES), lambda b, q: (layer, 0, 0, 0, 0))
    n_all = PAST_LEN + DEC_SEQ
    n_pad = DEC_SEQ + 2 * SWA_PAD
    scratch = [
        pltpu.VMEM((MLA_HEADS, n_all, MLA_NOPE + LANES), BF16),
        pltpu.VMEM((n_all, BRANCH_W), BF16),
        pltpu.VMEM((PAST_LEN, SWA_KV_W), BF16), pltpu.VMEM((PAST_LEN, SWA_KV_W), BF16),
        pltpu.VMEM((n_pad, SWA_KV_W), BF16), pltpu.VMEM((n_pad, SWA_KV_W), BF16),
        pltpu.VMEM((PAST_LEN, BRANCH_W), BF16), pltpu.VMEM((PAST_LEN, BRANCH_W), BF16),
        pltpu.VMEM((PAST_LEN, BRANCH_W), BF16), pltpu.VMEM((PAST_LEN, BRANCH_W), BF16),
        pltpu.VMEM((DEC_SEQ, BRANCH_W), BF16),
    ]
    return pl.pallas_call(
        functools.partial(_lat_attn_kernel, lam_init=lam_init),
        grid=(DEC_BATCH, nq),
        in_specs=[pl.BlockSpec((DEC_SEQ, P_WIDTH), batch_rows, pipeline_mode=once),
                  pl.BlockSpec((DEC_SEQ, LANES), batch_rows, pipeline_mode=once)]
        + cache_specs + table_specs + [bias_spec] + _small_specs(),
        out_specs=pl.BlockSpec((LAT_TQ, D_MODEL), lambda b, q: (b * nq + q, 0)),
        out_shape=jax.ShapeDtypeStruct((N_LAT_TOK, D_MODEL), BF16),
        scratch_shapes=scratch,
        compiler_params=pltpu.CompilerParams(
            dimension_semantics=("arbitrary", "arbitrary"), vmem_limit_bytes=VMEM_LIMIT_BIG),
        name="lat_attn",
    )(p, kr, *cache_args, *tables, bias, *small)


def _merge_kernel(h_ref, o_ref, wg_ref, wb_ref, *rest):
    n_side = (len(rest) - 1) // 2
    out_ref = rest[n_side]
    _side_cast(rest[:n_side], rest[n_side + 1:])
    acc = None
    for i in range(N_BRANCH):
        gate = jax.nn.sigmoid(_dot(h_ref[...], wg_ref[i]))
        term = gate * _dot(o_ref[:, i * BRANCH_W:(i + 1) * BRANCH_W], wb_ref[i])
        acc = term if acc is None else acc + term
    out_ref[...] = acc.astype(BF16)


def _merge_call(h, o, wg, wb, name, side_weights=()):
    n_tok = h.shape[0]
    tm, tn = 1024, 512
    grid = (n_tok // tm, D_MODEL // tn)
    side = []
    for w, layer in side_weights:
        lead = w.shape[1:-2]
        block = lead + (w.shape[-2] // grid[0], w.shape[-1] // grid[1])
        side.append((w, layer, block, lambda i, j, z=(0,) * len(lead): z + (i, j)))
    s_in, s_out, s_shape = _side_specs(side, 2)
    return pl.pallas_call(
        _merge_kernel,
        grid=grid,
        in_specs=[
            pl.BlockSpec((tm, D_MODEL), lambda i, j: (i, 0)),
            pl.BlockSpec((tm, D_MODEL), lambda i, j: (i, 0)),
            pl.BlockSpec((None, N_BRANCH, D_MODEL, tn), lambda i, j: (0, 0, 0, j)),
            pl.BlockSpec((None, N_BRANCH, BRANCH_W, tn), lambda i, j: (0, 0, 0, j)),
        ] + s_in,
        out_specs=[pl.BlockSpec((tm, tn), lambda i, j: (i, j))] + s_out,
        out_shape=[jax.ShapeDtypeStruct((n_tok, D_MODEL), BF16)] + s_shape,
        compiler_params=pltpu.CompilerParams(
            dimension_semantics=("arbitrary", "arbitrary"), vmem_limit_bytes=VMEM_LIMIT),
        name=name,
    )(h, o, wg, wb, *[w for w, _, _, _ in side])


def _outproj_kernel(mixed_ref, x_ref, mod_ref, g_ref, gffn_ref, w_ref, *rest):
    n_side = (len(rest) - 2) // 2
    xo_ref, h2_ref = rest[n_side:n_side + 2]
    _side_cast(rest[:n_side], rest[n_side + 2:])
    m = mod_ref[0]
    half = mixed_ref.shape[0] // 2
    for r in (slice(0, half), slice(half, 2 * half)):
        out = _dot(mixed_ref[r, :], w_ref[...])
        x = x_ref[r, :] + m[2:3] * (_rms(out) * g_ref[...])
        xo_ref[r, :] = x
        h2_ref[r, :] = (_rms(x) * gffn_ref[...] * (1.0 + m[4:5]) + m[3:4]).astype(BF16)


def _outproj_call(mixed, x, mod, g, g_ffn, w, seg, name, side_weights=()):
    n_tok = x.shape[0]
    tm = 512
    n_steps = n_tok // tm
    side = [(w_, layer, (w_.shape[1] // n_steps, w_.shape[2]), lambda i: (i, 0)) for w_, layer in side_weights]
    s_in, s_out, s_shape = _side_specs(side, 1)
    return pl.pallas_call(
        _outproj_kernel,
        grid=(n_steps,),
        in_specs=[
            pl.BlockSpec((tm, D_MODEL), lambda i: (i, 0)),
            pl.BlockSpec((tm, D_MODEL), lambda i: (i, 0)),
            pl.BlockSpec((1, 6, D_MODEL), lambda i: (seg(i, tm), 0, 0)),
            pl.BlockSpec((1, D_MODEL), lambda i: (0, 0)),
            pl.BlockSpec((1, D_MODEL), lambda i: (0, 0)),
            pl.BlockSpec((None, D_MODEL, D_MODEL), lambda i: (0, 0, 0)),
        ] + s_in,
        out_specs=[pl.BlockSpec((tm, D_MODEL), lambda i: (i, 0)), pl.BlockSpec((tm, D_MODEL), lambda i: (i, 0))]
        + s_out,
        out_shape=[jax.ShapeDtypeStruct((n_tok, D_MODEL), F32), jax.ShapeDtypeStruct((n_tok, D_MODEL), BF16)]
        + s_shape,
        compiler_params=pltpu.CompilerParams(
            dimension_semantics=("arbitrary",), vmem_limit_bytes=VMEM_LIMIT),
        name=name,
    )(mixed, x, mod, g, g_ffn, w, *[w_ for w_, _, _, _ in side])


def _ffn_kernel(x_ref, h2_ref, mod_ref, gpost_ref, wg_ref, wu_ref, wd_ref, xo_ref):
    j = pl.program_id(1)

    @pl.when(j == 0)
    def _():
        xo_ref[...] = jnp.zeros_like(xo_ref)

    a = _dot(h2_ref[...], wg_ref[...])
    b = _dot(h2_ref[...], wu_ref[...])
    t = (a * jax.nn.sigmoid(a) * b).astype(BF16)
    xo_ref[...] += _dot(t, wd_ref[...])

    @pl.when(j == pl.num_programs(1) - 1)
    def _():
        m = mod_ref[0]
        xo_ref[...] = x_ref[...] + m[5:6] * (_rms(xo_ref[...]) * gpost_ref[...])


def _ffn_call(x, h2, mod, gpost, wg, wu, wd, seg, name):
    n_tok = x.shape[0]
    tm, th = 1024, 256
    return pl.pallas_call(
        _ffn_kernel,
        grid=(n_tok // tm, FFN_HIDDEN // th),
        in_specs=[
            pl.BlockSpec((tm, D_MODEL), lambda i, j: (i, 0)),
            pl.BlockSpec((tm, D_MODEL), lambda i, j: (i, 0)),
            pl.BlockSpec((1, 6, D_MODEL), lambda i, j: (seg(i, tm), 0, 0)),
            pl.BlockSpec((1, D_MODEL), lambda i, j: (0, 0)),
            pl.BlockSpec((None, D_MODEL, th), lambda i, j: (0, 0, j)),
            pl.BlockSpec((None, D_MODEL, th), lambda i, j: (0, 0, j)),
            pl.BlockSpec((None, th, D_MODEL), lambda i, j: (0, j, 0)),
        ],
        out_specs=pl.BlockSpec((tm, D_MODEL), lambda i, j: (i, 0)),
        out_shape=jax.ShapeDtypeStruct((n_tok, D_MODEL), F32),
        compiler_params=pltpu.CompilerParams(
            dimension_semantics=("arbitrary", "arbitrary"), vmem_limit_bytes=VMEM_LIMIT_BIG),
        name=name,
    )(x, h2, mod, gpost, wg, wu, wd)


def _rope_tables():
    t = np.arange(DEC_SEQ)
    row = (t // GRID_W).astype(np.float32)
    col = (t % GRID_W).astype(np.float32)

    def half_angles(r):
        quarter = r // 4
        inv = (1.0 / (ROPE_BASE ** (np.arange(quarter, dtype=np.float32) / quarter))).astype(np.float32)
        return np.concatenate([row[:, None] * inv, col[:, None] * inv], axis=-1).astype(np.float32)

    a64 = half_angles(HEAD_DIM)
    cos128 = np.concatenate([np.cos(a64), np.cos(a64)], axis=1)
    sin128 = np.concatenate([-np.sin(a64), np.sin(a64)], axis=1)
    a32 = half_angles(MLA_ROPE)
    c, s, z = np.cos(a32), np.sin(a32), np.zeros_like(a32)
    cos64 = np.concatenate([c, c, c, c], axis=1)
    sina64 = np.concatenate([-s, z, -s, z], axis=1)
    sinb64 = np.concatenate([z, s, z, s], axis=1)
    return tuple(jnp.asarray(a, F32) for a in (cos128, sin128, cos64, sina64, sinb64))


def _ctx_seg(i, tm):
    del tm
    return i * 0


def _lat_seg(i, tm):
    return 1 + (i * tm) // DEC_SEQ


def kernel(x_prompt, x_sample, cache_mla_ckv, cache_mla_krope, cache_swa_k, cache_swa_v, cache_na_k, cache_na_v, cache_diff_k, cache_diff_v, c, c_ctx, w_ada, b_ada, mix_pre_g, mix_post_g, ffn_pre_g, ffn_post_g, w_in, mla_q_norm_g, mla_kv_norm_g, w_mla_uq, w_mla_ukv, swa_sink, na_rpb, diff_lq1, diff_lk1, diff_lq2, diff_lk2, diff_subln_g, w_mix_gate, w_branch, w_out, w_ffn_gate, w_ffn_up, w_ffn_down):
    cvec = jnp.concatenate([c_ctx[None, :], c, jnp.zeros((N_MOD_ROWS - 1 - DEC_BATCH, D_MODEL), F32)], axis=0)
    mod_all = _ada_call(cvec, w_ada, b_ada).reshape(DEPTH, N_MOD_ROWS, 6, D_MODEL)
    bias_all = _nbr_bias_call(na_rpb)
    tables = _rope_tables()
    caches = (cache_mla_ckv, cache_mla_krope, cache_swa_k, cache_swa_v, cache_na_k, cache_na_v,
              cache_diff_k, cache_diff_v)

    w_main, w_kr = _cast_w_in_call(w_in)
    wg = _cast_call(w_mix_gate, 512, "cast_w_gate")
    wb = _cast_call(w_branch, 512, "cast_w_branch")
    wo = _cast_call(w_out, 512, "cast_w_out")
    wfg = _cast_call(w_ffn_gate, 256, "cast_w_ffn_gate")
    wfu = _cast_call(w_ffn_up, 256, "cast_w_ffn_up")
    wfd = _cast_call(w_ffn_down, 512, "cast_w_ffn_down")

    y = x_prompt.reshape(N_CTX_TOK, D_MODEL)
    z = x_sample.reshape(N_LAT_TOK, D_MODEL)
    new_caches = ()
    for l in range(DEPTH):
        lam_init = _lambda_init(l)
        mod = mod_all[l]
        uq = w_mla_uq[l].reshape(MLA_Q_RANK, MLA_HEADS, MLA_NOPE + MLA_ROPE)
        uq_rope = jnp.pad(uq[:, :, MLA_NOPE:], ((0, 0), (0, 0), (0, LANES - MLA_ROPE)))
        w_uq = jnp.concatenate([uq[:, :, :MLA_NOPE].reshape(MLA_Q_RANK, MLA_QN_W),
                                uq_rope.reshape(MLA_Q_RANK, MLA_QN_W)], axis=1).astype(BF16)
        small = (mla_q_norm_g[l][None, :], w_uq, mla_kv_norm_g[l][None, :], w_mla_ukv[l].astype(BF16),
                 swa_sink[l], diff_lq1[l][None, :], diff_lk1[l][None, :], diff_lq2[l][None, :],
                 diff_lk2[l][None, :], diff_subln_g[l][None, :])
        g_pre, g_post = mix_pre_g[l][None, :], mix_post_g[l][None, :]
        f_pre, f_post = ffn_pre_g[l][None, :], ffn_post_g[l][None, :]

        nxt = l + 1 if l + 1 < DEPTH else None
        side = (lambda *ws: ()) if nxt is None else (lambda *ws: tuple((w, nxt) for w in ws))

        h, p, kr = _inproj_call(y, mod, g_pre, w_main, w_kr, l, _ctx_seg, F32, "ctx_inproj")
        o, *new_caches = _ctx_attn_call(p, kr, small, new_caches, l, lam_init)
        mixed, *nxt_a = _merge_call(h, o, wg, wb, "ctx_merge", side(w_mix_gate, w_branch))
        y, h2, *nxt_b = _outproj_call(mixed, y, mod, g_post, f_pre, wo, _ctx_seg, "ctx_outproj",
                                      side(w_out, w_ffn_gate))
        y = _ffn_call(y, h2, mod, f_post, wfg, wfu, wfd, _ctx_seg, "ctx_ffn")

        h, p, kr = _inproj_call(z, mod, g_pre, w_main, w_kr, l, _lat_seg, BF16, "lat_inproj")
        o = _lat_attn_call(p, kr, caches, tables, bias_all, small, l, lam_init)
        mixed, *nxt_c = _merge_call(h, o, wg, wb, "lat_merge", side(w_ffn_down))
        z, h2, *nxt_d = _outproj_call(mixed, z, mod, g_post, f_pre, wo, _lat_seg, "lat_outproj", side(w_ffn_up))
        z = _ffn_call(z, h2, mod, f_post, wfg, wfu, wfd, _lat_seg, "lat_ffn")
        if nxt is not None:
            (wg, wb), (wo, wfg), (wfd,), (wfu,) = nxt_a, nxt_b, nxt_c, nxt_d

    new_caches = [a.reshape((BATCH, DEPTH, SEQ) + t) for a, t in zip(new_caches, CACHE_TAILS)]
    return (y.reshape(BATCH, SEQ, D_MODEL), z.reshape(DEC_BATCH, DEC_SEQ, D_MODEL), *new_caches)
```

```python
import functools
import math

import numpy as np
import jax
import jax.numpy as jnp
from jax import lax
from jax.experimental import pallas as pl
from jax.experimental.pallas import tpu as pltpu

D_MODEL = 2048
BATCH = 32
SEQ = 256
DEPTH = 2
DEC_BATCH = 8
DEC_SEQ = 1024
PAST_LEN = 512
GRID_W = 64
HEAD_DIM = 128
ROPE_BASE = 10000.0
EPS = 1e-6
NEG_INF = -1e30
LOG2E = math.log2(math.e)

MLA_HEADS = 4
MLA_Q_RANK = 384
MLA_KV_RANK = 128
MLA_NOPE = 128
MLA_ROPE = 64
MLA_V = 128
SWA_HEADS = 4
SWA_KV_HEADS = 2
SWA_WINDOW = 128
NA_HEADS = 4
NA_ROWS = 8
NA_COLS = 16
DIFF_HEADS = 4
DIFF_QK = 64
DIFF_V = 128
N_BRANCH = 4
BRANCH_W = 512
FFN_HIDDEN = 5632

N_CTX_TOK = BATCH * SEQ
N_LAT_TOK = DEC_BATCH * DEC_SEQ
GRID_ROWS = DEC_SEQ // GRID_W

Q_DOWN0, CKV0, QB0, KB0, VB0, QC0, KC0, VC0, QD0, KD0, VD0 = (
    0, 384, 512, 1024, 1280, 1536, 2048, 2560, 3072, 3584, 4096)
KROPE_SRC0 = MLA_Q_RANK + MLA_KV_RANK
P_WIDTH = 4608
MLA_QN_W = MLA_HEADS * MLA_NOPE
MLA_KV_W = MLA_NOPE + MLA_V
SWA_KV_W = SWA_KV_HEADS * HEAD_DIM
Q_PIECES = ((Q_DOWN0, MLA_Q_RANK), (QB0, BRANCH_W), (QC0, BRANCH_W), (QD0, BRANCH_W))
KV_PIECES = ((CKV0, MLA_KV_RANK), (KB0, 2 * SWA_KV_W), (KC0, 2 * BRANCH_W), (KD0, 2 * BRANCH_W))
Q_WIDTH = sum(w for _, w in Q_PIECES)
KV_WIDTH = sum(w for _, w in KV_PIECES)
CQ_DOWN0, CQB0, CQC0, CQD0 = 0, MLA_Q_RANK, MLA_Q_RANK + BRANCH_W, MLA_Q_RANK + 2 * BRANCH_W
CCKV0, CKB0 = 0, MLA_KV_RANK
CVB0 = CKB0 + SWA_KV_W
CKC0 = CVB0 + SWA_KV_W
CVC0 = CKC0 + BRANCH_W
CKD0 = CVC0 + BRANCH_W
CVD0 = CKD0 + BRANCH_W
N_MOD_ROWS = 16

CACHE_TAILS = ((MLA_KV_RANK,), (MLA_ROPE,), (SWA_KV_HEADS, HEAD_DIM), (SWA_KV_HEADS, HEAD_DIM),
               (NA_HEADS, HEAD_DIM), (NA_HEADS, HEAD_DIM), (DIFF_HEADS, 2 * DIFF_QK), (DIFF_HEADS, DIFF_V))

VMEM_LIMIT = 56 * 1024 * 1024
VMEM_LIMIT_BIG = 62 * 1024 * 1024
LANES = 128

F32 = jnp.float32
BF16 = jnp.bfloat16


def _lambda_init(l):
    return 0.8 - 0.6 * math.exp(-0.3 * l)


def _rms(x):
    return x * lax.rsqrt(jnp.mean(x * x, axis=-1, keepdims=True) + EPS)


def _head_cols(h, base=0):
    return slice(base + h * HEAD_DIM, base + (h + 1) * HEAD_DIM)


def _dot(a, b):
    return jnp.dot(a, b, preferred_element_type=F32)


def _dot_nt(a, b):
    return lax.dot_general(a, b, (((1,), (1,)), ((), ())), preferred_element_type=F32)


def _attend(pieces, sink=None):
    m = None
    for s, _ in pieces:
        mi = jnp.max(s, axis=-1, keepdims=True)
        m = mi if m is None else jnp.maximum(m, mi)
    if sink is not None:
        sink = sink * LOG2E
        m = jnp.maximum(m, sink)
    den = None
    num = None
    for s, v in pieces:
        e = jnp.exp2(s - m)
        d = jnp.sum(e, axis=-1, keepdims=True)
        n = _dot(e.astype(BF16), v)
        den = d if den is None else den + d
        num = n if num is None else num + n
    if sink is not None:
        den = den + jnp.exp2(sink - m)
    return num / den


def _diff_attend(pieces1, pieces2, values, lam):
    def exps(pieces):
        m = None
        for s in pieces:
            mi = jnp.max(s, axis=-1, keepdims=True)
            m = mi if m is None else jnp.maximum(m, mi)
        es = [jnp.exp2(s - m) for s in pieces]
        den = None
        for e in es:
            d = jnp.sum(e, axis=-1, keepdims=True)
            den = d if den is None else den + d
        return es, den

    es1, den1 = exps(pieces1)
    es2, den2 = exps(pieces2)
    ratio = lam * den1 / den2
    out = None
    for e1, e2, v in zip(es1, es2, values):
        n = _dot((e1 - e2 * ratio).astype(BF16), v)
        out = n if out is None else out + n
    return out / den1


def _diff_lambda(lq1, lk1, lq2, lk2, lam_init):
    a = jnp.sum(lq1[...] * lk1[...], axis=-1, keepdims=True)
    b = jnp.sum(lq2[...] * lk2[...], axis=-1, keepdims=True)
    return jnp.exp(a) - jnp.exp(b) + lam_init


def _ada_kernel(c_ref, w_ref, b_ref, o_ref):
    c = c_ref[...]
    s = (c * jax.nn.sigmoid(c)).astype(BF16)
    o_ref[0] = _dot(s, w_ref[0].astype(BF16)) + b_ref[0]


def _ada_call(cvec, w_ada, b_ada):
    tn = 1024
    n = 6 * D_MODEL
    return pl.pallas_call(
        _ada_kernel,
        grid=(DEPTH, n // tn),
        in_specs=[
            pl.BlockSpec((N_MOD_ROWS, D_MODEL), lambda l, j: (0, 0)),
            pl.BlockSpec((1, D_MODEL, tn), lambda l, j: (l, 0, j)),
            pl.BlockSpec((1, 1, tn), lambda l, j: (l, 0, j)),
        ],
        out_specs=pl.BlockSpec((1, N_MOD_ROWS, tn), lambda l, j: (l, 0, j)),
        out_shape=jax.ShapeDtypeStruct((DEPTH, N_MOD_ROWS, n), F32),
        compiler_params=pltpu.CompilerParams(
            dimension_semantics=("arbitrary", "arbitrary"), vmem_limit_bytes=VMEM_LIMIT),
        name="ada_mod",
    )(cvec, w_ada, b_ada.reshape(DEPTH, 1, n))


NBR_PAIRS = 2 * NA_ROWS


def _nbr_bias_kernel(rpb_ref, o_ref):
    l = pl.program_id(0)
    h = pl.program_id(1)
    n_dr = 2 * NA_ROWS - 1
    n_dc = 2 * NA_COLS - 1
    base = (l * NA_HEADS + h) * (n_dr * n_dc)
    wq = lax.broadcasted_iota(jnp.int32, (GRID_W, LANES), 0)
    lane = lax.broadcasted_iota(jnp.int32, (GRID_W, LANES), 1)
    upper = lane >= GRID_W
    wk = jnp.where(upper, lane - GRID_W, lane)
    delta = wk - wq + (NA_COLS - 1)
    cs = jnp.clip(wq - NA_COLS // 2, 0, GRID_W - NA_COLS)
    valid = (wk >= cs) & (wk < cs + NA_COLS)
    for e in range(NBR_PAIRS):
        dr_lo, dr_hi = max(e - 1, 0), min(e, n_dr - 1)
        acc = jnp.full((GRID_W, LANES), NEG_INF, F32)
        for d in range(n_dc):
            lo = rpb_ref[base + dr_lo * n_dc + d]
            hi = rpb_ref[base + dr_hi * n_dc + d]
            acc = jnp.where(delta == d, jnp.where(upper, hi, lo), acc)
        o_ref[0, 0, e] = jnp.where(valid, acc * LOG2E, NEG_INF)


def _nbr_bias_call(na_rpb):
    n_pairs = NBR_PAIRS
    return pl.pallas_call(
        _nbr_bias_kernel,
        grid=(DEPTH, NA_HEADS),
        in_specs=[pl.BlockSpec(memory_space=pltpu.SMEM)],
        out_specs=pl.BlockSpec((1, 1, n_pairs, GRID_W, LANES), lambda l, h: (l, h, 0, 0, 0)),
        out_shape=jax.ShapeDtypeStruct((DEPTH, NA_HEADS, n_pairs, GRID_W, LANES), F32),
        compiler_params=pltpu.CompilerParams(dimension_semantics=("arbitrary", "arbitrary")),
        name="nbr_bias",
    )(na_rpb.reshape(-1))


def _inproj_kernel(x_ref, mod_ref, g_ref, w_ref, wkr_ref, h_ref, *p_refs_kr):
    *p_refs, kr_ref = p_refs_kr
    m = mod_ref[0]
    half = x_ref.shape[0] // 2
    for r in (slice(0, half), slice(half, 2 * half)):
        h = (_rms(x_ref[r, :]) * g_ref[...] * (1.0 + m[1:2]) + m[0:1]).astype(BF16)
        h_ref[r, :] = h
        kr_ref[r, :] = _dot_nt(h, wkr_ref[...])
        p = _dot_nt(h, w_ref[...])
        if len(p_refs) == 1:
            p_refs[0][r, :] = p.astype(p_refs[0].dtype)
        else:
            for ref, pieces in zip(p_refs, (Q_PIECES, KV_PIECES)):
                dst = 0
                for src, width in pieces:
                    ref[r, dst:dst + width] = p[:, src:src + width].astype(ref.dtype)
                    dst += width


def _inproj_call(x, mod, g, w_main, w_kr, layer, seg, split, name):
    n_tok = x.shape[0]
    tm = 512
    p_outs = ((Q_WIDTH, BF16), (KV_WIDTH, F32)) if split else ((P_WIDTH, BF16),)
    return pl.pallas_call(
        _inproj_kernel,
        grid=(n_tok // tm,),
        in_specs=[
            pl.BlockSpec((tm, D_MODEL), lambda i: (i, 0)),
            pl.BlockSpec((1, 6, D_MODEL), lambda i: (seg(i, tm), 0, 0)),
            pl.BlockSpec((1, D_MODEL), lambda i: (0, 0)),
            pl.BlockSpec((None, P_WIDTH, D_MODEL), lambda i: (layer, 0, 0)),
            pl.BlockSpec((None, LANES, D_MODEL), lambda i: (layer, 0, 0)),
        ],
        out_specs=[pl.BlockSpec((tm, D_MODEL), lambda i: (i, 0))]
        + [pl.BlockSpec((tm, w), lambda i: (i, 0)) for w, _ in p_outs]
        + [pl.BlockSpec((tm, LANES), lambda i: (i, 0))],
        out_shape=[jax.ShapeDtypeStruct((n_tok, D_MODEL), BF16)]
        + [jax.ShapeDtypeStruct((n_tok, w), dt) for w, dt in p_outs]
        + [jax.ShapeDtypeStruct((n_tok, LANES), F32)],
        compiler_params=pltpu.CompilerParams(
            dimension_semantics=("arbitrary",), vmem_limit_bytes=VMEM_LIMIT_BIG),
        name=name,
    )(x, mod, g, w_main, w_kr)


def _cast_kernel(x_ref, o_ref):
    o_ref[...] = x_ref[...].astype(BF16)


def _cast_call(w, block_rows, name):
    w2 = w.reshape(-1, w.shape[-1])
    rows, cols = w2.shape[0] // w.shape[0], w2.shape[1]
    out = pl.pallas_call(
        _cast_kernel,
        grid=(rows // block_rows,),
        in_specs=[pl.BlockSpec((block_rows, cols), lambda i: (i, 0))],
        out_specs=pl.BlockSpec((block_rows, cols), lambda i: (i, 0)),
        out_shape=jax.ShapeDtypeStruct((rows, cols), BF16),
        compiler_params=pltpu.CompilerParams(dimension_semantics=("arbitrary",), vmem_limit_bytes=VMEM_LIMIT),
        name=name,
    )(w2)
    return out.reshape((1,) + w.shape[1:])


def _side_specs(side, n_grid):
    in_specs, out_specs, out_shape = [], [], []
    for w, layer, block, index in side:
        if n_grid == 1:
            in_specs.append(pl.BlockSpec((None,) + block, lambda i, f=index, l=layer: (l,) + f(i)))
            out_specs.append(pl.BlockSpec((None,) + block, lambda i, f=index: (0,) + f(i)))
        else:
            in_specs.append(pl.BlockSpec((None,) + block, lambda i, j, f=index, l=layer: (l,) + f(i, j)))
            out_specs.append(pl.BlockSpec((None,) + block, lambda i, j, f=index: (0,) + f(i, j)))
        out_shape.append(jax.ShapeDtypeStruct((1,) + w.shape[1:], BF16))
    return in_specs, out_specs, out_shape


def _side_cast(srcs, dsts):
    for src, dst in zip(srcs, dsts):
        dst[...] = src[...].astype(BF16)


def _cast_w_in_kernel(x_ref, xkr_ref, main_ref, kr_ref):
    main_ref[...] = x_ref[0].astype(BF16)

    @pl.when(pl.program_id(1) == 0)
    def _():
        kr_ref[:MLA_ROPE, :] = xkr_ref[...].astype(BF16)
        kr_ref[MLA_ROPE:, :] = jnp.zeros((LANES - MLA_ROPE, D_MODEL), BF16)


def _cast_w_in_call(w_in):
    w_t = jnp.swapaxes(w_in, 1, 2)
    block_rows = 512
    src_row = lambda l, j: (l, pl.multiple_of(j * block_rows + jnp.where(j * block_rows >= KROPE_SRC0, MLA_ROPE, 0), MLA_ROPE), 0)
    return pl.pallas_call(
        _cast_w_in_kernel,
        grid=(DEPTH, P_WIDTH // block_rows),
        in_specs=[pl.BlockSpec((pl.Element(1), pl.Element(block_rows), pl.Element(D_MODEL)), src_row),
                  pl.BlockSpec((None, MLA_ROPE, D_MODEL), lambda l, j: (l, KROPE_SRC0 // MLA_ROPE, 0))],
        out_specs=[pl.BlockSpec((None, block_rows, D_MODEL), lambda l, j: (l, j, 0)),
                   pl.BlockSpec((None, LANES, D_MODEL), lambda l, j: (l, 0, 0))],
        out_shape=[jax.ShapeDtypeStruct((DEPTH, P_WIDTH, D_MODEL), BF16),
                   jax.ShapeDtypeStruct((DEPTH, LANES, D_MODEL), BF16)],
        compiler_params=pltpu.CompilerParams(
            dimension_semantics=("arbitrary", "arbitrary"), vmem_limit_bytes=VMEM_LIMIT),
        name="cast_w_in",
    )(w_t, w_t)


def _mla_queries(q_down, qn_g, w_uq):
    cq = _rms(q_down) * qn_g[...]
    return _dot(cq.astype(BF16), w_uq[...])


def _ctx_attn_kernel(*refs, lam_init, first):
    (pq_ref, pkv_ref, kr_ref, qn_g, w_uq, kvn_g, w_ukv, sink_ref, lq1, lk1, lq2, lk2, subln_g) = refs[:13]
    n_in = 13 + (0 if first else len(CACHE_TAILS))
    o_ref = refs[n_in]
    ckv_o, kr_o, kb_o, vb_o, kc_o, vc_o, kd_o, vd_o = refs[n_in + 1:]
    ls = 0
    if first:
        for dst in (ckv_o, kr_o, kb_o, vb_o, kc_o, vc_o, kd_o, vd_o):
            dst[0, 1:] = jnp.zeros(dst.shape[1:], F32)[1:]

    scale_a = LOG2E * (MLA_NOPE + MLA_ROPE) ** -0.5
    qa = _mla_queries(pq_ref[:, CQ_DOWN0:CQ_DOWN0 + MLA_Q_RANK].astype(F32), qn_g, w_uq) * scale_a
    ckv = _rms(pkv_ref[:, CCKV0:CCKV0 + MLA_KV_RANK]) * kvn_g[...]
    ckv_o[0, ls] = ckv
    kr = kr_ref[...]
    kr_o[0, ls] = kr[:, :MLA_ROPE]
    kr_b = kr.astype(BF16)
    kv = _dot(ckv.astype(BF16), w_ukv[...])
    for h in range(MLA_HEADS):
        q = jnp.concatenate([qa[:, _head_cols(h)], qa[:, _head_cols(h, MLA_QN_W)]],
                            axis=1).astype(BF16)
        kv0 = h * MLA_KV_W
        k = jnp.concatenate([kv[:, kv0:kv0 + MLA_NOPE].astype(BF16), kr_b], axis=1)
        v = kv[:, kv0 + MLA_NOPE:kv0 + MLA_KV_W].astype(BF16)
        o = _attend([(_dot_nt(q, k), v)])
        o_ref[:, _head_cols(h)] = o.astype(BF16)

    scale = LOG2E * HEAD_DIM ** -0.5

    def qhead(col0, h):
        return pq_ref[:, _head_cols(h, col0)].astype(F32)

    def head(col0, h):
        return pkv_ref[:, _head_cols(h, col0)]

    g_size = SWA_HEADS // SWA_KV_HEADS
    for g in range(SWA_KV_HEADS):
        k = head(CKB0, g)
        v = head(CVB0, g)
        kb_o[0, ls, pl.ds(g, SEQ, stride=SWA_KV_HEADS), :] = k
        vb_o[0, ls, pl.ds(g, SEQ, stride=SWA_KV_HEADS), :] = v
        k = k.astype(BF16)
        v = v.astype(BF16)
        for h in range(g * g_size, (g + 1) * g_size):
            q = (qhead(CQB0, h) * scale).astype(BF16)
            o = _attend([(_dot_nt(q, k), v)], sink=sink_ref[h])
            o_ref[:, _head_cols(h, BRANCH_W)] = o.astype(BF16)

    for h in range(NA_HEADS):
        k = head(CKC0, h)
        v = head(CVC0, h)
        kc_o[0, ls, pl.ds(h, SEQ, stride=NA_HEADS), :] = k
        vc_o[0, ls, pl.ds(h, SEQ, stride=NA_HEADS), :] = v
        q = (qhead(CQC0, h) * scale).astype(BF16)
        o = _attend([(_dot_nt(q, k.astype(BF16)), v.astype(BF16))])
        o_ref[:, _head_cols(h, 2 * BRANCH_W)] = o.astype(BF16)

    lam = _diff_lambda(lq1, lk1, lq2, lk2, lam_init)
    scale_d = LOG2E * DIFF_QK ** -0.5
    lo = lax.broadcasted_iota(jnp.int32, (SEQ, 2 * DIFF_QK), 1) < DIFF_QK
    for h in range(DIFF_HEADS):
        k = head(CKD0, h)
        v = head(CVD0, h)
        kd_o[0, ls, pl.ds(h, SEQ, stride=DIFF_HEADS), :] = k
        vd_o[0, ls, pl.ds(h, SEQ, stride=DIFF_HEADS), :] = v
        k = k.astype(BF16)
        v = v.astype(BF16)
        q = qhead(CQD0, h) * scale_d
        o1 = _attend([(_dot_nt(jnp.where(lo, q, 0.0).astype(BF16), k), v)])
        o2 = _attend([(_dot_nt(jnp.where(lo, 0.0, q).astype(BF16), k), v)])
        o = _rms(o1 - lam * o2) * subln_g[...] * (1.0 - lam_init)
        o_ref[:, _head_cols(h, 3 * BRANCH_W)] = o.astype(BF16)


def _small_specs():
    zero2 = (lambda *a: (0, 0))
    return [
        pl.BlockSpec((1, MLA_Q_RANK), zero2),
        pl.BlockSpec((MLA_Q_RANK, 2 * MLA_QN_W), zero2),
        pl.BlockSpec((1, MLA_KV_RANK), zero2),
        pl.BlockSpec((MLA_KV_RANK, MLA_HEADS * MLA_KV_W), zero2),
        pl.BlockSpec(memory_space=pltpu.SMEM),
        pl.BlockSpec((1, DIFF_QK), zero2),
        pl.BlockSpec((1, DIFF_QK), zero2),
        pl.BlockSpec((1, DIFF_QK), zero2),
        pl.BlockSpec((1, DIFF_QK), zero2),
        pl.BlockSpec((1, DIFF_V), zero2),
    ]


def _ctx_attn_call(pq, pkv, kr, small, prev, layer, lam_init):
    first = not prev
    row = lambda b: (b, 0)

    def flat(tail):
        return (SEQ * tail[0], tail[1]) if len(tail) == 2 else (SEQ,) + tail

    def cache_spec(tail):
        if first:
            return pl.BlockSpec((1, DEPTH) + flat(tail), lambda b: (b, 0, 0, 0))
        return pl.BlockSpec((1, 1) + flat(tail), lambda b: (b, layer, 0, 0))

    n_fixed = 13
    return pl.pallas_call(
        functools.partial(_ctx_attn_kernel, lam_init=lam_init, first=first),
        grid=(BATCH,),
        in_specs=[pl.BlockSpec((SEQ, Q_WIDTH), row), pl.BlockSpec((SEQ, KV_WIDTH), row),
                  pl.BlockSpec((SEQ, LANES), row)] + _small_specs()
        + [pl.BlockSpec(memory_space=pl.ANY) for _ in prev],
        out_specs=[pl.BlockSpec((SEQ, D_MODEL), row)] + [cache_spec(t) for t in CACHE_TAILS],
        out_shape=[jax.ShapeDtypeStruct((N_CTX_TOK, D_MODEL), BF16)]
        + [jax.ShapeDtypeStruct((BATCH, DEPTH) + flat(t), F32) for t in CACHE_TAILS],
        input_output_aliases={n_fixed + k: 1 + k for k in range(len(prev))},
        compiler_params=pltpu.CompilerParams(
            dimension_semantics=("arbitrary",), vmem_limit_bytes=VMEM_LIMIT),
        name="ctx_attn",
    )(pq, pkv, kr, *small, *prev)


LAT_TQ = 256
SWA_PAD = SWA_WINDOW
NBR_WIN_ROWS = 12


def _rope_pair64(x, c_ref, sa_ref, sb_ref, rows):
    out = []
    for j in range(x.shape[1] // LANES):
        xj = x[:, j * LANES:(j + 1) * LANES]
        out.append(xj * c_ref[rows, :] + pltpu.roll(xj, 96, 1) * sa_ref[rows, :]
                   + pltpu.roll(xj, 32, 1) * sb_ref[rows, :])
    return jnp.concatenate(out, axis=1) if len(out) > 1 else out[0]


def _rope128(x, c_ref, s_ref, rows):
    out = []
    for j in range(x.shape[1] // LANES):
        xj = x[:, j * LANES:(j + 1) * LANES]
        out.append(xj * c_ref[rows, :] + pltpu.roll(xj, 64, 1) * s_ref[rows, :])
    return jnp.concatenate(out, axis=1) if len(out) > 1 else out[0]


def _lat_attn_kernel(p_ref, kr_ref, c_ckv, c_kr, c_kb, c_vb, c_kc, c_vc, c_kd, c_vd,
                     cos128, sin128, cos64, sina64, sinb64, bias_ref,
                     qn_g, w_uq, kvn_g, w_ukv, sink_ref, lq1, lk1, lq2, lk2, subln_g,
                     o_ref,
                     ka_s, va_s, kbc_s, vbc_s, kbp_s, vbp_s, kcc_s, vcc_s, kdc_s, vdc_s, kdo_s,
                     *, lam_init):
    def cached_head(ref, h, n_heads):
        return ref[pl.ds(h, PAST_LEN, stride=n_heads), :].astype(BF16)

    qi = pl.program_id(1)
    all_rows = slice(None)
    n_own = DEC_SEQ

    @pl.when(qi == 0)
    def _prologue():
        ckv_own = (_rms(p_ref[:, CKV0:CKV0 + MLA_KV_RANK].astype(F32)) * kvn_g[...]).astype(BF16)
        ckv_all = jnp.concatenate([c_ckv[...].astype(BF16), ckv_own], axis=0)
        kr_ctx = jnp.concatenate([c_kr[...], jnp.zeros((PAST_LEN, LANES - MLA_ROPE), F32)], axis=1)
        kr_own = _rope_pair64(kr_ref[...], cos64, sina64, sinb64, all_rows)
        kr_all = jnp.concatenate([kr_ctx, kr_own], axis=0).astype(BF16)
        for h in range(MLA_HEADS):
            kvh = _dot(ckv_all, w_ukv[:, h * MLA_KV_W:(h + 1) * MLA_KV_W])
            ka_s[h, :, 0:MLA_NOPE] = kvh[:, 0:MLA_NOPE].astype(BF16)
            ka_s[h, :, MLA_NOPE:MLA_NOPE + LANES] = kr_all
            va_s[:, _head_cols(h)] = kvh[:, MLA_NOPE:MLA_KV_W].astype(BF16)
        for g in range(SWA_KV_HEADS):
            kbc_s[:, _head_cols(g)] = cached_head(c_kb, g, SWA_KV_HEADS)
            vbc_s[:, _head_cols(g)] = cached_head(c_vb, g, SWA_KV_HEADS)
        zpad = jnp.zeros((SWA_PAD, SWA_KV_W), BF16)
        kbp_s[0:SWA_PAD, :] = zpad
        kbp_s[SWA_PAD + n_own:, :] = zpad
        vbp_s[0:SWA_PAD, :] = zpad
        vbp_s[SWA_PAD + n_own:, :] = zpad
        kbp_s[SWA_PAD:SWA_PAD + n_own, :] = _rope128(
            p_ref[:, KB0:KB0 + SWA_KV_W].astype(F32), cos128, sin128, all_rows).astype(BF16)
        vbp_s[SWA_PAD:SWA_PAD + n_own, :] = p_ref[:, VB0:VB0 + SWA_KV_W]
        for h in range(NA_HEADS):
            cols = _head_cols(h)
            kcc_s[:, cols] = cached_head(c_kc, h, NA_HEADS)
            vcc_s[:, cols] = cached_head(c_vc, h, NA_HEADS)
            kdc_s[:, cols] = cached_head(c_kd, h, DIFF_HEADS)
            vdc_s[:, cols] = cached_head(c_vd, h, DIFF_HEADS)
        kdo_s[...] = _rope_pair64(
            p_ref[:, KD0:KD0 + BRANCH_W].astype(F32), cos64, sina64, sinb64, all_rows).astype(BF16)

    q0 = pl.multiple_of(qi * LAT_TQ, LAT_TQ)
    rows = pl.ds(q0, LAT_TQ)

    scale_a = LOG2E * (MLA_NOPE + MLA_ROPE) ** -0.5
    qa = _mla_queries(p_ref[rows, Q_DOWN0:Q_DOWN0 + MLA_Q_RANK].astype(F32), qn_g, w_uq) * scale_a
    qa_r = _rope_pair64(qa[:, MLA_QN_W:2 * MLA_QN_W], cos64, sina64, sinb64, rows)
    for h in range(MLA_HEADS):
        q = jnp.concatenate([qa[:, _head_cols(h)], qa_r[:, _head_cols(h)]],
                            axis=1).astype(BF16)
        o = _attend([(_dot_nt(q, ka_s[h]), va_s[:, _head_cols(h)])])
        o_ref[:, _head_cols(h)] = o.astype(BF16)

    scale = LOG2E * HEAD_DIM ** -0.5
    qb = (_rope128(p_ref[rows, QB0:QB0 + BRANCH_W].astype(F32), cos128, sin128, rows) * scale).astype(BF16)
    n_loc = LAT_TQ + 2 * SWA_PAD
    qpos = lax.broadcasted_iota(jnp.int32, (LAT_TQ, n_loc), 0)
    kpos = lax.broadcasted_iota(jnp.int32, (LAT_TQ, n_loc), 1) - SWA_PAD
    kabs = kpos + q0
    valid = (jnp.abs(qpos - kpos) <= SWA_WINDOW) & (kabs >= 0) & (kabs < n_own)
    loc_rows = pl.ds(q0, n_loc)
    g_size = SWA_HEADS // SWA_KV_HEADS
    for h in range(SWA_HEADS):
        g = h // g_size
        q = qb[:, _head_cols(h)]
        s_ctx = _dot_nt(q, kbc_s[:, _head_cols(g)])
        s_loc = jnp.where(valid, _dot_nt(q, kbp_s[loc_rows, _head_cols(g)]), NEG_INF)
        o = _attend([(s_ctx, vbc_s[:, _head_cols(g)]),
                     (s_loc, vbp_s[loc_rows, _head_cols(g)])], sink=sink_ref[h])
        o_ref[:, _head_cols(h, BRANCH_W)] = o.astype(BF16)

    q_rows = LAT_TQ // GRID_W
    ws = jnp.clip(qi * q_rows - NA_ROWS // 2, 0, GRID_ROWS - NBR_WIN_ROWS)
    ws = (ws // 2) * 2
    krows = pl.ds(pl.multiple_of(ws * GRID_W, 2 * GRID_W), NBR_WIN_ROWS * GRID_W)
    lower = lax.broadcasted_iota(jnp.int32, (GRID_W, LANES), 1) < GRID_W
    for h in range(NA_HEADS):
        cols = _head_cols(h)
        bias_rows = []
        for t in range(q_rows):
            r = qi * q_rows + t
            off = jnp.clip(r - NA_ROWS // 2, 0, GRID_ROWS - NA_ROWS) - r + (NA_ROWS - 1)
            blocks = []
            for j in range(NBR_WIN_ROWS // 2):
                dr = ws + 2 * j - r + (NA_ROWS - 1)
                ok_lo = ((dr >= off) & (dr < off + NA_ROWS)).astype(jnp.int32)
                ok_hi = ((dr + 1 >= off) & (dr + 1 < off + NA_ROWS)).astype(jnp.int32)
                ok = jnp.where(lower, ok_lo, ok_hi) > 0
                blk = bias_ref[0, h, jnp.clip(dr + 1, 0, NBR_PAIRS - 1)]
                blocks.append(jnp.where(ok, blk, NEG_INF))
            bias_rows.append(jnp.concatenate(blocks, axis=1))
        bias = jnp.concatenate(bias_rows, axis=0)
        q = (p_ref[rows, _head_cols(h, QC0)].astype(F32) * scale).astype(BF16)
        s_loc = _dot_nt(q, p_ref[krows, _head_cols(h, KC0)]) + bias
        s_ctx = _dot_nt(q, kcc_s[:, cols])
        o = _attend([(s_ctx, vcc_s[:, cols]), (s_loc, p_ref[krows, _head_cols(h, VC0)])])
        o_ref[:, _head_cols(h, 2 * BRANCH_W)] = o.astype(BF16)

    scale_d = LOG2E * DIFF_QK ** -0.5
    qd = _rope_pair64(p_ref[rows, QD0:QD0 + BRANCH_W].astype(F32), cos64, sina64, sinb64, rows) * scale_d
    lam = _diff_lambda(lq1, lk1, lq2, lk2, lam_init)
    lo = lax.broadcasted_iota(jnp.int32, (LAT_TQ, 2 * DIFF_QK), 1) < DIFF_QK
    for h in range(DIFF_HEADS):
        cols = _head_cols(h)
        q = qd[:, cols]
        v_own = p_ref[:, _head_cols(h, VD0)]
        q1 = jnp.where(lo, q, 0.0).astype(BF16)
        q2 = jnp.where(lo, 0.0, q).astype(BF16)
        o = _diff_attend([_dot_nt(q1, kdc_s[:, cols]), _dot_nt(q1, kdo_s[:, cols])],
                         [_dot_nt(q2, kdc_s[:, cols]), _dot_nt(q2, kdo_s[:, cols])],
                         [vdc_s[:, cols], v_own], lam)
        o = _rms(o) * subln_g[...] * (1.0 - lam_init)
        o_ref[:, _head_cols(h, 3 * BRANCH_W)] = o.astype(BF16)


def _lat_attn_call(p, kr, caches, tables, bias, small, layer, lam_init):
    nq = DEC_SEQ // LAT_TQ
    batch_rows = lambda b, q: (b, 0)
    zero2 = lambda b, q: (0, 0)
    once = pl.Buffered(1)
    cache_args = [a.reshape(a.shape[:2] + (-1, a.shape[-1])) for a in caches]
    cache_specs = [pl.BlockSpec((None, None) + a.shape[2:], lambda b, q: (b, layer, 0, 0)) for a in cache_args]
    table_specs = [pl.BlockSpec((DEC_SEQ, LANES), zero2) for _ in tables]
    bias_spec = pl.BlockSpec((1, NA_HEADS, NBR_PAIRS, GRID_W, LANES), lambda b, q: (layer, 0, 0, 0, 0))
    n_all = PAST_LEN + DEC_SEQ
    n_pad = DEC_SEQ + 2 * SWA_PAD
    scratch = [
        pltpu.VMEM((MLA_HEADS, n_all, MLA_NOPE + LANES), BF16),
        pltpu.VMEM((n_all, BRANCH_W), BF16),
        pltpu.VMEM((PAST_LEN, SWA_KV_W), BF16), pltpu.VMEM((PAST_LEN, SWA_KV_W), BF16),
        pltpu.VMEM((n_pad, SWA_KV_W), BF16), pltpu.VMEM((n_pad, SWA_KV_W), BF16),
        pltpu.VMEM((PAST_LEN, BRANCH_W), BF16), pltpu.VMEM((PAST_LEN, BRANCH_W), BF16),
        pltpu.VMEM((PAST_LEN, BRANCH_W), BF16), pltpu.VMEM((PAST_LEN, BRANCH_W), BF16),
        pltpu.VMEM((DEC_SEQ, BRANCH_W), BF16),
    ]
    return pl.pallas_call(
        functools.partial(_lat_attn_kernel, lam_init=lam_init),
        grid=(DEC_BATCH, nq),
        in_specs=[pl.BlockSpec((DEC_SEQ, P_WIDTH), batch_rows, pipeline_mode=once),
                  pl.BlockSpec((DEC_SEQ, LANES), batch_rows, pipeline_mode=once)]
        + cache_specs + table_specs + [bias_spec] + _small_specs(),
        out_specs=pl.BlockSpec((LAT_TQ, D_MODEL), lambda b, q: (b * nq + q, 0)),
        out_shape=jax.ShapeDtypeStruct((N_LAT_TOK, D_MODEL), BF16),
        scratch_shapes=scratch,
        compiler_params=pltpu.CompilerParams(
            dimension_semantics=("arbitrary", "arbitrary"), vmem_limit_bytes=VMEM_LIMIT_BIG),
        name="lat_attn",
    )(p, kr, *cache_args, *tables, bias, *small)


def _merge_kernel(h_ref, o_ref, wg_ref, wb_ref, *rest):
    n_side = (len(rest) - 1) // 2
    out_ref = rest[n_side]
    _side_cast(rest[:n_side], rest[n_side + 1:])
    acc = None
    for i in range(N_BRANCH):
        gate = jax.nn.sigmoid(_dot(h_ref[...], wg_ref[i]))
        term = gate * _dot(o_ref[:, i * BRANCH_W:(i + 1) * BRANCH_W], wb_ref[i])
        acc = term if acc is None else acc + term
    out_ref[...] = acc.astype(BF16)


def _merge_call(h, o, wg, wb, name, side_weights=()):
    n_tok = h.shape[0]
    tm, tn = 1024, 512
    grid = (n_tok // tm, D_MODEL // tn)
    side = []
    for w, layer in side_weights:
        lead = w.shape[1:-2]
        block = lead + (w.shape[-2] // grid[0], w.shape[-1] // grid[1])
        side.append((w, layer, block, lambda i, j, z=(0,) * len(lead): z + (i, j)))
    s_in, s_out, s_shape = _side_specs(side, 2)
    return pl.pallas_call(
        _merge_kernel,
        grid=grid,
        in_specs=[
            pl.BlockSpec((tm, D_MODEL), lambda i, j: (i, 0)),
            pl.BlockSpec((tm, D_MODEL), lambda i, j: (i, 0)),
            pl.BlockSpec((None, N_BRANCH, D_MODEL, tn), lambda i, j: (0, 0, 0, j)),
            pl.BlockSpec((None, N_BRANCH, BRANCH_W, tn), lambda i, j: (0, 0, 0, j)),
        ] + s_in,
        out_specs=[pl.BlockSpec((tm, tn), lambda i, j: (i, j))] + s_out,
        out_shape=[jax.ShapeDtypeStruct((n_tok, D_MODEL), BF16)] + s_shape,
        compiler_params=pltpu.CompilerParams(
            dimension_semantics=("arbitrary", "arbitrary"), vmem_limit_bytes=VMEM_LIMIT),
        name=name,
    )(h, o, wg, wb, *[w for w, _, _, _ in side])


def _outproj_kernel(mixed_ref, x_ref, mod_ref, g_ref, gffn_ref, w_ref, *rest):
    n_side = (len(rest) - 2) // 2
    xo_ref, h2_ref = rest[n_side:n_side + 2]
    _side_cast(rest[:n_side], rest[n_side + 2:])
    m = mod_ref[0]
    half = mixed_ref.shape[0] // 2
    for r in (slice(0, half), slice(half, 2 * half)):
        out = _dot(mixed_ref[r, :], w_ref[...])
        x = x_ref[r, :] + m[2:3] * (_rms(out) * g_ref[...])
        xo_ref[r, :] = x
        h2_ref[r, :] = (_rms(x) * gffn_ref[...] * (1.0 + m[4:5]) + m[3:4]).astype(BF16)


def _outproj_call(mixed, x, mod, g, g_ffn, w, seg, name, side_weights=()):
    n_tok = x.shape[0]
    tm = 512
    n_steps = n_tok // tm
    side = [(w_, layer, (w_.shape[1] // n_steps, w_.shape[2]), lambda i: (i, 0)) for w_, layer in side_weights]
    s_in, s_out, s_shape = _side_specs(side, 1)
    return pl.pallas_call(
        _outproj_kernel,
        grid=(n_steps,),
        in_specs=[
            pl.BlockSpec((tm, D_MODEL), lambda i: (i, 0)),
            pl.BlockSpec((tm, D_MODEL), lambda i: (i, 0)),
            pl.BlockSpec((1, 6, D_MODEL), lambda i: (seg(i, tm), 0, 0)),
            pl.BlockSpec((1, D_MODEL), lambda i: (0, 0)),
            pl.BlockSpec((1, D_MODEL), lambda i: (0, 0)),
            pl.BlockSpec((None, D_MODEL, D_MODEL), lambda i: (0, 0, 0)),
        ] + s_in,
        out_specs=[pl.BlockSpec((tm, D_MODEL), lambda i: (i, 0)), pl.BlockSpec((tm, D_MODEL), lambda i: (i, 0))]
        + s_out,
        out_shape=[jax.ShapeDtypeStruct((n_tok, D_MODEL), F32), jax.ShapeDtypeStruct((n_tok, D_MODEL), BF16)]
        + s_shape,
        compiler_params=pltpu.CompilerParams(
            dimension_semantics=("arbitrary",), vmem_limit_bytes=VMEM_LIMIT),
        name=name,
    )(mixed, x, mod, g, g_ffn, w, *[w_ for w_, _, _, _ in side])


def _ffn_kernel(x_ref, h2_ref, mod_ref, gpost_ref, wg_ref, wu_ref, wd_ref, xo_ref):
    j = pl.program_id(1)

    @pl.when(j == 0)
    def _():
        xo_ref[...] = jnp.zeros_like(xo_ref)

    a = _dot(h2_ref[...], wg_ref[...])
    b = _dot(h2_ref[...], wu_ref[...])
    t = (a * jax.nn.sigmoid(a) * b).astype(BF16)
    xo_ref[...] += _dot(t, wd_ref[...])

    @pl.when(j == pl.num_programs(1) - 1)
    def _():
        m = mod_ref[0]
        xo_ref[...] = x_ref[...] + m[5:6] * (_rms(xo_ref[...]) * gpost_ref[...])


def _ffn_call(x, h2, mod, gpost, wg, wu, wd, seg, name):
    n_tok = x.shape[0]
    tm, th = 1024, 256
    return pl.pallas_call(
        _ffn_kernel,
        grid=(n_tok // tm, FFN_HIDDEN // th),
        in_specs=[
            pl.BlockSpec((tm, D_MODEL), lambda i, j: (i, 0)),
            pl.BlockSpec((tm, D_MODEL), lambda i, j: (i, 0)),
            pl.BlockSpec((1, 6, D_MODEL), lambda i, j: (seg(i, tm), 0, 0)),
            pl.BlockSpec((1, D_MODEL), lambda i, j: (0, 0)),
            pl.BlockSpec((None, D_MODEL, th), lambda i, j: (0, 0, j)),
            pl.BlockSpec((None, D_MODEL, th), lambda i, j: (0, 0, j)),
            pl.BlockSpec((None, th, D_MODEL), lambda i, j: (0, j, 0)),
        ],
        out_specs=pl.BlockSpec((tm, D_MODEL), lambda i, j: (i, 0)),
        out_shape=jax.ShapeDtypeStruct((n_tok, D_MODEL), F32),
        compiler_params=pltpu.CompilerParams(
            dimension_semantics=("arbitrary", "arbitrary"), vmem_limit_bytes=VMEM_LIMIT_BIG),
        name=name,
    )(x, h2, mod, gpost, wg, wu, wd)


def _rope_tables():
    t = np.arange(DEC_SEQ)
    row = (t // GRID_W).astype(np.float32)
    col = (t % GRID_W).astype(np.float32)

    def half_angles(r):
        quarter = r // 4
        inv = (1.0 / (ROPE_BASE ** (np.arange(quarter, dtype=np.float32) / quarter))).astype(np.float32)
        return np.concatenate([row[:, None] * inv, col[:, None] * inv], axis=-1).astype(np.float32)

    a64 = half_angles(HEAD_DIM)
    cos128 = np.concatenate([np.cos(a64), np.cos(a64)], axis=1)
    sin128 = np.concatenate([-np.sin(a64), np.sin(a64)], axis=1)
    a32 = half_angles(MLA_ROPE)
    c, s, z = np.cos(a32), np.sin(a32), np.zeros_like(a32)
    cos64 = np.concatenate([c, c, c, c], axis=1)
    sina64 = np.concatenate([-s, z, -s, z], axis=1)
    sinb64 = np.concatenate([z, s, z, s], axis=1)
    return tuple(jnp.asarray(a, F32) for a in (cos128, sin128, cos64, sina64, sinb64))


def _ctx_seg(i, tm):
    del tm
    return i * 0


def _lat_seg(i, tm):
    return 1 + (i * tm) // DEC_SEQ


def kernel(x_prompt, x_sample, cache_mla_ckv, cache_mla_krope, cache_swa_k, cache_swa_v, cache_na_k, cache_na_v, cache_diff_k, cache_diff_v, c, c_ctx, w_ada, b_ada, mix_pre_g, mix_post_g, ffn_pre_g, ffn_post_g, w_in, mla_q_norm_g, mla_kv_norm_g, w_mla_uq, w_mla_ukv, swa_sink, na_rpb, diff_lq1, diff_lk1, diff_lq2, diff_lk2, diff_subln_g, w_mix_gate, w_branch, w_out, w_ffn_gate, w_ffn_up, w_ffn_down):
    cvec = jnp.concatenate([c_ctx[None, :], c, jnp.zeros((N_MOD_ROWS - 1 - DEC_BATCH, D_MODEL), F32)], axis=0)
    mod_all = _ada_call(cvec, w_ada, b_ada).reshape(DEPTH, N_MOD_ROWS, 6, D_MODEL)
    bias_all = _nbr_bias_call(na_rpb)
    tables = _rope_tables()
    caches = (cache_mla_ckv, cache_mla_krope, cache_swa_k, cache_swa_v, cache_na_k, cache_na_v,
              cache_diff_k, cache_diff_v)

    w_main, w_kr = _cast_w_in_call(w_in)
    wg = _cast_call(w_mix_gate, 512, "cast_w_gate")
    wb = _cast_call(w_branch, 512, "cast_w_branch")
    wo = _cast_call(w_out, 512, "cast_w_out")
    wfg = _cast_call(w_ffn_gate, 256, "cast_w_ffn_gate")
    wfu = _cast_call(w_ffn_up, 256, "cast_w_ffn_up")
    wfd = _cast_call(w_ffn_down, 512, "cast_w_ffn_down")

    y = x_prompt.reshape(N_CTX_TOK, D_MODEL)
    z = x_sample.reshape(N_LAT_TOK, D_MODEL)
    new_caches = ()
    for l in range(DEPTH):
        lam_init = _lambda_init(l)
        mod = mod_all[l]
        uq = w_mla_uq[l].reshape(MLA_Q_RANK, MLA_HEADS, MLA_NOPE + MLA_ROPE)
        uq_rope = jnp.pad(uq[:, :, MLA_NOPE:], ((0, 0), (0, 0), (0, LANES - MLA_ROPE)))
        w_uq = jnp.concatenate([uq[:, :, :MLA_NOPE].reshape(MLA_Q_RANK, MLA_QN_W),
                                uq_rope.reshape(MLA_Q_RANK, MLA_QN_W)], axis=1).astype(BF16)
        small = (mla_q_norm_g[l][None, :], w_uq, mla_kv_norm_g[l][None, :], w_mla_ukv[l].astype(BF16),
                 swa_sink[l], diff_lq1[l][None, :], diff_lk1[l][None, :], diff_lq2[l][None, :],
                 diff_lk2[l][None, :], diff_subln_g[l][None, :])
        g_pre, g_post = mix_pre_g[l][None, :], mix_post_g[l][None, :]
        f_pre, f_post = ffn_pre_g[l][None, :], ffn_post_g[l][None, :]

        nxt = l + 1 if l + 1 < DEPTH else None
        side = (lambda *ws: ()) if nxt is None else (lambda *ws: tuple((w, nxt) for w in ws))

        h, pq, pkv, kr = _inproj_call(y, mod, g_pre, w_main, w_kr, l, _ctx_seg, True, "ctx_inproj")
        o, *new_caches = _ctx_attn_call(pq, pkv, kr, small, new_caches, l, lam_init)
        mixed, *nxt_a = _merge_call(h, o, wg, wb, "ctx_merge", side(w_mix_gate, w_branch))
        y, h2, *nxt_b = _outproj_call(mixed, y, mod, g_post, f_pre, wo, _ctx_seg, "ctx_outproj",
                                      side(w_out, w_ffn_gate))
        y = _ffn_call(y, h2, mod, f_post, wfg, wfu, wfd, _ctx_seg, "ctx_ffn")

        h, p, kr = _inproj_call(z, mod, g_pre, w_main, w_kr, l, _lat_seg, False, "lat_inproj")
        o = _lat_attn_call(p, kr, caches, tables, bias_all, small, l, lam_init)
        mixed, *nxt_c = _merge_call(h, o, wg, wb, "lat_merge", side(w_ffn_down))
        z, h2, *nxt_d = _outproj_call(mixed, z, mod, g_post, f_pre, wo, _lat_seg, "lat_outproj", side(w_ffn_up))
        z = _ffn_call(z, h2, mod, f_post, wfg, wfu, wfd, _lat_seg, "lat_ffn")
        if nxt is not None:
            (wg, wb), (wo, wfg), (wfd,), (wfu,) = nxt_a, nxt_b, nxt_c, nxt_d

    new_caches = [a.reshape((BATCH, DEPTH, SEQ) + t) for a, t in zip(new_caches, CACHE_TAILS)]
    return (y.reshape(BATCH, SEQ, D_MODEL), z.reshape(DEC_BATCH, DEC_SEQ, D_MODEL), *new_caches)
```

```python
import functools
import math

import numpy as np
import jax
import jax.numpy as jnp
from jax import lax
from jax.experimental import pallas as pl
from jax.experimental.pallas import tpu as pltpu

D_MODEL = 2048
BATCH = 32
SEQ = 256
DEPTH = 2
DEC_BATCH = 8
DEC_SEQ = 1024
PAST_LEN = 512
GRID_W = 64
HEAD_DIM = 128
ROPE_BASE = 10000.0
EPS = 1e-6
NEG_INF = -1e30
LOG2E = math.log2(math.e)

MLA_HEADS = 4
MLA_Q_RANK = 384
MLA_KV_RANK = 128
MLA_NOPE = 128
MLA_ROPE = 64
MLA_V = 128
SWA_HEADS = 4
SWA_KV_HEADS = 2
SWA_WINDOW = 128
NA_HEADS = 4
NA_ROWS = 8
NA_COLS = 16
DIFF_HEADS = 4
DIFF_QK = 64
DIFF_V = 128
N_BRANCH = 4
BRANCH_W = 512
FFN_HIDDEN = 5632

N_CTX_TOK = BATCH * SEQ
N_LAT_TOK = DEC_BATCH * DEC_SEQ
GRID_ROWS = DEC_SEQ // GRID_W

Q_DOWN0, CKV0, QB0, KB0, VB0, QC0, KC0, VC0, QD0, KD0, VD0 = (
    0, 384, 512, 1024, 1280, 1536, 2048, 2560, 3072, 3584, 4096)
KROPE_SRC0 = MLA_Q_RANK + MLA_KV_RANK
P_WIDTH = 4608
MLA_QN_W = MLA_HEADS * MLA_NOPE
MLA_KV_W = MLA_NOPE + MLA_V
SWA_KV_W = SWA_KV_HEADS * HEAD_DIM
Q_PIECES = ((Q_DOWN0, MLA_Q_RANK), (QB0, BRANCH_W), (QC0, BRANCH_W), (QD0, BRANCH_W))
KV_PIECES = ((CKV0, MLA_KV_RANK), (KB0, 2 * SWA_KV_W), (KC0, 2 * BRANCH_W), (KD0, 2 * BRANCH_W))
Q_WIDTH = sum(w for _, w in Q_PIECES)
KV_WIDTH = sum(w for _, w in KV_PIECES)
CQ_DOWN0, CQB0, CQC0, CQD0 = 0, MLA_Q_RANK, MLA_Q_RANK + BRANCH_W, MLA_Q_RANK + 2 * BRANCH_W
CCKV0, CKB0 = 0, MLA_KV_RANK
CVB0 = CKB0 + SWA_KV_W
CKC0 = CVB0 + SWA_KV_W
CVC0 = CKC0 + BRANCH_W
CKD0 = CVC0 + BRANCH_W
CVD0 = CKD0 + BRANCH_W
N_MOD_ROWS = 16

CACHE_TAILS = ((MLA_KV_RANK,), (MLA_ROPE,), (SWA_KV_HEADS, HEAD_DIM), (SWA_KV_HEADS, HEAD_DIM),
               (NA_HEADS, HEAD_DIM), (NA_HEADS, HEAD_DIM), (DIFF_HEADS, 2 * DIFF_QK), (DIFF_HEADS, DIFF_V))

VMEM_LIMIT = 56 * 1024 * 1024
VMEM_LIMIT_BIG = 62 * 1024 * 1024
LANES = 128

F32 = jnp.float32
BF16 = jnp.bfloat16


def _lambda_init(l):
    return 0.8 - 0.6 * math.exp(-0.3 * l)


def _rms(x):
    return x * lax.rsqrt(jnp.mean(x * x, axis=-1, keepdims=True) + EPS)


def _head_cols(h, base=0):
    return slice(base + h * HEAD_DIM, base + (h + 1) * HEAD_DIM)


def _dot(a, b):
    return jnp.dot(a, b, preferred_element_type=F32)


def _dot_nt(a, b):
    return lax.dot_general(a, b, (((1,), (1,)), ((), ())), preferred_element_type=F32)


def _attend(pieces, sink=None):
    m = None
    for s, _ in pieces:
        mi = jnp.max(s, axis=-1, keepdims=True)
        m = mi if m is None else jnp.maximum(m, mi)
    if sink is not None:
        sink = sink * LOG2E
        m = jnp.maximum(m, sink)
    den = None
    num = None
    for s, v in pieces:
        e = jnp.exp2(s - m)
        d = jnp.sum(e, axis=-1, keepdims=True)
        n = _dot(e.astype(BF16), v)
        den = d if den is None else den + d
        num = n if num is None else num + n
    if sink is not None:
        den = den + jnp.exp2(sink - m)
    return num / den


def _diff_attend(pieces1, pieces2, values, lam):
    def exps(pieces):
        m = None
        for s in pieces:
            mi = jnp.max(s, axis=-1, keepdims=True)
            m = mi if m is None else jnp.maximum(m, mi)
        es = [jnp.exp2(s - m) for s in pieces]
        den = None
        for e in es:
            d = jnp.sum(e, axis=-1, keepdims=True)
            den = d if den is None else den + d
        return es, den

    es1, den1 = exps(pieces1)
    es2, den2 = exps(pieces2)
    ratio = lam * den1 / den2
    out = None
    for e1, e2, v in zip(es1, es2, values):
        n = _dot((e1 - e2 * ratio).astype(BF16), v)
        out = n if out is None else out + n
    return out / den1


def _diff_lambda(lq1, lk1, lq2, lk2, lam_init):
    a = jnp.sum(lq1[...] * lk1[...], axis=-1, keepdims=True)
    b = jnp.sum(lq2[...] * lk2[...], axis=-1, keepdims=True)
    return jnp.exp(a) - jnp.exp(b) + lam_init


def _ada_kernel(c_ref, w_ref, b_ref, o_ref):
    c = c_ref[...]
    s = (c * jax.nn.sigmoid(c)).astype(BF16)
    o_ref[0] = _dot(s, w_ref[0].astype(BF16)) + b_ref[0]


def _ada_call(cvec, w_ada, b_ada):
    tn = 1024
    n = 6 * D_MODEL
    return pl.pallas_call(
        _ada_kernel,
        grid=(DEPTH, n // tn),
        in_specs=[
            pl.BlockSpec((N_MOD_ROWS, D_MODEL), lambda l, j: (0, 0)),
            pl.BlockSpec((1, D_MODEL, tn), lambda l, j: (l, 0, j)),
            pl.BlockSpec((1, 1, tn), lambda l, j: (l, 0, j)),
        ],
        out_specs=pl.BlockSpec((1, N_MOD_ROWS, tn), lambda l, j: (l, 0, j)),
        out_shape=jax.ShapeDtypeStruct((DEPTH, N_MOD_ROWS, n), F32),
        compiler_params=pltpu.CompilerParams(
            dimension_semantics=("arbitrary", "arbitrary"), vmem_limit_bytes=VMEM_LIMIT),
        name="ada_mod",
    )(cvec, w_ada, b_ada.reshape(DEPTH, 1, n))


NBR_PAIRS = 2 * NA_ROWS


def _nbr_bias_kernel(rpb_ref, o_ref):
    l = pl.program_id(0)
    h = pl.program_id(1)
    n_dr = 2 * NA_ROWS - 1
    n_dc = 2 * NA_COLS - 1
    base = (l * NA_HEADS + h) * (n_dr * n_dc)
    wq = lax.broadcasted_iota(jnp.int32, (GRID_W, LANES), 0)
    lane = lax.broadcasted_iota(jnp.int32, (GRID_W, LANES), 1)
    upper = lane >= GRID_W
    wk = jnp.where(upper, lane - GRID_W, lane)
    delta = wk - wq + (NA_COLS - 1)
    cs = jnp.clip(wq - NA_COLS // 2, 0, GRID_W - NA_COLS)
    valid = (wk >= cs) & (wk < cs + NA_COLS)
    for e in range(NBR_PAIRS):
        dr_lo, dr_hi = max(e - 1, 0), min(e, n_dr - 1)
        acc = jnp.full((GRID_W, LANES), NEG_INF, F32)
        for d in range(n_dc):
            lo = rpb_ref[base + dr_lo * n_dc + d]
            hi = rpb_ref[base + dr_hi * n_dc + d]
            acc = jnp.where(delta == d, jnp.where(upper, hi, lo), acc)
        o_ref[0, 0, e] = jnp.where(valid, acc * LOG2E, NEG_INF)


def _nbr_bias_call(na_rpb):
    n_pairs = NBR_PAIRS
    return pl.pallas_call(
        _nbr_bias_kernel,
        grid=(DEPTH, NA_HEADS),
        in_specs=[pl.BlockSpec(memory_space=pltpu.SMEM)],
        out_specs=pl.BlockSpec((1, 1, n_pairs, GRID_W, LANES), lambda l, h: (l, h, 0, 0, 0)),
        out_shape=jax.ShapeDtypeStruct((DEPTH, NA_HEADS, n_pairs, GRID_W, LANES), F32),
        compiler_params=pltpu.CompilerParams(dimension_semantics=("arbitrary", "arbitrary")),
        name="nbr_bias",
    )(na_rpb.reshape(-1))


def _inproj_kernel(x_ref, mod_ref, g_ref, w_ref, wkr_ref, h_ref, *p_refs_kr):
    *p_refs, kr_ref = p_refs_kr
    m = mod_ref[0]
    half = x_ref.shape[0] // 2
    for r in (slice(0, half), slice(half, 2 * half)):
        h = (_rms(x_ref[r, :]) * g_ref[...] * (1.0 + m[1:2]) + m[0:1]).astype(BF16)
        h_ref[r, :] = h
        kr_ref[r, :] = _dot_nt(h, wkr_ref[...])
        p = _dot_nt(h, w_ref[...])
        if len(p_refs) == 1:
            p_refs[0][r, :] = p.astype(p_refs[0].dtype)
        else:
            for ref, pieces in zip(p_refs, (Q_PIECES, KV_PIECES)):
                dst = 0
                for src, width in pieces:
                    ref[r, dst:dst + width] = p[:, src:src + width].astype(ref.dtype)
                    dst += width


def _inproj_call(x, mod, g, w_main, w_kr, layer, seg, split, name):
    n_tok = x.shape[0]
    tm = 512
    p_outs = ((Q_WIDTH, BF16), (KV_WIDTH, F32)) if split else ((P_WIDTH, BF16),)
    return pl.pallas_call(
        _inproj_kernel,
        grid=(n_tok // tm,),
        in_specs=[
            pl.BlockSpec((tm, D_MODEL), lambda i: (i, 0)),
            pl.BlockSpec((1, 6, D_MODEL), lambda i: (seg(i, tm), 0, 0)),
            pl.BlockSpec((1, D_MODEL), lambda i: (0, 0)),
            pl.BlockSpec((None, P_WIDTH, D_MODEL), lambda i: (layer, 0, 0)),
            pl.BlockSpec((None, LANES, D_MODEL), lambda i: (layer, 0, 0)),
        ],
        out_specs=[pl.BlockSpec((tm, D_MODEL), lambda i: (i, 0))]
        + [pl.BlockSpec((tm, w), lambda i: (i, 0)) for w, _ in p_outs]
        + [pl.BlockSpec((tm, LANES), lambda i: (i, 0))],
        out_shape=[jax.ShapeDtypeStruct((n_tok, D_MODEL), BF16)]
        + [jax.ShapeDtypeStruct((n_tok, w), dt) for w, dt in p_outs]
        + [jax.ShapeDtypeStruct((n_tok, LANES), F32)],
        compiler_params=pltpu.CompilerParams(
            dimension_semantics=("arbitrary",), vmem_limit_bytes=VMEM_LIMIT_BIG),
        name=name,
    )(x, mod, g, w_main, w_kr)


def _cast_kernel(x_ref, o_ref):
    o_ref[...] = x_ref[...].astype(BF16)


def _cast_call(w, block_rows, name):
    w2 = w.reshape(-1, w.shape[-1])
    rows, cols = w2.shape[0] // w.shape[0], w2.shape[1]
    out = pl.pallas_call(
        _cast_kernel,
        grid=(rows // block_rows,),
        in_specs=[pl.BlockSpec((block_rows, cols), lambda i: (i, 0))],
        out_specs=pl.BlockSpec((block_rows, cols), lambda i: (i, 0)),
        out_shape=jax.ShapeDtypeStruct((rows, cols), BF16),
        compiler_params=pltpu.CompilerParams(dimension_semantics=("arbitrary",), vmem_limit_bytes=VMEM_LIMIT),
        name=name,
    )(w2)
    return out.reshape((1,) + w.shape[1:])


def _side_specs(side, n_grid):
    in_specs, out_specs, out_shape = [], [], []
    for w, layer, block, index in side:
        if n_grid == 1:
            in_specs.append(pl.BlockSpec((None,) + block, lambda i, f=index, l=layer: (l,) + f(i)))
            out_specs.append(pl.BlockSpec((None,) + block, lambda i, f=index: (0,) + f(i)))
        else:
            in_specs.append(pl.BlockSpec((None,) + block, lambda i, j, f=index, l=layer: (l,) + f(i, j)))
            out_specs.append(pl.BlockSpec((None,) + block, lambda i, j, f=index: (0,) + f(i, j)))
        out_shape.append(jax.ShapeDtypeStruct((1,) + w.shape[1:], BF16))
    return in_specs, out_specs, out_shape


def _side_cast(srcs, dsts):
    for src, dst in zip(srcs, dsts):
        dst[...] = src[...].astype(BF16)


def _cast_w_in_kernel(x_ref, xkr_ref, main_ref, kr_ref):
    main_ref[...] = x_ref[0].astype(BF16)

    @pl.when(pl.program_id(1) == 0)
    def _():
        kr_ref[:MLA_ROPE, :] = xkr_ref[...].astype(BF16)
        kr_ref[MLA_ROPE:, :] = jnp.zeros((LANES - MLA_ROPE, D_MODEL), BF16)


def _cast_w_in_call(w_in):
    w_t = jnp.swapaxes(w_in, 1, 2)
    block_rows = 512
    src_row = lambda l, j: (l, pl.multiple_of(j * block_rows + jnp.where(j * block_rows >= KROPE_SRC0, MLA_ROPE, 0), MLA_ROPE), 0)
    return pl.pallas_call(
        _cast_w_in_kernel,
        grid=(DEPTH, P_WIDTH // block_rows),
        in_specs=[pl.BlockSpec((pl.Element(1), pl.Element(block_rows), pl.Element(D_MODEL)), src_row),
                  pl.BlockSpec((None, MLA_ROPE, D_MODEL), lambda l, j: (l, KROPE_SRC0 // MLA_ROPE, 0))],
        out_specs=[pl.BlockSpec((None, block_rows, D_MODEL), lambda l, j: (l, j, 0)),
                   pl.BlockSpec((None, LANES, D_MODEL), lambda l, j: (l, 0, 0))],
        out_shape=[jax.ShapeDtypeStruct((DEPTH, P_WIDTH, D_MODEL), BF16),
                   jax.ShapeDtypeStruct((DEPTH, LANES, D_MODEL), BF16)],
        compiler_params=pltpu.CompilerParams(
            dimension_semantics=("arbitrary", "arbitrary"), vmem_limit_bytes=VMEM_LIMIT),
        name="cast_w_in",
    )(w_t, w_t)


CTX_BATCHES = 2


def _mla_queries(q_down, qn_g, w_uq):
    cq = _rms(q_down) * qn_g[...]
    return _dot(cq.astype(BF16), w_uq[...])


def _ctx_attn_kernel(*refs, lam_init, first):
    (pq_ref, pkv_ref, kr_ref, qn_g, w_uq, kvn_g, w_ukv, sink_ref, lq1, lk1, lq2, lk2, subln_g) = refs[:13]
    n_in = 13 + (0 if first else len(CACHE_TAILS))
    o_ref = refs[n_in]
    ckv_o, kr_o, kb_o, vb_o, kc_o, vc_o, kd_o, vd_o = refs[n_in + 1:]
    ls = 0
    outs = (ckv_o, kr_o, kb_o, vb_o, kc_o, vc_o, kd_o, vd_o)
    if first:
        for dst in outs:
            dst[:, 1:] = jnp.zeros(dst.shape, F32)[:, 1:]
    for bb in range(CTX_BATCHES):
        _ctx_attn_batch(bb, slice(bb * SEQ, (bb + 1) * SEQ), pq_ref, pkv_ref, kr_ref, qn_g, w_uq, kvn_g, w_ukv,
                        sink_ref, lq1, lk1, lq2, lk2, subln_g, o_ref, outs, ls, lam_init)


def _ctx_attn_batch(bb, rows, pq_ref, pkv_ref, kr_ref, qn_g, w_uq, kvn_g, w_ukv, sink_ref, lq1, lk1, lq2, lk2,
                    subln_g, o_ref, outs, ls, lam_init):
    ckv_o, kr_o, kb_o, vb_o, kc_o, vc_o, kd_o, vd_o = outs

    scale_a = LOG2E * (MLA_NOPE + MLA_ROPE) ** -0.5
    qa = _mla_queries(pq_ref[rows, CQ_DOWN0:CQ_DOWN0 + MLA_Q_RANK].astype(F32), qn_g, w_uq) * scale_a
    ckv = _rms(pkv_ref[rows, CCKV0:CCKV0 + MLA_KV_RANK]) * kvn_g[...]
    ckv_o[bb, ls] = ckv
    kr = kr_ref[rows, :]
    kr_o[bb, ls] = kr[:, :MLA_ROPE]
    kr_b = kr.astype(BF16)
    kv = _dot(ckv.astype(BF16), w_ukv[...])
    for h in range(MLA_HEADS):
        q = jnp.concatenate([qa[:, _head_cols(h)], qa[:, _head_cols(h, MLA_QN_W)]],
                            axis=1).astype(BF16)
        kv0 = h * MLA_KV_W
        k = jnp.concatenate([kv[:, kv0:kv0 + MLA_NOPE].astype(BF16), kr_b], axis=1)
        v = kv[:, kv0 + MLA_NOPE:kv0 + MLA_KV_W].astype(BF16)
        o = _attend([(_dot_nt(q, k), v)])
        o_ref[rows, _head_cols(h)] = o.astype(BF16)

    scale = LOG2E * HEAD_DIM ** -0.5

    def qhead(col0, h):
        return pq_ref[rows, _head_cols(h, col0)].astype(F32)

    def head(col0, h):
        return pkv_ref[rows, _head_cols(h, col0)]

    g_size = SWA_HEADS // SWA_KV_HEADS
    for g in range(SWA_KV_HEADS):
        k = head(CKB0, g)
        v = head(CVB0, g)
        kb_o[bb, ls, pl.ds(g, SEQ, stride=SWA_KV_HEADS), :] = k
        vb_o[bb, ls, pl.ds(g, SEQ, stride=SWA_KV_HEADS), :] = v
        k = k.astype(BF16)
        v = v.astype(BF16)
        for h in range(g * g_size, (g + 1) * g_size):
            q = (qhead(CQB0, h) * scale).astype(BF16)
            o = _attend([(_dot_nt(q, k), v)], sink=sink_ref[h])
            o_ref[rows, _head_cols(h, BRANCH_W)] = o.astype(BF16)

    for h in range(NA_HEADS):
        k = head(CKC0, h)
        v = head(CVC0, h)
        kc_o[bb, ls, pl.ds(h, SEQ, stride=NA_HEADS), :] = k
        vc_o[bb, ls, pl.ds(h, SEQ, stride=NA_HEADS), :] = v
        q = (qhead(CQC0, h) * scale).astype(BF16)
        o = _attend([(_dot_nt(q, k.astype(BF16)), v.astype(BF16))])
        o_ref[rows, _head_cols(h, 2 * BRANCH_W)] = o.astype(BF16)

    lam = _diff_lambda(lq1, lk1, lq2, lk2, lam_init)
    scale_d = LOG2E * DIFF_QK ** -0.5
    lo = lax.broadcasted_iota(jnp.int32, (SEQ, 2 * DIFF_QK), 1) < DIFF_QK
    for h in range(DIFF_HEADS):
        k = head(CKD0, h)
        v = head(CVD0, h)
        kd_o[bb, ls, pl.ds(h, SEQ, stride=DIFF_HEADS), :] = k
        vd_o[bb, ls, pl.ds(h, SEQ, stride=DIFF_HEADS), :] = v
        k = k.astype(BF16)
        v = v.astype(BF16)
        q = qhead(CQD0, h) * scale_d
        o1 = _attend([(_dot_nt(jnp.where(lo, q, 0.0).astype(BF16), k), v)])
        o2 = _attend([(_dot_nt(jnp.where(lo, 0.0, q).astype(BF16), k), v)])
        o = _rms(o1 - lam * o2) * subln_g[...] * (1.0 - lam_init)
        o_ref[rows, _head_cols(h, 3 * BRANCH_W)] = o.astype(BF16)


def _small_specs():
    zero2 = (lambda *a: (0, 0))
    return [
        pl.BlockSpec((1, MLA_Q_RANK), zero2),
        pl.BlockSpec((MLA_Q_RANK, 2 * MLA_QN_W), zero2),
        pl.BlockSpec((1, MLA_KV_RANK), zero2),
        pl.BlockSpec((MLA_KV_RANK, MLA_HEADS * MLA_KV_W), zero2),
        pl.BlockSpec(memory_space=pltpu.SMEM),
        pl.BlockSpec((1, DIFF_QK), zero2),
        pl.BlockSpec((1, DIFF_QK), zero2),
        pl.BlockSpec((1, DIFF_QK), zero2),
        pl.BlockSpec((1, DIFF_QK), zero2),
        pl.BlockSpec((1, DIFF_V), zero2),
    ]


def _ctx_attn_call(pq, pkv, kr, small, prev, layer, lam_init):
    first = not prev
    row = lambda b: (b, 0)
    rows = CTX_BATCHES * SEQ

    def flat(tail):
        return (SEQ * tail[0], tail[1]) if len(tail) == 2 else (SEQ,) + tail

    def cache_spec(tail):
        if first:
            return pl.BlockSpec((CTX_BATCHES, DEPTH) + flat(tail), lambda b: (b, 0, 0, 0))
        return pl.BlockSpec((CTX_BATCHES, 1) + flat(tail), lambda b: (b, layer, 0, 0))

    n_fixed = 13
    return pl.pallas_call(
        functools.partial(_ctx_attn_kernel, lam_init=lam_init, first=first),
        grid=(BATCH // CTX_BATCHES,),
        in_specs=[pl.BlockSpec((rows, Q_WIDTH), row), pl.BlockSpec((rows, KV_WIDTH), row),
                  pl.BlockSpec((rows, LANES), row)] + _small_specs()
        + [pl.BlockSpec(memory_space=pl.ANY) for _ in prev],
        out_specs=[pl.BlockSpec((rows, D_MODEL), row)] + [cache_spec(t) for t in CACHE_TAILS],
        out_shape=[jax.ShapeDtypeStruct((N_CTX_TOK, D_MODEL), BF16)]
        + [jax.ShapeDtypeStruct((BATCH, DEPTH) + flat(t), F32) for t in CACHE_TAILS],
        input_output_aliases={n_fixed + k: 1 + k for k in range(len(prev))},
        compiler_params=pltpu.CompilerParams(
            dimension_semantics=("arbitrary",), vmem_limit_bytes=VMEM_LIMIT),
        name="ctx_attn",
    )(pq, pkv, kr, *small, *prev)


LAT_TQ = 256
SWA_PAD = SWA_WINDOW
NBR_WIN_ROWS = 12


def _rope_pair64(x, c_ref, sa_ref, sb_ref, rows):
    out = []
    for j in range(x.shape[1] // LANES):
        xj = x[:, j * LANES:(j + 1) * LANES]
        out.append(xj * c_ref[rows, :] + pltpu.roll(xj, 96, 1) * sa_ref[rows, :]
                   + pltpu.roll(xj, 32, 1) * sb_ref[rows, :])
    return jnp.concatenate(out, axis=1) if len(out) > 1 else out[0]


def _rope128(x, c_ref, s_ref, rows):
    out = []
    for j in range(x.shape[1] // LANES):
        xj = x[:, j * LANES:(j + 1) * LANES]
        out.append(xj * c_ref[rows, :] + pltpu.roll(xj, 64, 1) * s_ref[rows, :])
    return jnp.concatenate(out, axis=1) if len(out) > 1 else out[0]


def _lat_attn_kernel(p_ref, kr_ref, c_ckv, c_kr, c_kb, c_vb, c_kc, c_vc, c_kd, c_vd,
                     cos128, sin128, cos64, sina64, sinb64, bias_ref,
                     qn_g, w_uq, kvn_g, w_ukv, sink_ref, lq1, lk1, lq2, lk2, subln_g,
                     o_ref,
                     ka_s, va_s, kbc_s, vbc_s, kbp_s, vbp_s, kcc_s, vcc_s, kdc_s, vdc_s, kdo_s,
                     *, lam_init):
    def cached_head(ref, h, n_heads):
        return ref[pl.ds(h, PAST_LEN, stride=n_heads), :].astype(BF16)

    qi = pl.program_id(1)
    all_rows = slice(None)
    n_own = DEC_SEQ

    @pl.when(qi == 0)
    def _prologue():
        ckv_own = (_rms(p_ref[:, CKV0:CKV0 + MLA_KV_RANK].astype(F32)) * kvn_g[...]).astype(BF16)
        ckv_all = jnp.concatenate([c_ckv[...].astype(BF16), ckv_own], axis=0)
        kr_ctx = jnp.concatenate([c_kr[...], jnp.zeros((PAST_LEN, LANES - MLA_ROPE), F32)], axis=1)
        kr_own = _rope_pair64(kr_ref[...], cos64, sina64, sinb64, all_rows)
        kr_all = jnp.concatenate([kr_ctx, kr_own], axis=0).astype(BF16)
        for h in range(MLA_HEADS):
            kvh = _dot(ckv_all, w_ukv[:, h * MLA_KV_W:(h + 1) * MLA_KV_W])
            ka_s[h, :, 0:MLA_NOPE] = kvh[:, 0:MLA_NOPE].astype(BF16)
            ka_s[h, :, MLA_NOPE:MLA_NOPE + LANES] = kr_all
            va_s[:, _head_cols(h)] = kvh[:, MLA_NOPE:MLA_KV_W].astype(BF16)
        for g in range(SWA_KV_HEADS):
            kbc_s[:, _head_cols(g)] = cached_head(c_kb, g, SWA_KV_HEADS)
            vbc_s[:, _head_cols(g)] = cached_head(c_vb, g, SWA_KV_HEADS)
        zpad = jnp.zeros((SWA_PAD, SWA_KV_W), BF16)
        kbp_s[0:SWA_PAD, :] = zpad
        kbp_s[SWA_PAD + n_own:, :] = zpad
        vbp_s[0:SWA_PAD, :] = zpad
        vbp_s[SWA_PAD + n_own:, :] = zpad
        kbp_s[SWA_PAD:SWA_PAD + n_own, :] = _rope128(
            p_ref[:, KB0:KB0 + SWA_KV_W].astype(F32), cos128, sin128, all_rows).astype(BF16)
        vbp_s[SWA_PAD:SWA_PAD + n_own, :] = p_ref[:, VB0:VB0 + SWA_KV_W]
        for h in range(NA_HEADS):
            cols = _head_cols(h)
            kcc_s[:, cols] = cached_head(c_kc, h, NA_HEADS)
            vcc_s[:, cols] = cached_head(c_vc, h, NA_HEADS)
            kdc_s[:, cols] = cached_head(c_kd, h, DIFF_HEADS)
            vdc_s[:, cols] = cached_head(c_vd, h, DIFF_HEADS)
        kdo_s[...] = _rope_pair64(
            p_ref[:, KD0:KD0 + BRANCH_W].astype(F32), cos64, sina64, sinb64, all_rows).astype(BF16)

    q0 = pl.multiple_of(qi * LAT_TQ, LAT_TQ)
    rows = pl.ds(q0, LAT_TQ)

    scale_a = LOG2E * (MLA_NOPE + MLA_ROPE) ** -0.5
    qa = _mla_queries(p_ref[rows, Q_DOWN0:Q_DOWN0 + MLA_Q_RANK].astype(F32), qn_g, w_uq) * scale_a
    qa_r = _rope_pair64(qa[:, MLA_QN_W:2 * MLA_QN_W], cos64, sina64, sinb64, rows)
    for h in range(MLA_HEADS):
        q = jnp.concatenate([qa[:, _head_cols(h)], qa_r[:, _head_cols(h)]],
                            axis=1).astype(BF16)
        o = _attend([(_dot_nt(q, ka_s[h]), va_s[:, _head_cols(h)])])
        o_ref[:, _head_cols(h)] = o.astype(BF16)

    scale = LOG2E * HEAD_DIM ** -0.5
    qb = (_rope128(p_ref[rows, QB0:QB0 + BRANCH_W].astype(F32), cos128, sin128, rows) * scale).astype(BF16)
    n_loc = LAT_TQ + 2 * SWA_PAD
    qpos = lax.broadcasted_iota(jnp.int32, (LAT_TQ, n_loc), 0)
    kpos = lax.broadcasted_iota(jnp.int32, (LAT_TQ, n_loc), 1) - SWA_PAD
    kabs = kpos + q0
    valid = (jnp.abs(qpos - kpos) <= SWA_WINDOW) & (kabs >= 0) & (kabs < n_own)
    loc_rows = pl.ds(q0, n_loc)
    g_size = SWA_HEADS // SWA_KV_HEADS
    for h in range(SWA_HEADS):
        g = h // g_size
        q = qb[:, _head_cols(h)]
        s_ctx = _dot_nt(q, kbc_s[:, _head_cols(g)])
        s_loc = jnp.where(valid, _dot_nt(q, kbp_s[loc_rows, _head_cols(g)]), NEG_INF)
        o = _attend([(s_ctx, vbc_s[:, _head_cols(g)]),
                     (s_loc, vbp_s[loc_rows, _head_cols(g)])], sink=sink_ref[h])
        o_ref[:, _head_cols(h, BRANCH_W)] = o.astype(BF16)

    q_rows = LAT_TQ // GRID_W
    ws = jnp.clip(qi * q_rows - NA_ROWS // 2, 0, GRID_ROWS - NBR_WIN_ROWS)
    ws = (ws // 2) * 2
    krows = pl.ds(pl.multiple_of(ws * GRID_W, 2 * GRID_W), NBR_WIN_ROWS * GRID_W)
    lower = lax.broadcasted_iota(jnp.int32, (GRID_W, LANES), 1) < GRID_W
    for h in range(NA_HEADS):
        cols = _head_cols(h)
        bias_rows = []
        for t in range(q_rows):
            r = qi * q_rows + t
            off = jnp.clip(r - NA_ROWS // 2, 0, GRID_ROWS - NA_ROWS) - r + (NA_ROWS - 1)
            blocks = []
            for j in range(NBR_WIN_ROWS // 2):
                dr = ws + 2 * j - r + (NA_ROWS - 1)
                ok_lo = ((dr >= off) & (dr < off + NA_ROWS)).astype(jnp.int32)
                ok_hi = ((dr + 1 >= off) & (dr + 1 < off + NA_ROWS)).astype(jnp.int32)
                ok = jnp.where(lower, ok_lo, ok_hi) > 0
                blk = bias_ref[0, h, jnp.clip(dr + 1, 0, NBR_PAIRS - 1)]
                blocks.append(jnp.where(ok, blk, NEG_INF))
            bias_rows.append(jnp.concatenate(blocks, axis=1))
        bias = jnp.concatenate(bias_rows, axis=0)
        q = (p_ref[rows, _head_cols(h, QC0)].astype(F32) * scale).astype(BF16)
        s_loc = _dot_nt(q, p_ref[krows, _head_cols(h, KC0)]) + bias
        s_ctx = _dot_nt(q, kcc_s[:, cols])
        o = _attend([(s_ctx, vcc_s[:, cols]), (s_loc, p_ref[krows, _head_cols(h, VC0)])])
        o_ref[:, _head_cols(h, 2 * BRANCH_W)] = o.astype(BF16)

    scale_d = LOG2E * DIFF_QK ** -0.5
    qd = _rope_pair64(p_ref[rows, QD0:QD0 + BRANCH_W].astype(F32), cos64, sina64, sinb64, rows) * scale_d
    lam = _diff_lambda(lq1, lk1, lq2, lk2, lam_init)
    lo = lax.broadcasted_iota(jnp.int32, (LAT_TQ, 2 * DIFF_QK), 1) < DIFF_QK
    for h in range(DIFF_HEADS):
        cols = _head_cols(h)
        q = qd[:, cols]
        v_own = p_ref[:, _head_cols(h, VD0)]
        q1 = jnp.where(lo, q, 0.0).astype(BF16)
        q2 = jnp.where(lo, 0.0, q).astype(BF16)
        o = _diff_attend([_dot_nt(q1, kdc_s[:, cols]), _dot_nt(q1, kdo_s[:, cols])],
                         [_dot_nt(q2, kdc_s[:, cols]), _dot_nt(q2, kdo_s[:, cols])],
                         [vdc_s[:, cols], v_own], lam)
        o = _rms(o) * subln_g[...] * (1.0 - lam_init)
        o_ref[:, _head_cols(h, 3 * BRANCH_W)] = o.astype(BF16)


def _lat_attn_call(p, kr, caches, tables, bias, small, layer, lam_init):
    nq = DEC_SEQ // LAT_TQ
    batch_rows = lambda b, q: (b, 0)
    zero2 = lambda b, q: (0, 0)
    once = pl.Buffered(1)
    cache_args = [a.reshape(a.shape[:2] + (-1, a.shape[-1])) for a in caches]
    cache_specs = [pl.BlockSpec((None, None) + a.shape[2:], lambda b, q: (b, layer, 0, 0)) for a in cache_args]
    table_specs = [pl.BlockSpec((DEC_SEQ, LANES), zero2) for _ in tables]
    bias_spec = pl.BlockSpec((1, NA_HEADS, NBR_PAIRS, GRID_W, LANES), lambda b, q: (layer, 0, 0, 0, 0))
    n_all = PAST_LEN + DEC_SEQ
    n_pad = DEC_SEQ + 2 * SWA_PAD
    scratch = [
        pltpu.VMEM((MLA_HEADS, n_all, MLA_NOPE + LANES), BF16),
        pltpu.VMEM((n_all, BRANCH_W), BF16),
        pltpu.VMEM((PAST_LEN, SWA_KV_W), BF16), pltpu.VMEM((PAST_LEN, SWA_KV_W), BF16),
        pltpu.VMEM((n_pad, SWA_KV_W), BF16), pltpu.VMEM((n_pad, SWA_KV_W), BF16),
        pltpu.VMEM((PAST_LEN, BRANCH_W), BF16), pltpu.VMEM((PAST_LEN, BRANCH_W), BF16),
        pltpu.VMEM((PAST_LEN, BRANCH_W), BF16), pltpu.VMEM((PAST_LEN, BRANCH_W), BF16),
        pltpu.VMEM((DEC_SEQ, BRANCH_W), BF16),
    ]
    return pl.pallas_call(
        functools.partial(_lat_attn_kernel, lam_init=lam_init),
        grid=(DEC_BATCH, nq),
        in_specs=[pl.BlockSpec((DEC_SEQ, P_WIDTH), batch_rows, pipeline_mode=once),
                  pl.BlockSpec((DEC_SEQ, LANES), batch_rows, pipeline_mode=once)]
        + cache_specs + table_specs + [bias_spec] + _small_specs(),
        out_specs=pl.BlockSpec((LAT_TQ, D_MODEL), lambda b, q: (b * nq + q, 0)),
        out_shape=jax.ShapeDtypeStruct((N_LAT_TOK, D_MODEL), BF16),
        scratch_shapes=scratch,
        compiler_params=pltpu.CompilerParams(
            dimension_semantics=("arbitrary", "arbitrary"), vmem_limit_bytes=VMEM_LIMIT_BIG),
        name="lat_attn",
    )(p, kr, *cache_args, *tables, bias, *small)


def _merge_kernel(h_ref, o_ref, wg_ref, wb_ref, *rest):
    n_side = (len(rest) - 1) // 2
    out_ref = rest[n_side]
    _side_cast(rest[:n_side], rest[n_side + 1:])
    acc = None
    for i in range(N_BRANCH):
        gate = jax.nn.sigmoid(_dot(h_ref[...], wg_ref[i]))
        term = gate * _dot(o_ref[:, i * BRANCH_W:(i + 1) * BRANCH_W], wb_ref[i])
        acc = term if acc is None else acc + term
    out_ref[...] = acc.astype(BF16)


def _merge_call(h, o, wg, wb, name, side_weights=()):
    n_tok = h.shape[0]
    tm, tn = 1024, 512
    grid = (n_tok // tm, D_MODEL // tn)
    side = []
    for w, layer in side_weights:
        lead = w.shape[1:-2]
        block = lead + (w.shape[-2] // grid[0], w.shape[-1] // grid[1])
        side.append((w, layer, block, lambda i, j, z=(0,) * len(lead): z + (i, j)))
    s_in, s_out, s_shape = _side_specs(side, 2)
    return pl.pallas_call(
        _merge_kernel,
        grid=grid,
        in_specs=[
            pl.BlockSpec((tm, D_MODEL), lambda i, j: (i, 0)),
            pl.BlockSpec((tm, D_MODEL), lambda i, j: (i, 0)),
            pl.BlockSpec((None, N_BRANCH, D_MODEL, tn), lambda i, j: (0, 0, 0, j)),
            pl.BlockSpec((None, N_BRANCH, BRANCH_W, tn), lambda i, j: (0, 0, 0, j)),
        ] + s_in,
        out_specs=[pl.BlockSpec((tm, tn), lambda i, j: (i, j))] + s_out,
        out_shape=[jax.ShapeDtypeStruct((n_tok, D_MODEL), BF16)] + s_shape,
        compiler_params=pltpu.CompilerParams(
            dimension_semantics=("arbitrary", "arbitrary"), vmem_limit_bytes=VMEM_LIMIT),
        name=name,
    )(h, o, wg, wb, *[w for w, _, _, _ in side])


def _outproj_kernel(mixed_ref, x_ref, mod_ref, g_ref, gffn_ref, w_ref, *rest):
    n_side = (len(rest) - 2) // 2
    xo_ref, h2_ref = rest[n_side:n_side + 2]
    _side_cast(rest[:n_side], rest[n_side + 2:])
    m = mod_ref[0]
    half = mixed_ref.shape[0] // 2
    for r in (slice(0, half), slice(half, 2 * half)):
        out = _dot(mixed_ref[r, :], w_ref[...])
        x = x_ref[r, :] + m[2:3] * (_rms(out) * g_ref[...])
        xo_ref[r, :] = x
        h2_ref[r, :] = (_rms(x) * gffn_ref[...] * (1.0 + m[4:5]) + m[3:4]).astype(BF16)


def _outproj_call(mixed, x, mod, g, g_ffn, w, seg, name, side_weights=()):
    n_tok = x.shape[0]
    tm = 512
    n_steps = n_tok // tm
    side = [(w_, layer, (w_.shape[1] // n_steps, w_.shape[2]), lambda i: (i, 0)) for w_, layer in side_weights]
    s_in, s_out, s_shape = _side_specs(side, 1)
    return pl.pallas_call(
        _outproj_kernel,
        grid=(n_steps,),
        in_specs=[
            pl.BlockSpec((tm, D_MODEL), lambda i: (i, 0)),
            pl.BlockSpec((tm, D_MODEL), lambda i: (i, 0)),
            pl.BlockSpec((1, 6, D_MODEL), lambda i: (seg(i, tm), 0, 0)),
            pl.BlockSpec((1, D_MODEL), lambda i: (0, 0)),
            pl.BlockSpec((1, D_MODEL), lambda i: (0, 0)),
            pl.BlockSpec((None, D_MODEL, D_MODEL), lambda i: (0, 0, 0)),
        ] + s_in,
        out_specs=[pl.BlockSpec((tm, D_MODEL), lambda i: (i, 0)), pl.BlockSpec((tm, D_MODEL), lambda i: (i, 0))]
        + s_out,
        out_shape=[jax.ShapeDtypeStruct((n_tok, D_MODEL), F32), jax.ShapeDtypeStruct((n_tok, D_MODEL), BF16)]
        + s_shape,
        compiler_params=pltpu.CompilerParams(
            dimension_semantics=("arbitrary",), vmem_limit_bytes=VMEM_LIMIT),
        name=name,
    )(mixed, x, mod, g, g_ffn, w, *[w_ for w_, _, _, _ in side])


def _ffn_kernel(x_ref, h2_ref, mod_ref, gpost_ref, wg_ref, wu_ref, wd_ref, xo_ref):
    j = pl.program_id(1)

    @pl.when(j == 0)
    def _():
        xo_ref[...] = jnp.zeros_like(xo_ref)

    a = _dot(h2_ref[...], wg_ref[...])
    b = _dot(h2_ref[...], wu_ref[...])
    t = (a * jax.nn.sigmoid(a) * b).astype(BF16)
    xo_ref[...] += _dot(t, wd_ref[...])

    @pl.when(j == pl.num_programs(1) - 1)
    def _():
        m = mod_ref[0]
        xo_ref[...] = x_ref[...] + m[5:6] * (_rms(xo_ref[...]) * gpost_ref[...])


def _ffn_call(x, h2, mod, gpost, wg, wu, wd, seg, name):
    n_tok = x.shape[0]
    tm, th = 1024, 256
    return pl.pallas_call(
        _ffn_kernel,
        grid=(n_tok // tm, FFN_HIDDEN // th),
        in_specs=[
            pl.BlockSpec((tm, D_MODEL), lambda i, j: (i, 0)),
            pl.BlockSpec((tm, D_MODEL), lambda i, j: (i, 0)),
            pl.BlockSpec((1, 6, D_MODEL), lambda i, j: (seg(i, tm), 0, 0)),
            pl.BlockSpec((1, D_MODEL), lambda i, j: (0, 0)),
            pl.BlockSpec((None, D_MODEL, th), lambda i, j: (0, 0, j)),
            pl.BlockSpec((None, D_MODEL, th), lambda i, j: (0, 0, j)),
            pl.BlockSpec((None, th, D_MODEL), lambda i, j: (0, j, 0)),
        ],
        out_specs=pl.BlockSpec((tm, D_MODEL), lambda i, j: (i, 0)),
        out_shape=jax.ShapeDtypeStruct((n_tok, D_MODEL), F32),
        compiler_params=pltpu.CompilerParams(
            dimension_semantics=("arbitrary", "arbitrary"), vmem_limit_bytes=VMEM_LIMIT_BIG),
        name=name,
    )(x, h2, mod, gpost, wg, wu, wd)


def _rope_tables():
    t = np.arange(DEC_SEQ)
    row = (t // GRID_W).astype(np.float32)
    col = (t % GRID_W).astype(np.float32)

    def half_angles(r):
        quarter = r // 4
        inv = (1.0 / (ROPE_BASE ** (np.arange(quarter, dtype=np.float32) / quarter))).astype(np.float32)
        return np.concatenate([row[:, None] * inv, col[:, None] * inv], axis=-1).astype(np.float32)

    a64 = half_angles(HEAD_DIM)
    cos128 = np.concatenate([np.cos(a64), np.cos(a64)], axis=1)
    sin128 = np.concatenate([-np.sin(a64), np.sin(a64)], axis=1)
    a32 = half_angles(MLA_ROPE)
    c, s, z = np.cos(a32), np.sin(a32), np.zeros_like(a32)
    cos64 = np.concatenate([c, c, c, c], axis=1)
    sina64 = np.concatenate([-s, z, -s, z], axis=1)
    sinb64 = np.concatenate([z, s, z, s], axis=1)
    return tuple(jnp.asarray(a, F32) for a in (cos128, sin128, cos64, sina64, sinb64))


def _ctx_seg(i, tm):
    del tm
    return i * 0


def _lat_seg(i, tm):
    return 1 + (i * tm) // DEC_SEQ


def kernel(x_prompt, x_sample, cache_mla_ckv, cache_mla_krope, cache_swa_k, cache_swa_v, cache_na_k, cache_na_v, cache_diff_k, cache_diff_v, c, c_ctx, w_ada, b_ada, mix_pre_g, mix_post_g, ffn_pre_g, ffn_post_g, w_in, mla_q_norm_g, mla_kv_norm_g, w_mla_uq, w_mla_ukv, swa_sink, na_rpb, diff_lq1, diff_lk1, diff_lq2, diff_lk2, diff_subln_g, w_mix_gate, w_branch, w_out, w_ffn_gate, w_ffn_up, w_ffn_down):
    cvec = jnp.concatenate([c_ctx[None, :], c, jnp.zeros((N_MOD_ROWS - 1 - DEC_BATCH, D_MODEL), F32)], axis=0)
    mod_all = _ada_call(cvec, w_ada, b_ada).reshape(DEPTH, N_MOD_ROWS, 6, D_MODEL)
    bias_all = _nbr_bias_call(na_rpb)
    tables = _rope_tables()
    caches = (cache_mla_ckv, cache_mla_krope, cache_swa_k, cache_swa_v, cache_na_k, cache_na_v,
              cache_diff_k, cache_diff_v)

    w_main, w_kr = _cast_w_in_call(w_in)
    wg = _cast_call(w_mix_gate, 512, "cast_w_gate")
    wb = _cast_call(w_branch, 512, "cast_w_branch")
    wo = _cast_call(w_out, 512, "cast_w_out")
    wfg = _cast_call(w_ffn_gate, 256, "cast_w_ffn_gate")
    wfu = _cast_call(w_ffn_up, 256, "cast_w_ffn_up")
    wfd = _cast_call(w_ffn_down, 512, "cast_w_ffn_down")

    y = x_prompt.reshape(N_CTX_TOK, D_MODEL)
    z = x_sample.reshape(N_LAT_TOK, D_MODEL)
    new_caches = ()
    for l in range(DEPTH):
        lam_init = _lambda_init(l)
        mod = mod_all[l]
        uq = w_mla_uq[l].reshape(MLA_Q_RANK, MLA_HEADS, MLA_NOPE + MLA_ROPE)
        uq_rope = jnp.pad(uq[:, :, MLA_NOPE:], ((0, 0), (0, 0), (0, LANES - MLA_ROPE)))
        w_uq = jnp.concatenate([uq[:, :, :MLA_NOPE].reshape(MLA_Q_RANK, MLA_QN_W),
                                uq_rope.reshape(MLA_Q_RANK, MLA_QN_W)], axis=1).astype(BF16)
        small = (mla_q_norm_g[l][None, :], w_uq, mla_kv_norm_g[l][None, :], w_mla_ukv[l].astype(BF16),
                 swa_sink[l], diff_lq1[l][None, :], diff_lk1[l][None, :], diff_lq2[l][None, :],
                 diff_lk2[l][None, :], diff_subln_g[l][None, :])
        g_pre, g_post = mix_pre_g[l][None, :], mix_post_g[l][None, :]
        f_pre, f_post = ffn_pre_g[l][None, :], ffn_post_g[l][None, :]

        nxt = l + 1 if l + 1 < DEPTH else None
        side = (lambda *ws: ()) if nxt is None else (lambda *ws: tuple((w, nxt) for w in ws))

        h, pq, pkv, kr = _inproj_call(y, mod, g_pre, w_main, w_kr, l, _ctx_seg, True, "ctx_inproj")
        o, *new_caches = _ctx_attn_call(pq, pkv, kr, small, new_caches, l, lam_init)
        mixed, *nxt_a = _merge_call(h, o, wg, wb, "ctx_merge", side(w_mix_gate, w_branch))
        y, h2, *nxt_b = _outproj_call(mixed, y, mod, g_post, f_pre, wo, _ctx_seg, "ctx_outproj",
                                      side(w_out, w_ffn_gate))
        y = _ffn_call(y, h2, mod, f_post, wfg, wfu, wfd, _ctx_seg, "ctx_ffn")

        h, p, kr = _inproj_call(z, mod, g_pre, w_main, w_kr, l, _lat_seg, False, "lat_inproj")
        o = _lat_attn_call(p, kr, caches, tables, bias_all, small, l, lam_init)
        mixed, *nxt_c = _merge_call(h, o, wg, wb, "lat_merge", side(w_ffn_down))
        z, h2, *nxt_d = _outproj_call(mixed, z, mod, g_post, f_pre, wo, _lat_seg, "lat_outproj", side(w_ffn_up))
        z = _ffn_call(z, h2, mod, f_post, wfg, wfu, wfd, _lat_seg, "lat_ffn")
        if nxt is not None:
            (wg, wb), (wo, wfg), (wfd,), (wfu,) = nxt_a, nxt_b, nxt_c, nxt_d

    new_caches = [a.reshape((BATCH, DEPTH, SEQ) + t) for a, t in zip(new_caches, CACHE_TAILS)]
    return (y.reshape(BATCH, SEQ, D_MODEL), z.reshape(DEC_BATCH, DEC_SEQ, D_MODEL), *new_caches)
```

```python
import functools
import math

import numpy as np
import jax
import jax.numpy as jnp
from jax import lax
from jax.experimental import pallas as pl
from jax.experimental.pallas import tpu as pltpu

D_MODEL = 2048
BATCH = 32
SEQ = 256
DEPTH = 2
DEC_BATCH = 8
DEC_SEQ = 1024
PAST_LEN = 512
GRID_W = 64
HEAD_DIM = 128
ROPE_BASE = 10000.0
EPS = 1e-6
NEG_INF = -1e30
LOG2E = math.log2(math.e)

MLA_HEADS = 4
MLA_Q_RANK = 384
MLA_KV_RANK = 128
MLA_NOPE = 128
MLA_ROPE = 64
MLA_V = 128
SWA_HEADS = 4
SWA_KV_HEADS = 2
SWA_WINDOW = 128
NA_HEADS = 4
NA_ROWS = 8
NA_COLS = 16
DIFF_HEADS = 4
DIFF_QK = 64
DIFF_V = 128
N_BRANCH = 4
BRANCH_W = 512
FFN_HIDDEN = 5632

N_CTX_TOK = BATCH * SEQ
N_LAT_TOK = DEC_BATCH * DEC_SEQ
GRID_ROWS = DEC_SEQ // GRID_W

Q_DOWN0, CKV0, QB0, KB0, VB0, QC0, KC0, VC0, QD0, KD0, VD0 = (
    0, 384, 512, 1024, 1280, 1536, 2048, 2560, 3072, 3584, 4096)
KROPE_SRC0 = MLA_Q_RANK + MLA_KV_RANK
P_WIDTH = 4608
MLA_QN_W = MLA_HEADS * MLA_NOPE
MLA_KV_W = MLA_NOPE + MLA_V
SWA_KV_W = SWA_KV_HEADS * HEAD_DIM
Q_PIECES = ((Q_DOWN0, MLA_Q_RANK), (QB0, BRANCH_W), (QC0, BRANCH_W), (QD0, BRANCH_W))
KV_PIECES = ((CKV0, MLA_KV_RANK), (KB0, 2 * SWA_KV_W), (KC0, 2 * BRANCH_W), (KD0, 2 * BRANCH_W))
Q_WIDTH = sum(w for _, w in Q_PIECES)
KV_WIDTH = sum(w for _, w in KV_PIECES)
CQ_DOWN0, CQB0, CQC0, CQD0 = 0, MLA_Q_RANK, MLA_Q_RANK + BRANCH_W, MLA_Q_RANK + 2 * BRANCH_W
CCKV0, CKB0 = 0, MLA_KV_RANK
CVB0 = CKB0 + SWA_KV_W
CKC0 = CVB0 + SWA_KV_W
CVC0 = CKC0 + BRANCH_W
CKD0 = CVC0 + BRANCH_W
CVD0 = CKD0 + BRANCH_W
N_MOD_ROWS = 16

CACHE_TAILS = ((MLA_KV_RANK,), (MLA_ROPE,), (SWA_KV_HEADS, HEAD_DIM), (SWA_KV_HEADS, HEAD_DIM),
               (NA_HEADS, HEAD_DIM), (NA_HEADS, HEAD_DIM), (DIFF_HEADS, 2 * DIFF_QK), (DIFF_HEADS, DIFF_V))

VMEM_LIMIT = 56 * 1024 * 1024
VMEM_LIMIT_BIG = 62 * 1024 * 1024
LANES = 128

F32 = jnp.float32
BF16 = jnp.bfloat16


def _lambda_init(l):
    return 0.8 - 0.6 * math.exp(-0.3 * l)


def _rms(x):
    return x * lax.rsqrt(jnp.mean(x * x, axis=-1, keepdims=True) + EPS)


def _head_cols(h, base=0):
    return slice(base + h * HEAD_DIM, base + (h + 1) * HEAD_DIM)


def _dot(a, b):
    return jnp.dot(a, b, preferred_element_type=F32)


def _dot_nt(a, b):
    return lax.dot_general(a, b, (((1,), (1,)), ((), ())), preferred_element_type=F32)


def _attend(pieces, sink=None):
    m = None
    for s, _ in pieces:
        mi = jnp.max(s, axis=-1, keepdims=True)
        m = mi if m is None else jnp.maximum(m, mi)
    if sink is not None:
        sink = sink * LOG2E
        m = jnp.maximum(m, sink)
    den = None
    num = None
    for s, v in pieces:
        e = jnp.exp2(s - m)
        d = jnp.sum(e, axis=-1, keepdims=True)
        n = _dot(e.astype(BF16), v)
        den = d if den is None else den + d
        num = n if num is None else num + n
    if sink is not None:
        den = den + jnp.exp2(sink - m)
    return num / den


def _diff_attend(pieces1, pieces2, values, lam):
    def exps(pieces):
        m = None
        for s in pieces:
            mi = jnp.max(s, axis=-1, keepdims=True)
            m = mi if m is None else jnp.maximum(m, mi)
        es = [jnp.exp2(s - m) for s in pieces]
        den = None
        for e in es:
            d = jnp.sum(e, axis=-1, keepdims=True)
            den = d if den is None else den + d
        return es, den

    es1, den1 = exps(pieces1)
    es2, den2 = exps(pieces2)
    ratio = lam * den1 / den2
    out = None
    for e1, e2, v in zip(es1, es2, values):
        n = _dot((e1 - e2 * ratio).astype(BF16), v)
        out = n if out is None else out + n
    return out / den1


def _diff_lambda(lq1, lk1, lq2, lk2, lam_init):
    a = jnp.sum(lq1[...] * lk1[...], axis=-1, keepdims=True)
    b = jnp.sum(lq2[...] * lk2[...], axis=-1, keepdims=True)
    return jnp.exp(a) - jnp.exp(b) + lam_init


def _ada_kernel(c_ref, w_ref, b_ref, o_ref):
    c = c_ref[...]
    s = (c * jax.nn.sigmoid(c)).astype(BF16)
    o_ref[0] = _dot(s, w_ref[0].astype(BF16)) + b_ref[0]


def _ada_call(cvec, w_ada, b_ada):
    tn = 1024
    n = 6 * D_MODEL
    return pl.pallas_call(
        _ada_kernel,
        grid=(DEPTH, n // tn),
        in_specs=[
            pl.BlockSpec((N_MOD_ROWS, D_MODEL), lambda l, j: (0, 0)),
            pl.BlockSpec((1, D_MODEL, tn), lambda l, j: (l, 0, j)),
            pl.BlockSpec((1, 1, tn), lambda l, j: (l, 0, j)),
        ],
        out_specs=pl.BlockSpec((1, N_MOD_ROWS, tn), lambda l, j: (l, 0, j)),
        out_shape=jax.ShapeDtypeStruct((DEPTH, N_MOD_ROWS, n), F32),
        compiler_params=pltpu.CompilerParams(
            dimension_semantics=("arbitrary", "arbitrary"), vmem_limit_bytes=VMEM_LIMIT),
        name="ada_mod",
    )(cvec, w_ada, b_ada.reshape(DEPTH, 1, n))


NBR_PAIRS = 2 * NA_ROWS


def _nbr_bias_kernel(rpb_ref, o_ref):
    l = pl.program_id(0)
    h = pl.program_id(1)
    n_dr = 2 * NA_ROWS - 1
    n_dc = 2 * NA_COLS - 1
    base = (l * NA_HEADS + h) * (n_dr * n_dc)
    wq = lax.broadcasted_iota(jnp.int32, (GRID_W, LANES), 0)
    lane = lax.broadcasted_iota(jnp.int32, (GRID_W, LANES), 1)
    upper = lane >= GRID_W
    wk = jnp.where(upper, lane - GRID_W, lane)
    delta = wk - wq + (NA_COLS - 1)
    cs = jnp.clip(wq - NA_COLS // 2, 0, GRID_W - NA_COLS)
    valid = (wk >= cs) & (wk < cs + NA_COLS)
    for e in range(NBR_PAIRS):
        dr_lo, dr_hi = max(e - 1, 0), min(e, n_dr - 1)
        acc = jnp.full((GRID_W, LANES), NEG_INF, F32)
        for d in range(n_dc):
            lo = rpb_ref[base + dr_lo * n_dc + d]
            hi = rpb_ref[base + dr_hi * n_dc + d]
            acc = jnp.where(delta == d, jnp.where(upper, hi, lo), acc)
        o_ref[0, 0, e] = jnp.where(valid, acc * LOG2E, NEG_INF)


def _nbr_bias_call(na_rpb):
    n_pairs = NBR_PAIRS
    return pl.pallas_call(
        _nbr_bias_kernel,
        grid=(DEPTH, NA_HEADS),
        in_specs=[pl.BlockSpec(memory_space=pltpu.SMEM)],
        out_specs=pl.BlockSpec((1, 1, n_pairs, GRID_W, LANES), lambda l, h: (l, h, 0, 0, 0)),
        out_shape=jax.ShapeDtypeStruct((DEPTH, NA_HEADS, n_pairs, GRID_W, LANES), F32),
        compiler_params=pltpu.CompilerParams(dimension_semantics=("arbitrary", "arbitrary")),
        name="nbr_bias",
    )(na_rpb.reshape(-1))


def _inproj_kernel(x_ref, mod_ref, g_ref, w_ref, wkr_ref, h_ref, *p_refs_kr):
    *p_refs, kr_ref = p_refs_kr
    m = mod_ref[0]
    half = x_ref.shape[0] // 2
    for r in (slice(0, half), slice(half, 2 * half)):
        h = (_rms(x_ref[r, :]) * g_ref[...] * (1.0 + m[1:2]) + m[0:1]).astype(BF16)
        h_ref[r, :] = h
        kr_ref[r, :] = _dot_nt(h, wkr_ref[...])
        p = _dot_nt(h, w_ref[...])
        if len(p_refs) == 1:
            p_refs[0][r, :] = p.astype(p_refs[0].dtype)
        else:
            for ref, pieces in zip(p_refs, (Q_PIECES, KV_PIECES)):
                dst = 0
                for src, width in pieces:
                    ref[r, dst:dst + width] = p[:, src:src + width].astype(ref.dtype)
                    dst += width


def _inproj_call(x, mod, g, w_main, w_kr, layer, seg, split, name):
    n_tok = x.shape[0]
    tm = 512
    p_outs = ((Q_WIDTH, BF16), (KV_WIDTH, F32)) if split else ((P_WIDTH, BF16),)
    return pl.pallas_call(
        _inproj_kernel,
        grid=(n_tok // tm,),
        in_specs=[
            pl.BlockSpec((tm, D_MODEL), lambda i: (i, 0)),
            pl.BlockSpec((1, 6, D_MODEL), lambda i: (seg(i, tm), 0, 0)),
            pl.BlockSpec((1, D_MODEL), lambda i: (0, 0)),
            pl.BlockSpec((None, P_WIDTH, D_MODEL), lambda i: (layer, 0, 0)),
            pl.BlockSpec((None, LANES, D_MODEL), lambda i: (layer, 0, 0)),
        ],
        out_specs=[pl.BlockSpec((tm, D_MODEL), lambda i: (i, 0))]
        + [pl.BlockSpec((tm, w), lambda i: (i, 0)) for w, _ in p_outs]
        + [pl.BlockSpec((tm, LANES), lambda i: (i, 0))],
        out_shape=[jax.ShapeDtypeStruct((n_tok, D_MODEL), BF16)]
        + [jax.ShapeDtypeStruct((n_tok, w), dt) for w, dt in p_outs]
        + [jax.ShapeDtypeStruct((n_tok, LANES), F32)],
        compiler_params=pltpu.CompilerParams(
            dimension_semantics=("arbitrary",), vmem_limit_bytes=VMEM_LIMIT_BIG),
        name=name,
    )(x, mod, g, w_main, w_kr)


def _cast_kernel(x_ref, o_ref):
    o_ref[...] = x_ref[...].astype(BF16)


def _cast_call(w, block_rows, name):
    w2 = w.reshape(-1, w.shape[-1])
    rows, cols = w2.shape[0] // w.shape[0], w2.shape[1]
    out = pl.pallas_call(
        _cast_kernel,
        grid=(rows // block_rows,),
        in_specs=[pl.BlockSpec((block_rows, cols), lambda i: (i, 0))],
        out_specs=pl.BlockSpec((block_rows, cols), lambda i: (i, 0)),
        out_shape=jax.ShapeDtypeStruct((rows, cols), BF16),
        compiler_params=pltpu.CompilerParams(dimension_semantics=("arbitrary",), vmem_limit_bytes=VMEM_LIMIT),
        name=name,
    )(w2)
    return out.reshape((1,) + w.shape[1:])


def _side_specs(side, n_grid):
    in_specs, out_specs, out_shape = [], [], []
    for w, layer, block, index in side:
        if n_grid == 1:
            in_specs.append(pl.BlockSpec((None,) + block, lambda i, f=index, l=layer: (l,) + f(i)))
            out_specs.append(pl.BlockSpec((None,) + block, lambda i, f=index: (0,) + f(i)))
        else:
            in_specs.append(pl.BlockSpec((None,) + block, lambda i, j, f=index, l=layer: (l,) + f(i, j)))
            out_specs.append(pl.BlockSpec((None,) + block, lambda i, j, f=index: (0,) + f(i, j)))
        out_shape.append(jax.ShapeDtypeStruct((1,) + w.shape[1:], BF16))
    return in_specs, out_specs, out_shape


def _side_cast(srcs, dsts):
    for src, dst in zip(srcs, dsts):
        dst[...] = src[...].astype(BF16)


def _cast_w_in_kernel(x_ref, xkr_ref, main_ref, kr_ref):
    main_ref[...] = x_ref[0].astype(BF16)

    @pl.when(pl.program_id(1) == 0)
    def _():
        kr_ref[:MLA_ROPE, :] = xkr_ref[...].astype(BF16)
        kr_ref[MLA_ROPE:, :] = jnp.zeros((LANES - MLA_ROPE, D_MODEL), BF16)


def _cast_w_in_call(w_in):
    w_t = jnp.swapaxes(w_in, 1, 2)
    block_rows = 512
    src_row = lambda l, j: (l, pl.multiple_of(j * block_rows + jnp.where(j * block_rows >= KROPE_SRC0, MLA_ROPE, 0), MLA_ROPE), 0)
    return pl.pallas_call(
        _cast_w_in_kernel,
        grid=(DEPTH, P_WIDTH // block_rows),
        in_specs=[pl.BlockSpec((pl.Element(1), pl.Element(block_rows), pl.Element(D_MODEL)), src_row),
                  pl.BlockSpec((None, MLA_ROPE, D_MODEL), lambda l, j: (l, KROPE_SRC0 // MLA_ROPE, 0))],
        out_specs=[pl.BlockSpec((None, block_rows, D_MODEL), lambda l, j: (l, j, 0)),
                   pl.BlockSpec((None, LANES, D_MODEL), lambda l, j: (l, 0, 0))],
        out_shape=[jax.ShapeDtypeStruct((DEPTH, P_WIDTH, D_MODEL), BF16),
                   jax.ShapeDtypeStruct((DEPTH, LANES, D_MODEL), BF16)],
        compiler_params=pltpu.CompilerParams(
            dimension_semantics=("arbitrary", "arbitrary"), vmem_limit_bytes=VMEM_LIMIT),
        name="cast_w_in",
    )(w_t, w_t)


def _mla_queries(q_down, qn_g, w_uq):
    cq = _rms(q_down) * qn_g[...]
    return _dot(cq.astype(BF16), w_uq[...])


def _ctx_attn_kernel(*refs, lam_init, first):
    (pq_ref, pkv_ref, kr_ref, qn_g, w_uq, kvn_g, w_ukv, sink_ref, lq1, lk1, lq2, lk2, subln_g) = refs[:13]
    n_in = 13 + (0 if first else len(CACHE_TAILS))
    o_ref = refs[n_in]
    ckv_o, kr_o, kb_o, vb_o, kc_o, vc_o, kd_o, vd_o = refs[n_in + 1:]
    ls = 0
    if first:
        for dst in (ckv_o, kr_o, kb_o, vb_o, kc_o, vc_o, kd_o, vd_o):
            dst[0, 1:] = jnp.zeros(dst.shape[1:], F32)[1:]

    scale_a = LOG2E * (MLA_NOPE + MLA_ROPE) ** -0.5
    qa = _mla_queries(pq_ref[:, CQ_DOWN0:CQ_DOWN0 + MLA_Q_RANK].astype(F32), qn_g, w_uq) * scale_a
    ckv = _rms(pkv_ref[:, CCKV0:CCKV0 + MLA_KV_RANK]) * kvn_g[...]
    ckv_o[0, ls] = ckv
    kr = kr_ref[...]
    kr_o[0, ls] = kr[:, :MLA_ROPE]
    kr_b = kr.astype(BF16)
    kv = _dot(ckv.astype(BF16), w_ukv[...])
    for h in range(MLA_HEADS):
        q = jnp.concatenate([qa[:, _head_cols(h)], qa[:, _head_cols(h, MLA_QN_W)]],
                            axis=1).astype(BF16)
        kv0 = h * MLA_KV_W
        k = jnp.concatenate([kv[:, kv0:kv0 + MLA_NOPE].astype(BF16), kr_b], axis=1)
        v = kv[:, kv0 + MLA_NOPE:kv0 + MLA_KV_W].astype(BF16)
        o = _attend([(_dot_nt(q, k), v)])
        o_ref[:, _head_cols(h)] = o.astype(BF16)

    scale = LOG2E * HEAD_DIM ** -0.5

    def qhead(col0, h):
        return pq_ref[:, _head_cols(h, col0)].astype(F32)

    def head(col0, h):
        return pkv_ref[:, _head_cols(h, col0)]

    g_size = SWA_HEADS // SWA_KV_HEADS
    for g in range(SWA_KV_HEADS):
        k = head(CKB0, g)
        v = head(CVB0, g)
        kb_o[0, ls, pl.ds(g, SEQ, stride=SWA_KV_HEADS), :] = k
        vb_o[0, ls, pl.ds(g, SEQ, stride=SWA_KV_HEADS), :] = v
        k = k.astype(BF16)
        v = v.astype(BF16)
        for h in range(g * g_size, (g + 1) * g_size):
            q = (qhead(CQB0, h) * scale).astype(BF16)
            o = _attend([(_dot_nt(q, k), v)], sink=sink_ref[h])
            o_ref[:, _head_cols(h, BRANCH_W)] = o.astype(BF16)

    for h in range(NA_HEADS):
        k = head(CKC0, h)
        v = head(CVC0, h)
        kc_o[0, ls, pl.ds(h, SEQ, stride=NA_HEADS), :] = k
        vc_o[0, ls, pl.ds(h, SEQ, stride=NA_HEADS), :] = v
        q = (qhead(CQC0, h) * scale).astype(BF16)
        o = _attend([(_dot_nt(q, k.astype(BF16)), v.astype(BF16))])
        o_ref[:, _head_cols(h, 2 * BRANCH_W)] = o.astype(BF16)

    lam = _diff_lambda(lq1, lk1, lq2, lk2, lam_init)
    scale_d = LOG2E * DIFF_QK ** -0.5
    lo = lax.broadcasted_iota(jnp.int32, (SEQ, 2 * DIFF_QK), 1) < DIFF_QK
    for h in range(DIFF_HEADS):
        k = head(CKD0, h)
        v = head(CVD0, h)
        kd_o[0, ls, pl.ds(h, SEQ, stride=DIFF_HEADS), :] = k
        vd_o[0, ls, pl.ds(h, SEQ, stride=DIFF_HEADS), :] = v
        k = k.astype(BF16)
        v = v.astype(BF16)
        q = qhead(CQD0, h) * scale_d
        o1 = _attend([(_dot_nt(jnp.where(lo, q, 0.0).astype(BF16), k), v)])
        o2 = _attend([(_dot_nt(jnp.where(lo, 0.0, q).astype(BF16), k), v)])
        o = _rms(o1 - lam * o2) * subln_g[...] * (1.0 - lam_init)
        o_ref[:, _head_cols(h, 3 * BRANCH_W)] = o.astype(BF16)


def _small_specs():
    zero2 = (lambda *a: (0, 0))
    return [
        pl.BlockSpec((1, MLA_Q_RANK), zero2),
        pl.BlockSpec((MLA_Q_RANK, 2 * MLA_QN_W), zero2),
        pl.BlockSpec((1, MLA_KV_RANK), zero2),
        pl.BlockSpec((MLA_KV_RANK, MLA_HEADS * MLA_KV_W), zero2),
        pl.BlockSpec(memory_space=pltpu.SMEM),
        pl.BlockSpec((1, DIFF_QK), zero2),
        pl.BlockSpec((1, DIFF_QK), zero2),
        pl.BlockSpec((1, DIFF_QK), zero2),
        pl.BlockSpec((1, DIFF_QK), zero2),
        pl.BlockSpec((1, DIFF_V), zero2),
    ]


def _ctx_attn_call(pq, pkv, kr, small, prev, layer, lam_init):
    first = not prev
    row = lambda b: (b, 0)

    def flat(tail):
        return (SEQ * tail[0], tail[1]) if len(tail) == 2 else (SEQ,) + tail

    def cache_spec(tail):
        if first:
            return pl.BlockSpec((1, DEPTH) + flat(tail), lambda b: (b, 0, 0, 0))
        return pl.BlockSpec((1, 1) + flat(tail), lambda b: (b, layer, 0, 0))

    n_fixed = 13
    return pl.pallas_call(
        functools.partial(_ctx_attn_kernel, lam_init=lam_init, first=first),
        grid=(BATCH,),
        in_specs=[pl.BlockSpec((SEQ, Q_WIDTH), row), pl.BlockSpec((SEQ, KV_WIDTH), row),
                  pl.BlockSpec((SEQ, LANES), row)] + _small_specs()
        + [pl.BlockSpec(memory_space=pl.ANY) for _ in prev],
        out_specs=[pl.BlockSpec((SEQ, D_MODEL), row)] + [cache_spec(t) for t in CACHE_TAILS],
        out_shape=[jax.ShapeDtypeStruct((N_CTX_TOK, D_MODEL), BF16)]
        + [jax.ShapeDtypeStruct((BATCH, DEPTH) + flat(t), F32) for t in CACHE_TAILS],
        input_output_aliases={n_fixed + k: 1 + k for k in range(len(prev))},
        compiler_params=pltpu.CompilerParams(
            dimension_semantics=("arbitrary",), vmem_limit_bytes=VMEM_LIMIT),
        name="ctx_attn",
    )(pq, pkv, kr, *small, *prev)


LAT_TQ = 256
SWA_PAD = SWA_WINDOW
NBR_WIN_ROWS = 12


def _rope_pair64(x, c_ref, sa_ref, sb_ref, rows):
    out = []
    for j in range(x.shape[1] // LANES):
        xj = x[:, j * LANES:(j + 1) * LANES]
        out.append(xj * c_ref[rows, :] + pltpu.roll(xj, 96, 1) * sa_ref[rows, :]
                   + pltpu.roll(xj, 32, 1) * sb_ref[rows, :])
    return jnp.concatenate(out, axis=1) if len(out) > 1 else out[0]


def _rope128(x, c_ref, s_ref, rows):
    out = []
    for j in range(x.shape[1] // LANES):
        xj = x[:, j * LANES:(j + 1) * LANES]
        out.append(xj * c_ref[rows, :] + pltpu.roll(xj, 64, 1) * s_ref[rows, :])
    return jnp.concatenate(out, axis=1) if len(out) > 1 else out[0]


def _lat_attn_kernel(p_ref, kr_ref, c_ckv, c_kr, c_kb, c_vb, c_kc, c_vc, c_kd, c_vd,
                     cos128, sin128, cos64, sina64, sinb64, bias_ref,
                     qn_g, w_uq, kvn_g, w_ukv, sink_ref, lq1, lk1, lq2, lk2, subln_g,
                     o_ref,
                     ka_s, va_s, kbc_s, vbc_s, kbp_s, vbp_s, kcc_s, vcc_s, kdc_s, vdc_s, kdo_s,
                     *, lam_init):
    def cached_head(ref, h, n_heads):
        return ref[pl.ds(h, PAST_LEN, stride=n_heads), :].astype(BF16)

    qi = pl.program_id(1)
    all_rows = slice(None)
    n_own = DEC_SEQ

    @pl.when(qi == 0)
    def _prologue():
        ckv_own = (_rms(p_ref[:, CKV0:CKV0 + MLA_KV_RANK].astype(F32)) * kvn_g[...]).astype(BF16)
        ckv_all = jnp.concatenate([c_ckv[...].astype(BF16), ckv_own], axis=0)
        kr_ctx = jnp.concatenate([c_kr[...], jnp.zeros((PAST_LEN, LANES - MLA_ROPE), F32)], axis=1)
        kr_own = _rope_pair64(kr_ref[...], cos64, sina64, sinb64, all_rows)
        kr_all = jnp.concatenate([kr_ctx, kr_own], axis=0).astype(BF16)
        for h in range(MLA_HEADS):
            kvh = _dot(ckv_all, w_ukv[:, h * MLA_KV_W:(h + 1) * MLA_KV_W])
            ka_s[h, :, 0:MLA_NOPE] = kvh[:, 0:MLA_NOPE].astype(BF16)
            ka_s[h, :, MLA_NOPE:MLA_NOPE + LANES] = kr_all
            va_s[:, _head_cols(h)] = kvh[:, MLA_NOPE:MLA_KV_W].astype(BF16)
        for g in range(SWA_KV_HEADS):
            kbc_s[:, _head_cols(g)] = cached_head(c_kb, g, SWA_KV_HEADS)
            vbc_s[:, _head_cols(g)] = cached_head(c_vb, g, SWA_KV_HEADS)
        zpad = jnp.zeros((SWA_PAD, SWA_KV_W), BF16)
        kbp_s[0:SWA_PAD, :] = zpad
        kbp_s[SWA_PAD + n_own:, :] = zpad
        vbp_s[0:SWA_PAD, :] = zpad
        vbp_s[SWA_PAD + n_own:, :] = zpad
        kbp_s[SWA_PAD:SWA_PAD + n_own, :] = _rope128(
            p_ref[:, KB0:KB0 + SWA_KV_W].astype(F32), cos128, sin128, all_rows).astype(BF16)
        vbp_s[SWA_PAD:SWA_PAD + n_own, :] = p_ref[:, VB0:VB0 + SWA_KV_W]
        for h in range(NA_HEADS):
            cols = _head_cols(h)
            kcc_s[:, cols] = cached_head(c_kc, h, NA_HEADS)
            vcc_s[:, cols] = cached_head(c_vc, h, NA_HEADS)
            kdc_s[:, cols] = cached_head(c_kd, h, DIFF_HEADS)
            vdc_s[:, cols] = cached_head(c_vd, h, DIFF_HEADS)
        kdo_s[...] = _rope_pair64(
            p_ref[:, KD0:KD0 + BRANCH_W].astype(F32), cos64, sina64, sinb64, all_rows).astype(BF16)

    q0 = pl.multiple_of(qi * LAT_TQ, LAT_TQ)
    rows = pl.ds(q0, LAT_TQ)

    scale_a = LOG2E * (MLA_NOPE + MLA_ROPE) ** -0.5
    qa = _mla_queries(p_ref[rows, Q_DOWN0:Q_DOWN0 + MLA_Q_RANK].astype(F32), qn_g, w_uq) * scale_a
    qa_r = _rope_pair64(qa[:, MLA_QN_W:2 * MLA_QN_W], cos64, sina64, sinb64, rows)
    for h in range(MLA_HEADS):
        q = jnp.concatenate([qa[:, _head_cols(h)], qa_r[:, _head_cols(h)]],
                            axis=1).astype(BF16)
        o = _attend([(_dot_nt(q, ka_s[h]), va_s[:, _head_cols(h)])])
        o_ref[:, _head_cols(h)] = o.astype(BF16)

    scale = LOG2E * HEAD_DIM ** -0.5
    qb = (_rope128(p_ref[rows, QB0:QB0 + BRANCH_W].astype(F32), cos128, sin128, rows) * scale).astype(BF16)
    n_loc = LAT_TQ + 2 * SWA_PAD
    qpos = lax.broadcasted_iota(jnp.int32, (LAT_TQ, n_loc), 0)
    kpos = lax.broadcasted_iota(jnp.int32, (LAT_TQ, n_loc), 1) - SWA_PAD
    kabs = kpos + q0
    valid = (jnp.abs(qpos - kpos) <= SWA_WINDOW) & (kabs >= 0) & (kabs < n_own)
    loc_rows = pl.ds(q0, n_loc)
    g_size = SWA_HEADS // SWA_KV_HEADS
    for h in range(SWA_HEADS):
        g = h // g_size
        q = qb[:, _head_cols(h)]
        s_ctx = _dot_nt(q, kbc_s[:, _head_cols(g)])
        s_loc = jnp.where(valid, _dot_nt(q, kbp_s[loc_rows, _head_cols(g)]), NEG_INF)
        o = _attend([(s_ctx, vbc_s[:, _head_cols(g)]),
                     (s_loc, vbp_s[loc_rows, _head_cols(g)])], sink=sink_ref[h])
        o_ref[:, _head_cols(h, BRANCH_W)] = o.astype(BF16)

    q_rows = LAT_TQ // GRID_W
    ws = jnp.clip(qi * q_rows - NA_ROWS // 2, 0, GRID_ROWS - NBR_WIN_ROWS)
    ws = (ws // 2) * 2
    krows = pl.ds(pl.multiple_of(ws * GRID_W, 2 * GRID_W), NBR_WIN_ROWS * GRID_W)
    lower = lax.broadcasted_iota(jnp.int32, (GRID_W, LANES), 1) < GRID_W
    for h in range(NA_HEADS):
        cols = _head_cols(h)
        bias_rows = []
        for t in range(q_rows):
            r = qi * q_rows + t
            off = jnp.clip(r - NA_ROWS // 2, 0, GRID_ROWS - NA_ROWS) - r + (NA_ROWS - 1)
            blocks = []
            for j in range(NBR_WIN_ROWS // 2):
                dr = ws + 2 * j - r + (NA_ROWS - 1)
                ok_lo = ((dr >= off) & (dr < off + NA_ROWS)).astype(jnp.int32)
                ok_hi = ((dr + 1 >= off) & (dr + 1 < off + NA_ROWS)).astype(jnp.int32)
                ok = jnp.where(lower, ok_lo, ok_hi) > 0
                blk = bias_ref[0, h, jnp.clip(dr + 1, 0, NBR_PAIRS - 1)]
                blocks.append(jnp.where(ok, blk, NEG_INF))
            bias_rows.append(jnp.concatenate(blocks, axis=1))
        bias = jnp.concatenate(bias_rows, axis=0)
        q = (p_ref[rows, _head_cols(h, QC0)].astype(F32) * scale).astype(BF16)
        s_loc = _dot_nt(q, p_ref[krows, _head_cols(h, KC0)]) + bias
        s_ctx = _dot_nt(q, kcc_s[:, cols])
        o = _attend([(s_ctx, vcc_s[:, cols]), (s_loc, p_ref[krows, _head_cols(h, VC0)])])
        o_ref[:, _head_cols(h, 2 * BRANCH_W)] = o.astype(BF16)

    scale_d = LOG2E * DIFF_QK ** -0.5
    qd = _rope_pair64(p_ref[rows, QD0:QD0 + BRANCH_W].astype(F32), cos64, sina64, sinb64, rows) * scale_d
    lam = _diff_lambda(lq1, lk1, lq2, lk2, lam_init)
    lo = lax.broadcasted_iota(jnp.int32, (LAT_TQ, 2 * DIFF_QK), 1) < DIFF_QK
    for h in range(DIFF_HEADS):
        cols = _head_cols(h)
        q = qd[:, cols]
        v_own = p_ref[:, _head_cols(h, VD0)]
        q1 = jnp.where(lo, q, 0.0).astype(BF16)
        q2 = jnp.where(lo, 0.0, q).astype(BF16)
        o = _diff_attend([_dot_nt(q1, kdc_s[:, cols]), _dot_nt(q1, kdo_s[:, cols])],
                         [_dot_nt(q2, kdc_s[:, cols]), _dot_nt(q2, kdo_s[:, cols])],
                         [vdc_s[:, cols], v_own], lam)
        o = _rms(o) * subln_g[...] * (1.0 - lam_init)
        o_ref[:, _head_cols(h, 3 * BRANCH_W)] = o.astype(BF16)


def _lat_attn_call(p, kr, caches, tables, bias, small, layer, lam_init):
    nq = DEC_SEQ // LAT_TQ
    batch_rows = lambda b, q: (b, 0)
    zero2 = lambda b, q: (0, 0)
    once = pl.Buffered(1)
    cache_args = [a.reshape(a.shape[:2] + (-1, a.shape[-1])) for a in caches]
    cache_specs = [pl.BlockSpec((None, None) + a.shape[2:], lambda b, q: (b, layer, 0, 0)) for a in cache_args]
    table_specs = [pl.BlockSpec((DEC_SEQ, LANES), zero2) for _ in tables]
    bias_spec = pl.BlockSpec((1, NA_HEADS, NBR_PAIRS, GRID_W, LANES), lambda b, q: (layer, 0, 0, 0, 0))
    n_all = PAST_LEN + DEC_SEQ
    n_pad = DEC_SEQ + 2 * SWA_PAD
    scratch = [
        pltpu.VMEM((MLA_HEADS, n_all, MLA_NOPE + LANES), BF16),
        pltpu.VMEM((n_all, BRANCH_W), BF16),
        pltpu.VMEM((PAST_LEN, SWA_KV_W), BF16), pltpu.VMEM((PAST_LEN, SWA_KV_W), BF16),
        pltpu.VMEM((n_pad, SWA_KV_W), BF16), pltpu.VMEM((n_pad, SWA_KV_W), BF16),
        pltpu.VMEM((PAST_LEN, BRANCH_W), BF16), pltpu.VMEM((PAST_LEN, BRANCH_W), BF16),
        pltpu.VMEM((PAST_LEN, BRANCH_W), BF16), pltpu.VMEM((PAST_LEN, BRANCH_W), BF16),
        pltpu.VMEM((DEC_SEQ, BRANCH_W), BF16),
    ]
    return pl.pallas_call(
        functools.partial(_lat_attn_kernel, lam_init=lam_init),
        grid=(DEC_BATCH, nq),
        in_specs=[pl.BlockSpec((DEC_SEQ, P_WIDTH), batch_rows, pipeline_mode=once),
                  pl.BlockSpec((DEC_SEQ, LANES), batch_rows, pipeline_mode=once)]
        + cache_specs + table_specs + [bias_spec] + _small_specs(),
        out_specs=pl.BlockSpec((LAT_TQ, D_MODEL), lambda b, q: (b * nq + q, 0)),
        out_shape=jax.ShapeDtypeStruct((N_LAT_TOK, D_MODEL), BF16),
        scratch_shapes=scratch,
        compiler_params=pltpu.CompilerParams(
            dimension_semantics=("arbitrary", "arbitrary"), vmem_limit_bytes=VMEM_LIMIT_BIG),
        name="lat_attn",
    )(p, kr, *cache_args, *tables, bias, *small)


def _merge_kernel(h_ref, o_ref, wg_ref, wb_ref, *rest):
    n_side = (len(rest) - 1) // 2
    out_ref = rest[n_side]
    _side_cast(rest[:n_side], rest[n_side + 1:])
    acc = None
    for i in range(N_BRANCH):
        gate = jax.nn.sigmoid(_dot(h_ref[...], wg_ref[i]))
        term = gate * _dot(o_ref[:, i * BRANCH_W:(i + 1) * BRANCH_W], wb_ref[i])
        acc = term if acc is None else acc + term
    out_ref[...] = acc.astype(BF16)


def _merge_call(h, o, wg, wb, name, side_weights=()):
    n_tok = h.shape[0]
    tm, tn = 1024, 512
    grid = (n_tok // tm, D_MODEL // tn)
    side = []
    for w, layer in side_weights:
        lead = w.shape[1:-2]
        block = lead + (w.shape[-2] // grid[0], w.shape[-1] // grid[1])
        side.append((w, layer, block, lambda i, j, z=(0,) * len(lead): z + (i, j)))
    s_in, s_out, s_shape = _side_specs(side, 2)
    return pl.pallas_call(
        _merge_kernel,
        grid=grid,
        in_specs=[
            pl.BlockSpec((tm, D_MODEL), lambda i, j: (i, 0)),
            pl.BlockSpec((tm, D_MODEL), lambda i, j: (i, 0)),
            pl.BlockSpec((None, N_BRANCH, D_MODEL, tn), lambda i, j: (0, 0, 0, j)),
            pl.BlockSpec((None, N_BRANCH, BRANCH_W, tn), lambda i, j: (0, 0, 0, j)),
        ] + s_in,
        out_specs=[pl.BlockSpec((tm, tn), lambda i, j: (i, j))] + s_out,
        out_shape=[jax.ShapeDtypeStruct((n_tok, D_MODEL), BF16)] + s_shape,
        compiler_params=pltpu.CompilerParams(
            dimension_semantics=("arbitrary", "arbitrary"), vmem_limit_bytes=VMEM_LIMIT),
        name=name,
    )(h, o, wg, wb, *[w for w, _, _, _ in side])


def _outproj_kernel(mixed_ref, x_ref, mod_ref, g_ref, gffn_ref, w_ref, *rest):
    n_side = (len(rest) - 2) // 2
    xo_ref, h2_ref = rest[n_side:n_side + 2]
    _side_cast(rest[:n_side], rest[n_side + 2:])
    m = mod_ref[0]
    half = mixed_ref.shape[0] // 2
    for r in (slice(0, half), slice(half, 2 * half)):
        out = _dot(mixed_ref[r, :], w_ref[...])
        x = x_ref[r, :] + m[2:3] * (_rms(out) * g_ref[...])
        xo_ref[r, :] = x
        h2_ref[r, :] = (_rms(x) * gffn_ref[...] * (1.0 + m[4:5]) + m[3:4]).astype(BF16)


def _outproj_call(mixed, x, mod, g, g_ffn, w, seg, name, side_weights=()):
    n_tok = x.shape[0]
    tm = 512
    n_steps = n_tok // tm
    side = [(w_, layer, (w_.shape[1] // n_steps, w_.shape[2]), lambda i: (i, 0)) for w_, layer in side_weights]
    s_in, s_out, s_shape = _side_specs(side, 1)
    return pl.pallas_call(
        _outproj_kernel,
        grid=(n_steps,),
        in_specs=[
            pl.BlockSpec((tm, D_MODEL), lambda i: (i, 0)),
            pl.BlockSpec((tm, D_MODEL), lambda i: (i, 0)),
            pl.BlockSpec((1, 6, D_MODEL), lambda i: (seg(i, tm), 0, 0)),
            pl.BlockSpec((1, D_MODEL), lambda i: (0, 0)),
            pl.BlockSpec((1, D_MODEL), lambda i: (0, 0)),
            pl.BlockSpec((None, D_MODEL, D_MODEL), lambda i: (0, 0, 0)),
        ] + s_in,
        out_specs=[pl.BlockSpec((tm, D_MODEL), lambda i: (i, 0)), pl.BlockSpec((tm, D_MODEL), lambda i: (i, 0))]
        + s_out,
        out_shape=[jax.ShapeDtypeStruct((n_tok, D_MODEL), F32), jax.ShapeDtypeStruct((n_tok, D_MODEL), BF16)]
        + s_shape,
        compiler_params=pltpu.CompilerParams(
            dimension_semantics=("arbitrary",), vmem_limit_bytes=VMEM_LIMIT),
        name=name,
    )(mixed, x, mod, g, g_ffn, w, *[w_ for w_, _, _, _ in side])


def _ffn_kernel(x_ref, h2_ref, mod_ref, gpost_ref, wg_ref, wu_ref, wd_ref, xo_ref):
    j = pl.program_id(1)

    @pl.when(j == 0)
    def _():
        xo_ref[...] = jnp.zeros_like(xo_ref)

    half = h2_ref.shape[0] // 2
    for r in (slice(0, half), slice(half, 2 * half)):
        a = _dot(h2_ref[r, :], wg_ref[...])
        b = _dot(h2_ref[r, :], wu_ref[...])
        t = (a * jax.nn.sigmoid(a) * b).astype(BF16)
        xo_ref[r, :] += _dot(t, wd_ref[...])

    @pl.when(j == pl.num_programs(1) - 1)
    def _():
        m = mod_ref[0]
        xo_ref[...] = x_ref[...] + m[5:6] * (_rms(xo_ref[...]) * gpost_ref[...])


def _ffn_call(x, h2, mod, gpost, wg, wu, wd, seg, name):
    n_tok = x.shape[0]
    tm, th = 1024, 256
    return pl.pallas_call(
        _ffn_kernel,
        grid=(n_tok // tm, FFN_HIDDEN // th),
        in_specs=[
            pl.BlockSpec((tm, D_MODEL), lambda i, j: (i, 0)),
            pl.BlockSpec((tm, D_MODEL), lambda i, j: (i, 0)),
            pl.BlockSpec((1, 6, D_MODEL), lambda i, j: (seg(i, tm), 0, 0)),
            pl.BlockSpec((1, D_MODEL), lambda i, j: (0, 0)),
            pl.BlockSpec((None, D_MODEL, th), lambda i, j: (0, 0, j)),
            pl.BlockSpec((None, D_MODEL, th), lambda i, j: (0, 0, j)),
            pl.BlockSpec((None, th, D_MODEL), lambda i, j: (0, j, 0)),
        ],
        out_specs=pl.BlockSpec((tm, D_MODEL), lambda i, j: (i, 0)),
        out_shape=jax.ShapeDtypeStruct((n_tok, D_MODEL), F32),
        compiler_params=pltpu.CompilerParams(
            dimension_semantics=("arbitrary", "arbitrary"), vmem_limit_bytes=VMEM_LIMIT_BIG),
        name=name,
    )(x, h2, mod, gpost, wg, wu, wd)


def _rope_tables():
    t = np.arange(DEC_SEQ)
    row = (t // GRID_W).astype(np.float32)
    col = (t % GRID_W).astype(np.float32)

    def half_angles(r):
        quarter = r // 4
        inv = (1.0 / (ROPE_BASE ** (np.arange(quarter, dtype=np.float32) / quarter))).astype(np.float32)
        return np.concatenate([row[:, None] * inv, col[:, None] * inv], axis=-1).astype(np.float32)

    a64 = half_angles(HEAD_DIM)
    cos128 = np.concatenate([np.cos(a64), np.cos(a64)], axis=1)
    sin128 = np.concatenate([-np.sin(a64), np.sin(a64)], axis=1)
    a32 = half_angles(MLA_ROPE)
    c, s, z = np.cos(a32), np.sin(a32), np.zeros_like(a32)
    cos64 = np.concatenate([c, c, c, c], axis=1)
    sina64 = np.concatenate([-s, z, -s, z], axis=1)
    sinb64 = np.concatenate([z, s, z, s], axis=1)
    return tuple(jnp.asarray(a, F32) for a in (cos128, sin128, cos64, sina64, sinb64))


def _ctx_seg(i, tm):
    del tm
    return i * 0


def _lat_seg(i, tm):
    return 1 + (i * tm) // DEC_SEQ


def kernel(x_prompt, x_sample, cache_mla_ckv, cache_mla_krope, cache_swa_k, cache_swa_v, cache_na_k, cache_na_v, cache_diff_k, cache_diff_v, c, c_ctx, w_ada, b_ada, mix_pre_g, mix_post_g, ffn_pre_g, ffn_post_g, w_in, mla_q_norm_g, mla_kv_norm_g, w_mla_uq, w_mla_ukv, swa_sink, na_rpb, diff_lq1, diff_lk1, diff_lq2, diff_lk2, diff_subln_g, w_mix_gate, w_branch, w_out, w_ffn_gate, w_ffn_up, w_ffn_down):
    cvec = jnp.concatenate([c_ctx[None, :], c, jnp.zeros((N_MOD_ROWS - 1 - DEC_BATCH, D_MODEL), F32)], axis=0)
    mod_all = _ada_call(cvec, w_ada, b_ada).reshape(DEPTH, N_MOD_ROWS, 6, D_MODEL)
    bias_all = _nbr_bias_call(na_rpb)
    tables = _rope_tables()
    caches = (cache_mla_ckv, cache_mla_krope, cache_swa_k, cache_swa_v, cache_na_k, cache_na_v,
              cache_diff_k, cache_diff_v)

    w_main, w_kr = _cast_w_in_call(w_in)
    wg = _cast_call(w_mix_gate, 512, "cast_w_gate")
    wb = _cast_call(w_branch, 512, "cast_w_branch")
    wo = _cast_call(w_out, 512, "cast_w_out")
    wfg = _cast_call(w_ffn_gate, 256, "cast_w_ffn_gate")
    wfu = _cast_call(w_ffn_up, 256, "cast_w_ffn_up")
    wfd = _cast_call(w_ffn_down, 512, "cast_w_ffn_down")

    y = x_prompt.reshape(N_CTX_TOK, D_MODEL)
    z = x_sample.reshape(N_LAT_TOK, D_MODEL)
    new_caches = ()
    for l in range(DEPTH):
        lam_init = _lambda_init(l)
        mod = mod_all[l]
        uq = w_mla_uq[l].reshape(MLA_Q_RANK, MLA_HEADS, MLA_NOPE + MLA_ROPE)
        uq_rope = jnp.pad(uq[:, :, MLA_NOPE:], ((0, 0), (0, 0), (0, LANES - MLA_ROPE)))
        w_uq = jnp.concatenate([uq[:, :, :MLA_NOPE].reshape(MLA_Q_RANK, MLA_QN_W),
                                uq_rope.reshape(MLA_Q_RANK, MLA_QN_W)], axis=1).astype(BF16)
        small = (mla_q_norm_g[l][None, :], w_uq, mla_kv_norm_g[l][None, :], w_mla_ukv[l].astype(BF16),
                 swa_sink[l], diff_lq1[l][None, :], diff_lk1[l][None, :], diff_lq2[l][None, :],
                 diff_lk2[l][None, :], diff_subln_g[l][None, :])
        g_pre, g_post = mix_pre_g[l][None, :], mix_post_g[l][None, :]
        f_pre, f_post = ffn_pre_g[l][None, :], ffn_post_g[l][None, :]

        nxt = l + 1 if l + 1 < DEPTH else None
        side = (lambda *ws: ()) if nxt is None else (lambda *ws: tuple((w, nxt) for w in ws))

        h, pq, pkv, kr = _inproj_call(y, mod, g_pre, w_main, w_kr, l, _ctx_seg, True, "ctx_inproj")
        o, *new_caches = _ctx_attn_call(pq, pkv, kr, small, new_caches, l, lam_init)
        mixed, *nxt_a = _merge_call(h, o, wg, wb, "ctx_merge", side(w_mix_gate, w_branch))
        y, h2, *nxt_b = _outproj_call(mixed, y, mod, g_post, f_pre, wo, _ctx_seg, "ctx_outproj",
                                      side(w_out, w_ffn_gate))
        y = _ffn_call(y, h2, mod, f_post, wfg, wfu, wfd, _ctx_seg, "ctx_ffn")

        h, p, kr = _inproj_call(z, mod, g_pre, w_main, w_kr, l, _lat_seg, False, "lat_inproj")
        o = _lat_attn_call(p, kr, caches, tables, bias_all, small, l, lam_init)
        mixed, *nxt_c = _merge_call(h, o, wg, wb, "lat_merge", side(w_ffn_down))
        z, h2, *nxt_d = _outproj_call(mixed, z, mod, g_post, f_pre, wo, _lat_seg, "lat_outproj", side(w_ffn_up))
        z = _ffn_call(z, h2, mod, f_post, wfg, wfu, wfd, _lat_seg, "lat_ffn")
        if nxt is not None:
            (wg, wb), (wo, wfg), (wfd,), (wfu,) = nxt_a, nxt_b, nxt_c, nxt_d

    new_caches = [a.reshape((BATCH, DEPTH, SEQ) + t) for a, t in zip(new_caches, CACHE_TAILS)]
    return (y.reshape(BATCH, SEQ, D_MODEL), z.reshape(DEC_BATCH, DEC_SEQ, D_MODEL), *new_caches)
```
